```python
import math, functools
import jax, jax.numpy as jnp
from jax import lax
import numpy as np

D_MODEL = 1024
BATCH = 8
SEQ = 4096
DEPTH = 4

GRID_W = 64
CTX_LEN = 256
HEAD_DIM = 64
N_Q_HEADS = 8
N_KV_HEADS = 2
GQA_GROUP = N_Q_HEADS // N_KV_HEADS
D_ATTN = N_Q_HEADS * HEAD_DIM
KV_W = N_KV_HEADS * HEAD_DIM
Q_BLOCK = 128
ATTN_SCALE = HEAD_DIM ** -0.5
ROPE_THETA = 10000.0
ROPE_AXIS_DIM = HEAD_DIM // 2
QK_EPS = 1e-6
D_HYENA = 256
HYENA_ORDER = 2
HYENA_BANDS = 16
HYENA_EMB = 1 + 2 * HYENA_BANDS
HYENA_FILTER_HIDDEN = 64
HYENA_DECAY_TARGET = 1e-2
HYENA_FAST_DECAY_PCT = 0.3
HYENA_SLOW_DECAY_PCT = 1.5
D_POOL = 256
POOL_WINDOWS = (2, 4, 8, 16)
POOL_GROUP = D_POOL // len(POOL_WINDOWS)
Q_END = D_ATTN
K_END = Q_END + KV_W
V_END = K_END + KV_W
HY_END = V_END + 3 * D_HYENA
D_IN = HY_END + D_POOL
D_MIX = D_ATTN + D_HYENA + D_POOL
D_FF = 2816
N_EXPERTS = 8
TOP_K = 2
D_FF_EXPERT = 3584
N_DENSE = (DEPTH + 1) // 2
N_MOE = DEPTH // 2
DEEPNORM_ALPHA = (2.0 * DEPTH) ** 0.25
DEEPNORM_BETA = (8.0 * DEPTH) ** -0.25
LN_EPS = 1e-5
ADALN_EPS = 1e-6

kernel_name = 'hybrid_attn_hyena_pool_moe_dit'


def ln_plain(h, eps=ADALN_EPS):
    hf = h.astype(jnp.float32)
    mu = jnp.mean(hf, -1, keepdims=True)
    var = jnp.mean(jnp.square(hf - mu), -1, keepdims=True)
    return ((hf - mu) * lax.rsqrt(var + eps)).astype(h.dtype)


def ln_affine(h, g, b):
    hf = h.astype(jnp.float32)
    mu = jnp.mean(hf, -1, keepdims=True)
    var = jnp.mean(jnp.square(hf - mu), -1, keepdims=True)
    y = (hf - mu) * lax.rsqrt(var + LN_EPS) * g.astype(jnp.float32) + b.astype(jnp.float32)
    return y.astype(h.dtype)


def rms_norm(h, g):
    hf = h.astype(jnp.float32)
    y = hf * lax.rsqrt(jnp.mean(jnp.square(hf), -1, keepdims=True) + QK_EPS) * g.astype(jnp.float32)
    return y.astype(h.dtype)


def adaln(cond, w, b):
    return jnp.split(jax.nn.silu(cond) @ w + b, 6, axis=-1)


def modulate(h, shift, scale):
    return ln_plain(h) * (1.0 + scale[:, None]) + shift[:, None]


def deepnorm_update(h, gate, f, g, b):
    return ln_affine(DEEPNORM_ALPHA * h + gate[:, None] * f, g, b)


def axial_rope_tables(L, dtype):
    rows = L // GRID_W
    row = jnp.repeat(jnp.arange(rows), GRID_W).astype(jnp.float32)
    col = jnp.tile(jnp.arange(GRID_W), rows).astype(jnp.float32)
    inv = ROPE_THETA ** (-jnp.arange(0, ROPE_AXIS_DIM, 2, dtype=jnp.float32) / ROPE_AXIS_DIM)
    ang = jnp.concatenate([row[:, None] * inv, col[:, None] * inv], axis=-1)
    return jnp.cos(ang).astype(dtype), jnp.sin(ang).astype(dtype)


def apply_axial_rope(h, cos, sin):
    half = ROPE_AXIS_DIM // 2
    bshape = (h.shape[1],) + (1,) * (h.ndim - 3) + (2, half)
    cos = cos.reshape(bshape)
    sin = sin.reshape(bshape)
    hr = h.reshape(h.shape[:-1] + (2, 2, half))
    a, b = hr[..., 0, :], hr[..., 1, :]
    return jnp.stack([a * cos - b * sin, b * cos + a * sin], axis=-2).reshape(h.shape)


def head_groups(proj, q_gain, k_gain):
    B, L, _ = proj.shape
    q = proj[..., :Q_END].reshape(B, L, N_KV_HEADS, GQA_GROUP, HEAD_DIM)
    k = proj[..., Q_END:K_END].reshape(B, L, N_KV_HEADS, HEAD_DIM)
    v = proj[..., K_END:V_END].reshape(B, L, N_KV_HEADS, HEAD_DIM)
    return (rms_norm(q, q_gain), rms_norm(k, k_gain), v, proj[..., V_END:HY_END], proj[..., HY_END:])


def attend(q, keys, vals):
    s = jnp.einsum('bqhgd,bkhd->bhgqk', q, keys, preferred_element_type=jnp.float32) * ATTN_SCALE
    p = jax.nn.softmax(s, axis=-1).astype(vals.dtype)
    return jnp.einsum('bhgqk,bkhd->bqhgd', p, vals)


def latent_attention(q, k, v, k_ctx, v_ctx):
    B, S = q.shape[:2]
    keys = jnp.concatenate([k, k_ctx], axis=1)
    vals = jnp.concatenate([v, v_ctx], axis=1)
    qb = q.reshape((B, S // Q_BLOCK, Q_BLOCK) + q.shape[2:]).swapaxes(0, 1)
    o = lax.map(lambda qi: attend(qi, keys, vals), qb)
    return o.swapaxes(0, 1).reshape(B, S, D_ATTN)


def short_conv(z, w, b):
    zp = jnp.pad(z, ((0, 0), (1, 1), (0, 0)))
    return zp[:, :-2] * w[0] + zp[:, 1:-1] * w[1] + zp[:, 2:] * w[2] + b


def hyena_filter_spectra(L, w1, b1, freq, w2, b2, w3):
    f32 = jnp.float32
    t = jnp.linspace(0.0, 1.0, L, dtype=f32)[:, None]
    omega = 2.0 * math.pi * jnp.arange(L, dtype=f32)[:, None] / L
    bands = jnp.linspace(1e-4, HYENA_BANDS - 1, HYENA_BANDS, dtype=f32)[None, :]
    feats = jnp.concatenate([t, jnp.cos(omega * bands), -jnp.sin(omega * bands)], axis=-1)
    fr = freq.astype(f32)
    h = jnp.sin(fr * (feats @ w1.astype(f32) + b1.astype(f32)))
    h = jnp.sin(fr * (h @ w2.astype(f32) + b2.astype(f32)))
    h = (h @ w3.astype(f32)).reshape(L, HYENA_ORDER, 2, D_HYENA)
    max_decay = math.log(HYENA_DECAY_TARGET) / HYENA_FAST_DECAY_PCT
    min_decay = math.log(HYENA_DECAY_TARGET) / HYENA_SLOW_DECAY_PCT
    deltas = jnp.linspace(min_decay, max_decay, D_HYENA, dtype=f32)
    h = h * jnp.exp(-t[:, :, None, None] * jnp.abs(deltas))
    h = h / jnp.sum(jnp.abs(h), axis=(0, 2), keepdims=True)
    fwd, bwd = h[:, :, 0], h[:, :, 1]
    buf = jnp.concatenate([fwd, jnp.zeros((1,) + fwd.shape[1:], f32), bwd[:0:-1]], axis=0)
    return jnp.fft.rfft(buf, axis=0)


def long_conv(z, spec):
    L = z.shape[1]
    Z = jnp.fft.rfft(z, n=2 * L, axis=1)
    return jnp.fft.irfft(Z * spec[None], n=2 * L, axis=1)[:, :L]


def hyena_mixer(z_in, conv_w, conv_b, spec, d):
    zc = short_conv(z_in, conv_w, conv_b).astype(jnp.float32)
    v, g1, g2 = jnp.split(zc, 3, axis=-1)
    df = d.astype(jnp.float32)
    z1 = g1 * (long_conv(v, spec[:, 0]) + df[0] * v)
    y = g2 * (long_conv(z1, spec[:, 1]) + df[1] * z1)
    return y.astype(z_in.dtype)


def pool_mixer(p, w, scale):
    B, L, _ = p.shape
    pf = p.astype(jnp.float32)
    cs = jnp.pad(jnp.cumsum(pf, axis=1), ((0, 0), (1, 0), (0, 0)))
    t = jnp.arange(L)
    outs = []
    for g, win in enumerate(POOL_WINDOWS):
        lo = jnp.clip(t - win // 2, 0, L)
        hi = jnp.clip(t + win - win // 2, 0, L)
        sl = slice(g * POOL_GROUP, (g + 1) * POOL_GROUP)
        mean = (cs[:, hi, sl] - cs[:, lo, sl]) / (hi - lo).astype(jnp.float32)[:, None]
        outs.append(mean - pf[:, :, sl])
    y = jnp.stack(outs, axis=2)
    y = jnp.einsum('blgc,gcd->blgd', y, w.astype(jnp.float32)).reshape(B, L, D_POOL) * scale.astype(jnp.float32)
    return y.astype(p.dtype)


def swiglu(u, w_gate, w_up, w_down):
    return (jax.nn.silu(u @ w_gate) * (u @ w_up)) @ w_down


def moe_swiglu(u, router_w, w_gate, w_up, w_down):
    shp = u.shape
    tok = u.reshape(-1, shp[-1])
    logits = jnp.dot(tok, router_w, preferred_element_type=jnp.float32)
    top_v, top_i = lax.top_k(logits, TOP_K)
    gates = jax.nn.softmax(top_v, axis=-1)
    comb = jnp.sum(jax.nn.one_hot(top_i, N_EXPERTS, dtype=jnp.float32) * gates[..., None], axis=1)
    y = jnp.zeros(tok.shape, jnp.float32)
    for e in range(N_EXPERTS):
        y = y + comb[:, e:e + 1] * swiglu(tok, w_gate[e], w_up[e], w_down[e]).astype(jnp.float32)
    return y.astype(u.dtype).reshape(shp)


def setup_inputs(seed: int = 0) -> dict:
    key = jax.random.key(seed)
    ks = jax.random.split(key, 32)
    f32 = jnp.float32
    D = D_MODEL

    def nrm(k, shape, scale):
        return jax.random.normal(k, shape, f32) * scale

    def gain(k, shape):
        return 1.0 + nrm(k, shape, 0.02)

    return {
        'x': nrm(ks[0], (BATCH, SEQ, D), 1.0),
        'c': nrm(ks[1], (BATCH, D), 1.0),
        'ctx': nrm(ks[2], (BATCH, CTX_LEN, D), 1.0),
        'c_ctx': nrm(ks[3], (D,), 1.0),
        'w_mod': nrm(ks[4], (DEPTH, D, 6 * D), 0.5 * D ** -0.5),
        'b_mod': nrm(ks[5], (DEPTH, 6 * D), 0.02),
        'w_in': nrm(ks[6], (DEPTH, D, D_IN), D ** -0.5),
        'q_gain': gain(ks[7], (DEPTH, HEAD_DIM)),
        'k_gain': gain(ks[8], (DEPTH, HEAD_DIM)),
        'hy_conv_w': nrm(ks[9], (DEPTH, 3, 3 * D_HYENA), 3 ** -0.5),
        'hy_conv_b': nrm(ks[10], (DEPTH, 3 * D_HYENA), 0.02),
        'hy_f_w1': nrm(ks[11], (DEPTH, HYENA_EMB, HYENA_FILTER_HIDDEN), HYENA_EMB ** -0.5),
        'hy_f_b1': nrm(ks[12], (DEPTH, HYENA_FILTER_HIDDEN), 0.1),
        'hy_f_freq': gain(ks[13], (DEPTH, HYENA_FILTER_HIDDEN)),
        'hy_f_w2': nrm(ks[14], (DEPTH, HYENA_FILTER_HIDDEN, HYENA_FILTER_HIDDEN), HYENA_FILTER_HIDDEN ** -0.5),
        'hy_f_b2': nrm(ks[15], (DEPTH, HYENA_FILTER_HIDDEN), 0.1),
        'hy_f_w3': nrm(ks[16], (DEPTH, HYENA_FILTER_HIDDEN, HYENA_ORDER * 2 * D_HYENA), HYENA_FILTER_HIDDEN ** -0.5),
        'hy_d': nrm(ks[17], (DEPTH, HYENA_ORDER, D_HYENA), 1.0),
        'pool_w': nrm(ks[18], (DEPTH, len(POOL_WINDOWS), POOL_GROUP, POOL_GROUP), POOL_GROUP ** -0.5),
        'pool_scale': gain(ks[19], (DEPTH, D_POOL)),
        'w_out': nrm(ks[20], (DEPTH, D_MIX, D), DEEPNORM_BETA * D_MIX ** -0.5),
        'ln1_g': gain(ks[21], (DEPTH, D)),
        'ln1_b': nrm(ks[22], (DEPTH, D), 0.02),
        'ln2_g': gain(ks[23], (DEPTH, D)),
        'ln2_b': nrm(ks[24], (DEPTH, D), 0.02),
        'ffn_w_gate': nrm(ks[25], (N_DENSE, D, D_FF), D ** -0.5),
        'ffn_w_up': nrm(ks[26], (N_DENSE, D, D_FF), D ** -0.5),
        'ffn_w_down': nrm(ks[27], (N_DENSE, D_FF, D), DEEPNORM_BETA * D_FF ** -0.5),
        'router_w': nrm(ks[28], (N_MOE, D, N_EXPERTS), D ** -0.5),
        'moe_w_gate': nrm(ks[29], (N_MOE, N_EXPERTS, D, D_FF_EXPERT), D ** -0.5),
        'moe_w_up': nrm(ks[30], (N_MOE, N_EXPERTS, D, D_FF_EXPERT), D ** -0.5),
        'moe_w_down': nrm(ks[31], (N_MOE, N_EXPERTS, D_FF_EXPERT, D), DEEPNORM_BETA * D_FF_EXPERT ** -0.5),
    }


def reference(x, c, ctx, c_ctx, w_mod, b_mod, w_in, q_gain, k_gain, hy_conv_w, hy_conv_b, hy_f_w1, hy_f_b1, hy_f_freq, hy_f_w2, hy_f_b2, hy_f_w3, hy_d, pool_w, pool_scale, w_out, ln1_g, ln1_b, ln2_g, ln2_b, ffn_w_gate, ffn_w_up, ffn_w_down, router_w, moe_w_gate, moe_w_up, moe_w_down):
    B, S, _ = x.shape
    C = ctx.shape[1]
    cos, sin = axial_rope_tables(S, x.dtype)
    h_ctx = ctx
    for l in range(DEPTH):
        last = l == DEPTH - 1
        m = adaln(c, w_mod[l], b_mod[l])
        mc = adaln(c_ctx[None, :], w_mod[l], b_mod[l])
        filt = (hy_f_w1[l], hy_f_b1[l], hy_f_freq[l], hy_f_w2[l], hy_f_b2[l], hy_f_w3[l])
        if l % 2 == 0:
            j = l // 2
            ffn = functools.partial(swiglu, w_gate=ffn_w_gate[j], w_up=ffn_w_up[j], w_down=ffn_w_down[j])
        else:
            j = l // 2
            ffn = functools.partial(moe_swiglu, router_w=router_w[j], w_gate=moe_w_gate[j], w_up=moe_w_up[j], w_down=moe_w_down[j])

        uc = modulate(h_ctx, mc[0], mc[1])
        if last:
            kv = uc @ w_in[l][:, Q_END:V_END]
            kc = rms_norm(kv[..., :KV_W].reshape(B, C, N_KV_HEADS, HEAD_DIM), k_gain[l])
            vc = kv[..., KV_W:].reshape(B, C, N_KV_HEADS, HEAD_DIM)
        else:
            qc, kc, vc, hyc, plc = head_groups(uc @ w_in[l], q_gain[l], k_gain[l])
            att_c = attend(qc, kc, vc).reshape(B, C, D_ATTN)
            hyo_c = hyena_mixer(hyc, hy_conv_w[l], hy_conv_b[l], hyena_filter_spectra(C, *filt), hy_d[l])
            plo_c = pool_mixer(plc, pool_w[l], pool_scale[l])
            mix_c = jnp.concatenate([att_c, hyo_c, plo_c], axis=-1) @ w_out[l]

        u = modulate(x, m[0], m[1])
        q, k, v, hy, pl = head_groups(u @ w_in[l], q_gain[l], k_gain[l])
        q = apply_axial_rope(q, cos, sin)
        k = apply_axial_rope(k, cos, sin)
        att = latent_attention(q, k, v, kc, vc)
        hyo = hyena_mixer(hy, hy_conv_w[l], hy_conv_b[l], hyena_filter_spectra(S, *filt), hy_d[l])
        plo = pool_mixer(pl, pool_w[l], pool_scale[l])
        mix = jnp.concatenate([att, hyo, plo], axis=-1) @ w_out[l]
        x = deepnorm_update(x, m[2], mix, ln1_g[l], ln1_b[l])
        x = deepnorm_update(x, m[5], ffn(modulate(x, m[3], m[4])), ln2_g[l], ln2_b[l])

        if not last:
            h_ctx = deepnorm_update(h_ctx, mc[2], mix_c, ln1_g[l], ln1_b[l])
            h_ctx = deepnorm_update(h_ctx, mc[5], ffn(modulate(h_ctx, mc[3], mc[4])), ln2_g[l], ln2_b[l])
    return x
```

```python
import functools
import math

import numpy as np
import jax
import jax.numpy as jnp
from jax import lax
from jax.experimental import pallas as pl
from jax.experimental.pallas import tpu as pltpu

F32 = jnp.float32
BF16 = jnp.bfloat16
HIGHEST = lax.Precision.HIGHEST

GRID_W = 64
HEAD_DIM = 64
N_Q_HEADS = 8
N_KV_HEADS = 2
GQA_GROUP = N_Q_HEADS // N_KV_HEADS
D_ATTN = N_Q_HEADS * HEAD_DIM
KV_W = N_KV_HEADS * HEAD_DIM
QK_W = D_ATTN + KV_W
ATTN_SCALE = HEAD_DIM ** -0.5
ROPE_THETA = 10000.0
ROPE_AXIS_DIM = HEAD_DIM // 2
QK_EPS = 1e-6
D_HYENA = 256
HYENA_ORDER = 2
HYENA_BANDS = 16
HYENA_EMB = 1 + 2 * HYENA_BANDS
HYENA_DECAY_TARGET = 1e-2
HYENA_FAST_DECAY_PCT = 0.3
HYENA_SLOW_DECAY_PCT = 1.5
D_POOL = 256
POOL_WINDOWS = (2, 4, 8, 16)
POOL_GROUP = D_POOL // len(POOL_WINDOWS)
POOL_PAD = 8
TOP_K = 2
LN_EPS = 1e-5
ADALN_EPS = 1e-6

LANES = 128
FFT_N2 = 64
VMEM_LIMIT = 56 * 1024 * 1024


def _cparams():
    return pltpu.CompilerParams(vmem_limit_bytes=VMEM_LIMIT)


def _layernorm(x, eps):
    mu = jnp.mean(x, axis=-1, keepdims=True)
    xc = x - mu
    var = jnp.mean(xc * xc, axis=-1, keepdims=True)
    return xc * lax.rsqrt(var + eps)


def _mod_kernel(c_ref, w_ref, b_ref, o_ref):
    c = c_ref[...]
    s = c * jax.nn.sigmoid(c)
    o_ref[0] = jnp.dot(s, w_ref[0], preferred_element_type=F32, precision=HIGHEST) + b_ref[0]


def _mod_call(cc, w_mod, b_mod):
    depth, d, d6 = w_mod.shape
    r = cc.shape[0]
    tn = 1536
    return pl.pallas_call(
        _mod_kernel,
        grid=(depth, d6 // tn),
        in_specs=[pl.BlockSpec((r, d), lambda l, j: (0, 0)),
                  pl.BlockSpec((1, d, tn), lambda l, j: (l, 0, j)),
                  pl.BlockSpec((1, 1, tn), lambda l, j: (l, 0, j))],
        out_specs=pl.BlockSpec((1, r, tn), lambda l, j: (l, 0, j)),
        out_shape=jax.ShapeDtypeStruct((depth, r, d6), F32),
        compiler_params=_cparams(),
        name="adaln_mod",
    )(cc, w_mod, b_mod.reshape(depth, 1, d6))


def _rope_tables(seq_len, rope):
    lane = np.arange(LANES)
    d = lane % HEAD_DIM
    if not rope:
        one = np.ones((seq_len, LANES), np.float32)
        zero = np.zeros((seq_len, LANES), np.float32)
        return one, zero, zero
    t = np.arange(seq_len)
    row = (t // GRID_W).astype(np.float64)
    col = (t % GRID_W).astype(np.float64)
    half = ROPE_AXIS_DIM // 2
    inv = ROPE_THETA ** (-np.arange(0, ROPE_AXIS_DIM, 2, dtype=np.float64) / ROPE_AXIS_DIM)
    pos = np.where((d // ROPE_AXIS_DIM)[None, :] == 0, row[:, None], col[:, None])
    ang = pos * inv[d % half][None, :]
    is_b = ((d % ROPE_AXIS_DIM) >= half)[None, :]
    cos = np.cos(ang)
    sin = np.sin(ang)
    s_up = np.where(is_b, sin, 0.0)
    s_dn = np.where(is_b, 0.0, -sin)
    return cos.astype(np.float32), s_up.astype(np.float32), s_dn.astype(np.float32)


def _proj_kernel(x_ref, sh_ref, sc_ref, w_ref, gain_ref, bd_ref, cos_ref, sup_ref, sdn_ref,
                 qk_ref, v_ref, hp_ref):
    u = _layernorm(x_ref[...], ADALN_EPS) * (1.0 + sc_ref[0]) + sh_ref[0]
    p = jnp.dot(u.astype(BF16), w_ref[...], preferred_element_type=F32)
    qk = p[:, :QK_W]
    ms = jnp.dot((qk * qk).astype(BF16), bd_ref[...], preferred_element_type=F32)
    qn = qk * lax.rsqrt(ms + QK_EPS) * gain_ref[...]
    cos = cos_ref[...]
    sup = sup_ref[...]
    sdn = sdn_ref[...]
    for j in range(QK_W // LANES):
        blk = qn[:, j * LANES:(j + 1) * LANES]
        up = pltpu.roll(blk, ROPE_AXIS_DIM // 2, axis=1)
        dn = pltpu.roll(blk, LANES - ROPE_AXIS_DIM // 2, axis=1)
        qk_ref[:, j * LANES:(j + 1) * LANES] = (blk * cos + up * sup + dn * sdn).astype(qk_ref.dtype)
    v_ref[...] = p[:, QK_W:QK_W + KV_W].astype(v_ref.dtype)
    hp_ref[...] = p[:, QK_W + KV_W:]


def _proj_call(x2, mod3, mod_row, seq_len, w_in_bf, gain, bd, tables, tm):
    r, d = x2.shape
    d_in = w_in_bf.shape[1]
    d_hp = d_in - QK_W - KV_W
    nseq = seq_len // tm
    cos, sup, sdn = tables
    tab_spec = pl.BlockSpec((tm, LANES), lambda i: (i % nseq, 0))
    return pl.pallas_call(
        _proj_kernel,
        grid=(r // tm,),
        in_specs=[pl.BlockSpec((tm, d), lambda i: (i, 0)),
                  pl.BlockSpec((1, 1, d), lambda i: (mod_row(i) * 6 + 0, 0, 0)),
                  pl.BlockSpec((1, 1, d), lambda i: (mod_row(i) * 6 + 1, 0, 0)),
                  pl.BlockSpec((d, d_in), lambda i: (0, 0)),
                  pl.BlockSpec((1, QK_W), lambda i: (0, 0)),
                  pl.BlockSpec((QK_W, QK_W), lambda i: (0, 0)),
                  tab_spec, tab_spec, tab_spec],
        out_specs=[pl.BlockSpec((tm, QK_W), lambda i: (i, 0)),
                   pl.BlockSpec((tm, KV_W), lambda i: (i, 0)),
                   pl.BlockSpec((tm, d_hp), lambda i: (i, 0))],
        out_shape=[jax.ShapeDtypeStruct((r, QK_W), BF16),
                   jax.ShapeDtypeStruct((r, KV_W), BF16),
                   jax.ShapeDtypeStruct((r, d_hp), F32)],
        compiler_params=_cparams(),
        name="ln_mod_in_proj",
    )(x2, mod3, mod3, w_in_bf, gain, bd, cos, sup, sdn)


def _attn_kernel(q_ref, k_ref, v_ref, o_ref):
    for h in range(N_KV_HEADS):
        kh = k_ref[:, h * HEAD_DIM:(h + 1) * HEAD_DIM]
        vh = v_ref[:, h * HEAD_DIM:(h + 1) * HEAD_DIM]
        for g in range(GQA_GROUP):
            c0 = (h * GQA_GROUP + g) * HEAD_DIM
            q = q_ref[:, c0:c0 + HEAD_DIM]
            s = lax.dot_general(q, kh, (((1,), (1,)), ((), ())), preferred_element_type=F32)
            m = jnp.max(s, axis=-1, keepdims=True)
            p = jnp.exp(s - m)
            l = jnp.sum(p, axis=-1, keepdims=True)
            o = jnp.dot(p.astype(BF16), vh, preferred_element_type=F32) / l
            o_ref[:, c0:c0 + HEAD_DIM] = o.astype(o_ref.dtype)


def _attn_call(qk, keys, vals, tq):
    b, lq, _ = qk.shape
    lk = keys.shape[1]
    return pl.pallas_call(
        _attn_kernel,
        grid=(b, lq // tq),
        in_specs=[pl.BlockSpec((None, tq, D_ATTN), lambda i, j: (i, j, 0)),
                  pl.BlockSpec((None, lk, KV_W), lambda i, j: (i, 0, 0)),
                  pl.BlockSpec((None, lk, KV_W), lambda i, j: (i, 0, 0))],
        out_specs=pl.BlockSpec((None, tq, D_ATTN), lambda i, j: (i, j, 0)),
        out_shape=jax.ShapeDtypeStruct((b, lq, D_ATTN), BF16),
        compiler_params=_cparams(),
        name="attention",
    )(qk, keys, vals)


def _filter_consts(seq_len):
    t = np.linspace(0.0, 1.0, seq_len, dtype=np.float32).astype(np.float64)[:, None]
    omega = 2.0 * math.pi * np.arange(seq_len, dtype=np.float64)[:, None] / seq_len
    bands = np.linspace(1e-4, HYENA_BANDS - 1, HYENA_BANDS, dtype=np.float32).astype(np.float64)[None, :]
    feats = np.concatenate([t, np.cos(omega * bands), -np.sin(omega * bands)], axis=-1)
    feats = np.pad(feats, ((0, 0), (0, LANES - HYENA_EMB)))
    max_decay = math.log(HYENA_DECAY_TARGET) / HYENA_FAST_DECAY_PCT
    min_decay = math.log(HYENA_DECAY_TARGET) / HYENA_SLOW_DECAY_PCT
    deltas = np.linspace(min_decay, max_decay, D_HYENA, dtype=np.float32).astype(np.float64)
    decay = np.exp(-t * np.abs(deltas)[None, :])
    decay = np.concatenate([decay, decay], axis=1)
    return feats.astype(np.float32), decay.astype(np.float32)


def _filter_kernel(feat_ref, w1_ref, b1_ref, fr_ref, w2_ref, b2_ref, w3_ref, dec_ref, o_ref):
    fr = fr_ref[...]
    h = jnp.sin(fr * (jnp.dot(feat_ref[...], w1_ref[...], preferred_element_type=F32, precision=HIGHEST)
                      + b1_ref[...]))
    h = jnp.sin(fr * (jnp.dot(h, w2_ref[...], preferred_element_type=F32, precision=HIGHEST) + b2_ref[...]))
    h = jnp.dot(h, w3_ref[...], preferred_element_type=F32, precision=HIGHEST) * dec_ref[...]
    tot = jnp.sum(jnp.abs(h), axis=0, keepdims=True)
    tot = tot[:, :D_HYENA] + tot[:, D_HYENA:]
    inv = 1.0 / tot
    o_ref[...] = h * jnp.concatenate([inv, inv], axis=1)


def _filter_call(seq_len, w1, b1, freq, w2, b2, w3):
    feats, decay = _filter_consts(seq_len)
    hid = w2.shape[0]
    w1p = jnp.pad(w1, ((0, LANES - HYENA_EMB), (0, 0)))
    blk = 2 * D_HYENA
    full = lambda shape: pl.BlockSpec(shape, lambda o: (0,) * len(shape))
    return pl.pallas_call(
        _filter_kernel,
        grid=(HYENA_ORDER,),
        in_specs=[full((seq_len, LANES)), full((LANES, hid)), full((1, hid)), full((1, hid)),
                  full((hid, hid)), full((1, hid)),
                  pl.BlockSpec((hid, blk), lambda o: (0, o)),
                  full((seq_len, blk))],
        out_specs=pl.BlockSpec((seq_len, blk), lambda o: (0, o)),
        out_shape=jax.ShapeDtypeStruct((seq_len, HYENA_ORDER * blk), F32),
        compiler_params=_cparams(),
        name="hyena_filter",
    )(jnp.asarray(feats), w1p, b1.reshape(1, hid), freq.reshape(1, hid), w2, b2.reshape(1, hid), w3,
      jnp.asarray(decay))


def _filter_buffer(filt, seq_len):
    f = filt.reshape(seq_len, HYENA_ORDER, 2, D_HYENA)
    fwd, bwd = f[:, :, 0], f[:, :, 1]
    buf = jnp.concatenate([fwd, jnp.zeros((1,) + fwd.shape[1:], F32), bwd[:0:-1]], axis=0)
    return buf.reshape(2 * seq_len, HYENA_ORDER * D_HYENA)


@functools.lru_cache(maxsize=None)
def _fft_consts(seq_len):
    n = 2 * seq_len
    n2 = FFT_N2
    n1 = n // n2
    h1 = n1 // 2
    k1 = np.arange(n1)
    w1 = np.exp(-2j * np.pi * np.outer(k1, np.arange(n1)) / n1)
    f1 = np.zeros((n1, 2, 2 * h1))
    f1[:, 0, :h1], f1[:, 0, h1:] = w1[:, :h1].real, -w1[:, :h1].imag
    f1[:, 1, :h1], f1[:, 1, h1:] = w1[:, :h1].imag, w1[:, :h1].real
    f1 = f1.reshape(2 * n1, 2 * h1)
    f1r = np.stack([w1.real, w1.imag], axis=1).reshape(2 * n1, n1)
    a2 = np.arange(n2)
    tw = np.exp(-2j * np.pi * np.outer(k1, a2) / n)
    w2 = np.exp(-2j * np.pi * np.outer(a2, a2) / n2)
    m = w2[None, :, :] * tw[:, None, :]
    ff = np.concatenate([np.concatenate([m.real, -m.imag], axis=2),
                         np.concatenate([m.imag, m.real], axis=2)], axis=1)
    g = np.conj(w2.T)[None, :, :] * np.conj(tw)[:, :, None]
    gi = np.concatenate([np.concatenate([g.real, -g.imag], axis=2),
                         np.concatenate([g.imag, g.real], axis=2)], axis=1)
    v = np.conj(w1[:, :h1]).T
    f1i = np.zeros((2, h1, n1, 2))
    f1i[0, :, :, 0], f1i[0, :, :, 1] = v.real, -v.imag
    f1i[1, :, :, 0], f1i[1, :, :, 1] = v.imag, v.real
    f1i = f1i.reshape(2 * h1, 2 * n1)
    f32 = lambda a: np.asarray(a, np.float32)
    return dict(n=n, n1=n1, h1=h1, f1=f32(f1), f1r=f32(f1r), ff=f32(ff), gi=f32(gi), f1i=f32(f1i))


def _fft1_kernel(x_ref, f_ref, o_ref, *, exact):
    if exact:
        y = jnp.dot(f_ref[...], x_ref[...], preferred_element_type=F32, precision=HIGHEST)
    else:
        y = jnp.dot(f_ref[...], x_ref[...].astype(BF16), preferred_element_type=F32)
    o_ref[...] = y.astype(o_ref.dtype)


def _fft1_call(x3, f, out_dtype, exact, tc):
    g, k, cols = x3.shape
    m = f.shape[0]
    return pl.pallas_call(
        functools.partial(_fft1_kernel, exact=exact),
        grid=(g, cols // tc),
        in_specs=[pl.BlockSpec((None, k, tc), lambda i, j: (i, 0, j)),
                  pl.BlockSpec((m, k), lambda i, j: (0, 0))],
        out_specs=pl.BlockSpec((None, m, tc), lambda i, j: (i, 0, j)),
        out_shape=jax.ShapeDtypeStruct((g, m, cols), out_dtype),
        compiler_params=_cparams(),
        name="fft_outer_dft",
    )(x3, f)


def _spec_kernel(a_ref, ff_ref, o_ref, *, scale):
    for kk in range(a_ref.shape[0]):
        o_ref[kk] = jnp.dot(ff_ref[kk], a_ref[kk], preferred_element_type=F32, precision=HIGHEST) * scale


def _spec_call(a3, ff, scale, kt):
    n1, r, cols = a3.shape
    return pl.pallas_call(
        functools.partial(_spec_kernel, scale=scale),
        grid=(n1 // kt,),
        in_specs=[pl.BlockSpec((kt, r, cols), lambda i: (i, 0, 0)),
                  pl.BlockSpec((kt, r, r), lambda i: (i, 0, 0))],
        out_specs=pl.BlockSpec((kt, r, cols), lambda i: (i, 0, 0)),
        out_shape=jax.ShapeDtypeStruct((n1, r, cols), F32),
        compiler_params=_cparams(),
        name="fft_filter_spectrum",
    )(a3, ff)


def _fft2_kernel(a_ref, ff_ref, gi_ref, h_ref, o_ref):
    groups, kt = a_ref.shape[:2]
    half = FFT_N2

    def body(kk, carry):
        ff = ff_ref[kk]
        gi = gi_ref[kk]
        hr = h_ref[kk, :half, :]
        hi = h_ref[kk, half:, :]
        for g in range(groups):
            xh = jnp.dot(ff, a_ref[g, kk], preferred_element_type=F32)
            xr, xi = xh[:half], xh[half:]
            y = jnp.concatenate([xr * hr - xi * hi, xr * hi + xi * hr], axis=0).astype(BF16)
            o_ref[g, kk] = jnp.dot(gi, y, preferred_element_type=F32).astype(o_ref.dtype)
        return carry

    lax.fori_loop(0, kt, body, 0)


def _fft2_call(a4, ff_bf, gi_bf, spec, order, kt):
    g, n1, r, c = a4.shape
    return pl.pallas_call(
        _fft2_kernel,
        grid=(n1 // kt,),
        in_specs=[pl.BlockSpec((g, kt, r, c), lambda i: (0, i, 0, 0)),
                  pl.BlockSpec((kt, r, r), lambda i: (i, 0, 0)),
                  pl.BlockSpec((kt, r, r), lambda i: (i, 0, 0)),
                  pl.BlockSpec((kt, r, c), lambda i: (i, 0, order))],
        out_specs=pl.BlockSpec((g, kt, r, c), lambda i: (0, i, 0, 0)),
        out_shape=jax.ShapeDtypeStruct(a4.shape, BF16),
        compiler_params=_cparams(),
        name="fft_inner_conv",
    )(a4, ff_bf, gi_bf, spec)


def _fft3_kernel(b_ref, f_ref, z_ref, g_ref, d_ref, o_ref):
    y = jnp.dot(f_ref[...], b_ref[...], preferred_element_type=F32)
    o_ref[...] = g_ref[...] * (y + d_ref[...] * z_ref[...])


def _fft3_call(b3, f1i_bf, z3, g3, d_row, tc):
    g, k, cols = b3.shape
    m = f1i_bf.shape[0]
    io = pl.BlockSpec((None, m, tc), lambda i, j: (i, 0, j))
    return pl.pallas_call(
        _fft3_kernel,
        grid=(g, cols // tc),
        in_specs=[pl.BlockSpec((None, k, tc), lambda i, j: (i, 0, j)),
                  pl.BlockSpec((m, k), lambda i, j: (0, 0)),
                  io, io,
                  pl.BlockSpec((1, tc), lambda i, j: (0, 0))],
        out_specs=io,
        out_shape=jax.ShapeDtypeStruct((g, m, cols), F32),
        compiler_params=_cparams(),
        name="fft_outer_inverse_gate",
    )(b3, f1i_bf, z3, g3, d_row)


def _sconv_kernel(z_ref, w_ref, b_ref, o_ref):
    z = z_ref[...]
    n = z.shape[0]
    row = lax.broadcasted_iota(jnp.int32, z.shape, 0)
    prev = jnp.where(row == 0, 0.0, pltpu.roll(z, 1, axis=0))
    nxt = jnp.where(row == n - 1, 0.0, pltpu.roll(z, n - 1, axis=0))
    o_ref[...] = prev * w_ref[0:1, :] + z * w_ref[1:2, :] + nxt * w_ref[2:3, :] + b_ref[...]


def _sconv_call(hp3, conv_w, conv_b):
    b, l, _ = hp3.shape
    per = D_HYENA // LANES
    nblk = 3 * per
    return pl.pallas_call(
        _sconv_kernel,
        grid=(b, nblk),
        in_specs=[pl.BlockSpec((None, l, LANES), lambda i, j: (i, 0, j)),
                  pl.BlockSpec((3, LANES), lambda i, j: (0, j)),
                  pl.BlockSpec((1, LANES), lambda i, j: (0, j))],
        out_specs=pl.BlockSpec((None, None, l, LANES), lambda i, j: (j // per, i, 0, j % per)),
        out_shape=jax.ShapeDtypeStruct((3, b, l, D_HYENA), F32),
        compiler_params=_cparams(),
        name="hyena_short_conv",
    )(hp3, conv_w, conv_b.reshape(1, -1))


def _hyena_spectrum(filt, seq_len):
    cst = _fft_consts(seq_len)
    n, n1 = cst["n"], cst["n1"]
    buf = _filter_buffer(filt, seq_len)
    cols = FFT_N2 * buf.shape[1]
    a = _fft1_call(buf.reshape(1, n1, cols), jnp.asarray(cst["f1r"]), F32, True, min(cols, 2048))
    a3 = a.reshape(n1, 2 * FFT_N2, buf.shape[1])
    return _spec_call(a3, jnp.asarray(cst["ff"]), 1.0 / n, min(n1, 8))


def _hyena_latent(hp3, conv_w, conv_b, spec, d):
    b, l, _ = hp3.shape
    cst = _fft_consts(l)
    n1, h1 = cst["n1"], cst["h1"]
    cols = FFT_N2 * D_HYENA
    tc = min(cols, 2048)
    kt = min(n1, 8)
    f1 = jnp.asarray(cst["f1"], BF16)
    ff = jnp.asarray(cst["ff"], BF16)
    gi = jnp.asarray(cst["gi"], BF16)
    f1i = jnp.asarray(cst["f1i"], BF16)
    zc = _sconv_call(hp3, conv_w, conv_b).reshape(3, b // 2, 2 * h1, cols)
    z = zc[0]
    for o in range(HYENA_ORDER):
        a = _fft1_call(z, f1, BF16, False, tc)
        bb = _fft2_call(a.reshape(b // 2, n1, 2 * FFT_N2, D_HYENA), ff, gi, spec, o, kt)
        d_row = jnp.tile(d[o], tc // D_HYENA).reshape(1, tc)
        z = _fft3_call(bb.reshape(b // 2, 2 * n1, cols), f1i, z, zc[1 + o], d_row, tc)
    return z.reshape(b, l, D_HYENA)


@functools.lru_cache(maxsize=None)
def _dft_consts(seq_len):
    n = 2 * seq_len
    w = np.exp(-2j * np.pi * np.outer(np.arange(n), np.arange(n)) / n)
    fwd_full = np.concatenate([w.real, w.imag], axis=0)
    fwd = fwd_full[:, :seq_len]
    wi = np.conj(w[:, :seq_len]).T
    inv = np.concatenate([wi.real, -wi.imag], axis=1)
    f32 = lambda a: np.asarray(a, np.float32)
    return dict(n=n, fwd_full=f32(fwd_full), fwd=f32(fwd), inv=f32(inv))


def _cspec_kernel(buf_ref, f_ref, o_ref, *, scale):
    o_ref[...] = jnp.dot(f_ref[...], buf_ref[...], preferred_element_type=F32, precision=HIGHEST) * scale


def _ctx_spectrum(filt, seq_len):
    cst = _dft_consts(seq_len)
    buf = _filter_buffer(filt, seq_len)
    n = cst["n"]
    return pl.pallas_call(
        functools.partial(_cspec_kernel, scale=1.0 / n),
        out_shape=jax.ShapeDtypeStruct((2 * n, buf.shape[1]), F32),
        compiler_params=_cparams(),
        name="ctx_filter_spectrum",
    )(buf, jnp.asarray(cst["fwd_full"]))


def _ctx_hyena_kernel(z_ref, w_ref, b_ref, f_ref, fi_ref, h_ref, d_ref, o_ref):
    z = z_ref[...]
    l = z.shape[0]
    n = 2 * l
    row = lax.broadcasted_iota(jnp.int32, z.shape, 0)
    prev = jnp.where(row == 0, 0.0, pltpu.roll(z, 1, axis=0))
    nxt = jnp.where(row == l - 1, 0.0, pltpu.roll(z, l - 1, axis=0))
    zc = prev * w_ref[0:1, :] + z * w_ref[1:2, :] + nxt * w_ref[2:3, :] + b_ref[...]
    cur = zc[:, :D_HYENA]
    for o in range(HYENA_ORDER):
        gate = zc[:, (1 + o) * D_HYENA:(2 + o) * D_HYENA]
        xh = jnp.dot(f_ref[...], cur.astype(BF16), preferred_element_type=F32)
        xr, xi = xh[:n], xh[n:]
        hr = h_ref[:n, o * D_HYENA:(o + 1) * D_HYENA]
        hi = h_ref[n:, o * D_HYENA:(o + 1) * D_HYENA]
        y = jnp.concatenate([xr * hr - xi * hi, xr * hi + xi * hr], axis=0).astype(BF16)
        conv = jnp.dot(fi_ref[...], y, preferred_element_type=F32)
        cur = gate * (conv + d_ref[o:o + 1, :] * cur)
    o_ref[...] = cur


def _ctx_hyena_call(hp3, conv_w, conv_b, spec, d):
    b, l, _ = hp3.shape
    cst = _dft_consts(l)
    n = cst["n"]
    w3 = 3 * D_HYENA
    full = lambda shape: pl.BlockSpec(shape, lambda i: (0,) * len(shape))
    return pl.pallas_call(
        _ctx_hyena_kernel,
        grid=(b,),
        in_specs=[pl.BlockSpec((None, l, w3), lambda i: (i, 0, 0)),
                  full((3, w3)), full((1, w3)), full((2 * n, l)), full((l, 2 * n)),
                  full((2 * n, HYENA_ORDER * D_HYENA)), full((HYENA_ORDER, D_HYENA))],
        out_specs=pl.BlockSpec((None, l, D_HYENA), lambda i: (i, 0, 0)),
        out_shape=jax.ShapeDtypeStruct((b, l, D_HYENA), F32),
        compiler_params=_cparams(),
        name="ctx_hyena",
    )(hp3, conv_w, conv_b.reshape(1, -1), jnp.asarray(cst["fwd"], BF16), jnp.asarray(cst["inv"], BF16),
      spec, d)


def _pool_kernel(p_ref, w_ref, sc_ref, o_ref):
    x = p_ref[...]
    l, c = x.shape
    half = pl.program_id(1)
    zpad = jnp.zeros((POOL_PAD, c), F32)
    xp = jnp.concatenate([zpad, x, zpad], axis=0)
    tot = l + 2 * POOL_PAD
    sums = []
    f = xp
    for win in POOL_WINDOWS:
        f = f + pltpu.roll(f, tot - win // 2, axis=0)
        sums.append(pltpu.roll(f, win // 2, axis=0)[POOL_PAD:POOL_PAD + l])
    t = lax.broadcasted_iota(jnp.int32, (l, c), 0)
    grp = lax.broadcasted_iota(jnp.int32, (l, c), 1) // POOL_GROUP + half * (LANES // POOL_GROUP)
    total = sums[-1]
    cnt = None
    for gi in range(len(POOL_WINDOWS) - 1, -1, -1):
        win = POOL_WINDOWS[gi]
        cw = (jnp.minimum(t + (win - win // 2), l) - jnp.maximum(t - win // 2, 0)).astype(F32)
        if cnt is None:
            cnt = cw
        else:
            sel = grp == gi
            total = jnp.where(sel, sums[gi], total)
            cnt = jnp.where(sel, cw, cnt)
    dlt = total / cnt - x
    y = jnp.dot(dlt.astype(BF16), w_ref[0], preferred_element_type=F32)
    o_ref[...] = y * sc_ref[...]


def _pool_call(hp3, pool_w, pool_scale):
    b, l, w = hp3.shape
    nh = D_POOL // LANES
    first = (w - D_POOL) // LANES
    per = LANES // POOL_GROUP
    wbd = jnp.zeros((nh, LANES, LANES), F32)
    for g in range(len(POOL_WINDOWS)):
        r0 = (g % per) * POOL_GROUP
        wbd = wbd.at[g // per, r0:r0 + POOL_GROUP, r0:r0 + POOL_GROUP].set(pool_w[g])
    return pl.pallas_call(
        _pool_kernel,
        grid=(b, nh),
        in_specs=[pl.BlockSpec((None, l, LANES), lambda i, j: (i, 0, first + j)),
                  pl.BlockSpec((1, LANES, LANES), lambda i, j: (j, 0, 0)),
                  pl.BlockSpec((1, LANES), lambda i, j: (0, j))],
        out_specs=pl.BlockSpec((None, l, LANES), lambda i, j: (i, 0, j)),
        out_shape=jax.ShapeDtypeStruct((b, l, D_POOL), F32),
        compiler_params=_cparams(),
        name="pool_mixer",
    )(hp3, wbd.astype(BF16), pool_scale.reshape(1, D_POOL))


def _outproj_kernel(x_ref, att_ref, hy_ref, pl_ref, gate_ref, w_ref, g_ref, b_ref, o_ref, *, alpha):
    mix = jnp.dot(att_ref[...], w_ref[0:D_ATTN, :], preferred_element_type=F32)
    mix += jnp.dot(hy_ref[...].astype(BF16), w_ref[D_ATTN:D_ATTN + D_HYENA, :], preferred_element_type=F32)
    mix += jnp.dot(pl_ref[...].astype(BF16), w_ref[D_ATTN + D_HYENA:, :], preferred_element_type=F32)
    y = alpha * x_ref[...] + gate_ref[0] * mix
    o_ref[...] = _layernorm(y, LN_EPS) * g_ref[...] + b_ref[...]


def _outproj_call(x2, att2, hy2, pl2, mod3, mod_row, w_out_bf, g, b, alpha, tm):
    r, d = x2.shape
    row = lambda w: pl.BlockSpec((tm, w), lambda i: (i, 0))
    vec = pl.BlockSpec((1, d), lambda i: (0, 0))
    return pl.pallas_call(
        functools.partial(_outproj_kernel, alpha=alpha),
        grid=(r // tm,),
        in_specs=[row(d), row(D_ATTN), row(D_HYENA), row(D_POOL),
                  pl.BlockSpec((1, 1, d), lambda i: (mod_row(i) * 6 + 2, 0, 0)),
                  pl.BlockSpec(w_out_bf.shape, lambda i: (0, 0)),
                  vec, vec],
        out_specs=row(d),
        out_shape=jax.ShapeDtypeStruct((r, d), F32),
        compiler_params=_cparams(),
        name="out_proj_deepnorm",
    )(x2, att2, hy2, pl2, mod3, w_out_bf, g.reshape(1, d), b.reshape(1, d))


def _ffn_kernel(*refs, alpha, n_exp, n_chunk):
    if n_exp > 1:
        (x_ref, sh_ref, sc_ref, gate_ref, rw_ref, wg_ref, wu_ref, wd_ref, g_ref, b_ref,
         o_ref, u_ref, acc_ref, comb_ref) = refs
    else:
        (x_ref, sh_ref, sc_ref, gate_ref, wg_ref, wu_ref, wd_ref, g_ref, b_ref,
         o_ref, u_ref, acc_ref) = refs
    j = pl.program_id(1)

    @pl.when(j == 0)
    def _():
        u = (_layernorm(x_ref[...], ADALN_EPS) * (1.0 + sc_ref[0]) + sh_ref[0]).astype(BF16)
        u_ref[...] = u
        acc_ref[...] = jnp.zeros_like(acc_ref)
        if n_exp > 1:
            logits = jnp.dot(u, rw_ref[...].astype(BF16), preferred_element_type=F32)
            lane = lax.broadcasted_iota(jnp.int32, logits.shape, 1)
            m1 = jnp.max(logits, axis=-1, keepdims=True)
            i1 = jnp.min(jnp.where(logits == m1, lane, n_exp), axis=-1, keepdims=True)
            rest = jnp.where(lane == i1, -jnp.inf, logits)
            m2 = jnp.max(rest, axis=-1, keepdims=True)
            i2 = jnp.min(jnp.where(rest == m2, lane, n_exp), axis=-1, keepdims=True)
            e2 = jnp.exp(m2 - m1)
            den = 1.0 + e2
            comb_ref[...] = jnp.where(lane == i1, 1.0 / den, 0.0) + jnp.where(lane == i2, e2 / den, 0.0)

    u = u_ref[...]
    a = jnp.dot(u, wg_ref[0].astype(BF16), preferred_element_type=F32)
    bb = jnp.dot(u, wu_ref[0].astype(BF16), preferred_element_type=F32)
    h = a * jax.nn.sigmoid(a) * bb
    if n_exp > 1:
        comb = comb_ref[...]
        lane = lax.broadcasted_iota(jnp.int32, comb.shape, 1)
        h = h * jnp.sum(jnp.where(lane == j // n_chunk, comb, 0.0), axis=-1, keepdims=True)
    acc_ref[...] += jnp.dot(h.astype(BF16), wd_ref[0].astype(BF16), preferred_element_type=F32)

    @pl.when(j == n_exp * n_chunk - 1)
    def _():
        y = alpha * x_ref[...] + gate_ref[0] * acc_ref[...]
        o_ref[...] = _layernorm(y, LN_EPS) * g_ref[...] + b_ref[...]


def _ffn_call(x2, mod3, mod_row, router_w, w_gate, w_up, w_down, g, b, alpha, tm, fc):
    r, d = x2.shape
    n_exp, _, dff = w_gate.shape
    n_chunk = dff // fc
    steps = n_exp * n_chunk
    vec = pl.BlockSpec((1, d), lambda i, j: (0, 0))
    mod = lambda which: pl.BlockSpec((1, 1, d), lambda i, j: (mod_row(i) * 6 + which, 0, 0))
    in_specs = [pl.BlockSpec((tm, d), lambda i, j: (i, 0)), mod(3), mod(4), mod(5)]
    args = [x2, mod3, mod3, mod3]
    scratch = [pltpu.VMEM((tm, d), BF16), pltpu.VMEM((tm, d), F32)]
    if n_exp > 1:
        in_specs.append(pl.BlockSpec(router_w.shape, lambda i, j: (0, 0)))
        args.append(router_w)
        scratch.append(pltpu.VMEM((tm, n_exp), F32))
    in_specs += [pl.BlockSpec((1, d, fc), lambda i, j: (j // n_chunk, 0, j % n_chunk)),
                 pl.BlockSpec((1, d, fc), lambda i, j: (j // n_chunk, 0, j % n_chunk)),
                 pl.BlockSpec((1, fc, d), lambda i, j: (j // n_chunk, j % n_chunk, 0)),
                 vec, vec]
    args += [w_gate, w_up, w_down, g.reshape(1, d), b.reshape(1, d)]
    return pl.pallas_call(
        functools.partial(_ffn_kernel, alpha=alpha, n_exp=n_exp, n_chunk=n_chunk),
        grid=(r // tm, steps),
        in_specs=in_specs,
        out_specs=pl.BlockSpec((tm, d), lambda i, j: (i, 0)),
        out_shape=jax.ShapeDtypeStruct((r, d), F32),
        scratch_shapes=scratch,
        compiler_params=_cparams(),
        name="channel_mixer",
    )(*args)


def _pick_tile(n, pref):
    t = min(n, pref)
    while n % t:
        t //= 2
    return t


def kernel(x, c, ctx, c_ctx, w_mod, b_mod, w_in, q_gain, k_gain, hy_conv_w, hy_conv_b, hy_f_w1, hy_f_b1, hy_f_freq, hy_f_w2, hy_f_b2, hy_f_w3, hy_d, pool_w, pool_scale, w_out, ln1_g, ln1_b, ln2_g, ln2_b, ffn_w_gate, ffn_w_up, ffn_w_down, router_w, moe_w_gate, moe_w_up, moe_w_down):
    bsz, seq, d = x.shape
    clen = ctx.shape[1]
    depth = w_mod.shape[0]
    alpha = (2.0 * depth) ** 0.25

    n_rows = -(-(bsz + 1) // 8) * 8
    cc = jnp.zeros((n_rows, d), F32).at[:bsz].set(c).at[bsz].set(c_ctx)
    mod = _mod_call(cc, w_mod, b_mod)

    tm_lat = _pick_tile(seq, 512)
    tm_ctx = _pick_tile(clen, 512)
    tm_ffn = _pick_tile(seq, 1024)
    tm_ffn_ctx = _pick_tile(bsz * clen, 1024)
    lat_row = lambda tm: (lambda i: i // (seq // tm))
    ctx_row = lambda i: bsz
    rope_lat = tuple(jnp.asarray(t) for t in _rope_tables(seq, True))
    rope_ctx = tuple(jnp.asarray(t) for t in _rope_tables(clen, False))
    head_avg = np.kron(np.eye(QK_W // HEAD_DIM), np.full((HEAD_DIM, HEAD_DIM), 1.0 / HEAD_DIM))
    head_avg = jnp.asarray(head_avg, BF16)

    xl = x.reshape(bsz * seq, d)
    xc = ctx.reshape(bsz * clen, d)
    for l in range(depth):
        last = l == depth - 1
        mod3 = mod[l].reshape(n_rows * 6, 1, d)
        w_in_bf = w_in[l].astype(BF16)
        w_out_bf = w_out[l].astype(BF16)
        gain = jnp.concatenate([jnp.tile(q_gain[l] * ATTN_SCALE, N_Q_HEADS),
                                jnp.tile(k_gain[l], N_KV_HEADS)]).reshape(1, QK_W)
        filt_w = (hy_f_w1[l], hy_f_b1[l], hy_f_freq[l], hy_f_w2[l], hy_f_b2[l], hy_f_w3[l])
        j = l // 2
        if l % 2 == 0:
            ffn_w = (None, ffn_w_gate[j][None], ffn_w_up[j][None], ffn_w_down[j][None])
        else:
            ffn_w = (router_w[j], moe_w_gate[j], moe_w_up[j], moe_w_down[j])
        fc = 256

        qk_c, v_c, hp_c = _proj_call(xc, mod3, ctx_row, clen, w_in_bf, gain, head_avg, rope_ctx, tm_ctx)
        qk_c3 = qk_c.reshape(bsz, clen, QK_W)
        k_c3 = qk_c3[:, :, D_ATTN:]
        v_c3 = v_c.reshape(bsz, clen, KV_W)
        if not last:
            hp_c3 = hp_c.reshape(bsz, clen, -1)
            att_c = _attn_call(qk_c3, k_c3, v_c3, _pick_tile(clen, 256))
            spec_c = _ctx_spectrum(_filter_call(clen, *filt_w), clen)
            hy_c = _ctx_hyena_call(hp_c3, hy_conv_w[l], hy_conv_b[l], spec_c, hy_d[l])
            pl_c = _pool_call(hp_c3, pool_w[l], pool_scale[l])
            xc_mid = _outproj_call(xc, att_c.reshape(bsz * clen, D_ATTN), hy_c.reshape(bsz * clen, D_HYENA),
                                   pl_c.reshape(bsz * clen, D_POOL), mod3, ctx_row, w_out_bf,
                                   ln1_g[l], ln1_b[l], alpha, tm_ctx)

        qk_l, v_l, hp_l = _proj_call(xl, mod3, lat_row(tm_lat), seq, w_in_bf, gain, head_avg, rope_lat, tm_lat)
        qk_l3 = qk_l.reshape(bsz, seq, QK_W)
        hp_l3 = hp_l.reshape(bsz, seq, -1)
        keys = jnp.concatenate([qk_l3[:, :, D_ATTN:], k_c3], axis=1)
        vals = jnp.concatenate([v_l.reshape(bsz, seq, KV_W), v_c3], axis=1)
        att = _attn_call(qk_l3, keys, vals, _pick_tile(seq, 256))
        spec = _hyena_spectrum(_filter_call(seq, *filt_w), seq)
        hyo = _hyena_latent(hp_l3, hy_conv_w[l], hy_conv_b[l], spec, hy_d[l])
        plo = _pool_call(hp_l3, pool_w[l], pool_scale[l])
        xl = _outproj_call(xl, att.reshape(bsz * seq, D_ATTN), hyo.reshape(bsz * seq, D_HYENA),
                           plo.reshape(bsz * seq, D_POOL), mod3, lat_row(tm_lat), w_out_bf,
                           ln1_g[l], ln1_b[l], alpha, tm_lat)
        xl = _ffn_call(xl, mod3, lat_row(tm_ffn), *ffn_w, ln2_g[l], ln2_b[l], alpha, tm_ffn, fc)

        if not last:
            xc = _ffn_call(xc_mid, mod3, ctx_row, *ffn_w, ln2_g[l], ln2_b[l], alpha, tm_ffn_ctx, fc)
    return xl.reshape(bsz, seq, d)
```

```python
import functools
import math

import numpy as np
import jax
import jax.numpy as jnp
from jax import lax
from jax.experimental import pallas as pl
from jax.experimental.pallas import tpu as pltpu

F32 = jnp.float32
BF16 = jnp.bfloat16
HIGHEST = lax.Precision.HIGHEST

GRID_W = 64
HEAD_DIM = 64
N_Q_HEADS = 8
N_KV_HEADS = 2
GQA_GROUP = N_Q_HEADS // N_KV_HEADS
D_ATTN = N_Q_HEADS * HEAD_DIM
KV_W = N_KV_HEADS * HEAD_DIM
QK_W = D_ATTN + KV_W
ATTN_SCALE = HEAD_DIM ** -0.5
ROPE_THETA = 10000.0
ROPE_AXIS_DIM = HEAD_DIM // 2
QK_EPS = 1e-6
D_HYENA = 256
HYENA_ORDER = 2
HYENA_BANDS = 16
HYENA_EMB = 1 + 2 * HYENA_BANDS
HYENA_DECAY_TARGET = 1e-2
HYENA_FAST_DECAY_PCT = 0.3
HYENA_SLOW_DECAY_PCT = 1.5
D_POOL = 256
POOL_WINDOWS = (2, 4, 8, 16)
POOL_GROUP = D_POOL // len(POOL_WINDOWS)
POOL_PAD = 8
TOP_K = 2
LN_EPS = 1e-5
ADALN_EPS = 1e-6

LANES = 128
FFT_N2 = 64
VMEM_LIMIT = 56 * 1024 * 1024


def _cparams():
    return pltpu.CompilerParams(vmem_limit_bytes=VMEM_LIMIT)


def _layernorm(x, eps):
    mu = jnp.mean(x, axis=-1, keepdims=True)
    xc = x - mu
    var = jnp.mean(xc * xc, axis=-1, keepdims=True)
    return xc * lax.rsqrt(var + eps)


def _mod_kernel(c_ref, w_ref, b_ref, o_ref):
    c = c_ref[...]
    s = c * jax.nn.sigmoid(c)
    o_ref[0] = jnp.dot(s, w_ref[0], preferred_element_type=F32, precision=HIGHEST) + b_ref[0]


def _mod_call(cc, w_mod, b_mod):
    depth, d, d6 = w_mod.shape
    r = cc.shape[0]
    tn = 1536
    return pl.pallas_call(
        _mod_kernel,
        grid=(depth, d6 // tn),
        in_specs=[pl.BlockSpec((r, d), lambda l, j: (0, 0)),
                  pl.BlockSpec((1, d, tn), lambda l, j: (l, 0, j)),
                  pl.BlockSpec((1, 1, tn), lambda l, j: (l, 0, j))],
        out_specs=pl.BlockSpec((1, r, tn), lambda l, j: (l, 0, j)),
        out_shape=jax.ShapeDtypeStruct((depth, r, d6), F32),
        compiler_params=_cparams(),
        name="adaln_mod",
    )(cc, w_mod, b_mod.reshape(depth, 1, d6))


def _rope_tables(seq_len, rope):
    lane = np.arange(LANES)
    d = lane % HEAD_DIM
    if not rope:
        one = np.ones((seq_len, LANES), np.float32)
        zero = np.zeros((seq_len, LANES), np.float32)
        return one, zero, zero
    t = np.arange(seq_len)
    row = (t // GRID_W).astype(np.float64)
    col = (t % GRID_W).astype(np.float64)
    half = ROPE_AXIS_DIM // 2
    inv = ROPE_THETA ** (-np.arange(0, ROPE_AXIS_DIM, 2, dtype=np.float64) / ROPE_AXIS_DIM)
    pos = np.where((d // ROPE_AXIS_DIM)[None, :] == 0, row[:, None], col[:, None])
    ang = pos * inv[d % half][None, :]
    is_b = ((d % ROPE_AXIS_DIM) >= half)[None, :]
    cos = np.cos(ang)
    sin = np.sin(ang)
    s_up = np.where(is_b, sin, 0.0)
    s_dn = np.where(is_b, 0.0, -sin)
    return cos.astype(np.float32), s_up.astype(np.float32), s_dn.astype(np.float32)


def _proj_kernel(x_ref, sh_ref, sc_ref, w_ref, gain_ref, bd_ref, cos_ref, sup_ref, sdn_ref,
                 qk_ref, v_ref, hp_ref):
    u = _layernorm(x_ref[...], ADALN_EPS) * (1.0 + sc_ref[0]) + sh_ref[0]
    p = jnp.dot(u.astype(BF16), w_ref[...], preferred_element_type=F32)
    qk = p[:, :QK_W]
    ms = jnp.dot((qk * qk).astype(BF16), bd_ref[...], preferred_element_type=F32)
    qn = qk * lax.rsqrt(ms + QK_EPS) * gain_ref[...]
    cos = cos_ref[...]
    sup = sup_ref[...]
    sdn = sdn_ref[...]
    for j in range(QK_W // LANES):
        blk = qn[:, j * LANES:(j + 1) * LANES]
        up = pltpu.roll(blk, ROPE_AXIS_DIM // 2, axis=1)
        dn = pltpu.roll(blk, LANES - ROPE_AXIS_DIM // 2, axis=1)
        qk_ref[:, j * LANES:(j + 1) * LANES] = (blk * cos + up * sup + dn * sdn).astype(qk_ref.dtype)
    v_ref[...] = p[:, QK_W:QK_W + KV_W].astype(v_ref.dtype)
    hp_ref[...] = p[:, QK_W + KV_W:]


def _proj_call(x2, mod3, mod_row, seq_len, w_in_bf, gain, bd, tables, tm):
    r, d = x2.shape
    d_in = w_in_bf.shape[1]
    d_hp = d_in - QK_W - KV_W
    nseq = seq_len // tm
    cos, sup, sdn = tables
    tab_spec = pl.BlockSpec((tm, LANES), lambda i: (i % nseq, 0))
    return pl.pallas_call(
        _proj_kernel,
        grid=(r // tm,),
        in_specs=[pl.BlockSpec((tm, d), lambda i: (i, 0)),
                  pl.BlockSpec((1, 1, d), lambda i: (mod_row(i) * 6 + 0, 0, 0)),
                  pl.BlockSpec((1, 1, d), lambda i: (mod_row(i) * 6 + 1, 0, 0)),
                  pl.BlockSpec((d, d_in), lambda i: (0, 0)),
                  pl.BlockSpec((1, QK_W), lambda i: (0, 0)),
                  pl.BlockSpec((QK_W, QK_W), lambda i: (0, 0)),
                  tab_spec, tab_spec, tab_spec],
        out_specs=[pl.BlockSpec((tm, QK_W), lambda i: (i, 0)),
                   pl.BlockSpec((tm, KV_W), lambda i: (i, 0)),
                   pl.BlockSpec((tm, d_hp), lambda i: (i, 0))],
        out_shape=[jax.ShapeDtypeStruct((r, QK_W), BF16),
                   jax.ShapeDtypeStruct((r, KV_W), BF16),
                   jax.ShapeDtypeStruct((r, d_hp), F32)],
        compiler_params=_cparams(),
        name="ln_mod_in_proj",
    )(x2, mod3, mod3, w_in_bf, gain, bd, cos, sup, sdn)


def _attn_kernel(q_ref, k_ref, v_ref, o_ref):
    for h in range(N_KV_HEADS):
        kh = k_ref[:, h * HEAD_DIM:(h + 1) * HEAD_DIM]
        vh = v_ref[:, h * HEAD_DIM:(h + 1) * HEAD_DIM]
        for g in range(GQA_GROUP):
            c0 = (h * GQA_GROUP + g) * HEAD_DIM
            q = q_ref[:, c0:c0 + HEAD_DIM]
            s = lax.dot_general(q, kh, (((1,), (1,)), ((), ())), preferred_element_type=F32)
            m = jnp.max(s, axis=-1, keepdims=True)
            p = jnp.exp(s - m)
            l = jnp.sum(p, axis=-1, keepdims=True)
            o = jnp.dot(p.astype(BF16), vh, preferred_element_type=F32) / l
            o_ref[:, c0:c0 + HEAD_DIM] = o.astype(o_ref.dtype)


def _attn_call(qk, keys, vals, tq):
    b, lq, _ = qk.shape
    lk = keys.shape[1]
    return pl.pallas_call(
        _attn_kernel,
        grid=(b, lq // tq),
        in_specs=[pl.BlockSpec((None, tq, D_ATTN), lambda i, j: (i, j, 0)),
                  pl.BlockSpec((None, lk, KV_W), lambda i, j: (i, 0, 0)),
                  pl.BlockSpec((None, lk, KV_W), lambda i, j: (i, 0, 0))],
        out_specs=pl.BlockSpec((None, tq, D_ATTN), lambda i, j: (i, j, 0)),
        out_shape=jax.ShapeDtypeStruct((b, lq, D_ATTN), BF16),
        compiler_params=_cparams(),
        name="attention",
    )(qk, keys, vals)


def _filter_consts(seq_len):
    t = np.linspace(0.0, 1.0, seq_len, dtype=np.float32).astype(np.float64)[:, None]
    omega = 2.0 * math.pi * np.arange(seq_len, dtype=np.float64)[:, None] / seq_len
    bands = np.linspace(1e-4, HYENA_BANDS - 1, HYENA_BANDS, dtype=np.float32).astype(np.float64)[None, :]
    feats = np.concatenate([t, np.cos(omega * bands), -np.sin(omega * bands)], axis=-1)
    feats = np.pad(feats, ((0, 0), (0, LANES - HYENA_EMB)))
    max_decay = math.log(HYENA_DECAY_TARGET) / HYENA_FAST_DECAY_PCT
    min_decay = math.log(HYENA_DECAY_TARGET) / HYENA_SLOW_DECAY_PCT
    deltas = np.linspace(min_decay, max_decay, D_HYENA, dtype=np.float32).astype(np.float64)
    decay = np.exp(-t * np.abs(deltas)[None, :])
    decay = np.concatenate([decay, decay], axis=1)
    return feats.astype(np.float32), decay.astype(np.float32)


def _filter_kernel(feat_ref, w1_ref, b1_ref, fr_ref, w2_ref, b2_ref, w3_ref, dec_ref, o_ref):
    fr = fr_ref[...]
    h = jnp.sin(fr * (jnp.dot(feat_ref[...], w1_ref[...], preferred_element_type=F32, precision=HIGHEST)
                      + b1_ref[...]))
    h = jnp.sin(fr * (jnp.dot(h, w2_ref[...], preferred_element_type=F32, precision=HIGHEST) + b2_ref[...]))
    h = jnp.dot(h, w3_ref[...], preferred_element_type=F32, precision=HIGHEST) * dec_ref[...]
    tot = jnp.sum(jnp.abs(h), axis=0, keepdims=True)
    tot = tot[:, :D_HYENA] + tot[:, D_HYENA:]
    inv = 1.0 / tot
    o_ref[...] = h * jnp.concatenate([inv, inv], axis=1)


def _filter_call(seq_len, w1, b1, freq, w2, b2, w3):
    feats, decay = _filter_consts(seq_len)
    hid = w2.shape[0]
    w1p = jnp.pad(w1, ((0, LANES - HYENA_EMB), (0, 0)))
    blk = 2 * D_HYENA
    full = lambda shape: pl.BlockSpec(shape, lambda o: (0,) * len(shape))
    return pl.pallas_call(
        _filter_kernel,
        grid=(HYENA_ORDER,),
        in_specs=[full((seq_len, LANES)), full((LANES, hid)), full((1, hid)), full((1, hid)),
                  full((hid, hid)), full((1, hid)),
                  pl.BlockSpec((hid, blk), lambda o: (0, o)),
                  full((seq_len, blk))],
        out_specs=pl.BlockSpec((seq_len, blk), lambda o: (0, o)),
        out_shape=jax.ShapeDtypeStruct((seq_len, HYENA_ORDER * blk), F32),
        compiler_params=_cparams(),
        name="hyena_filter",
    )(jnp.asarray(feats), w1p, b1.reshape(1, hid), freq.reshape(1, hid), w2, b2.reshape(1, hid), w3,
      jnp.asarray(decay))


def _filter_buffer(filt, seq_len):
    f = filt.reshape(seq_len, HYENA_ORDER, 2, D_HYENA)
    fwd, bwd = f[:, :, 0], f[:, :, 1]
    buf = jnp.concatenate([fwd, jnp.zeros((1,) + fwd.shape[1:], F32), bwd[:0:-1]], axis=0)
    return buf.reshape(2 * seq_len, HYENA_ORDER * D_HYENA)


@functools.lru_cache(maxsize=None)
def _fft_consts(seq_len):
    n = 2 * seq_len
    n2 = FFT_N2
    n1 = n // n2
    h1 = n1 // 2
    k1 = np.arange(n1)
    w1 = np.exp(-2j * np.pi * np.outer(k1, np.arange(n1)) / n1)
    f1 = np.zeros((n1, 2, 2 * h1))
    f1[:, 0, :h1], f1[:, 0, h1:] = w1[:, :h1].real, -w1[:, :h1].imag
    f1[:, 1, :h1], f1[:, 1, h1:] = w1[:, :h1].imag, w1[:, :h1].real
    f1 = f1.reshape(2 * n1, 2 * h1)
    f1r = np.stack([w1.real, w1.imag], axis=1).reshape(2 * n1, n1)
    a2 = np.arange(n2)
    tw = np.exp(-2j * np.pi * np.outer(k1, a2) / n)
    w2 = np.exp(-2j * np.pi * np.outer(a2, a2) / n2)
    m = w2[None, :, :] * tw[:, None, :]
    ff = np.concatenate([np.concatenate([m.real, -m.imag], axis=2),
                         np.concatenate([m.imag, m.real], axis=2)], axis=1)
    g = np.conj(w2.T)[None, :, :] * np.conj(tw)[:, :, None]
    gi = np.concatenate([np.concatenate([g.real, -g.imag], axis=2),
                         np.concatenate([g.imag, g.real], axis=2)], axis=1)
    v = np.conj(w1[:, :h1]).T
    f1i = np.zeros((2, h1, n1, 2))
    f1i[0, :, :, 0], f1i[0, :, :, 1] = v.real, -v.imag
    f1i[1, :, :, 0], f1i[1, :, :, 1] = v.imag, v.real
    f1i = f1i.reshape(2 * h1, 2 * n1)
    f32 = lambda a: np.asarray(a, np.float32)
    return dict(n=n, n1=n1, h1=h1, f1=f32(f1), f1r=f32(f1r), ff=f32(ff), gi=f32(gi), f1i=f32(f1i))


def _fft1_kernel(x_ref, f_ref, o_ref, *, exact):
    if exact:
        y = jnp.dot(f_ref[...], x_ref[...], preferred_element_type=F32, precision=HIGHEST)
    else:
        y = jnp.dot(f_ref[...], x_ref[...].astype(BF16), preferred_element_type=F32)
    o_ref[...] = y.astype(o_ref.dtype)


def _fft1_call(x3, f, out_dtype, exact, tc):
    g, k, cols = x3.shape
    m = f.shape[0]
    return pl.pallas_call(
        functools.partial(_fft1_kernel, exact=exact),
        grid=(g, cols // tc),
        in_specs=[pl.BlockSpec((None, k, tc), lambda i, j: (i, 0, j)),
                  pl.BlockSpec((m, k), lambda i, j: (0, 0))],
        out_specs=pl.BlockSpec((None, m, tc), lambda i, j: (i, 0, j)),
        out_shape=jax.ShapeDtypeStruct((g, m, cols), out_dtype),
        compiler_params=_cparams(),
        name="fft_outer_dft",
    )(x3, f)


def _spec_kernel(a_ref, ff_ref, o_ref, *, scale):
    for kk in range(a_ref.shape[0]):
        o_ref[kk] = jnp.dot(ff_ref[kk], a_ref[kk], preferred_element_type=F32, precision=HIGHEST) * scale


def _spec_call(a3, ff, scale, kt):
    n1, r, cols = a3.shape
    return pl.pallas_call(
        functools.partial(_spec_kernel, scale=scale),
        grid=(n1 // kt,),
        in_specs=[pl.BlockSpec((kt, r, cols), lambda i: (i, 0, 0)),
                  pl.BlockSpec((kt, r, r), lambda i: (i, 0, 0))],
        out_specs=pl.BlockSpec((kt, r, cols), lambda i: (i, 0, 0)),
        out_shape=jax.ShapeDtypeStruct((n1, r, cols), F32),
        compiler_params=_cparams(),
        name="fft_filter_spectrum",
    )(a3, ff)


def _fft2_kernel(a_ref, ff_ref, gi_ref, h_ref, o_ref):
    groups, kt = a_ref.shape[:2]
    half = FFT_N2

    def body(kk, carry):
        ff = ff_ref[kk]
        gi = gi_ref[kk]
        hr = h_ref[kk, :half, :]
        hi = h_ref[kk, half:, :]
        for g in range(groups):
            xh = jnp.dot(ff, a_ref[g, kk], preferred_element_type=F32)
            xr, xi = xh[:half], xh[half:]
            y = jnp.concatenate([xr * hr - xi * hi, xr * hi + xi * hr], axis=0).astype(BF16)
            o_ref[g, kk] = jnp.dot(gi, y, preferred_element_type=F32).astype(o_ref.dtype)
        return carry

    lax.fori_loop(0, kt, body, 0)


def _fft2_call(a4, ff_bf, gi_bf, spec, order, kt):
    g, n1, r, c = a4.shape
    return pl.pallas_call(
        _fft2_kernel,
        grid=(n1 // kt,),
        in_specs=[pl.BlockSpec((g, kt, r, c), lambda i: (0, i, 0, 0)),
                  pl.BlockSpec((kt, r, r), lambda i: (i, 0, 0)),
                  pl.BlockSpec((kt, r, r), lambda i: (i, 0, 0)),
                  pl.BlockSpec((kt, r, c), lambda i: (i, 0, order))],
        out_specs=pl.BlockSpec((g, kt, r, c), lambda i: (0, i, 0, 0)),
        out_shape=jax.ShapeDtypeStruct(a4.shape, BF16),
        compiler_params=_cparams(),
        name="fft_inner_conv",
    )(a4, ff_bf, gi_bf, spec)


def _fft3_kernel(b_ref, f_ref, z_ref, g_ref, d_ref, o_ref):
    y = jnp.dot(f_ref[...], b_ref[...], preferred_element_type=F32)
    o_ref[...] = g_ref[...] * (y + d_ref[...] * z_ref[...])


def _fft3_call(b3, f1i_bf, z3, g3, d_row, tc):
    g, k, cols = b3.shape
    m = f1i_bf.shape[0]
    io = pl.BlockSpec((None, m, tc), lambda i, j: (i, 0, j))
    return pl.pallas_call(
        _fft3_kernel,
        grid=(g, cols // tc),
        in_specs=[pl.BlockSpec((None, k, tc), lambda i, j: (i, 0, j)),
                  pl.BlockSpec((m, k), lambda i, j: (0, 0)),
                  io, io,
                  pl.BlockSpec((1, tc), lambda i, j: (0, 0))],
        out_specs=io,
        out_shape=jax.ShapeDtypeStruct((g, m, cols), F32),
        compiler_params=_cparams(),
        name="fft_outer_inverse_gate",
    )(b3, f1i_bf, z3, g3, d_row)


def _sconv_kernel(z_ref, w_ref, b_ref, o_ref):
    z = z_ref[...]
    n = z.shape[0]
    row = lax.broadcasted_iota(jnp.int32, z.shape, 0)
    prev = jnp.where(row == 0, 0.0, pltpu.roll(z, 1, axis=0))
    nxt = jnp.where(row == n - 1, 0.0, pltpu.roll(z, n - 1, axis=0))
    o_ref[...] = prev * w_ref[0:1, :] + z * w_ref[1:2, :] + nxt * w_ref[2:3, :] + b_ref[...]


def _sconv_call(hp3, conv_w, conv_b):
    b, l, _ = hp3.shape
    per = D_HYENA // LANES
    nblk = 3 * per
    return pl.pallas_call(
        _sconv_kernel,
        grid=(b, nblk),
        in_specs=[pl.BlockSpec((None, l, LANES), lambda i, j: (i, 0, j)),
                  pl.BlockSpec((3, LANES), lambda i, j: (0, j)),
                  pl.BlockSpec((1, LANES), lambda i, j: (0, j))],
        out_specs=pl.BlockSpec((None, None, l, LANES), lambda i, j: (j // per, i, 0, j % per)),
        out_shape=jax.ShapeDtypeStruct((3, b, l, D_HYENA), F32),
        compiler_params=_cparams(),
        name="hyena_short_conv",
    )(hp3, conv_w, conv_b.reshape(1, -1))


def _hyena_spectrum(filt, seq_len):
    cst = _fft_consts(seq_len)
    n, n1 = cst["n"], cst["n1"]
    buf = _filter_buffer(filt, seq_len)
    cols = FFT_N2 * buf.shape[1]
    a = _fft1_call(buf.reshape(1, n1, cols), jnp.asarray(cst["f1r"]), F32, True, min(cols, 2048))
    a3 = a.reshape(n1, 2 * FFT_N2, buf.shape[1])
    return _spec_call(a3, jnp.asarray(cst["ff"]), 1.0 / n, min(n1, 8))


def _hyena_latent(hp3, conv_w, conv_b, spec, d):
    b, l, _ = hp3.shape
    cst = _fft_consts(l)
    n1, h1 = cst["n1"], cst["h1"]
    cols = FFT_N2 * D_HYENA
    tc = min(cols, 2048)
    kt = min(n1, 8)
    f1 = jnp.asarray(cst["f1"], BF16)
    ff = jnp.asarray(cst["ff"], BF16)
    gi = jnp.asarray(cst["gi"], BF16)
    f1i = jnp.asarray(cst["f1i"], BF16)
    zc = _sconv_call(hp3, conv_w, conv_b).reshape(3, b // 2, 2 * h1, cols)
    z = zc[0]
    for o in range(HYENA_ORDER):
        a = _fft1_call(z, f1, BF16, False, tc)
        bb = _fft2_call(a.reshape(b // 2, n1, 2 * FFT_N2, D_HYENA), ff, gi, spec, o, kt)
        d_row = jnp.tile(d[o], tc // D_HYENA).reshape(1, tc)
        z = _fft3_call(bb.reshape(b // 2, 2 * n1, cols), f1i, z, zc[1 + o], d_row, tc)
    return z.reshape(b, l, D_HYENA)


@functools.lru_cache(maxsize=None)
def _dft_consts(seq_len):
    n = 2 * seq_len
    w = np.exp(-2j * np.pi * np.outer(np.arange(n), np.arange(n)) / n)
    fwd_full = np.concatenate([w.real, w.imag], axis=0)
    fwd = fwd_full[:, :seq_len]
    wi = np.conj(w[:, :seq_len]).T
    inv = np.concatenate([wi.real, -wi.imag], axis=1)
    f32 = lambda a: np.asarray(a, np.float32)
    return dict(n=n, fwd_full=f32(fwd_full), fwd=f32(fwd), inv=f32(inv))


def _cspec_kernel(buf_ref, f_ref, o_ref, *, scale):
    o_ref[...] = jnp.dot(f_ref[...], buf_ref[...], preferred_element_type=F32, precision=HIGHEST) * scale


def _ctx_spectrum(filt, seq_len):
    cst = _dft_consts(seq_len)
    buf = _filter_buffer(filt, seq_len)
    n = cst["n"]
    return pl.pallas_call(
        functools.partial(_cspec_kernel, scale=1.0 / n),
        out_shape=jax.ShapeDtypeStruct((2 * n, buf.shape[1]), F32),
        compiler_params=_cparams(),
        name="ctx_filter_spectrum",
    )(buf, jnp.asarray(cst["fwd_full"]))


def _ctx_hyena_kernel(z_ref, w_ref, b_ref, f_ref, fi_ref, h_ref, d_ref, o_ref):
    z = z_ref[...]
    l = z.shape[0]
    n = 2 * l
    row = lax.broadcasted_iota(jnp.int32, z.shape, 0)
    prev = jnp.where(row == 0, 0.0, pltpu.roll(z, 1, axis=0))
    nxt = jnp.where(row == l - 1, 0.0, pltpu.roll(z, l - 1, axis=0))
    zc = prev * w_ref[0:1, :] + z * w_ref[1:2, :] + nxt * w_ref[2:3, :] + b_ref[...]
    cur = zc[:, :D_HYENA]
    for o in range(HYENA_ORDER):
        gate = zc[:, (1 + o) * D_HYENA:(2 + o) * D_HYENA]
        xh = jnp.dot(f_ref[...], cur.astype(BF16), preferred_element_type=F32)
        xr, xi = xh[:n], xh[n:]
        hr = h_ref[:n, o * D_HYENA:(o + 1) * D_HYENA]
        hi = h_ref[n:, o * D_HYENA:(o + 1) * D_HYENA]
        y = jnp.concatenate([xr * hr - xi * hi, xr * hi + xi * hr], axis=0).astype(BF16)
        conv = jnp.dot(fi_ref[...], y, preferred_element_type=F32)
        cur = gate * (conv + d_ref[o:o + 1, :] * cur)
    o_ref[...] = cur


def _ctx_hyena_call(hp3, conv_w, conv_b, spec, d):
    b, l, _ = hp3.shape
    cst = _dft_consts(l)
    n = cst["n"]
    w3 = 3 * D_HYENA
    full = lambda shape: pl.BlockSpec(shape, lambda i: (0,) * len(shape))
    return pl.pallas_call(
        _ctx_hyena_kernel,
        grid=(b,),
        in_specs=[pl.BlockSpec((None, l, w3), lambda i: (i, 0, 0)),
                  full((3, w3)), full((1, w3)), full((2 * n, l)), full((l, 2 * n)),
                  full((2 * n, HYENA_ORDER * D_HYENA)), full((HYENA_ORDER, D_HYENA))],
        out_specs=pl.BlockSpec((None, l, D_HYENA), lambda i: (i, 0, 0)),
        out_shape=jax.ShapeDtypeStruct((b, l, D_HYENA), F32),
        compiler_params=_cparams(),
        name="ctx_hyena",
    )(hp3, conv_w, conv_b.reshape(1, -1), jnp.asarray(cst["fwd"], BF16), jnp.asarray(cst["inv"], BF16),
      spec, d)


def _pool_kernel(p_ref, w_ref, sc_ref, o_ref):
    x = p_ref[...]
    l, c = x.shape
    half = pl.program_id(1)
    zpad = jnp.zeros((POOL_PAD, c), F32)
    xp = jnp.concatenate([zpad, x, zpad], axis=0)
    tot = l + 2 * POOL_PAD
    sums = []
    f = xp
    for win in POOL_WINDOWS:
        f = f + pltpu.roll(f, tot - win // 2, axis=0)
        sums.append(pltpu.roll(f, win // 2, axis=0)[POOL_PAD:POOL_PAD + l])
    t = lax.broadcasted_iota(jnp.int32, (l, c), 0)
    grp = lax.broadcasted_iota(jnp.int32, (l, c), 1) // POOL_GROUP + half * (LANES // POOL_GROUP)
    total = sums[-1]
    cnt = None
    for gi in range(len(POOL_WINDOWS) - 1, -1, -1):
        win = POOL_WINDOWS[gi]
        cw = (jnp.minimum(t + (win - win // 2), l) - jnp.maximum(t - win // 2, 0)).astype(F32)
        if cnt is None:
            cnt = cw
        else:
            sel = grp == gi
            total = jnp.where(sel, sums[gi], total)
            cnt = jnp.where(sel, cw, cnt)
    dlt = total / cnt - x
    y = jnp.dot(dlt.astype(BF16), w_ref[0], preferred_element_type=F32)
    o_ref[...] = y * sc_ref[...]


def _pool_call(hp3, pool_w, pool_scale):
    b, l, w = hp3.shape
    nh = D_POOL // LANES
    first = (w - D_POOL) // LANES
    per = LANES // POOL_GROUP
    wbd = jnp.zeros((nh, LANES, LANES), F32)
    for g in range(len(POOL_WINDOWS)):
        r0 = (g % per) * POOL_GROUP
        wbd = wbd.at[g // per, r0:r0 + POOL_GROUP, r0:r0 + POOL_GROUP].set(pool_w[g])
    return pl.pallas_call(
        _pool_kernel,
        grid=(b, nh),
        in_specs=[pl.BlockSpec((None, l, LANES), lambda i, j: (i, 0, first + j)),
                  pl.BlockSpec((1, LANES, LANES), lambda i, j: (j, 0, 0)),
                  pl.BlockSpec((1, LANES), lambda i, j: (0, j))],
        out_specs=pl.BlockSpec((None, l, LANES), lambda i, j: (i, 0, j)),
        out_shape=jax.ShapeDtypeStruct((b, l, D_POOL), F32),
        compiler_params=_cparams(),
        name="pool_mixer",
    )(hp3, wbd.astype(BF16), pool_scale.reshape(1, D_POOL))


def _outproj_kernel(x_ref, att_ref, hy_ref, pl_ref, gate_ref, w_ref, g_ref, b_ref, o_ref, *, alpha):
    mix = jnp.dot(att_ref[...], w_ref[0:D_ATTN, :], preferred_element_type=F32)
    mix += jnp.dot(hy_ref[...].astype(BF16), w_ref[D_ATTN:D_ATTN + D_HYENA, :], preferred_element_type=F32)
    mix += jnp.dot(pl_ref[...].astype(BF16), w_ref[D_ATTN + D_HYENA:, :], preferred_element_type=F32)
    y = alpha * x_ref[...] + gate_ref[0] * mix
    o_ref[...] = _layernorm(y, LN_EPS) * g_ref[...] + b_ref[...]


def _outproj_call(x2, att2, hy2, pl2, mod3, mod_row, w_out_bf, g, b, alpha, tm):
    r, d = x2.shape
    row = lambda w: pl.BlockSpec((tm, w), lambda i: (i, 0))
    vec = pl.BlockSpec((1, d), lambda i: (0, 0))
    return pl.pallas_call(
        functools.partial(_outproj_kernel, alpha=alpha),
        grid=(r // tm,),
        in_specs=[row(d), row(D_ATTN), row(D_HYENA), row(D_POOL),
                  pl.BlockSpec((1, 1, d), lambda i: (mod_row(i) * 6 + 2, 0, 0)),
                  pl.BlockSpec(w_out_bf.shape, lambda i: (0, 0)),
                  vec, vec],
        out_specs=row(d),
        out_shape=jax.ShapeDtypeStruct((r, d), F32),
        compiler_params=_cparams(),
        name="out_proj_deepnorm",
    )(x2, att2, hy2, pl2, mod3, w_out_bf, g.reshape(1, d), b.reshape(1, d))


def _ffn_kernel(*refs, alpha, n_exp, n_chunk):
    if n_exp > 1:
        (x_ref, sh_ref, sc_ref, gate_ref, rw_ref, wg_ref, wu_ref, wd_ref, g_ref, b_ref,
         o_ref, u_ref, acc_ref, comb_ref) = refs
    else:
        (x_ref, sh_ref, sc_ref, gate_ref, wg_ref, wu_ref, wd_ref, g_ref, b_ref,
         o_ref, u_ref, acc_ref) = refs
    j = pl.program_id(1)

    @pl.when(j == 0)
    def _():
        u = (_layernorm(x_ref[...], ADALN_EPS) * (1.0 + sc_ref[0]) + sh_ref[0]).astype(BF16)
        u_ref[...] = u
        acc_ref[...] = jnp.zeros_like(acc_ref)
        if n_exp > 1:
            logits = jnp.dot(u, rw_ref[...].astype(BF16), preferred_element_type=F32)
            lane = lax.broadcasted_iota(jnp.int32, logits.shape, 1)
            m1 = jnp.max(logits, axis=-1, keepdims=True)
            i1 = jnp.min(jnp.where(logits == m1, lane, n_exp), axis=-1, keepdims=True)
            rest = jnp.where(lane == i1, -jnp.inf, logits)
            m2 = jnp.max(rest, axis=-1, keepdims=True)
            i2 = jnp.min(jnp.where(rest == m2, lane, n_exp), axis=-1, keepdims=True)
            e2 = jnp.exp(m2 - m1)
            den = 1.0 + e2
            comb_ref[...] = jnp.where(lane == i1, 1.0 / den, 0.0) + jnp.where(lane == i2, e2 / den, 0.0)

    u = u_ref[...]
    a = jnp.dot(u, wg_ref[0].astype(BF16), preferred_element_type=F32)
    bb = jnp.dot(u, wu_ref[0].astype(BF16), preferred_element_type=F32)
    h = a * jax.nn.sigmoid(a) * bb
    if n_exp > 1:
        comb = comb_ref[...]
        lane = lax.broadcasted_iota(jnp.int32, comb.shape, 1)
        h = h * jnp.sum(jnp.where(lane == j // n_chunk, comb, 0.0), axis=-1, keepdims=True)
    acc_ref[...] += jnp.dot(h.astype(BF16), wd_ref[0].astype(BF16), preferred_element_type=F32)

    @pl.when(j == n_exp * n_chunk - 1)
    def _():
        y = alpha * x_ref[...] + gate_ref[0] * acc_ref[...]
        o_ref[...] = _layernorm(y, LN_EPS) * g_ref[...] + b_ref[...]


def _ffn_call(x2, mod3, mod_row, router_w, w_gate, w_up, w_down, w_base, n_exp, g, b, alpha, tm, fc):
    r, d = x2.shape
    dff = w_gate.shape[2]
    n_chunk = dff // fc
    steps = n_exp * n_chunk
    vec = pl.BlockSpec((1, d), lambda i, j: (0, 0))
    mod = lambda which: pl.BlockSpec((1, 1, d), lambda i, j: (mod_row(i) * 6 + which, 0, 0))
    in_specs = [pl.BlockSpec((tm, d), lambda i, j: (i, 0)), mod(3), mod(4), mod(5)]
    args = [x2, mod3, mod3, mod3]
    scratch = [pltpu.VMEM((tm, d), BF16), pltpu.VMEM((tm, d), F32)]
    if n_exp > 1:
        in_specs.append(pl.BlockSpec(router_w.shape, lambda i, j: (0, 0)))
        args.append(router_w)
        scratch.append(pltpu.VMEM((tm, n_exp), F32))
    in_specs += [pl.BlockSpec((1, d, fc), lambda i, j: (w_base + j // n_chunk, 0, j % n_chunk)),
                 pl.BlockSpec((1, d, fc), lambda i, j: (w_base + j // n_chunk, 0, j % n_chunk)),
                 pl.BlockSpec((1, fc, d), lambda i, j: (w_base + j // n_chunk, j % n_chunk, 0)),
                 vec, vec]
    args += [w_gate, w_up, w_down, g.reshape(1, d), b.reshape(1, d)]
    return pl.pallas_call(
        functools.partial(_ffn_kernel, alpha=alpha, n_exp=n_exp, n_chunk=n_chunk),
        grid=(r // tm, steps),
        in_specs=in_specs,
        out_specs=pl.BlockSpec((tm, d), lambda i, j: (i, 0)),
        out_shape=jax.ShapeDtypeStruct((r, d), F32),
        scratch_shapes=scratch,
        compiler_params=_cparams(),
        name="channel_mixer",
    )(*args)


MOE_WINDOW = 1024
MOE_TILE = 128
MOE_SUPER = 8
MOE_CTILES = 3
MOE_GCHUNK = 512


def _router_kernel(x_ref, sh_ref, sc_ref, rw_ref, u_ref, comb_ref):
    u = (_layernorm(x_ref[...], ADALN_EPS) * (1.0 + sc_ref[0]) + sh_ref[0]).astype(BF16)
    u_ref[...] = u
    logits = jnp.dot(u, rw_ref[...].astype(BF16), preferred_element_type=F32)
    n_exp = logits.shape[1]
    lane = lax.broadcasted_iota(jnp.int32, logits.shape, 1)
    m1 = jnp.max(logits, axis=-1, keepdims=True)
    i1 = jnp.min(jnp.where(logits == m1, lane, n_exp), axis=-1, keepdims=True)
    rest = jnp.where(lane == i1, -jnp.inf, logits)
    m2 = jnp.max(rest, axis=-1, keepdims=True)
    i2 = jnp.min(jnp.where(rest == m2, lane, n_exp), axis=-1, keepdims=True)
    e2 = jnp.exp(m2 - m1)
    den = 1.0 + e2
    comb_ref[...] = jnp.where(lane == i1, 1.0 / den, 0.0) + jnp.where(lane == i2, e2 / den, 0.0)


def _router_call(x2, mod3, mod_row, router_w, tm):
    r, d = x2.shape
    n_exp = router_w.shape[1]
    mod = lambda which: pl.BlockSpec((1, 1, d), lambda i: (mod_row(i) * 6 + which, 0, 0))
    return pl.pallas_call(
        _router_kernel,
        grid=(r // tm,),
        in_specs=[pl.BlockSpec((tm, d), lambda i: (i, 0)), mod(3), mod(4),
                  pl.BlockSpec(router_w.shape, lambda i: (0, 0))],
        out_specs=[pl.BlockSpec((tm, d), lambda i: (i, 0)), pl.BlockSpec((tm, n_exp), lambda i: (i, 0))],
        out_shape=[jax.ShapeDtypeStruct((r, d), BF16), jax.ShapeDtypeStruct((r, n_exp), F32)],
        compiler_params=_cparams(),
        name="moe_router",
    )(x2, mod3, mod3, router_w)


def _moe_plan(comb, n_tok):
    n_exp = comb.shape[1]
    win, tile = MOE_WINDOW, MOE_TILE
    n_win = n_tok // win
    cap_tiles = 2 * win // tile + n_exp
    i32 = jnp.int32
    mask = (comb > 0.0).reshape(n_win, win, n_exp)
    pos = jnp.cumsum(mask.astype(i32), axis=1) - 1
    cnt = jnp.sum(mask.astype(i32), axis=1)
    ntile = (cnt + tile - 1) // tile
    woff = jnp.cumsum(ntile, axis=1) - ntile
    rows = jnp.where(mask, tile * woff[:, None, :] + pos, -1)
    rowsel = jnp.sort(rows, axis=-1)[..., -TOP_K:].transpose(0, 2, 1)
    win_rows = tile * jnp.sum(ntile, axis=1)

    etiles = jnp.sum(ntile, axis=0)
    nsup = (etiles + MOE_SUPER - 1) // MOE_SUPER
    ebase = MOE_SUPER * (jnp.cumsum(nsup) - nsup)
    erank = ebase[None, :] + jnp.cumsum(ntile, axis=0) - ntile
    n_sup = -(-(n_win * cap_tiles) // MOE_SUPER) + n_exp
    n_rank = n_sup * MOE_SUPER
    slots = win // tile
    s = jnp.arange(slots, dtype=i32)[None, None, :]
    ok = s < ntile[:, :, None]
    rank = jnp.where(ok, erank[:, :, None] + s, n_rank)
    wm_tile = (jnp.arange(n_win, dtype=i32) * cap_tiles)[:, None, None] + woff[:, :, None] + s
    src = jnp.zeros((n_rank + 1,), i32).at[rank.reshape(-1)].set(wm_tile.reshape(-1))[:n_rank]
    sup_total = jnp.sum(nsup)
    g = jnp.arange(n_sup, dtype=i32)
    sup_valid = (g < sup_total).astype(i32)
    sup_exp = jnp.clip(jnp.searchsorted(jnp.cumsum(nsup), g, side="right").astype(i32), 0, n_exp - 1)
    sup_exp = jnp.where(sup_valid > 0, sup_exp, sup_exp[jnp.maximum(sup_total - 1, 0)])

    nblk = (ntile + MOE_CTILES - 1) // MOE_CTILES
    flat_nblk = nblk.reshape(-1)
    start = jnp.cumsum(flat_nblk) - flat_nblk
    n_steps = n_win * (-(-cap_tiles // MOE_CTILES) + n_exp)
    total_steps = jnp.sum(flat_nblk)
    sidx = jnp.arange(n_steps, dtype=i32)
    owner = jnp.searchsorted(jnp.cumsum(flat_nblk), sidx, side="right").astype(i32)
    owner = jnp.clip(owner, 0, n_win * n_exp - 1)
    c_valid = (sidx < total_steps).astype(i32)
    last_owner = owner[jnp.maximum(total_steps - 1, 0)]
    owner = jnp.where(c_valid > 0, owner, last_owner)
    c_w = owner // n_exp
    c_e = owner % n_exp
    c_b = jnp.where(c_valid > 0, sidx - start[owner], 0)
    c_t0 = erank.reshape(-1)[owner] + MOE_CTILES * c_b
    prev_w = jnp.concatenate([jnp.full((1,), -1, i32), c_w[:-1]])
    next_w = jnp.concatenate([c_w[1:], jnp.full((1,), -1, i32)])
    next_valid = jnp.concatenate([c_valid[1:], jnp.zeros((1,), i32)])
    c_first = ((c_w != prev_w) & (c_valid > 0)).astype(i32)
    c_last = (((c_w != next_w) | (next_valid == 0)) & (c_valid > 0)).astype(i32)
    return dict(n_win=n_win, cap_tiles=cap_tiles, n_sup=n_sup, n_rank=n_rank, n_steps=n_steps,
                rowsel=rowsel, win_rows=win_rows.astype(i32), pos=pos.reshape(n_tok, n_exp).astype(F32),
                src=src, sup_exp=sup_exp.astype(i32), sup_valid=sup_valid,
                c_w=c_w.astype(i32), c_e=c_e.astype(i32), c_b=c_b.astype(i32), c_t0=c_t0.astype(i32),
                c_valid=c_valid, c_first=c_first, c_last=c_last)


def _dispatch_kernel(nrows_ref, u_ref, sel_ref, o_ref):
    w = pl.program_id(0)
    r1 = sel_ref[0:1, :]
    r2 = sel_ref[1:2, :]
    for cidx in range(o_ref.shape[0] // MOE_GCHUNK):
        base = cidx * MOE_GCHUNK
        rows = pl.ds(base, MOE_GCHUNK)

        @pl.when(base < nrows_ref[w])
        def _():
            rid = lax.broadcasted_iota(jnp.int32, (MOE_GCHUNK, r1.shape[1]), 0) + base
            onehot = jnp.where(rid == r1, 1.0, jnp.where(rid == r2, 1.0, 0.0)).astype(BF16)
            o_ref[rows, :] = jnp.dot(onehot, u_ref[...], preferred_element_type=F32).astype(o_ref.dtype)

        @pl.when(base >= nrows_ref[w])
        def _():
            o_ref[rows, :] = jnp.zeros((MOE_GCHUNK, o_ref.shape[1]), o_ref.dtype)


def _dispatch_call(u, plan):
    n_tok, d = u.shape
    cap_rows = plan["cap_tiles"] * MOE_TILE
    n_win = plan["n_win"]
    return pl.pallas_call(
        _dispatch_kernel,
        grid_spec=pltpu.PrefetchScalarGridSpec(
            num_scalar_prefetch=1,
            grid=(n_win,),
            in_specs=[pl.BlockSpec((MOE_WINDOW, d), lambda w, n: (w, 0)),
                      pl.BlockSpec((None, TOP_K, MOE_WINDOW), lambda w, n: (w, 0, 0))],
            out_specs=pl.BlockSpec((cap_rows, d), lambda w, n: (w, 0))),
        out_shape=jax.ShapeDtypeStruct((n_win * cap_rows, d), BF16),
        compiler_params=_cparams(),
        name="moe_dispatch",
    )(plan["win_rows"], u, plan["rowsel"])


def _expert_kernel(src_ref, exp_ref, valid_ref, *refs, n_chunk):
    tiles = refs[:MOE_SUPER]
    wg_ref, wu_ref, wd_ref, o_ref, u_ref, acc_ref = refs[MOE_SUPER:]
    g = pl.program_id(0)
    j = pl.program_id(1)
    ok = valid_ref[g] > 0

    @pl.when(j == 0)
    def _():
        for k in range(MOE_SUPER):
            u_ref[k * MOE_TILE:(k + 1) * MOE_TILE, :] = tiles[k][...]
        acc_ref[...] = jnp.zeros_like(acc_ref)

    @pl.when(ok)
    def _():
        u = u_ref[...]
        a = jnp.dot(u, wg_ref[0].astype(BF16), preferred_element_type=F32)
        b = jnp.dot(u, wu_ref[0].astype(BF16), preferred_element_type=F32)
        h = a * jax.nn.sigmoid(a) * b
        acc_ref[...] += jnp.dot(h.astype(BF16), wd_ref[0].astype(BF16), preferred_element_type=F32)

    @pl.when(j == n_chunk - 1)
    def _():
        o_ref[...] = acc_ref[...].astype(o_ref.dtype)


def _expert_call(xw, plan, w_gate, w_up, w_down, layer_base, fc):
    d = xw.shape[1]
    dff = w_gate.shape[2]
    n_chunk = dff // fc
    rows = MOE_SUPER * MOE_TILE

    def tile_spec(k):
        return pl.BlockSpec((MOE_TILE, d), lambda g, j, src, ex, va: (src[g * MOE_SUPER + k], 0))

    def chunk(j, va, g):
        return jnp.where(va[g] > 0, j, n_chunk - 1)

    in_specs = [tile_spec(k) for k in range(MOE_SUPER)] + [
        pl.BlockSpec((1, d, fc), lambda g, j, src, ex, va: (layer_base + ex[g], 0, chunk(j, va, g))),
        pl.BlockSpec((1, d, fc), lambda g, j, src, ex, va: (layer_base + ex[g], 0, chunk(j, va, g))),
        pl.BlockSpec((1, fc, d), lambda g, j, src, ex, va: (layer_base + ex[g], chunk(j, va, g), 0))]
    return pl.pallas_call(
        functools.partial(_expert_kernel, n_chunk=n_chunk),
        grid_spec=pltpu.PrefetchScalarGridSpec(
            num_scalar_prefetch=3,
            grid=(plan["n_sup"], n_chunk),
            in_specs=in_specs,
            out_specs=pl.BlockSpec((rows, d), lambda g, j, src, ex, va: (g, 0)),
            scratch_shapes=[pltpu.VMEM((rows, d), BF16), pltpu.VMEM((rows, d), F32)]),
        out_shape=jax.ShapeDtypeStruct((plan["n_sup"] * rows, d), BF16),
        compiler_params=_cparams(),
        name="moe_experts",
    )(plan["src"], plan["sup_exp"], plan["sup_valid"], *([xw] * MOE_SUPER), w_gate, w_up, w_down)


def _combine_kernel(cw_ref, ce_ref, cb_ref, ct_ref, cv_ref, cf_ref, cl_ref,
                    t0_ref, t1_ref, t2_ref, pos_ref, comb_ref, x_ref, gate_ref, g_ref, b_ref, o_ref, *, alpha):
    s = pl.program_id(0)
    blk_rows = MOE_CTILES * MOE_TILE

    @pl.when(cf_ref[s] > 0)
    def _():
        o_ref[...] = jnp.zeros_like(o_ref)

    @pl.when(cv_ref[s] > 0)
    def _():
        comb = comb_ref[...]
        lane = lax.broadcasted_iota(jnp.int32, comb.shape, 1)
        pick = lane == ce_ref[s]
        gcol = jnp.sum(jnp.where(pick, comb, 0.0), axis=-1, keepdims=True)
        pcol = jnp.sum(jnp.where(pick, pos_ref[...], 0.0), axis=-1, keepdims=True)
        rel = pcol.astype(jnp.int32) - cb_ref[s] * blk_rows
        col = lax.broadcasted_iota(jnp.int32, (comb.shape[0], blk_rows), 1)
        q = jnp.where(rel == col, gcol, 0.0).astype(BF16)
        rows = jnp.concatenate([t0_ref[...], t1_ref[...], t2_ref[...]], axis=0)
        o_ref[...] += jnp.dot(q, rows, preferred_element_type=F32)

    @pl.when(cl_ref[s] > 0)
    def _():
        y = alpha * x_ref[...] + gate_ref[0] * o_ref[...]
        o_ref[...] = _layernorm(y, LN_EPS) * g_ref[...] + b_ref[...]


def _combine_call(ys, plan, comb, x2, mod3, mod_row, g, b, alpha):
    n_tok, d = x2.shape
    n_exp = comb.shape[1]
    last_tile = plan["n_rank"] - 1

    def tile_spec(k):
        return pl.BlockSpec((MOE_TILE, d),
                            lambda s, cw, ce, cb, ct, cv, cf, cl: (jnp.minimum(ct[s] + k, last_tile), 0))

    win = lambda width: pl.BlockSpec((MOE_WINDOW, width), lambda s, cw, *_: (cw[s], 0))
    vec = pl.BlockSpec((1, d), lambda s, *_: (0, 0))
    return pl.pallas_call(
        functools.partial(_combine_kernel, alpha=alpha),
        grid_spec=pltpu.PrefetchScalarGridSpec(
            num_scalar_prefetch=7,
            grid=(plan["n_steps"],),
            in_specs=[tile_spec(0), tile_spec(1), tile_spec(2), win(n_exp), win(n_exp), win(d),
                      pl.BlockSpec((1, 1, d), lambda s, cw, *_: (mod_row(cw[s]) * 6 + 5, 0, 0)),
                      vec, vec],
            out_specs=win(d)),
        out_shape=jax.ShapeDtypeStruct((n_tok, d), F32),
        compiler_params=_cparams(),
        name="moe_combine_deepnorm",
    )(plan["c_w"], plan["c_e"], plan["c_b"], plan["c_t0"], plan["c_valid"], plan["c_first"], plan["c_last"],
      ys, ys, ys, plan["pos"], comb, x2, mod3, g.reshape(1, d), b.reshape(1, d))


def _moe_call(x2, mod3, tok_row, router_w, w_gate, w_up, w_down, layer_base, g, b, alpha, fc):
    n_tok = x2.shape[0]
    tm = MOE_WINDOW
    u, comb = _router_call(x2, mod3, lambda i: tok_row(i * tm), router_w, tm)
    plan = _moe_plan(comb, n_tok)
    xw = _dispatch_call(u, plan)
    ys = _expert_call(xw, plan, w_gate, w_up, w_down, layer_base, fc)
    return _combine_call(ys, plan, comb, x2, mod3, lambda w: tok_row(w * MOE_WINDOW), g, b, alpha)


def _pick_tile(n, pref):
    t = min(n, pref)
    while n % t:
        t //= 2
    return t


def kernel(x, c, ctx, c_ctx, w_mod, b_mod, w_in, q_gain, k_gain, hy_conv_w, hy_conv_b, hy_f_w1, hy_f_b1, hy_f_freq, hy_f_w2, hy_f_b2, hy_f_w3, hy_d, pool_w, pool_scale, w_out, ln1_g, ln1_b, ln2_g, ln2_b, ffn_w_gate, ffn_w_up, ffn_w_down, router_w, moe_w_gate, moe_w_up, moe_w_down):
    bsz, seq, d = x.shape
    clen = ctx.shape[1]
    depth = w_mod.shape[0]
    alpha = (2.0 * depth) ** 0.25
    assert seq % MOE_WINDOW == 0 and bsz % 2 == 0
    n_exp = router_w.shape[2]
    moe_wg = moe_w_gate.reshape((-1,) + moe_w_gate.shape[2:])
    moe_wu = moe_w_up.reshape((-1,) + moe_w_up.shape[2:])
    moe_wd = moe_w_down.reshape((-1,) + moe_w_down.shape[2:])

    n_rows = -(-(bsz + 1) // 8) * 8
    cc = jnp.zeros((n_rows, d), F32).at[:bsz].set(c).at[bsz].set(c_ctx)
    mod = _mod_call(cc, w_mod, b_mod)

    tm_lat = _pick_tile(seq, 512)
    tm_ctx = _pick_tile(clen, 512)
    tm_ffn = _pick_tile(seq, 1024)
    tm_ffn_ctx = _pick_tile(bsz * clen, 1024)
    lat_row = lambda tm: (lambda i: i // (seq // tm))
    ctx_row = lambda i: bsz
    rope_lat = tuple(jnp.asarray(t) for t in _rope_tables(seq, True))
    rope_ctx = tuple(jnp.asarray(t) for t in _rope_tables(clen, False))
    head_avg = np.kron(np.eye(QK_W // HEAD_DIM), np.full((HEAD_DIM, HEAD_DIM), 1.0 / HEAD_DIM))
    head_avg = jnp.asarray(head_avg, BF16)

    xl = x.reshape(bsz * seq, d)
    xc = ctx.reshape(bsz * clen, d)
    for l in range(depth):
        last = l == depth - 1
        mod3 = mod[l].reshape(n_rows * 6, 1, d)
        w_in_bf = w_in[l].astype(BF16)
        w_out_bf = w_out[l].astype(BF16)
        gain = jnp.concatenate([jnp.tile(q_gain[l] * ATTN_SCALE, N_Q_HEADS),
                                jnp.tile(k_gain[l], N_KV_HEADS)]).reshape(1, QK_W)
        filt_w = (hy_f_w1[l], hy_f_b1[l], hy_f_freq[l], hy_f_w2[l], hy_f_b2[l], hy_f_w3[l])
        j = l // 2
        routed = l % 2 == 1
        if routed:
            ffn_w = (router_w[j], moe_wg, moe_wu, moe_wd, j * n_exp, n_exp)
        else:
            ffn_w = (None, ffn_w_gate, ffn_w_up, ffn_w_down, j, 1)
        fc = 256

        qk_c, v_c, hp_c = _proj_call(xc, mod3, ctx_row, clen, w_in_bf, gain, head_avg, rope_ctx, tm_ctx)
        qk_c3 = qk_c.reshape(bsz, clen, QK_W)
        k_c3 = qk_c3[:, :, D_ATTN:]
        v_c3 = v_c.reshape(bsz, clen, KV_W)
        if not last:
            hp_c3 = hp_c.reshape(bsz, clen, -1)
            att_c = _attn_call(qk_c3, k_c3, v_c3, _pick_tile(clen, 256))
            spec_c = _ctx_spectrum(_filter_call(clen, *filt_w), clen)
            hy_c = _ctx_hyena_call(hp_c3, hy_conv_w[l], hy_conv_b[l], spec_c, hy_d[l])
            pl_c = _pool_call(hp_c3, pool_w[l], pool_scale[l])
            xc_mid = _outproj_call(xc, att_c.reshape(bsz * clen, D_ATTN), hy_c.reshape(bsz * clen, D_HYENA),
                                   pl_c.reshape(bsz * clen, D_POOL), mod3, ctx_row, w_out_bf,
                                   ln1_g[l], ln1_b[l], alpha, tm_ctx)

        qk_l, v_l, hp_l = _proj_call(xl, mod3, lat_row(tm_lat), seq, w_in_bf, gain, head_avg, rope_lat, tm_lat)
        qk_l3 = qk_l.reshape(bsz, seq, QK_W)
        hp_l3 = hp_l.reshape(bsz, seq, -1)
        keys = jnp.concatenate([qk_l3[:, :, D_ATTN:], k_c3], axis=1)
        vals = jnp.concatenate([v_l.reshape(bsz, seq, KV_W), v_c3], axis=1)
        att = _attn_call(qk_l3, keys, vals, _pick_tile(seq, 256))
        spec = _hyena_spectrum(_filter_call(seq, *filt_w), seq)
        hyo = _hyena_latent(hp_l3, hy_conv_w[l], hy_conv_b[l], spec, hy_d[l])
        plo = _pool_call(hp_l3, pool_w[l], pool_scale[l])
        xl = _outproj_call(xl, att.reshape(bsz * seq, D_ATTN), hyo.reshape(bsz * seq, D_HYENA),
                           plo.reshape(bsz * seq, D_POOL), mod3, lat_row(tm_lat), w_out_bf,
                           ln1_g[l], ln1_b[l], alpha, tm_lat)
        if routed:
            xl = _moe_call(xl, mod3, lambda t: t // seq, router_w[j], moe_wg, moe_wu, moe_wd, j * n_exp,
                           ln2_g[l], ln2_b[l], alpha, fc)
        else:
            xl = _ffn_call(xl, mod3, lat_row(tm_ffn), *ffn_w, ln2_g[l], ln2_b[l], alpha, tm_ffn, fc)

        if not last:
            xc = _ffn_call(xc_mid, mod3, ctx_row, *ffn_w, ln2_g[l], ln2_b[l], alpha, tm_ffn_ctx, fc)
    return xl.reshape(bsz, seq, d)
```

```python
import functools
import math

import numpy as np
import jax
import jax.numpy as jnp
from jax import lax
from jax.experimental import pallas as pl
from jax.experimental.pallas import tpu as pltpu

F32 = jnp.float32
BF16 = jnp.bfloat16
HIGHEST = lax.Precision.HIGHEST

GRID_W = 64
HEAD_DIM = 64
N_Q_HEADS = 8
N_KV_HEADS = 2
GQA_GROUP = N_Q_HEADS // N_KV_HEADS
D_ATTN = N_Q_HEADS * HEAD_DIM
KV_W = N_KV_HEADS * HEAD_DIM
QK_W = D_ATTN + KV_W
ATTN_SCALE = HEAD_DIM ** -0.5
ROPE_THETA = 10000.0
ROPE_AXIS_DIM = HEAD_DIM // 2
QK_EPS = 1e-6
D_HYENA = 256
HYENA_ORDER = 2
HYENA_BANDS = 16
HYENA_EMB = 1 + 2 * HYENA_BANDS
HYENA_DECAY_TARGET = 1e-2
HYENA_FAST_DECAY_PCT = 0.3
HYENA_SLOW_DECAY_PCT = 1.5
D_POOL = 256
POOL_WINDOWS = (2, 4, 8, 16)
POOL_GROUP = D_POOL // len(POOL_WINDOWS)
POOL_PAD = 8
TOP_K = 2
LN_EPS = 1e-5
ADALN_EPS = 1e-6
LOG2E = math.log2(math.e)

LANES = 128
FFT_N2 = 64
FFT_UNROLL = 4
VMEM_LIMIT = 56 * 1024 * 1024


def _cparams():
    return pltpu.CompilerParams(vmem_limit_bytes=VMEM_LIMIT)


def _layernorm(x, eps):
    mu = jnp.mean(x, axis=-1, keepdims=True)
    xc = x - mu
    var = jnp.mean(xc * xc, axis=-1, keepdims=True)
    return xc * lax.rsqrt(var + eps)


def _mod_kernel(c_ref, w_ref, b_ref, o_ref):
    c = c_ref[...]
    s = c * jax.nn.sigmoid(c)
    o_ref[0] = jnp.dot(s, w_ref[0], preferred_element_type=F32, precision=HIGHEST) + b_ref[0]


def _mod_call(cc, w_mod, b_mod):
    depth, d, d6 = w_mod.shape
    r = cc.shape[0]
    tn = 1536
    return pl.pallas_call(
        _mod_kernel,
        grid=(depth, d6 // tn),
        in_specs=[pl.BlockSpec((r, d), lambda l, j: (0, 0)),
                  pl.BlockSpec((1, d, tn), lambda l, j: (l, 0, j)),
                  pl.BlockSpec((1, 1, tn), lambda l, j: (l, 0, j))],
        out_specs=pl.BlockSpec((1, r, tn), lambda l, j: (l, 0, j)),
        out_shape=jax.ShapeDtypeStruct((depth, r, d6), F32),
        compiler_params=_cparams(),
        name="adaln_mod",
    )(cc, w_mod, b_mod.reshape(depth, 1, d6))


def _rope_tables(seq_len, rope):
    lane = np.arange(LANES)
    d = lane % HEAD_DIM
    if not rope:
        one = np.ones((seq_len, LANES), np.float32)
        zero = np.zeros((seq_len, LANES), np.float32)
        return one, zero, zero
    t = np.arange(seq_len)
    row = (t // GRID_W).astype(np.float64)
    col = (t % GRID_W).astype(np.float64)
    half = ROPE_AXIS_DIM // 2
    inv = ROPE_THETA ** (-np.arange(0, ROPE_AXIS_DIM, 2, dtype=np.float64) / ROPE_AXIS_DIM)
    pos = np.where((d // ROPE_AXIS_DIM)[None, :] == 0, row[:, None], col[:, None])
    ang = pos * inv[d % half][None, :]
    is_b = ((d % ROPE_AXIS_DIM) >= half)[None, :]
    cos = np.cos(ang)
    sin = np.sin(ang)
    s_up = np.where(is_b, sin, 0.0)
    s_dn = np.where(is_b, 0.0, -sin)
    return cos.astype(np.float32), s_up.astype(np.float32), s_dn.astype(np.float32)


def _store_padded_heads(blk, ext, out_ref, col0):
    lo = lax.broadcasted_iota(jnp.int32, blk.shape, 1) < HEAD_DIM
    out_ref[:, col0:col0 + LANES] = jnp.where(lo, blk, ext).astype(out_ref.dtype)
    out_ref[:, col0 + LANES:col0 + 2 * LANES] = jnp.where(lo, pltpu.roll(blk, HEAD_DIM, axis=1),
                                                          ext).astype(out_ref.dtype)


def _proj_kernel(x_ref, sh_ref, sc_ref, w_ref, gain_ref, bd_ref, cos_ref, sup_ref, sdn_ref, ext_ref,
                 q_ref, k_ref, v_ref, hp_ref):
    u = _layernorm(x_ref[...], ADALN_EPS) * (1.0 + sc_ref[0]) + sh_ref[0]
    p = jnp.dot(u.astype(BF16), w_ref[...], preferred_element_type=F32)
    qk = p[:, :QK_W]
    ms = jnp.dot((qk * qk).astype(BF16), bd_ref[...], preferred_element_type=F32)
    qn = qk * lax.rsqrt(ms + QK_EPS) * gain_ref[...]
    cos = cos_ref[...]
    sup = sup_ref[...]
    sdn = sdn_ref[...]
    n_qblk = D_ATTN // LANES
    for j in range(QK_W // LANES):
        blk = qn[:, j * LANES:(j + 1) * LANES]
        up = pltpu.roll(blk, ROPE_AXIS_DIM // 2, axis=1)
        dn = pltpu.roll(blk, LANES - ROPE_AXIS_DIM // 2, axis=1)
        rot = blk * cos + up * sup + dn * sdn
        if j < n_qblk:
            _store_padded_heads(rot, ext_ref[0:1, :], q_ref, 2 * j * LANES)
        else:
            _store_padded_heads(rot, ext_ref[1:2, :], k_ref, 2 * (j - n_qblk) * LANES)
    _store_padded_heads(p[:, QK_W:QK_W + KV_W], ext_ref[2:3, :], v_ref, 0)
    hp_ref[...] = p[:, QK_W + KV_W:]


def _proj_call(x2, mod3, mod_row, seq_len, w_in_bf, gain, bd, tables, ext, tm):
    r, d = x2.shape
    d_in = w_in_bf.shape[1]
    d_hp = d_in - QK_W - KV_W
    nseq = seq_len // tm
    cos, sup, sdn = tables
    tab_spec = pl.BlockSpec((tm, LANES), lambda i: (i % nseq, 0))
    return pl.pallas_call(
        _proj_kernel,
        grid=(r // tm,),
        in_specs=[pl.BlockSpec((tm, d), lambda i: (i, 0)),
                  pl.BlockSpec((1, 1, d), lambda i: (mod_row(i) * 6 + 0, 0, 0)),
                  pl.BlockSpec((1, 1, d), lambda i: (mod_row(i) * 6 + 1, 0, 0)),
                  pl.BlockSpec((d, d_in), lambda i: (0, 0)),
                  pl.BlockSpec((1, QK_W), lambda i: (0, 0)),
                  pl.BlockSpec((QK_W, QK_W), lambda i: (0, 0)),
                  tab_spec, tab_spec, tab_spec,
                  pl.BlockSpec((3, LANES), lambda i: (0, 0))],
        out_specs=[pl.BlockSpec((tm, N_Q_HEADS * LANES), lambda i: (i, 0)),
                   pl.BlockSpec((tm, N_KV_HEADS * LANES), lambda i: (i, 0)),
                   pl.BlockSpec((tm, N_KV_HEADS * LANES), lambda i: (i, 0)),
                   pl.BlockSpec((tm, d_hp), lambda i: (i, 0))],
        out_shape=[jax.ShapeDtypeStruct((r, N_Q_HEADS * LANES), BF16),
                   jax.ShapeDtypeStruct((r, N_KV_HEADS * LANES), BF16),
                   jax.ShapeDtypeStruct((r, N_KV_HEADS * LANES), BF16),
                   jax.ShapeDtypeStruct((r, d_hp), F32)],
        compiler_params=_cparams(),
        name="ln_mod_in_proj",
    )(x2, mod3, mod3, w_in_bf, gain, bd, cos, sup, sdn, ext)


ATTN_KEY_CHUNK = 1024
ATTN_SAFE_SHIFT = 60.0


def _attn_kernel(fast_ref, q_ref, k_ref, v_ref, o_ref):
    tq = q_ref.shape[0]
    sk = k_ref.shape[0]

    def run(running_max):
        for h in range(N_KV_HEADS):
            qs = jnp.concatenate([q_ref[:, (h * GQA_GROUP + g) * LANES:(h * GQA_GROUP + g + 1) * LANES]
                                  for g in range(GQA_GROUP)], axis=0)
            acc = jnp.zeros((GQA_GROUP * tq, LANES), F32)
            m = jnp.full((GQA_GROUP * tq, 1), -jnp.inf, F32)
            for c0 in range(0, sk, ATTN_KEY_CHUNK):
                kw = min(ATTN_KEY_CHUNK, sk - c0)
                kc = k_ref[c0:c0 + kw, h * LANES:(h + 1) * LANES]
                vc = v_ref[c0:c0 + kw, h * LANES:(h + 1) * LANES]
                s = lax.dot_general(qs, kc, (((1,), (1,)), ((), ())), preferred_element_type=F32)
                if running_max:
                    m_new = jnp.maximum(m, jnp.max(s, axis=-1, keepdims=True))
                    acc = acc * jnp.exp2(m - m_new)
                    s = s - m_new
                    m = m_new
                acc = acc + jnp.dot(jnp.exp2(s).astype(BF16), vc, preferred_element_type=F32)
            o = acc[:, :HEAD_DIM] / acc[:, HEAD_DIM:HEAD_DIM + 1]
            for g in range(GQA_GROUP):
                c0 = (h * GQA_GROUP + g) * HEAD_DIM
                o_ref[:, c0:c0 + HEAD_DIM] = o[g * tq:(g + 1) * tq].astype(o_ref.dtype)

    @pl.when(fast_ref[0] > 0)
    def _():
        run(False)

    @pl.when(fast_ref[0] == 0)
    def _():
        run(True)


def _attn_call(fast, q, keys, vals, tq):
    b, lq, qw = q.shape
    lk, kw = keys.shape[1:]
    return pl.pallas_call(
        _attn_kernel,
        grid_spec=pltpu.PrefetchScalarGridSpec(
            num_scalar_prefetch=1,
            grid=(b, lq // tq),
            in_specs=[pl.BlockSpec((None, tq, qw), lambda i, j, f: (i, j, 0)),
                      pl.BlockSpec((None, lk, kw), lambda i, j, f: (i, 0, 0)),
                      pl.BlockSpec((None, lk, kw), lambda i, j, f: (i, 0, 0))],
            out_specs=pl.BlockSpec((None, tq, D_ATTN), lambda i, j, f: (i, j, 0))),
        out_shape=jax.ShapeDtypeStruct((b, lq, D_ATTN), BF16),
        compiler_params=_cparams(),
        name="attention",
    )(fast, q, keys, vals)


def _filter_consts(seq_len):
    t = np.linspace(0.0, 1.0, seq_len, dtype=np.float32).astype(np.float64)[:, None]
    omega = 2.0 * math.pi * np.arange(seq_len, dtype=np.float64)[:, None] / seq_len
    bands = np.linspace(1e-4, HYENA_BANDS - 1, HYENA_BANDS, dtype=np.float32).astype(np.float64)[None, :]
    feats = np.concatenate([t, np.cos(omega * bands), -np.sin(omega * bands)], axis=-1)
    feats = np.pad(feats, ((0, 0), (0, LANES - HYENA_EMB)))
    max_decay = math.log(HYENA_DECAY_TARGET) / HYENA_FAST_DECAY_PCT
    min_decay = math.log(HYENA_DECAY_TARGET) / HYENA_SLOW_DECAY_PCT
    deltas = np.linspace(min_decay, max_decay, D_HYENA, dtype=np.float32).astype(np.float64)
    decay = np.exp(-t * np.abs(deltas)[None, :])
    decay = np.concatenate([decay, decay], axis=1)
    return feats.astype(np.float32), decay.astype(np.float32)


def _filter_kernel(feat_ref, w1_ref, b1_ref, fr_ref, w2_ref, b2_ref, w3_ref, dec_ref, o_ref):
    fr = fr_ref[...]
    h = jnp.sin(fr * (jnp.dot(feat_ref[...], w1_ref[...], preferred_element_type=F32, precision=HIGHEST)
                      + b1_ref[...]))
    h = jnp.sin(fr * (jnp.dot(h, w2_ref[...], preferred_element_type=F32, precision=HIGHEST) + b2_ref[...]))
    h = jnp.dot(h, w3_ref[...], preferred_element_type=F32, precision=HIGHEST) * dec_ref[...]
    tot = jnp.sum(jnp.abs(h), axis=0, keepdims=True)
    tot = tot[:, :D_HYENA] + tot[:, D_HYENA:]
    inv = 1.0 / tot
    o_ref[...] = h * jnp.concatenate([inv, inv], axis=1)


def _filter_call(seq_len, w1, b1, freq, w2, b2, w3):
    feats, decay = _filter_consts(seq_len)
    hid = w2.shape[0]
    w1p = jnp.pad(w1, ((0, LANES - HYENA_EMB), (0, 0)))
    blk = 2 * D_HYENA
    full = lambda shape: pl.BlockSpec(shape, lambda o: (0,) * len(shape))
    return pl.pallas_call(
        _filter_kernel,
        grid=(HYENA_ORDER,),
        in_specs=[full((seq_len, LANES)), full((LANES, hid)), full((1, hid)), full((1, hid)),
                  full((hid, hid)), full((1, hid)),
                  pl.BlockSpec((hid, blk), lambda o: (0, o)),
                  full((seq_len, blk))],
        out_specs=pl.BlockSpec((seq_len, blk), lambda o: (0, o)),
        out_shape=jax.ShapeDtypeStruct((seq_len, HYENA_ORDER * blk), F32),
        compiler_params=_cparams(),
        name="hyena_filter",
    )(jnp.asarray(feats), w1p, b1.reshape(1, hid), freq.reshape(1, hid), w2, b2.reshape(1, hid), w3,
      jnp.asarray(decay))


def _filter_buffer(filt, seq_len):
    f = filt.reshape(seq_len, HYENA_ORDER, 2, D_HYENA)
    fwd, bwd = f[:, :, 0], f[:, :, 1]
    buf = jnp.concatenate([fwd, jnp.zeros((1,) + fwd.shape[1:], F32), bwd[:0:-1]], axis=0)
    return buf.reshape(2 * seq_len, HYENA_ORDER * D_HYENA)


@functools.lru_cache(maxsize=None)
def _fft_consts(seq_len):
    n = 2 * seq_len
    n2 = FFT_N2
    n1 = n // n2
    h1 = n1 // 2
    k1 = np.arange(n1)
    w1 = np.exp(-2j * np.pi * np.outer(k1, np.arange(n1)) / n1)
    f1 = np.zeros((n1, 2, 2 * h1))
    f1[:, 0, :h1], f1[:, 0, h1:] = w1[:, :h1].real, -w1[:, :h1].imag
    f1[:, 1, :h1], f1[:, 1, h1:] = w1[:, :h1].imag, w1[:, :h1].real
    f1 = f1.reshape(2 * n1, 2 * h1)
    f1r = np.stack([w1.real, w1.imag], axis=1).reshape(2 * n1, n1)
    a2 = np.arange(n2)
    tw = np.exp(-2j * np.pi * np.outer(k1, a2) / n)
    w2 = np.exp(-2j * np.pi * np.outer(a2, a2) / n2)
    m = w2[None, :, :] * tw[:, None, :]
    ff = np.concatenate([np.concatenate([m.real, -m.imag], axis=2),
                         np.concatenate([m.imag, m.real], axis=2)], axis=1)
    g = np.conj(w2.T)[None, :, :] * np.conj(tw)[:, :, None]
    gi = np.concatenate([np.concatenate([g.real, -g.imag], axis=2),
                         np.concatenate([g.imag, g.real], axis=2)], axis=1)
    v = np.conj(w1[:, :h1]).T
    f1i = np.zeros((2, h1, n1, 2))
    f1i[0, :, :, 0], f1i[0, :, :, 1] = v.real, -v.imag
    f1i[1, :, :, 0], f1i[1, :, :, 1] = v.imag, v.real
    f1i = f1i.reshape(2 * h1, 2 * n1)
    il = np.arange(2 * n2).reshape(2, n2).T.reshape(-1)
    ff_il = ff[:, :, il]
    gi_il = gi[:, il, :]
    f32 = lambda a: np.asarray(a, np.float32)
    return dict(n=n, n1=n1, h1=h1, f1=f32(f1), f1r=f32(f1r), ff=f32(ff), ff_il=f32(ff_il), gi_il=f32(gi_il),
                f1i=f32(f1i))


def _fft1_kernel(x_ref, f_ref, o_ref, *, exact):
    if exact:
        y = jnp.dot(f_ref[...], x_ref[...], preferred_element_type=F32, precision=HIGHEST)
    else:
        y = jnp.dot(f_ref[...], x_ref[...].astype(BF16), preferred_element_type=F32)
    o_ref[...] = y.astype(o_ref.dtype)


def _fft1_call(x3, f, out_dtype, exact, tc):
    g, k, cols = x3.shape
    m = f.shape[0]
    return pl.pallas_call(
        functools.partial(_fft1_kernel, exact=exact),
        grid=(g, cols // tc),
        in_specs=[pl.BlockSpec((None, k, tc), lambda i, j: (i, 0, j)),
                  pl.BlockSpec((m, k), lambda i, j: (0, 0))],
        out_specs=pl.BlockSpec((None, m, tc), lambda i, j: (i, 0, j)),
        out_shape=jax.ShapeDtypeStruct((g, m, cols), out_dtype),
        compiler_params=_cparams(),
        name="fft_outer_dft",
    )(x3, f)


def _spec_kernel(a_ref, ff_ref, o_ref, *, scale):
    for kk in range(a_ref.shape[0]):
        o_ref[kk] = jnp.dot(ff_ref[kk], a_ref[kk], preferred_element_type=F32, precision=HIGHEST) * scale


def _spec_call(a3, ff, scale, kt):
    n1, r, cols = a3.shape
    return pl.pallas_call(
        functools.partial(_spec_kernel, scale=scale),
        grid=(n1 // kt,),
        in_specs=[pl.BlockSpec((kt, r, cols), lambda i: (i, 0, 0)),
                  pl.BlockSpec((kt, r, r), lambda i: (i, 0, 0))],
        out_specs=pl.BlockSpec((kt, r, cols), lambda i: (i, 0, 0)),
        out_shape=jax.ShapeDtypeStruct((n1, r, cols), F32),
        compiler_params=_cparams(),
        name="fft_filter_spectrum",
    )(a3, ff)


def _ffta_kernel(z_ref, f_ref, o_ref):
    n1 = f_ref.shape[0] // 2
    h1 = f_ref.shape[1] // 2

    def body(n2, carry):
        rows = pl.ds(n2, h1, stride=FFT_N2)
        x = jnp.concatenate([z_ref[0, rows, :], z_ref[1, rows, :]], axis=0).astype(BF16)
        a = jnp.dot(f_ref[...], x, preferred_element_type=F32).astype(BF16)
        o_ref[pl.ds(n2, n1, stride=FFT_N2), :] = pltpu.bitcast(a, jnp.uint32)
        return carry

    lax.fori_loop(0, FFT_N2, body, 0, unroll=FFT_UNROLL)


def _ffta_call(z4, base, groups, f1_bf):
    _, _, l, c = z4.shape
    halves = c // LANES
    n1 = f1_bf.shape[0] // 2
    return pl.pallas_call(
        _ffta_kernel,
        grid=(groups, halves),
        in_specs=[pl.BlockSpec((None, 2, l, LANES), lambda i, j: (base + i, 0, 0, j)),
                  pl.BlockSpec(f1_bf.shape, lambda i, j: (0, 0))],
        out_specs=pl.BlockSpec((None, None, n1 * FFT_N2, LANES), lambda i, j: (i, j, 0, 0)),
        out_shape=jax.ShapeDtypeStruct((groups, halves, n1 * FFT_N2, LANES), jnp.uint32),
        compiler_params=_cparams(),
        name="fft_outer_dft",
    )(z4, f1_bf)


def _fftb_kernel(a_ref, ff_ref, gi_ref, h_ref, o_ref):
    groups, halves, kt = a_ref.shape[:3]
    half = FFT_N2

    def body(kk, carry):
        ff = ff_ref[kk]
        gi = gi_ref[kk]
        hr = h_ref[kk, :half, :]
        hi = h_ref[kk, half:, :]
        for g in range(groups):
            a = jnp.concatenate([pltpu.bitcast(a_ref[g, hh, kk], BF16) for hh in range(halves)], axis=1)
            xh = jnp.dot(ff, a, preferred_element_type=F32)
            xr, xi = xh[:half], xh[half:]
            y = jnp.concatenate([xr * hr - xi * hi, xr * hi + xi * hr], axis=0).astype(BF16)
            b = jnp.dot(gi, y, preferred_element_type=F32).astype(BF16)
            for hh in range(halves):
                o_ref[g, hh, kk] = pltpu.bitcast(b[:, hh * LANES:(hh + 1) * LANES], jnp.uint32)
        return carry

    lax.fori_loop(0, kt, body, 0)


def _fftb_call(a5, ff_bf, gi_bf, spec, order, kt):
    g, halves, n1, n2, _ = a5.shape
    r = 2 * n2
    return pl.pallas_call(
        _fftb_kernel,
        grid=(n1 // kt,),
        in_specs=[pl.BlockSpec((g, halves, kt, n2, LANES), lambda i: (0, 0, i, 0, 0)),
                  pl.BlockSpec((kt, r, r), lambda i: (i, 0, 0)),
                  pl.BlockSpec((kt, r, r), lambda i: (i, 0, 0)),
                  pl.BlockSpec((kt, r, halves * LANES), lambda i: (i, 0, order))],
        out_specs=pl.BlockSpec((g, halves, kt, n2, LANES), lambda i: (0, 0, i, 0, 0)),
        out_shape=jax.ShapeDtypeStruct(a5.shape, jnp.uint32),
        compiler_params=_cparams(),
        name="fft_inner_conv",
    )(a5, ff_bf, gi_bf, spec)


def _fftc_kernel(b_ref, f_ref, z_ref, g_ref, d_ref, o_ref):
    n1 = f_ref.shape[1] // 2
    h1 = f_ref.shape[0] // 2
    d = d_ref[...]

    def body(n2, carry):
        b = pltpu.bitcast(b_ref[pl.ds(n2, n1, stride=FFT_N2), :], BF16)
        y = jnp.dot(f_ref[...], b, preferred_element_type=F32)
        rows = pl.ds(n2, h1, stride=FFT_N2)
        for m in range(2):
            o_ref[m, rows, :] = g_ref[m, rows, :] * (y[m * h1:(m + 1) * h1] + d * z_ref[m, rows, :])
        return carry

    lax.fori_loop(0, FFT_N2, body, 0, unroll=FFT_UNROLL)


def _fftc_call(b4, f1i_bf, z4, z_base, g4, g_base, d2, order):
    groups, halves, rows, _ = b4.shape
    _, _, l, c = z4.shape
    pair = lambda base: pl.BlockSpec((None, 2, l, LANES), lambda i, j: (base + i, 0, 0, j))
    return pl.pallas_call(
        _fftc_kernel,
        grid=(groups, halves),
        in_specs=[pl.BlockSpec((None, None, rows, LANES), lambda i, j: (i, j, 0, 0)),
                  pl.BlockSpec(f1i_bf.shape, lambda i, j: (0, 0)),
                  pair(z_base), pair(g_base),
                  pl.BlockSpec((None, 1, LANES), lambda i, j: (order, 0, j))],
        out_specs=pair(0),
        out_shape=jax.ShapeDtypeStruct((groups, 2, l, c), F32),
        compiler_params=_cparams(),
        name="fft_outer_inverse_gate",
    )(b4, f1i_bf, z4, g4, d2.reshape(d2.shape[0], 1, d2.shape[1]))


def _sconv_kernel(z_ref, w_ref, b_ref, o_ref):
    z = z_ref[...]
    n = z.shape[0]
    row = lax.broadcasted_iota(jnp.int32, z.shape, 0)
    prev = jnp.where(row == 0, 0.0, pltpu.roll(z, 1, axis=0))
    nxt = jnp.where(row == n - 1, 0.0, pltpu.roll(z, n - 1, axis=0))
    o_ref[...] = prev * w_ref[0:1, :] + z * w_ref[1:2, :] + nxt * w_ref[2:3, :] + b_ref[...]


def _sconv_call(hp3, conv_w, conv_b):
    b, l, _ = hp3.shape
    per = D_HYENA // LANES
    nblk = 3 * per
    return pl.pallas_call(
        _sconv_kernel,
        grid=(b, nblk),
        in_specs=[pl.BlockSpec((None, l, LANES), lambda i, j: (i, 0, j)),
                  pl.BlockSpec((3, LANES), lambda i, j: (0, j)),
                  pl.BlockSpec((1, LANES), lambda i, j: (0, j))],
        out_specs=pl.BlockSpec((None, None, l, LANES), lambda i, j: (j // per, i, 0, j % per)),
        out_shape=jax.ShapeDtypeStruct((3, b, l, D_HYENA), F32),
        compiler_params=_cparams(),
        name="hyena_short_conv",
    )(hp3, conv_w, conv_b.reshape(1, -1))


def _hyena_spectrum(filt, seq_len):
    cst = _fft_consts(seq_len)
    n, n1 = cst["n"], cst["n1"]
    buf = _filter_buffer(filt, seq_len)
    cols = FFT_N2 * buf.shape[1]
    a = _fft1_call(buf.reshape(1, n1, cols), jnp.asarray(cst["f1r"]), F32, True, min(cols, 2048))
    a3 = a.reshape(n1, 2 * FFT_N2, buf.shape[1])
    return _spec_call(a3, jnp.asarray(cst["ff"]), 1.0 / n, min(n1, 8))


def _hyena_latent(hp3, conv_w, conv_b, spec, d):
    b, l, _ = hp3.shape
    cst = _fft_consts(l)
    n1, h1 = cst["n1"], cst["h1"]
    groups = b // 2
    halves = D_HYENA // LANES
    kt = min(n1, 8)
    f1 = jnp.asarray(cst["f1"], BF16)
    ff = jnp.asarray(cst["ff_il"], BF16)
    gi = jnp.asarray(cst["gi_il"], BF16)
    f1i = jnp.asarray(cst["f1i"], BF16)
    zc = _sconv_call(hp3, conv_w, conv_b).reshape(3 * groups, 2, l, D_HYENA)
    z, z_base = zc, 0
    for o in range(HYENA_ORDER):
        a = _ffta_call(z, z_base, groups, f1)
        bb = _fftb_call(a.reshape(groups, halves, n1, FFT_N2, LANES), ff, gi, spec, o, kt)
        z = _fftc_call(bb.reshape(groups, halves, n1 * FFT_N2, LANES), f1i, z, z_base, zc, (1 + o) * groups, d, o)
        z_base = 0
    return z.reshape(b, l, D_HYENA)


@functools.lru_cache(maxsize=None)
def _dft_consts(seq_len):
    n = 2 * seq_len
    w = np.exp(-2j * np.pi * np.outer(np.arange(n), np.arange(n)) / n)
    fwd_full = np.concatenate([w.real, w.imag], axis=0)
    fwd = fwd_full[:, :seq_len]
    wi = np.conj(w[:, :seq_len]).T
    inv = np.concatenate([wi.real, -wi.imag], axis=1)
    f32 = lambda a: np.asarray(a, np.float32)
    return dict(n=n, fwd_full=f32(fwd_full), fwd=f32(fwd), inv=f32(inv))


def _cspec_kernel(buf_ref, f_ref, o_ref, *, scale):
    o_ref[...] = jnp.dot(f_ref[...], buf_ref[...], preferred_element_type=F32, precision=HIGHEST) * scale


def _ctx_spectrum(filt, seq_len):
    cst = _dft_consts(seq_len)
    buf = _filter_buffer(filt, seq_len)
    n = cst["n"]
    return pl.pallas_call(
        functools.partial(_cspec_kernel, scale=1.0 / n),
        out_shape=jax.ShapeDtypeStruct((2 * n, buf.shape[1]), F32),
        compiler_params=_cparams(),
        name="ctx_filter_spectrum",
    )(buf, jnp.asarray(cst["fwd_full"]))


def _ctx_hyena_kernel(z_ref, w_ref, b_ref, f_ref, fi_ref, h_ref, d_ref, o_ref):
    z = z_ref[...]
    l = z.shape[0]
    n = 2 * l
    row = lax.broadcasted_iota(jnp.int32, z.shape, 0)
    prev = jnp.where(row == 0, 0.0, pltpu.roll(z, 1, axis=0))
    nxt = jnp.where(row == l - 1, 0.0, pltpu.roll(z, l - 1, axis=0))
    zc = prev * w_ref[0:1, :] + z * w_ref[1:2, :] + nxt * w_ref[2:3, :] + b_ref[...]
    cur = zc[:, :D_HYENA]
    for o in range(HYENA_ORDER):
        gate = zc[:, (1 + o) * D_HYENA:(2 + o) * D_HYENA]
        xh = jnp.dot(f_ref[...], cur.astype(BF16), preferred_element_type=F32)
        xr, xi = xh[:n], xh[n:]
        hr = h_ref[:n, o * D_HYENA:(o + 1) * D_HYENA]
        hi = h_ref[n:, o * D_HYENA:(o + 1) * D_HYENA]
        y = jnp.concatenate([xr * hr - xi * hi, xr * hi + xi * hr], axis=0).astype(BF16)
        conv = jnp.dot(fi_ref[...], y, preferred_element_type=F32)
        cur = gate * (conv + d_ref[o:o + 1, :] * cur)
    o_ref[...] = cur


def _ctx_hyena_call(hp3, conv_w, conv_b, spec, d):
    b, l, _ = hp3.shape
    cst = _dft_consts(l)
    n = cst["n"]
    w3 = 3 * D_HYENA
    full = lambda shape: pl.BlockSpec(shape, lambda i: (0,) * len(shape))
    return pl.pallas_call(
        _ctx_hyena_kernel,
        grid=(b,),
        in_specs=[pl.BlockSpec((None, l, w3), lambda i: (i, 0, 0)),
                  full((3, w3)), full((1, w3)), full((2 * n, l)), full((l, 2 * n)),
                  full((2 * n, HYENA_ORDER * D_HYENA)), full((HYENA_ORDER, D_HYENA))],
        out_specs=pl.BlockSpec((None, l, D_HYENA), lambda i: (i, 0, 0)),
        out_shape=jax.ShapeDtypeStruct((b, l, D_HYENA), F32),
        compiler_params=_cparams(),
        name="ctx_hyena",
    )(hp3, conv_w, conv_b.reshape(1, -1), jnp.asarray(cst["fwd"], BF16), jnp.asarray(cst["inv"], BF16),
      spec, d)


def _pool_kernel(p_ref, w_ref, sc_ref, o_ref):
    x = p_ref[...]
    l, c = x.shape
    half = pl.program_id(1)
    zpad = jnp.zeros((POOL_PAD, c), F32)
    xp = jnp.concatenate([zpad, x, zpad], axis=0)
    tot = l + 2 * POOL_PAD
    sums = []
    f = xp
    for win in POOL_WINDOWS:
        f = f + pltpu.roll(f, tot - win // 2, axis=0)
        sums.append(pltpu.roll(f, win // 2, axis=0)[POOL_PAD:POOL_PAD + l])
    t = lax.broadcasted_iota(jnp.int32, (l, c), 0)
    grp = lax.broadcasted_iota(jnp.int32, (l, c), 1) // POOL_GROUP + half * (LANES // POOL_GROUP)
    total = sums[-1]
    cnt = None
    for gi in range(len(POOL_WINDOWS) - 1, -1, -1):
        win = POOL_WINDOWS[gi]
        cw = (jnp.minimum(t + (win - win // 2), l) - jnp.maximum(t - win // 2, 0)).astype(F32)
        if cnt is None:
            cnt = cw
        else:
            sel = grp == gi
            total = jnp.where(sel, sums[gi], total)
            cnt = jnp.where(sel, cw, cnt)
    dlt = total / cnt - x
    y = jnp.dot(dlt.astype(BF16), w_ref[0], preferred_element_type=F32)
    o_ref[...] = y * sc_ref[...]


def _pool_call(hp3, pool_w, pool_scale):
    b, l, w = hp3.shape
    nh = D_POOL // LANES
    first = (w - D_POOL) // LANES
    per = LANES // POOL_GROUP
    wbd = jnp.zeros((nh, LANES, LANES), F32)
    for g in range(len(POOL_WINDOWS)):
        r0 = (g % per) * POOL_GROUP
        wbd = wbd.at[g // per, r0:r0 + POOL_GROUP, r0:r0 + POOL_GROUP].set(pool_w[g])
    return pl.pallas_call(
        _pool_kernel,
        grid=(b, nh),
        in_specs=[pl.BlockSpec((None, l, LANES), lambda i, j: (i, 0, first + j)),
                  pl.BlockSpec((1, LANES, LANES), lambda i, j: (j, 0, 0)),
                  pl.BlockSpec((1, LANES), lambda i, j: (0, j))],
        out_specs=pl.BlockSpec((None, l, LANES), lambda i, j: (i, 0, j)),
        out_shape=jax.ShapeDtypeStruct((b, l, D_POOL), F32),
        compiler_params=_cparams(),
        name="pool_mixer",
    )(hp3, wbd.astype(BF16), pool_scale.reshape(1, D_POOL))


def _outproj_kernel(x_ref, att_ref, hy_ref, pl_ref, gate_ref, w_ref, g_ref, b_ref, o_ref, *, alpha):
    mix = jnp.dot(att_ref[...], w_ref[0:D_ATTN, :], preferred_element_type=F32)
    mix += jnp.dot(hy_ref[...].astype(BF16), w_ref[D_ATTN:D_ATTN + D_HYENA, :], preferred_element_type=F32)
    mix += jnp.dot(pl_ref[...].astype(BF16), w_ref[D_ATTN + D_HYENA:, :], preferred_element_type=F32)
    y = alpha * x_ref[...] + gate_ref[0] * mix
    o_ref[...] = _layernorm(y, LN_EPS) * g_ref[...] + b_ref[...]


def _outproj_call(x2, att2, hy2, pl2, mod3, mod_row, w_out_bf, g, b, alpha, tm):
    r, d = x2.shape
    row = lambda w: pl.BlockSpec((tm, w), lambda i: (i, 0))
    vec = pl.BlockSpec((1, d), lambda i: (0, 0))
    return pl.pallas_call(
        functools.partial(_outproj_kernel, alpha=alpha),
        grid=(r // tm,),
        in_specs=[row(d), row(D_ATTN), row(D_HYENA), row(D_POOL),
                  pl.BlockSpec((1, 1, d), lambda i: (mod_row(i) * 6 + 2, 0, 0)),
                  pl.BlockSpec(w_out_bf.shape, lambda i: (0, 0)),
                  vec, vec],
        out_specs=row(d),
        out_shape=jax.ShapeDtypeStruct((r, d), F32),
        compiler_params=_cparams(),
        name="out_proj_deepnorm",
    )(x2, att2, hy2, pl2, mod3, w_out_bf, g.reshape(1, d), b.reshape(1, d))


def _ffn_kernel(*refs, alpha, n_exp, n_chunk):
    if n_exp > 1:
        (x_ref, sh_ref, sc_ref, gate_ref, rw_ref, wg_ref, wu_ref, wd_ref, g_ref, b_ref,
         o_ref, u_ref, acc_ref, comb_ref) = refs
    else:
        (x_ref, sh_ref, sc_ref, gate_ref, wg_ref, wu_ref, wd_ref, g_ref, b_ref,
         o_ref, u_ref, acc_ref) = refs
    j = pl.program_id(1)

    @pl.when(j == 0)
    def _():
        u = (_layernorm(x_ref[...], ADALN_EPS) * (1.0 + sc_ref[0]) + sh_ref[0]).astype(BF16)
        u_ref[...] = u
        acc_ref[...] = jnp.zeros_like(acc_ref)
        if n_exp > 1:
            logits = jnp.dot(u, rw_ref[...].astype(BF16), preferred_element_type=F32)
            lane = lax.broadcasted_iota(jnp.int32, logits.shape, 1)
            m1 = jnp.max(logits, axis=-1, keepdims=True)
            i1 = jnp.min(jnp.where(logits == m1, lane, n_exp), axis=-1, keepdims=True)
            rest = jnp.where(lane == i1, -jnp.inf, logits)
            m2 = jnp.max(rest, axis=-1, keepdims=True)
            i2 = jnp.min(jnp.where(rest == m2, lane, n_exp), axis=-1, keepdims=True)
            e2 = jnp.exp(m2 - m1)
            den = 1.0 + e2
            comb_ref[...] = jnp.where(lane == i1, 1.0 / den, 0.0) + jnp.where(lane == i2, e2 / den, 0.0)

    u = u_ref[...]
    a = jnp.dot(u, wg_ref[0].astype(BF16), preferred_element_type=F32)
    bb = jnp.dot(u, wu_ref[0].astype(BF16), preferred_element_type=F32)
    h = a * jax.nn.sigmoid(a) * bb
    if n_exp > 1:
        comb = comb_ref[...]
        lane = lax.broadcasted_iota(jnp.int32, comb.shape, 1)
        h = h * jnp.sum(jnp.where(lane == j // n_chunk, comb, 0.0), axis=-1, keepdims=True)
    acc_ref[...] += jnp.dot(h.astype(BF16), wd_ref[0].astype(BF16), preferred_element_type=F32)

    @pl.when(j == n_exp * n_chunk - 1)
    def _():
        y = alpha * x_ref[...] + gate_ref[0] * acc_ref[...]
        o_ref[...] = _layernorm(y, LN_EPS) * g_ref[...] + b_ref[...]


def _ffn_call(x2, mod3, mod_row, router_w, w_gate, w_up, w_down, w_base, n_exp, g, b, alpha, tm, fc):
    r, d = x2.shape
    dff = w_gate.shape[2]
    n_chunk = dff // fc
    steps = n_exp * n_chunk
    vec = pl.BlockSpec((1, d), lambda i, j: (0, 0))
    mod = lambda which: pl.BlockSpec((1, 1, d), lambda i, j: (mod_row(i) * 6 + which, 0, 0))
    in_specs = [pl.BlockSpec((tm, d), lambda i, j: (i, 0)), mod(3), mod(4), mod(5)]
    args = [x2, mod3, mod3, mod3]
    scratch = [pltpu.VMEM((tm, d), BF16), pltpu.VMEM((tm, d), F32)]
    if n_exp > 1:
        in_specs.append(pl.BlockSpec(router_w.shape, lambda i, j: (0, 0)))
        args.append(router_w)
        scratch.append(pltpu.VMEM((tm, n_exp), F32))
    in_specs += [pl.BlockSpec((1, d, fc), lambda i, j: (w_base + j // n_chunk, 0, j % n_chunk)),
                 pl.BlockSpec((1, d, fc), lambda i, j: (w_base + j // n_chunk, 0, j % n_chunk)),
                 pl.BlockSpec((1, fc, d), lambda i, j: (w_base + j // n_chunk, j % n_chunk, 0)),
                 vec, vec]
    args += [w_gate, w_up, w_down, g.reshape(1, d), b.reshape(1, d)]
    return pl.pallas_call(
        functools.partial(_ffn_kernel, alpha=alpha, n_exp=n_exp, n_chunk=n_chunk),
        grid=(r // tm, steps),
        in_specs=in_specs,
        out_specs=pl.BlockSpec((tm, d), lambda i, j: (i, 0)),
        out_shape=jax.ShapeDtypeStruct((r, d), F32),
        scratch_shapes=scratch,
        compiler_params=_cparams(),
        name="channel_mixer",
    )(*args)


MOE_WINDOW = 1024
MOE_TILE = 128
MOE_SUPER = 8
MOE_CTILES = 3
MOE_GCHUNK = 512


def _router_kernel(x_ref, sh_ref, sc_ref, rw_ref, u_ref, comb_ref):
    u = (_layernorm(x_ref[...], ADALN_EPS) * (1.0 + sc_ref[0]) + sh_ref[0]).astype(BF16)
    u_ref[...] = u
    logits = jnp.dot(u, rw_ref[...].astype(BF16), preferred_element_type=F32)
    n_exp = logits.shape[1]
    lane = lax.broadcasted_iota(jnp.int32, logits.shape, 1)
    m1 = jnp.max(logits, axis=-1, keepdims=True)
    i1 = jnp.min(jnp.where(logits == m1, lane, n_exp), axis=-1, keepdims=True)
    rest = jnp.where(lane == i1, -jnp.inf, logits)
    m2 = jnp.max(rest, axis=-1, keepdims=True)
    i2 = jnp.min(jnp.where(rest == m2, lane, n_exp), axis=-1, keepdims=True)
    e2 = jnp.exp(m2 - m1)
    den = 1.0 + e2
    comb_ref[...] = jnp.where(lane == i1, 1.0 / den, 0.0) + jnp.where(lane == i2, e2 / den, 0.0)


def _router_call(x2, mod3, mod_row, router_w, tm):
    r, d = x2.shape
    n_exp = router_w.shape[1]
    mod = lambda which: pl.BlockSpec((1, 1, d), lambda i: (mod_row(i) * 6 + which, 0, 0))
    return pl.pallas_call(
        _router_kernel,
        grid=(r // tm,),
        in_specs=[pl.BlockSpec((tm, d), lambda i: (i, 0)), mod(3), mod(4),
                  pl.BlockSpec(router_w.shape, lambda i: (0, 0))],
        out_specs=[pl.BlockSpec((tm, d), lambda i: (i, 0)), pl.BlockSpec((tm, n_exp), lambda i: (i, 0))],
        out_shape=[jax.ShapeDtypeStruct((r, d), BF16), jax.ShapeDtypeStruct((r, n_exp), F32)],
        compiler_params=_cparams(),
        name="moe_router",
    )(x2, mod3, mod3, router_w)


def _moe_plan(comb, n_tok):
    n_exp = comb.shape[1]
    win, tile = MOE_WINDOW, MOE_TILE
    n_win = n_tok // win
    cap_tiles = 2 * win // tile + n_exp
    i32 = jnp.int32
    mask = (comb > 0.0).reshape(n_win, win, n_exp)
    pos = jnp.cumsum(mask.astype(i32), axis=1) - 1
    cnt = jnp.sum(mask.astype(i32), axis=1)
    ntile = (cnt + tile - 1) // tile
    woff = jnp.cumsum(ntile, axis=1) - ntile
    rows = jnp.where(mask, tile * woff[:, None, :] + pos, -1)
    rowsel = jnp.sort(rows, axis=-1)[..., -TOP_K:].transpose(0, 2, 1)
    win_rows = tile * jnp.sum(ntile, axis=1)

    etiles = jnp.sum(ntile, axis=0)
    nsup = (etiles + MOE_SUPER - 1) // MOE_SUPER
    ebase = MOE_SUPER * (jnp.cumsum(nsup) - nsup)
    erank = ebase[None, :] + jnp.cumsum(ntile, axis=0) - ntile
    n_sup = -(-(n_win * cap_tiles) // MOE_SUPER) + n_exp
    n_rank = n_sup * MOE_SUPER
    slots = win // tile
    s = jnp.arange(slots, dtype=i32)[None, None, :]
    ok = s < ntile[:, :, None]
    rank = jnp.where(ok, erank[:, :, None] + s, n_rank)
    wm_tile = (jnp.arange(n_win, dtype=i32) * cap_tiles)[:, None, None] + woff[:, :, None] + s
    src = jnp.zeros((n_rank + 1,), i32).at[rank.reshape(-1)].set(wm_tile.reshape(-1))[:n_rank]
    sup_total = jnp.sum(nsup)
    g = jnp.arange(n_sup, dtype=i32)
    sup_valid = (g < sup_total).astype(i32)
    sup_exp = jnp.clip(jnp.searchsorted(jnp.cumsum(nsup), g, side="right").astype(i32), 0, n_exp - 1)
    sup_exp = jnp.where(sup_valid > 0, sup_exp, sup_exp[jnp.maximum(sup_total - 1, 0)])

    nblk = (ntile + MOE_CTILES - 1) // MOE_CTILES
    flat_nblk = nblk.reshape(-1)
    start = jnp.cumsum(flat_nblk) - flat_nblk
    n_steps = n_win * (-(-cap_tiles // MOE_CTILES) + n_exp)
    total_steps = jnp.sum(flat_nblk)
    sidx = jnp.arange(n_steps, dtype=i32)
    owner = jnp.searchsorted(jnp.cumsum(flat_nblk), sidx, side="right").astype(i32)
    owner = jnp.clip(owner, 0, n_win * n_exp - 1)
    c_valid = (sidx < total_steps).astype(i32)
    last_owner = owner[jnp.maximum(total_steps - 1, 0)]
    owner = jnp.where(c_valid > 0, owner, last_owner)
    c_w = owner // n_exp
    c_e = owner % n_exp
    c_b = jnp.where(c_valid > 0, sidx - start[owner], 0)
    c_t0 = erank.reshape(-1)[owner] + MOE_CTILES * c_b
    prev_w = jnp.concatenate([jnp.full((1,), -1, i32), c_w[:-1]])
    next_w = jnp.concatenate([c_w[1:], jnp.full((1,), -1, i32)])
    next_valid = jnp.concatenate([c_valid[1:], jnp.zeros((1,), i32)])
    c_first = ((c_w != prev_w) & (c_valid > 0)).astype(i32)
    c_last = (((c_w != next_w) | (next_valid == 0)) & (c_valid > 0)).astype(i32)
    return dict(n_win=n_win, cap_tiles=cap_tiles, n_sup=n_sup, n_rank=n_rank, n_steps=n_steps,
                rowsel=rowsel, win_rows=win_rows.astype(i32), pos=pos.reshape(n_tok, n_exp).astype(F32),
                src=src, sup_exp=sup_exp.astype(i32), sup_valid=sup_valid,
                c_w=c_w.astype(i32), c_e=c_e.astype(i32), c_b=c_b.astype(i32), c_t0=c_t0.astype(i32),
                c_valid=c_valid, c_first=c_first, c_last=c_last)


def _dispatch_kernel(nrows_ref, u_ref, sel_ref, o_ref):
    w = pl.program_id(0)
    r1 = sel_ref[0:1, :]
    r2 = sel_ref[1:2, :]
    for cidx in range(o_ref.shape[0] // MOE_GCHUNK):
        base = cidx * MOE_GCHUNK
        rows = pl.ds(base, MOE_GCHUNK)

        @pl.when(base < nrows_ref[w])
        def _():
            rid = lax.broadcasted_iota(jnp.int32, (MOE_GCHUNK, r1.shape[1]), 0) + base
            onehot = jnp.where(rid == r1, 1.0, jnp.where(rid == r2, 1.0, 0.0)).astype(BF16)
            o_ref[rows, :] = jnp.dot(onehot, u_ref[...], preferred_element_type=F32).astype(o_ref.dtype)

        @pl.when(base >= nrows_ref[w])
        def _():
            o_ref[rows, :] = jnp.zeros((MOE_GCHUNK, o_ref.shape[1]), o_ref.dtype)


def _dispatch_call(u, plan):
    n_tok, d = u.shape
    cap_rows = plan["cap_tiles"] * MOE_TILE
    n_win = plan["n_win"]
    return pl.pallas_call(
        _dispatch_kernel,
        grid_spec=pltpu.PrefetchScalarGridSpec(
            num_scalar_prefetch=1,
            grid=(n_win,),
            in_specs=[pl.BlockSpec((MOE_WINDOW, d), lambda w, n: (w, 0)),
                      pl.BlockSpec((None, TOP_K, MOE_WINDOW), lambda w, n: (w, 0, 0))],
            out_specs=pl.BlockSpec((cap_rows, d), lambda w, n: (w, 0))),
        out_shape=jax.ShapeDtypeStruct((n_win * cap_rows, d), BF16),
        compiler_params=_cparams(),
        name="moe_dispatch",
    )(plan["win_rows"], u, plan["rowsel"])


def _expert_kernel(src_ref, exp_ref, valid_ref, *refs, n_chunk):
    tiles = refs[:MOE_SUPER]
    wg_ref, wu_ref, wd_ref, o_ref, u_ref, acc_ref = refs[MOE_SUPER:]
    g = pl.program_id(0)
    j = pl.program_id(1)
    ok = valid_ref[g] > 0

    @pl.when(j == 0)
    def _():
        for k in range(MOE_SUPER):
            u_ref[k * MOE_TILE:(k + 1) * MOE_TILE, :] = tiles[k][...]
        acc_ref[...] = jnp.zeros_like(acc_ref)

    @pl.when(ok)
    def _():
        u = u_ref[...]
        a = jnp.dot(u, wg_ref[0].astype(BF16), preferred_element_type=F32)
        b = jnp.dot(u, wu_ref[0].astype(BF16), preferred_element_type=F32)
        h = a * jax.nn.sigmoid(a) * b
        acc_ref[...] += jnp.dot(h.astype(BF16), wd_ref[0].astype(BF16), preferred_element_type=F32)

    @pl.when(j == n_chunk - 1)
    def _():
        o_ref[...] = acc_ref[...].astype(o_ref.dtype)


def _expert_call(xw, plan, w_gate, w_up, w_down, layer_base, fc):
    d = xw.shape[1]
    dff = w_gate.shape[2]
    n_chunk = dff // fc
    rows = MOE_SUPER * MOE_TILE

    def tile_spec(k):
        return pl.BlockSpec((MOE_TILE, d), lambda g, j, src, ex, va: (src[g * MOE_SUPER + k], 0))

    def chunk(j, va, g):
        return jnp.where(va[g] > 0, j, n_chunk - 1)

    in_specs = [tile_spec(k) for k in range(MOE_SUPER)] + [
        pl.BlockSpec((1, d, fc), lambda g, j, src, ex, va: (layer_base + ex[g], 0, chunk(j, va, g))),
        pl.BlockSpec((1, d, fc), lambda g, j, src, ex, va: (layer_base + ex[g], 0, chunk(j, va, g))),
        pl.BlockSpec((1, fc, d), lambda g, j, src, ex, va: (layer_base + ex[g], chunk(j, va, g), 0))]
    return pl.pallas_call(
        functools.partial(_expert_kernel, n_chunk=n_chunk),
        grid_spec=pltpu.PrefetchScalarGridSpec(
            num_scalar_prefetch=3,
            grid=(plan["n_sup"], n_chunk),
            in_specs=in_specs,
            out_specs=pl.BlockSpec((rows, d), lambda g, j, src, ex, va: (g, 0)),
            scratch_shapes=[pltpu.VMEM((rows, d), BF16), pltpu.VMEM((rows, d), F32)]),
        out_shape=jax.ShapeDtypeStruct((plan["n_sup"] * rows, d), BF16),
        compiler_params=_cparams(),
        name="moe_experts",
    )(plan["src"], plan["sup_exp"], plan["sup_valid"], *([xw] * MOE_SUPER), w_gate, w_up, w_down)


def _combine_kernel(cw_ref, ce_ref, cb_ref, ct_ref, cv_ref, cf_ref, cl_ref,
                    t0_ref, t1_ref, t2_ref, pos_ref, comb_ref, x_ref, gate_ref, g_ref, b_ref, o_ref, *, alpha):
    s = pl.program_id(0)
    blk_rows = MOE_CTILES * MOE_TILE

    @pl.when(cf_ref[s] > 0)
    def _():
        o_ref[...] = jnp.zeros_like(o_ref)

    @pl.when(cv_ref[s] > 0)
    def _():
        comb = comb_ref[...]
        lane = lax.broadcasted_iota(jnp.int32, comb.shape, 1)
        pick = lane == ce_ref[s]
        gcol = jnp.sum(jnp.where(pick, comb, 0.0), axis=-1, keepdims=True)
        pcol = jnp.sum(jnp.where(pick, pos_ref[...], 0.0), axis=-1, keepdims=True)
        rel = pcol.astype(jnp.int32) - cb_ref[s] * blk_rows
        col = lax.broadcasted_iota(jnp.int32, (comb.shape[0], blk_rows), 1)
        q = jnp.where(rel == col, gcol, 0.0).astype(BF16)
        rows = jnp.concatenate([t0_ref[...], t1_ref[...], t2_ref[...]], axis=0)
        o_ref[...] += jnp.dot(q, rows, preferred_element_type=F32)

    @pl.when(cl_ref[s] > 0)
    def _():
        y = alpha * x_ref[...] + gate_ref[0] * o_ref[...]
        o_ref[...] = _layernorm(y, LN_EPS) * g_ref[...] + b_ref[...]


def _combine_call(ys, plan, comb, x2, mod3, mod_row, g, b, alpha):
    n_tok, d = x2.shape
    n_exp = comb.shape[1]
    last_tile = plan["n_rank"] - 1

    def tile_spec(k):
        return pl.BlockSpec((MOE_TILE, d),
                            lambda s, cw, ce, cb, ct, cv, cf, cl: (jnp.minimum(ct[s] + k, last_tile), 0))

    win = lambda width: pl.BlockSpec((MOE_WINDOW, width), lambda s, cw, *_: (cw[s], 0))
    vec = pl.BlockSpec((1, d), lambda s, *_: (0, 0))
    return pl.pallas_call(
        functools.partial(_combine_kernel, alpha=alpha),
        grid_spec=pltpu.PrefetchScalarGridSpec(
            num_scalar_prefetch=7,
            grid=(plan["n_steps"],),
            in_specs=[tile_spec(0), tile_spec(1), tile_spec(2), win(n_exp), win(n_exp), win(d),
                      pl.BlockSpec((1, 1, d), lambda s, cw, *_: (mod_row(cw[s]) * 6 + 5, 0, 0)),
                      vec, vec],
            out_specs=win(d)),
        out_shape=jax.ShapeDtypeStruct((n_tok, d), F32),
        compiler_params=_cparams(),
        name="moe_combine_deepnorm",
    )(plan["c_w"], plan["c_e"], plan["c_b"], plan["c_t0"], plan["c_valid"], plan["c_first"], plan["c_last"],
      ys, ys, ys, plan["pos"], comb, x2, mod3, g.reshape(1, d), b.reshape(1, d))


def _moe_call(x2, mod3, tok_row, router_w, w_gate, w_up, w_down, layer_base, g, b, alpha, fc):
    n_tok = x2.shape[0]
    tm = MOE_WINDOW
    u, comb = _router_call(x2, mod3, lambda i: tok_row(i * tm), router_w, tm)
    plan = _moe_plan(comb, n_tok)
    xw = _dispatch_call(u, plan)
    ys = _expert_call(xw, plan, w_gate, w_up, w_down, layer_base, fc)
    return _combine_call(ys, plan, comb, x2, mod3, lambda w: tok_row(w * MOE_WINDOW), g, b, alpha)


def _pick_tile(n, pref):
    t = min(n, pref)
    while n % t:
        t //= 2
    return t


def kernel(x, c, ctx, c_ctx, w_mod, b_mod, w_in, q_gain, k_gain, hy_conv_w, hy_conv_b, hy_f_w1, hy_f_b1, hy_f_freq, hy_f_w2, hy_f_b2, hy_f_w3, hy_d, pool_w, pool_scale, w_out, ln1_g, ln1_b, ln2_g, ln2_b, ffn_w_gate, ffn_w_up, ffn_w_down, router_w, moe_w_gate, moe_w_up, moe_w_down):
    bsz, seq, d = x.shape
    clen = ctx.shape[1]
    depth = w_mod.shape[0]
    alpha = (2.0 * depth) ** 0.25
    assert seq % MOE_WINDOW == 0 and bsz % 2 == 0
    n_exp = router_w.shape[2]
    moe_wg = moe_w_gate.reshape((-1,) + moe_w_gate.shape[2:])
    moe_wu = moe_w_up.reshape((-1,) + moe_w_up.shape[2:])
    moe_wd = moe_w_down.reshape((-1,) + moe_w_down.shape[2:])

    n_rows = -(-(bsz + 1) // 8) * 8
    cc = jnp.zeros((n_rows, d), F32).at[:bsz].set(c).at[bsz].set(c_ctx)
    mod = _mod_call(cc, w_mod, b_mod)

    tm_lat = _pick_tile(seq, 512)
    tm_ctx = _pick_tile(clen, 512)
    tm_ffn = _pick_tile(seq, 1024)
    tm_ffn_ctx = _pick_tile(bsz * clen, 1024)
    lat_row = lambda tm: (lambda i: i // (seq // tm))
    ctx_row = lambda i: bsz
    rope_lat = tuple(jnp.asarray(t) for t in _rope_tables(seq, True))
    rope_ctx = tuple(jnp.asarray(t) for t in _rope_tables(clen, False))
    head_avg = np.kron(np.eye(QK_W // HEAD_DIM), np.full((HEAD_DIM, HEAD_DIM), 1.0 / HEAD_DIM))
    head_avg = jnp.asarray(head_avg, BF16)

    xl = x.reshape(bsz * seq, d)
    xc = ctx.reshape(bsz * clen, d)
    for l in range(depth):
        last = l == depth - 1
        mod3 = mod[l].reshape(n_rows * 6, 1, d)
        w_in_bf = w_in[l].astype(BF16)
        w_out_bf = w_out[l].astype(BF16)
        gain = jnp.concatenate([jnp.tile(q_gain[l] * (ATTN_SCALE * LOG2E), N_Q_HEADS),
                                jnp.tile(k_gain[l], N_KV_HEADS)]).reshape(1, QK_W)
        shift = (1.02 * HEAD_DIM * ATTN_SCALE * LOG2E) * jnp.max(jnp.abs(q_gain[l])) * jnp.max(jnp.abs(k_gain[l]))
        lane64 = (jnp.arange(LANES) == HEAD_DIM).astype(F32)
        ext = jnp.stack([lane64, -shift * lane64, lane64])
        fast = (shift <= ATTN_SAFE_SHIFT).astype(jnp.int32).reshape(1)
        filt_w = (hy_f_w1[l], hy_f_b1[l], hy_f_freq[l], hy_f_w2[l], hy_f_b2[l], hy_f_w3[l])
        j = l // 2
        routed = l % 2 == 1
        if routed:
            ffn_w = (router_w[j], moe_wg, moe_wu, moe_wd, j * n_exp, n_exp)
        else:
            ffn_w = (None, ffn_w_gate, ffn_w_up, ffn_w_down, j, 1)
        fc = 256

        q_c, k_c, v_c, hp_c = _proj_call(xc, mod3, ctx_row, clen, w_in_bf, gain, head_avg, rope_ctx, ext, tm_ctx)
        k_c3 = k_c.reshape(bsz, clen, -1)
        v_c3 = v_c.reshape(bsz, clen, -1)
        if not last:
            hp_c3 = hp_c.reshape(bsz, clen, -1)
            att_c = _attn_call(fast, q_c.reshape(bsz, clen, -1), k_c3, v_c3, _pick_tile(clen, 256))
            spec_c = _ctx_spectrum(_filter_call(clen, *filt_w), clen)
            hy_c = _ctx_hyena_call(hp_c3, hy_conv_w[l], hy_conv_b[l], spec_c, hy_d[l])
            pl_c = _pool_call(hp_c3, pool_w[l], pool_scale[l])
            xc_mid = _outproj_call(xc, att_c.reshape(bsz * clen, D_ATTN), hy_c.reshape(bsz * clen, D_HYENA),
                                   pl_c.reshape(bsz * clen, D_POOL), mod3, ctx_row, w_out_bf,
                                   ln1_g[l], ln1_b[l], alpha, tm_ctx)

        q_l, k_l, v_l, hp_l = _proj_call(xl, mod3, lat_row(tm_lat), seq, w_in_bf, gain, head_avg, rope_lat, ext,
                                         tm_lat)
        hp_l3 = hp_l.reshape(bsz, seq, -1)
        keys = jnp.concatenate([k_l.reshape(bsz, seq, -1), k_c3], axis=1)
        vals = jnp.concatenate([v_l.reshape(bsz, seq, -1), v_c3], axis=1)
        att = _attn_call(fast, q_l.reshape(bsz, seq, -1), keys, vals, _pick_tile(seq, 256))
        spec = _hyena_spectrum(_filter_call(seq, *filt_w), seq)
        hyo = _hyena_latent(hp_l3, hy_conv_w[l], hy_conv_b[l], spec, hy_d[l])
        plo = _pool_call(hp_l3, pool_w[l], pool_scale[l])
        xl = _outproj_call(xl, att.reshape(bsz * seq, D_ATTN), hyo.reshape(bsz * seq, D_HYENA),
                           plo.reshape(bsz * seq, D_POOL), mod3, lat_row(tm_lat), w_out_bf,
                           ln1_g[l], ln1_b[l], alpha, tm_lat)
        if routed:
            xl = _moe_call(xl, mod3, lambda t: t // seq, router_w[j], moe_wg, moe_wu, moe_wd, j * n_exp,
                           ln2_g[l], ln2_b[l], alpha, fc)
        else:
            xl = _ffn_call(xl, mod3, lat_row(tm_ffn), *ffn_w, ln2_g[l], ln2_b[l], alpha, tm_ffn, fc)

        if not last:
            xc = _ffn_call(xc_mid, mod3, ctx_row, *ffn_w, ln2_g[l], ln2_b[l], alpha, tm_ffn_ctx, fc)
    return xl.reshape(bsz, seq, d)
```

```python
import functools
import math

import numpy as np
import jax
import jax.numpy as jnp
from jax import lax
from jax.experimental import pallas as pl
from jax.experimental.pallas import tpu as pltpu

F32 = jnp.float32
BF16 = jnp.bfloat16
HIGHEST = lax.Precision.HIGHEST

GRID_W = 64
HEAD_DIM = 64
N_Q_HEADS = 8
N_KV_HEADS = 2
GQA_GROUP = N_Q_HEADS // N_KV_HEADS
D_ATTN = N_Q_HEADS * HEAD_DIM
KV_W = N_KV_HEADS * HEAD_DIM
QK_W = D_ATTN + KV_W
ATTN_SCALE = HEAD_DIM ** -0.5
ROPE_THETA = 10000.0
ROPE_AXIS_DIM = HEAD_DIM // 2
QK_EPS = 1e-6
D_HYENA = 256
HYENA_ORDER = 2
HYENA_BANDS = 16
HYENA_EMB = 1 + 2 * HYENA_BANDS
HYENA_DECAY_TARGET = 1e-2
HYENA_FAST_DECAY_PCT = 0.3
HYENA_SLOW_DECAY_PCT = 1.5
D_POOL = 256
POOL_WINDOWS = (2, 4, 8, 16)
POOL_GROUP = D_POOL // len(POOL_WINDOWS)
POOL_PAD = 8
TOP_K = 2
LN_EPS = 1e-5
ADALN_EPS = 1e-6
LOG2E = math.log2(math.e)

LANES = 128
FFT_N2 = 64
FFT_UNROLL = 4
VMEM_LIMIT = 56 * 1024 * 1024


def _cparams():
    return pltpu.CompilerParams(vmem_limit_bytes=VMEM_LIMIT)


def _layernorm(x, eps):
    mu = jnp.mean(x, axis=-1, keepdims=True)
    xc = x - mu
    var = jnp.mean(xc * xc, axis=-1, keepdims=True)
    return xc * lax.rsqrt(var + eps)


def _mod_kernel(c_ref, w_ref, b_ref, o_ref):
    c = c_ref[...]
    s = c * jax.nn.sigmoid(c)
    o_ref[0] = jnp.dot(s, w_ref[0], preferred_element_type=F32, precision=HIGHEST) + b_ref[0]


def _mod_call(cc, w_mod, b_mod):
    depth, d, d6 = w_mod.shape
    r = cc.shape[0]
    tn = 1536
    return pl.pallas_call(
        _mod_kernel,
        grid=(depth, d6 // tn),
        in_specs=[pl.BlockSpec((r, d), lambda l, j: (0, 0)),
                  pl.BlockSpec((1, d, tn), lambda l, j: (l, 0, j)),
                  pl.BlockSpec((1, 1, tn), lambda l, j: (l, 0, j))],
        out_specs=pl.BlockSpec((1, r, tn), lambda l, j: (l, 0, j)),
        out_shape=jax.ShapeDtypeStruct((depth, r, d6), F32),
        compiler_params=_cparams(),
        name="adaln_mod",
    )(cc, w_mod, b_mod.reshape(depth, 1, d6))


def _rope_tables(seq_len, rope):
    lane = np.arange(LANES)
    d = lane % HEAD_DIM
    if not rope:
        one = np.ones((seq_len, LANES), np.float32)
        zero = np.zeros((seq_len, LANES), np.float32)
        return one, zero, zero
    t = np.arange(seq_len)
    row = (t // GRID_W).astype(np.float64)
    col = (t % GRID_W).astype(np.float64)
    half = ROPE_AXIS_DIM // 2
    inv = ROPE_THETA ** (-np.arange(0, ROPE_AXIS_DIM, 2, dtype=np.float64) / ROPE_AXIS_DIM)
    pos = np.where((d // ROPE_AXIS_DIM)[None, :] == 0, row[:, None], col[:, None])
    ang = pos * inv[d % half][None, :]
    is_b = ((d % ROPE_AXIS_DIM) >= half)[None, :]
    cos = np.cos(ang)
    sin = np.sin(ang)
    s_up = np.where(is_b, sin, 0.0)
    s_dn = np.where(is_b, 0.0, -sin)
    return cos.astype(np.float32), s_up.astype(np.float32), s_dn.astype(np.float32)


def _store_padded_heads(blk, ext, out_ref, col0):
    lo = lax.broadcasted_iota(jnp.int32, blk.shape, 1) < HEAD_DIM
    out_ref[:, col0:col0 + LANES] = jnp.where(lo, blk, ext).astype(out_ref.dtype)
    out_ref[:, col0 + LANES:col0 + 2 * LANES] = jnp.where(lo, pltpu.roll(blk, HEAD_DIM, axis=1),
                                                          ext).astype(out_ref.dtype)


def _proj_kernel(x_ref, sh_ref, sc_ref, w_ref, gain_ref, bd_ref, cos_ref, sup_ref, sdn_ref, ext_ref,
                 q_ref, k_ref, v_ref, hp_ref):
    u = _layernorm(x_ref[...], ADALN_EPS) * (1.0 + sc_ref[0]) + sh_ref[0]
    p = jnp.dot(u.astype(BF16), w_ref[...], preferred_element_type=F32)
    qk = p[:, :QK_W]
    ms = jnp.dot((qk * qk).astype(BF16), bd_ref[...], preferred_element_type=F32)
    qn = qk * lax.rsqrt(ms + QK_EPS) * gain_ref[...]
    cos = cos_ref[...]
    sup = sup_ref[...]
    sdn = sdn_ref[...]
    n_qblk = D_ATTN // LANES
    for j in range(QK_W // LANES):
        blk = qn[:, j * LANES:(j + 1) * LANES]
        up = pltpu.roll(blk, ROPE_AXIS_DIM // 2, axis=1)
        dn = pltpu.roll(blk, LANES - ROPE_AXIS_DIM // 2, axis=1)
        rot = blk * cos + up * sup + dn * sdn
        if j < n_qblk:
            _store_padded_heads(rot, ext_ref[0:1, :], q_ref, 2 * j * LANES)
        else:
            _store_padded_heads(rot, ext_ref[1:2, :], k_ref, 2 * (j - n_qblk) * LANES)
    _store_padded_heads(p[:, QK_W:QK_W + KV_W], ext_ref[2:3, :], v_ref, 0)
    hp_ref[...] = p[:, QK_W + KV_W:]


def _proj_call(x2, mod3, mod_row, seq_len, w_in_bf, gain, bd, tables, ext, tm):
    r, d = x2.shape
    d_in = w_in_bf.shape[1]
    d_hp = d_in - QK_W - KV_W
    nseq = seq_len // tm
    cos, sup, sdn = tables
    tab_spec = pl.BlockSpec((tm, LANES), lambda i: (i % nseq, 0))
    return pl.pallas_call(
        _proj_kernel,
        grid=(r // tm,),
        in_specs=[pl.BlockSpec((tm, d), lambda i: (i, 0)),
                  pl.BlockSpec((1, 1, d), lambda i: (mod_row(i) * 6 + 0, 0, 0)),
                  pl.BlockSpec((1, 1, d), lambda i: (mod_row(i) * 6 + 1, 0, 0)),
                  pl.BlockSpec((d, d_in), lambda i: (0, 0)),
                  pl.BlockSpec((1, QK_W), lambda i: (0, 0)),
                  pl.BlockSpec((QK_W, QK_W), lambda i: (0, 0)),
                  tab_spec, tab_spec, tab_spec,
                  pl.BlockSpec((3, LANES), lambda i: (0, 0))],
        out_specs=[pl.BlockSpec((tm, N_Q_HEADS * LANES), lambda i: (i, 0)),
                   pl.BlockSpec((tm, N_KV_HEADS * LANES), lambda i: (i, 0)),
                   pl.BlockSpec((tm, N_KV_HEADS * LANES), lambda i: (i, 0)),
                   pl.BlockSpec((tm, d_hp), lambda i: (i, 0))],
        out_shape=[jax.ShapeDtypeStruct((r, N_Q_HEADS * LANES), BF16),
                   jax.ShapeDtypeStruct((r, N_KV_HEADS * LANES), BF16),
                   jax.ShapeDtypeStruct((r, N_KV_HEADS * LANES), BF16),
                   jax.ShapeDtypeStruct((r, d_hp), F32)],
        compiler_params=_cparams(),
        name="ln_mod_in_proj",
    )(x2, mod3, mod3, w_in_bf, gain, bd, cos, sup, sdn, ext)


ATTN_KEY_CHUNK = 1024
ATTN_SAFE_SHIFT = 60.0


def _attn_kernel(fast_ref, q_ref, *refs):
    o_ref = refs[-1]
    sources = [(refs[i], refs[i + 1]) for i in range(0, len(refs) - 1, 2)]
    tq = q_ref.shape[0]
    chunks = [(k_ref, v_ref, c0, min(ATTN_KEY_CHUNK, k_ref.shape[0] - c0))
              for k_ref, v_ref in sources for c0 in range(0, k_ref.shape[0], ATTN_KEY_CHUNK)]

    def run(running_max):
        for h in range(N_KV_HEADS):
            qs = jnp.concatenate([q_ref[:, (h * GQA_GROUP + g) * LANES:(h * GQA_GROUP + g + 1) * LANES]
                                  for g in range(GQA_GROUP)], axis=0)
            acc = jnp.zeros((GQA_GROUP * tq, LANES), F32)
            m = jnp.full((GQA_GROUP * tq, 1), -jnp.inf, F32)
            for k_ref, v_ref, c0, kw in chunks:
                kc = k_ref[c0:c0 + kw, h * LANES:(h + 1) * LANES]
                vc = v_ref[c0:c0 + kw, h * LANES:(h + 1) * LANES]
                s = lax.dot_general(qs, kc, (((1,), (1,)), ((), ())), preferred_element_type=F32)
                if running_max:
                    m_new = jnp.maximum(m, jnp.max(s, axis=-1, keepdims=True))
                    acc = acc * jnp.exp2(m - m_new)
                    s = s - m_new
                    m = m_new
                acc = acc + jnp.dot(jnp.exp2(s).astype(BF16), vc, preferred_element_type=F32)
            o = acc[:, :HEAD_DIM] / acc[:, HEAD_DIM:HEAD_DIM + 1]
            for g in range(GQA_GROUP):
                c0 = (h * GQA_GROUP + g) * HEAD_DIM
                o_ref[:, c0:c0 + HEAD_DIM] = o[g * tq:(g + 1) * tq].astype(o_ref.dtype)

    @pl.when(fast_ref[0] > 0)
    def _():
        run(False)

    @pl.when(fast_ref[0] == 0)
    def _():
        run(True)


def _attn_call(fast, q, kv, tq):
    b, lq, qw = q.shape
    kv_specs = [pl.BlockSpec((None,) + a.shape[1:], lambda i, j, f: (i, 0, 0)) for a in kv]
    return pl.pallas_call(
        _attn_kernel,
        grid_spec=pltpu.PrefetchScalarGridSpec(
            num_scalar_prefetch=1,
            grid=(b, lq // tq),
            in_specs=[pl.BlockSpec((None, tq, qw), lambda i, j, f: (i, j, 0))] + kv_specs,
            out_specs=pl.BlockSpec((None, tq, D_ATTN), lambda i, j, f: (i, j, 0))),
        out_shape=jax.ShapeDtypeStruct((b, lq, D_ATTN), BF16),
        compiler_params=_cparams(),
        name="attention",
    )(fast, q, *kv)


def _filter_consts(seq_len):
    t = np.linspace(0.0, 1.0, seq_len, dtype=np.float32).astype(np.float64)[:, None]
    omega = 2.0 * math.pi * np.arange(seq_len, dtype=np.float64)[:, None] / seq_len
    bands = np.linspace(1e-4, HYENA_BANDS - 1, HYENA_BANDS, dtype=np.float32).astype(np.float64)[None, :]
    feats = np.concatenate([t, np.cos(omega * bands), -np.sin(omega * bands)], axis=-1)
    feats = np.pad(feats, ((0, 0), (0, LANES - HYENA_EMB)))
    max_decay = math.log(HYENA_DECAY_TARGET) / HYENA_FAST_DECAY_PCT
    min_decay = math.log(HYENA_DECAY_TARGET) / HYENA_SLOW_DECAY_PCT
    deltas = np.linspace(min_decay, max_decay, D_HYENA, dtype=np.float32).astype(np.float64)
    decay = np.exp(-t * np.abs(deltas)[None, :])
    decay = np.concatenate([decay, decay], axis=1)
    return feats.astype(np.float32), decay.astype(np.float32)


def _filter_kernel(feat_ref, w1_ref, b1_ref, fr_ref, w2_ref, b2_ref, w3_ref, dec_ref, o_ref):
    fr = fr_ref[...]
    h = jnp.sin(fr * (jnp.dot(feat_ref[...], w1_ref[...], preferred_element_type=F32, precision=HIGHEST)
                      + b1_ref[...]))
    h = jnp.sin(fr * (jnp.dot(h, w2_ref[...], preferred_element_type=F32, precision=HIGHEST) + b2_ref[...]))
    h = jnp.dot(h, w3_ref[...], preferred_element_type=F32, precision=HIGHEST) * dec_ref[...]
    tot = jnp.sum(jnp.abs(h), axis=0, keepdims=True)
    tot = tot[:, :D_HYENA] + tot[:, D_HYENA:]
    inv = 1.0 / tot
    o_ref[...] = h * jnp.concatenate([inv, inv], axis=1)


def _filter_call(seq_len, w1, b1, freq, w2, b2, w3):
    feats, decay = _filter_consts(seq_len)
    hid = w2.shape[0]
    w1p = jnp.pad(w1, ((0, LANES - HYENA_EMB), (0, 0)))
    blk = 2 * D_HYENA
    full = lambda shape: pl.BlockSpec(shape, lambda o: (0,) * len(shape))
    return pl.pallas_call(
        _filter_kernel,
        grid=(HYENA_ORDER,),
        in_specs=[full((seq_len, LANES)), full((LANES, hid)), full((1, hid)), full((1, hid)),
                  full((hid, hid)), full((1, hid)),
                  pl.BlockSpec((hid, blk), lambda o: (0, o)),
                  full((seq_len, blk))],
        out_specs=pl.BlockSpec((seq_len, blk), lambda o: (0, o)),
        out_shape=jax.ShapeDtypeStruct((seq_len, HYENA_ORDER * blk), F32),
        compiler_params=_cparams(),
        name="hyena_filter",
    )(jnp.asarray(feats), w1p, b1.reshape(1, hid), freq.reshape(1, hid), w2, b2.reshape(1, hid), w3,
      jnp.asarray(decay))


def _filter_buffer(filt, seq_len):
    f = filt.reshape(seq_len, HYENA_ORDER, 2, D_HYENA)
    fwd, bwd = f[:, :, 0], f[:, :, 1]
    buf = jnp.concatenate([fwd, jnp.zeros((1,) + fwd.shape[1:], F32), bwd[:0:-1]], axis=0)
    return buf.reshape(2 * seq_len, HYENA_ORDER * D_HYENA)


@functools.lru_cache(maxsize=None)
def _fft_consts(seq_len):
    n = 2 * seq_len
    n2 = FFT_N2
    n1 = n // n2
    h1 = n1 // 2
    k1 = np.arange(n1)
    w1 = np.exp(-2j * np.pi * np.outer(k1, np.arange(n1)) / n1)
    f1 = np.zeros((n1, 2, 2 * h1))
    f1[:, 0, :h1], f1[:, 0, h1:] = w1[:, :h1].real, -w1[:, :h1].imag
    f1[:, 1, :h1], f1[:, 1, h1:] = w1[:, :h1].imag, w1[:, :h1].real
    f1 = f1.reshape(2 * n1, 2 * h1)
    f1r = np.stack([w1.real, w1.imag], axis=1).reshape(2 * n1, n1)
    a2 = np.arange(n2)
    tw = np.exp(-2j * np.pi * np.outer(k1, a2) / n)
    w2 = np.exp(-2j * np.pi * np.outer(a2, a2) / n2)
    m = w2[None, :, :] * tw[:, None, :]
    ff = np.concatenate([np.concatenate([m.real, -m.imag], axis=2),
                         np.concatenate([m.imag, m.real], axis=2)], axis=1)
    g = np.conj(w2.T)[None, :, :] * np.conj(tw)[:, :, None]
    gi = np.concatenate([np.concatenate([g.real, -g.imag], axis=2),
                         np.concatenate([g.imag, g.real], axis=2)], axis=1)
    v = np.conj(w1[:, :h1]).T
    f1i = np.zeros((2, h1, n1, 2))
    f1i[0, :, :, 0], f1i[0, :, :, 1] = v.real, -v.imag
    f1i[1, :, :, 0], f1i[1, :, :, 1] = v.imag, v.real
    f1i = f1i.reshape(2 * h1, 2 * n1)
    il = np.arange(2 * n2).reshape(2, n2).T.reshape(-1)
    ff_il = ff[:, :, il]
    gi_il = gi[:, il, :]
    f32 = lambda a: np.asarray(a, np.float32)
    return dict(n=n, n1=n1, h1=h1, f1=f32(f1), f1r=f32(f1r), ff=f32(ff), ff_il=f32(ff_il), gi_il=f32(gi_il),
                f1i=f32(f1i))


def _fft1_kernel(x_ref, f_ref, o_ref, *, exact):
    if exact:
        y = jnp.dot(f_ref[...], x_ref[...], preferred_element_type=F32, precision=HIGHEST)
    else:
        y = jnp.dot(f_ref[...], x_ref[...].astype(BF16), preferred_element_type=F32)
    o_ref[...] = y.astype(o_ref.dtype)


def _fft1_call(x3, f, out_dtype, exact, tc):
    g, k, cols = x3.shape
    m = f.shape[0]
    return pl.pallas_call(
        functools.partial(_fft1_kernel, exact=exact),
        grid=(g, cols // tc),
        in_specs=[pl.BlockSpec((None, k, tc), lambda i, j: (i, 0, j)),
                  pl.BlockSpec((m, k), lambda i, j: (0, 0))],
        out_specs=pl.BlockSpec((None, m, tc), lambda i, j: (i, 0, j)),
        out_shape=jax.ShapeDtypeStruct((g, m, cols), out_dtype),
        compiler_params=_cparams(),
        name="fft_outer_dft",
    )(x3, f)


def _spec_kernel(a_ref, ff_ref, o_ref, *, scale):
    for kk in range(a_ref.shape[0]):
        o_ref[kk] = jnp.dot(ff_ref[kk], a_ref[kk], preferred_element_type=F32, precision=HIGHEST) * scale


def _spec_call(a3, ff, scale, kt):
    n1, r, cols = a3.shape
    return pl.pallas_call(
        functools.partial(_spec_kernel, scale=scale),
        grid=(n1 // kt,),
        in_specs=[pl.BlockSpec((kt, r, cols), lambda i: (i, 0, 0)),
                  pl.BlockSpec((kt, r, r), lambda i: (i, 0, 0))],
        out_specs=pl.BlockSpec((kt, r, cols), lambda i: (i, 0, 0)),
        out_shape=jax.ShapeDtypeStruct((n1, r, cols), F32),
        compiler_params=_cparams(),
        name="fft_filter_spectrum",
    )(a3, ff)


def _ffta_kernel(z_ref, f_ref, o_ref):
    n1 = f_ref.shape[0] // 2
    h1 = f_ref.shape[1] // 2

    def body(n2, carry):
        rows = pl.ds(n2, h1, stride=FFT_N2)
        x = jnp.concatenate([z_ref[0, rows, :], z_ref[1, rows, :]], axis=0).astype(BF16)
        a = jnp.dot(f_ref[...], x, preferred_element_type=F32).astype(BF16)
        o_ref[pl.ds(n2, n1, stride=FFT_N2), :] = pltpu.bitcast(a, jnp.uint32)
        return carry

    lax.fori_loop(0, FFT_N2, body, 0, unroll=FFT_UNROLL)


def _ffta_call(z4, base, groups, f1_bf):
    _, _, l, c = z4.shape
    halves = c // LANES
    n1 = f1_bf.shape[0] // 2
    return pl.pallas_call(
        _ffta_kernel,
        grid=(groups, halves),
        in_specs=[pl.BlockSpec((None, 2, l, LANES), lambda i, j: (base + i, 0, 0, j)),
                  pl.BlockSpec(f1_bf.shape, lambda i, j: (0, 0))],
        out_specs=pl.BlockSpec((None, None, n1 * FFT_N2, LANES), lambda i, j: (i, j, 0, 0)),
        out_shape=jax.ShapeDtypeStruct((groups, halves, n1 * FFT_N2, LANES), jnp.uint32),
        compiler_params=_cparams(),
        name="fft_outer_dft",
    )(z4, f1_bf)


def _fftb_kernel(a_ref, ff_ref, gi_ref, h_ref, o_ref):
    groups, halves, kt = a_ref.shape[:3]
    half = FFT_N2

    def body(kk, carry):
        ff = ff_ref[kk]
        gi = gi_ref[kk]
        hr = h_ref[kk, :half, :]
        hi = h_ref[kk, half:, :]
        for g in range(groups):
            a = jnp.concatenate([pltpu.bitcast(a_ref[g, hh, kk], BF16) for hh in range(halves)], axis=1)
            xh = jnp.dot(ff, a, preferred_element_type=F32)
            xr, xi = xh[:half], xh[half:]
            y = jnp.concatenate([xr * hr - xi * hi, xr * hi + xi * hr], axis=0).astype(BF16)
            b = jnp.dot(gi, y, preferred_element_type=F32).astype(BF16)
            for hh in range(halves):
                o_ref[g, hh, kk] = pltpu.bitcast(b[:, hh * LANES:(hh + 1) * LANES], jnp.uint32)
        return carry

    lax.fori_loop(0, kt, body, 0)


def _fftb_call(a5, ff_bf, gi_bf, spec, order, kt):
    g, halves, n1, n2, _ = a5.shape
    r = 2 * n2
    return pl.pallas_call(
        _fftb_kernel,
        grid=(n1 // kt,),
        in_specs=[pl.BlockSpec((g, halves, kt, n2, LANES), lambda i: (0, 0, i, 0, 0)),
                  pl.BlockSpec((kt, r, r), lambda i: (i, 0, 0)),
                  pl.BlockSpec((kt, r, r), lambda i: (i, 0, 0)),
                  pl.BlockSpec((kt, r, halves * LANES), lambda i: (i, 0, order))],
        out_specs=pl.BlockSpec((g, halves, kt, n2, LANES), lambda i: (0, 0, i, 0, 0)),
        out_shape=jax.ShapeDtypeStruct(a5.shape, jnp.uint32),
        compiler_params=_cparams(),
        name="fft_inner_conv",
    )(a5, ff_bf, gi_bf, spec)


def _fftc_kernel(b_ref, f_ref, z_ref, g_ref, d_ref, o_ref):
    n1 = f_ref.shape[1] // 2
    h1 = f_ref.shape[0] // 2
    d = d_ref[...]

    def body(n2, carry):
        b = pltpu.bitcast(b_ref[pl.ds(n2, n1, stride=FFT_N2), :], BF16)
        y = jnp.dot(f_ref[...], b, preferred_element_type=F32)
        rows = pl.ds(n2, h1, stride=FFT_N2)
        for m in range(2):
            o_ref[m, rows, :] = g_ref[m, rows, :] * (y[m * h1:(m + 1) * h1] + d * z_ref[m, rows, :])
        return carry

    lax.fori_loop(0, FFT_N2, body, 0, unroll=FFT_UNROLL)


def _fftc_call(b4, f1i_bf, z4, z_base, g4, g_base, d2, order):
    groups, halves, rows, _ = b4.shape
    _, _, l, c = z4.shape
    pair = lambda base: pl.BlockSpec((None, 2, l, LANES), lambda i, j: (base + i, 0, 0, j))
    return pl.pallas_call(
        _fftc_kernel,
        grid=(groups, halves),
        in_specs=[pl.BlockSpec((None, None, rows, LANES), lambda i, j: (i, j, 0, 0)),
                  pl.BlockSpec(f1i_bf.shape, lambda i, j: (0, 0)),
                  pair(z_base), pair(g_base),
                  pl.BlockSpec((None, 1, LANES), lambda i, j: (order, 0, j))],
        out_specs=pair(0),
        out_shape=jax.ShapeDtypeStruct((groups, 2, l, c), F32),
        compiler_params=_cparams(),
        name="fft_outer_inverse_gate",
    )(b4, f1i_bf, z4, g4, d2.reshape(d2.shape[0], 1, d2.shape[1]))


def _sconv_kernel(z_ref, w_ref, b_ref, o_ref):
    z = z_ref[...]
    n = z.shape[0]
    row = lax.broadcasted_iota(jnp.int32, z.shape, 0)
    prev = jnp.where(row == 0, 0.0, pltpu.roll(z, 1, axis=0))
    nxt = jnp.where(row == n - 1, 0.0, pltpu.roll(z, n - 1, axis=0))
    o_ref[...] = prev * w_ref[0:1, :] + z * w_ref[1:2, :] + nxt * w_ref[2:3, :] + b_ref[...]


def _sconv_call(hp3, conv_w, conv_b):
    b, l, _ = hp3.shape
    per = D_HYENA // LANES
    nblk = 3 * per
    return pl.pallas_call(
        _sconv_kernel,
        grid=(b, nblk),
        in_specs=[pl.BlockSpec((None, l, LANES), lambda i, j: (i, 0, j)),
                  pl.BlockSpec((3, LANES), lambda i, j: (0, j)),
                  pl.BlockSpec((1, LANES), lambda i, j: (0, j))],
        out_specs=pl.BlockSpec((None, None, l, LANES), lambda i, j: (j // per, i, 0, j % per)),
        out_shape=jax.ShapeDtypeStruct((3, b, l, D_HYENA), F32),
        compiler_params=_cparams(),
        name="hyena_short_conv",
    )(hp3, conv_w, conv_b.reshape(1, -1))


def _hyena_spectrum(filt, seq_len):
    cst = _fft_consts(seq_len)
    n, n1 = cst["n"], cst["n1"]
    buf = _filter_buffer(filt, seq_len)
    cols = FFT_N2 * buf.shape[1]
    a = _fft1_call(buf.reshape(1, n1, cols), jnp.asarray(cst["f1r"]), F32, True, min(cols, 2048))
    a3 = a.reshape(n1, 2 * FFT_N2, buf.shape[1])
    return _spec_call(a3, jnp.asarray(cst["ff"]), 1.0 / n, min(n1, 8))


def _hyena_latent(hp3, conv_w, conv_b, spec, d):
    b, l, _ = hp3.shape
    cst = _fft_consts(l)
    n1, h1 = cst["n1"], cst["h1"]
    groups = b // 2
    halves = D_HYENA // LANES
    kt = min(n1, 8)
    f1 = jnp.asarray(cst["f1"], BF16)
    ff = jnp.asarray(cst["ff_il"], BF16)
    gi = jnp.asarray(cst["gi_il"], BF16)
    f1i = jnp.asarray(cst["f1i"], BF16)
    zc = _sconv_call(hp3, conv_w, conv_b).reshape(3 * groups, 2, l, D_HYENA)
    z, z_base = zc, 0
    for o in range(HYENA_ORDER):
        a = _ffta_call(z, z_base, groups, f1)
        bb = _fftb_call(a.reshape(groups, halves, n1, FFT_N2, LANES), ff, gi, spec, o, kt)
        z = _fftc_call(bb.reshape(groups, halves, n1 * FFT_N2, LANES), f1i, z, z_base, zc, (1 + o) * groups, d, o)
        z_base = 0
    return z.reshape(b, l, D_HYENA)


@functools.lru_cache(maxsize=None)
def _dft_consts(seq_len):
    n = 2 * seq_len
    w = np.exp(-2j * np.pi * np.outer(np.arange(n), np.arange(n)) / n)
    fwd_full = np.concatenate([w.real, w.imag], axis=0)
    fwd = fwd_full[:, :seq_len]
    wi = np.conj(w[:, :seq_len]).T
    inv = np.concatenate([wi.real, -wi.imag], axis=1)
    f32 = lambda a: np.asarray(a, np.float32)
    return dict(n=n, fwd_full=f32(fwd_full), fwd=f32(fwd), inv=f32(inv))


def _cspec_kernel(buf_ref, f_ref, o_ref, *, scale):
    o_ref[...] = jnp.dot(f_ref[...], buf_ref[...], preferred_element_type=F32, precision=HIGHEST) * scale


def _ctx_spectrum(filt, seq_len):
    cst = _dft_consts(seq_len)
    buf = _filter_buffer(filt, seq_len)
    n = cst["n"]
    return pl.pallas_call(
        functools.partial(_cspec_kernel, scale=1.0 / n),
        out_shape=jax.ShapeDtypeStruct((2 * n, buf.shape[1]), F32),
        compiler_params=_cparams(),
        name="ctx_filter_spectrum",
    )(buf, jnp.asarray(cst["fwd_full"]))


def _ctx_hyena_kernel(z_ref, w_ref, b_ref, f_ref, fi_ref, h_ref, d_ref, o_ref):
    z = z_ref[...]
    l = z.shape[0]
    n = 2 * l
    row = lax.broadcasted_iota(jnp.int32, z.shape, 0)
    prev = jnp.where(row == 0, 0.0, pltpu.roll(z, 1, axis=0))
    nxt = jnp.where(row == l - 1, 0.0, pltpu.roll(z, l - 1, axis=0))
    zc = prev * w_ref[0:1, :] + z * w_ref[1:2, :] + nxt * w_ref[2:3, :] + b_ref[...]
    cur = zc[:, :D_HYENA]
    for o in range(HYENA_ORDER):
        gate = zc[:, (1 + o) * D_HYENA:(2 + o) * D_HYENA]
        xh = jnp.dot(f_ref[...], cur.astype(BF16), preferred_element_type=F32)
        xr, xi = xh[:n], xh[n:]
        hr = h_ref[:n, o * D_HYENA:(o + 1) * D_HYENA]
        hi = h_ref[n:, o * D_HYENA:(o + 1) * D_HYENA]
        y = jnp.concatenate([xr * hr - xi * hi, xr * hi + xi * hr], axis=0).astype(BF16)
        conv = jnp.dot(fi_ref[...], y, preferred_element_type=F32)
        cur = gate * (conv + d_ref[o:o + 1, :] * cur)
    o_ref[...] = cur


def _ctx_hyena_call(hp3, conv_w, conv_b, spec, d):
    b, l, _ = hp3.shape
    cst = _dft_consts(l)
    n = cst["n"]
    w3 = 3 * D_HYENA
    full = lambda shape: pl.BlockSpec(shape, lambda i: (0,) * len(shape))
    return pl.pallas_call(
        _ctx_hyena_kernel,
        grid=(b,),
        in_specs=[pl.BlockSpec((None, l, w3), lambda i: (i, 0, 0)),
                  full((3, w3)), full((1, w3)), full((2 * n, l)), full((l, 2 * n)),
                  full((2 * n, HYENA_ORDER * D_HYENA)), full((HYENA_ORDER, D_HYENA))],
        out_specs=pl.BlockSpec((None, l, D_HYENA), lambda i: (i, 0, 0)),
        out_shape=jax.ShapeDtypeStruct((b, l, D_HYENA), F32),
        compiler_params=_cparams(),
        name="ctx_hyena",
    )(hp3, conv_w, conv_b.reshape(1, -1), jnp.asarray(cst["fwd"], BF16), jnp.asarray(cst["inv"], BF16),
      spec, d)


def _pool_kernel(p_ref, w_ref, sc_ref, o_ref):
    x = p_ref[...]
    l, c = x.shape
    half = pl.program_id(1)
    zpad = jnp.zeros((POOL_PAD, c), F32)
    xp = jnp.concatenate([zpad, x, zpad], axis=0)
    tot = l + 2 * POOL_PAD
    sums = []
    f = xp
    for win in POOL_WINDOWS:
        f = f + pltpu.roll(f, tot - win // 2, axis=0)
        sums.append(pltpu.roll(f, win // 2, axis=0)[POOL_PAD:POOL_PAD + l])
    t = lax.broadcasted_iota(jnp.int32, (l, c), 0)
    grp = lax.broadcasted_iota(jnp.int32, (l, c), 1) // POOL_GROUP + half * (LANES // POOL_GROUP)
    total = sums[-1]
    cnt = None
    for gi in range(len(POOL_WINDOWS) - 1, -1, -1):
        win = POOL_WINDOWS[gi]
        cw = (jnp.minimum(t + (win - win // 2), l) - jnp.maximum(t - win // 2, 0)).astype(F32)
        if cnt is None:
            cnt = cw
        else:
            sel = grp == gi
            total = jnp.where(sel, sums[gi], total)
            cnt = jnp.where(sel, cw, cnt)
    dlt = total / cnt - x
    y = jnp.dot(dlt.astype(BF16), w_ref[0], preferred_element_type=F32)
    o_ref[...] = y * sc_ref[...]


def _pool_call(hp3, pool_w, pool_scale):
    b, l, w = hp3.shape
    nh = D_POOL // LANES
    first = (w - D_POOL) // LANES
    per = LANES // POOL_GROUP
    wbd = jnp.zeros((nh, LANES, LANES), F32)
    for g in range(len(POOL_WINDOWS)):
        r0 = (g % per) * POOL_GROUP
        wbd = wbd.at[g // per, r0:r0 + POOL_GROUP, r0:r0 + POOL_GROUP].set(pool_w[g])
    return pl.pallas_call(
        _pool_kernel,
        grid=(b, nh),
        in_specs=[pl.BlockSpec((None, l, LANES), lambda i, j: (i, 0, first + j)),
                  pl.BlockSpec((1, LANES, LANES), lambda i, j: (j, 0, 0)),
                  pl.BlockSpec((1, LANES), lambda i, j: (0, j))],
        out_specs=pl.BlockSpec((None, l, LANES), lambda i, j: (i, 0, j)),
        out_shape=jax.ShapeDtypeStruct((b, l, D_POOL), F32),
        compiler_params=_cparams(),
        name="pool_mixer",
    )(hp3, wbd.astype(BF16), pool_scale.reshape(1, D_POOL))


def _outproj_kernel(x_ref, att_ref, hy_ref, pl_ref, gate_ref, w_ref, g_ref, b_ref, o_ref, *, alpha):
    mix = jnp.dot(att_ref[...], w_ref[0:D_ATTN, :], preferred_element_type=F32)
    mix += jnp.dot(hy_ref[...].astype(BF16), w_ref[D_ATTN:D_ATTN + D_HYENA, :], preferred_element_type=F32)
    mix += jnp.dot(pl_ref[...].astype(BF16), w_ref[D_ATTN + D_HYENA:, :], preferred_element_type=F32)
    y = alpha * x_ref[...] + gate_ref[0] * mix
    o_ref[...] = _layernorm(y, LN_EPS) * g_ref[...] + b_ref[...]


def _outproj_call(x2, att2, hy2, pl2, mod3, mod_row, w_out_bf, g, b, alpha, tm):
    r, d = x2.shape
    row = lambda w: pl.BlockSpec((tm, w), lambda i: (i, 0))
    vec = pl.BlockSpec((1, d), lambda i: (0, 0))
    return pl.pallas_call(
        functools.partial(_outproj_kernel, alpha=alpha),
        grid=(r // tm,),
        in_specs=[row(d), row(D_ATTN), row(D_HYENA), row(D_POOL),
                  pl.BlockSpec((1, 1, d), lambda i: (mod_row(i) * 6 + 2, 0, 0)),
                  pl.BlockSpec(w_out_bf.shape, lambda i: (0, 0)),
                  vec, vec],
        out_specs=row(d),
        out_shape=jax.ShapeDtypeStruct((r, d), F32),
        compiler_params=_cparams(),
        name="out_proj_deepnorm",
    )(x2, att2, hy2, pl2, mod3, w_out_bf, g.reshape(1, d), b.reshape(1, d))


def _ffn_kernel(*refs, alpha, n_exp, n_chunk):
    if n_exp > 1:
        (x_ref, sh_ref, sc_ref, gate_ref, rw_ref, wg_ref, wu_ref, wd_ref, g_ref, b_ref,
         o_ref, u_ref, acc_ref, comb_ref) = refs
    else:
        (x_ref, sh_ref, sc_ref, gate_ref, wg_ref, wu_ref, wd_ref, g_ref, b_ref,
         o_ref, u_ref, acc_ref) = refs
    j = pl.program_id(1)

    @pl.when(j == 0)
    def _():
        u = (_layernorm(x_ref[...], ADALN_EPS) * (1.0 + sc_ref[0]) + sh_ref[0]).astype(BF16)
        u_ref[...] = u
        acc_ref[...] = jnp.zeros_like(acc_ref)
        if n_exp > 1:
            logits = jnp.dot(u, rw_ref[...].astype(BF16), preferred_element_type=F32)
            lane = lax.broadcasted_iota(jnp.int32, logits.shape, 1)
            m1 = jnp.max(logits, axis=-1, keepdims=True)
            i1 = jnp.min(jnp.where(logits == m1, lane, n_exp), axis=-1, keepdims=True)
            rest = jnp.where(lane == i1, -jnp.inf, logits)
            m2 = jnp.max(rest, axis=-1, keepdims=True)
            i2 = jnp.min(jnp.where(rest == m2, lane, n_exp), axis=-1, keepdims=True)
            e2 = jnp.exp(m2 - m1)
            den = 1.0 + e2
            comb_ref[...] = jnp.where(lane == i1, 1.0 / den, 0.0) + jnp.where(lane == i2, e2 / den, 0.0)

    u = u_ref[...]
    a = jnp.dot(u, wg_ref[0].astype(BF16), preferred_element_type=F32)
    bb = jnp.dot(u, wu_ref[0].astype(BF16), preferred_element_type=F32)
    h = a * jax.nn.sigmoid(a) * bb
    if n_exp > 1:
        comb = comb_ref[...]
        lane = lax.broadcasted_iota(jnp.int32, comb.shape, 1)
        h = h * jnp.sum(jnp.where(lane == j // n_chunk, comb, 0.0), axis=-1, keepdims=True)
    acc_ref[...] += jnp.dot(h.astype(BF16), wd_ref[0].astype(BF16), preferred_element_type=F32)

    @pl.when(j == n_exp * n_chunk - 1)
    def _():
        y = alpha * x_ref[...] + gate_ref[0] * acc_ref[...]
        o_ref[...] = _layernorm(y, LN_EPS) * g_ref[...] + b_ref[...]


def _ffn_call(x2, mod3, mod_row, router_w, w_gate, w_up, w_down, w_base, n_exp, g, b, alpha, tm, fc):
    r, d = x2.shape
    dff = w_gate.shape[2]
    n_chunk = dff // fc
    steps = n_exp * n_chunk
    vec = pl.BlockSpec((1, d), lambda i, j: (0, 0))
    mod = lambda which: pl.BlockSpec((1, 1, d), lambda i, j: (mod_row(i) * 6 + which, 0, 0))
    in_specs = [pl.BlockSpec((tm, d), lambda i, j: (i, 0)), mod(3), mod(4), mod(5)]
    args = [x2, mod3, mod3, mod3]
    scratch = [pltpu.VMEM((tm, d), BF16), pltpu.VMEM((tm, d), F32)]
    if n_exp > 1:
        in_specs.append(pl.BlockSpec(router_w.shape, lambda i, j: (0, 0)))
        args.append(router_w)
        scratch.append(pltpu.VMEM((tm, n_exp), F32))
    in_specs += [pl.BlockSpec((1, d, fc), lambda i, j: (w_base + j // n_chunk, 0, j % n_chunk)),
                 pl.BlockSpec((1, d, fc), lambda i, j: (w_base + j // n_chunk, 0, j % n_chunk)),
                 pl.BlockSpec((1, fc, d), lambda i, j: (w_base + j // n_chunk, j % n_chunk, 0)),
                 vec, vec]
    args += [w_gate, w_up, w_down, g.reshape(1, d), b.reshape(1, d)]
    return pl.pallas_call(
        functools.partial(_ffn_kernel, alpha=alpha, n_exp=n_exp, n_chunk=n_chunk),
        grid=(r // tm, steps),
        in_specs=in_specs,
        out_specs=pl.BlockSpec((tm, d), lambda i, j: (i, 0)),
        out_shape=jax.ShapeDtypeStruct((r, d), F32),
        scratch_shapes=scratch,
        compiler_params=_cparams(),
        name="channel_mixer",
    )(*args)


MOE_WINDOW = 1024
MOE_TILE = 64
MOE_SUPER = 16
MOE_CTILES = 6
MOE_FC = 512
MOE_GCHUNK = 512
MOE_SUB = 256


def _router_kernel(x_ref, sh_ref, sc_ref, rw_ref, tri_ref, u_ref, comb_ref, pos_ref):
    u = (_layernorm(x_ref[...], ADALN_EPS) * (1.0 + sc_ref[0]) + sh_ref[0]).astype(BF16)
    u_ref[...] = u
    logits = jnp.dot(u, rw_ref[...].astype(BF16), preferred_element_type=F32)
    n_exp = logits.shape[1]
    lane = lax.broadcasted_iota(jnp.int32, logits.shape, 1)
    m1 = jnp.max(logits, axis=-1, keepdims=True)
    i1 = jnp.min(jnp.where(logits == m1, lane, n_exp), axis=-1, keepdims=True)
    rest = jnp.where(lane == i1, -jnp.inf, logits)
    m2 = jnp.max(rest, axis=-1, keepdims=True)
    i2 = jnp.min(jnp.where(rest == m2, lane, n_exp), axis=-1, keepdims=True)
    e2 = jnp.exp(m2 - m1)
    den = 1.0 + e2
    comb = jnp.where(lane == i1, 1.0 / den, 0.0) + jnp.where(lane == i2, e2 / den, 0.0)
    comb_ref[...] = comb
    routed = jnp.where(comb > 0.0, 1.0, 0.0).astype(BF16)
    pos_ref[...] = jnp.dot(tri_ref[...], routed, preferred_element_type=F32) - 1.0


def _router_call(x2, mod3, mod_row, router_w):
    r, d = x2.shape
    n_exp = router_w.shape[1]
    tm = MOE_WINDOW
    tri = jnp.asarray(np.tril(np.ones((tm, tm), np.float32)), BF16)
    mod = lambda which: pl.BlockSpec((1, 1, d), lambda i: (mod_row(i) * 6 + which, 0, 0))
    small = pl.BlockSpec((tm, n_exp), lambda i: (i, 0))
    return pl.pallas_call(
        _router_kernel,
        grid=(r // tm,),
        in_specs=[pl.BlockSpec((tm, d), lambda i: (i, 0)), mod(3), mod(4),
                  pl.BlockSpec(router_w.shape, lambda i: (0, 0)),
                  pl.BlockSpec((tm, tm), lambda i: (0, 0))],
        out_specs=[pl.BlockSpec((tm, d), lambda i: (i, 0)), small, small],
        out_shape=[jax.ShapeDtypeStruct((r, d), BF16), jax.ShapeDtypeStruct((r, n_exp), F32),
                   jax.ShapeDtypeStruct((r, n_exp), F32)],
        compiler_params=_cparams(),
        name="moe_router",
    )(x2, mod3, mod3, router_w, tri)


def _moe_plan(comb, pos_f, n_tok):
    n_exp = comb.shape[1]
    win, tile = MOE_WINDOW, MOE_TILE
    n_win = n_tok // win
    cap_tiles = 2 * win // tile + n_exp
    i32 = jnp.int32
    mask = (comb > 0.0).reshape(n_win, win, n_exp)
    pos = pos_f.astype(i32).reshape(n_win, win, n_exp)
    cnt = pos[:, -1, :] + 1
    ntile = (cnt + tile - 1) // tile
    woff = jnp.cumsum(ntile, axis=1) - ntile
    rows = jnp.where(mask, tile * woff[:, None, :] + pos, -1)
    row_a = jnp.max(rows, axis=-1)
    row_b = jnp.max(jnp.where(rows == row_a[..., None], -1, rows), axis=-1)
    rowsel = jnp.stack([row_a, row_b], axis=1)
    win_rows = tile * jnp.sum(ntile, axis=1)

    etiles = jnp.sum(ntile, axis=0)
    nsup = (etiles + MOE_SUPER - 1) // MOE_SUPER
    ebase = MOE_SUPER * (jnp.cumsum(nsup) - nsup)
    erank = ebase[None, :] + jnp.cumsum(ntile, axis=0) - ntile
    n_sup = -(-(n_win * cap_tiles) // MOE_SUPER) + n_exp
    n_rank = n_sup * MOE_SUPER
    slots = win // tile
    s = jnp.arange(slots, dtype=i32)[None, None, :]
    ok = s < ntile[:, :, None]
    rank = jnp.where(ok, erank[:, :, None] + s, n_rank)
    wm_tile = (jnp.arange(n_win, dtype=i32) * cap_tiles)[:, None, None] + woff[:, :, None] + s
    src = jnp.zeros((n_rank + 1,), i32).at[rank.reshape(-1)].set(wm_tile.reshape(-1))[:n_rank]
    sup_total = jnp.sum(nsup)
    g = jnp.arange(n_sup, dtype=i32)
    sup_valid = (g < sup_total).astype(i32)
    sup_exp = jnp.clip(jnp.sum((g[:, None] >= jnp.cumsum(nsup)[None, :]).astype(i32), axis=1), 0, n_exp - 1)
    sup_exp = jnp.where(sup_valid > 0, sup_exp, sup_exp[jnp.maximum(sup_total - 1, 0)])

    nblk = (ntile + MOE_CTILES - 1) // MOE_CTILES
    flat_nblk = nblk.reshape(-1)
    start = jnp.cumsum(flat_nblk) - flat_nblk
    n_steps = n_win * (-(-cap_tiles // MOE_CTILES) + n_exp)
    total_steps = jnp.sum(flat_nblk)
    sidx = jnp.arange(n_steps, dtype=i32)
    owner = jnp.sum((sidx[:, None] >= jnp.cumsum(flat_nblk)[None, :]).astype(i32), axis=1)
    owner = jnp.clip(owner, 0, n_win * n_exp - 1)
    c_valid = (sidx < total_steps).astype(i32)
    last_owner = owner[jnp.maximum(total_steps - 1, 0)]
    owner = jnp.where(c_valid > 0, owner, last_owner)
    c_w = owner // n_exp
    c_e = owner % n_exp
    c_b = jnp.where(c_valid > 0, sidx - start[owner], 0)
    c_t0 = erank.reshape(-1)[owner] + MOE_CTILES * c_b
    prev_w = jnp.concatenate([jnp.full((1,), -1, i32), c_w[:-1]])
    next_w = jnp.concatenate([c_w[1:], jnp.full((1,), -1, i32)])
    next_valid = jnp.concatenate([c_valid[1:], jnp.zeros((1,), i32)])
    c_first = ((c_w != prev_w) & (c_valid > 0)).astype(i32)
    c_last = (((c_w != next_w) | (next_valid == 0)) & (c_valid > 0)).astype(i32)
    return dict(n_win=n_win, cap_tiles=cap_tiles, n_sup=n_sup, n_rank=n_rank, n_steps=n_steps,
                rowsel=rowsel, win_rows=win_rows.astype(i32), pos=pos_f,
                src=src, sup_exp=sup_exp.astype(i32), sup_valid=sup_valid,
                c_w=c_w.astype(i32), c_e=c_e.astype(i32), c_b=c_b.astype(i32), c_t0=c_t0.astype(i32),
                c_valid=c_valid, c_first=c_first, c_last=c_last)


def _dispatch_kernel(nrows_ref, u_ref, sel_ref, o_ref):
    w = pl.program_id(0)
    r1 = sel_ref[0:1, :]
    r2 = sel_ref[1:2, :]
    for cidx in range(o_ref.shape[0] // MOE_GCHUNK):
        base = cidx * MOE_GCHUNK
        rows = pl.ds(base, MOE_GCHUNK)

        @pl.when(base < nrows_ref[w])
        def _():
            rid = lax.broadcasted_iota(jnp.int32, (MOE_GCHUNK, r1.shape[1]), 0) + base
            onehot = jnp.where(rid == r1, 1.0, jnp.where(rid == r2, 1.0, 0.0)).astype(BF16)
            o_ref[rows, :] = jnp.dot(onehot, u_ref[...], preferred_element_type=F32).astype(o_ref.dtype)

        @pl.when(base >= nrows_ref[w])
        def _():
            o_ref[rows, :] = jnp.zeros((MOE_GCHUNK, o_ref.shape[1]), o_ref.dtype)


def _dispatch_call(u, plan):
    n_tok, d = u.shape
    cap_rows = plan["cap_tiles"] * MOE_TILE
    n_win = plan["n_win"]
    return pl.pallas_call(
        _dispatch_kernel,
        grid_spec=pltpu.PrefetchScalarGridSpec(
            num_scalar_prefetch=1,
            grid=(n_win,),
            in_specs=[pl.BlockSpec((MOE_WINDOW, d), lambda w, n: (w, 0)),
                      pl.BlockSpec((None, TOP_K, MOE_WINDOW), lambda w, n: (w, 0, 0))],
            out_specs=pl.BlockSpec((cap_rows, d), lambda w, n: (w, 0))),
        out_shape=jax.ShapeDtypeStruct((n_win * cap_rows, d), BF16),
        compiler_params=_cparams(),
        name="moe_dispatch",
    )(plan["win_rows"], u, plan["rowsel"])


def _expert_kernel(src_ref, exp_ref, valid_ref, *refs, n_chunk):
    tiles = refs[:MOE_SUPER]
    wg_ref, wu_ref, wd_ref, o_ref, u_ref, acc_ref = refs[MOE_SUPER:]
    g = pl.program_id(0)
    j = pl.program_id(1)
    ok = valid_ref[g] > 0

    @pl.when(j == 0)
    def _():
        for k in range(MOE_SUPER):
            u_ref[k * MOE_TILE:(k + 1) * MOE_TILE, :] = tiles[k][...]
        acc_ref[...] = jnp.zeros_like(acc_ref)

    @pl.when(ok)
    def _():
        u = u_ref[...]
        part = None
        for c0 in range(0, wg_ref.shape[2], MOE_SUB):
            a = jnp.dot(u, wg_ref[0, :, c0:c0 + MOE_SUB].astype(BF16), preferred_element_type=F32)
            b = jnp.dot(u, wu_ref[0, :, c0:c0 + MOE_SUB].astype(BF16), preferred_element_type=F32)
            h = (a * jax.nn.sigmoid(a) * b).astype(BF16)
            y = jnp.dot(h, wd_ref[0, c0:c0 + MOE_SUB, :].astype(BF16), preferred_element_type=F32)
            part = y if part is None else part + y
        acc_ref[...] += part

    @pl.when(j == n_chunk - 1)
    def _():
        o_ref[...] = acc_ref[...].astype(o_ref.dtype)


def _expert_call(xw, plan, w_gate, w_up, w_down, layer_base, fc):
    d = xw.shape[1]
    dff = w_gate.shape[2]
    n_chunk = dff // fc
    rows = MOE_SUPER * MOE_TILE

    def tile_spec(k):
        return pl.BlockSpec((MOE_TILE, d), lambda g, j, src, ex, va: (src[g * MOE_SUPER + k], 0))

    def chunk(j, va, g):
        return jnp.where(va[g] > 0, j, n_chunk - 1)

    in_specs = [tile_spec(k) for k in range(MOE_SUPER)] + [
        pl.BlockSpec((1, d, fc), lambda g, j, src, ex, va: (layer_base + ex[g], 0, chunk(j, va, g))),
        pl.BlockSpec((1, d, fc), lambda g, j, src, ex, va: (layer_base + ex[g], 0, chunk(j, va, g))),
        pl.BlockSpec((1, fc, d), lambda g, j, src, ex, va: (layer_base + ex[g], chunk(j, va, g), 0))]
    return pl.pallas_call(
        functools.partial(_expert_kernel, n_chunk=n_chunk),
        grid_spec=pltpu.PrefetchScalarGridSpec(
            num_scalar_prefetch=3,
            grid=(plan["n_sup"], n_chunk),
            in_specs=in_specs,
            out_specs=pl.BlockSpec((rows, d), lambda g, j, src, ex, va: (g, 0)),
            scratch_shapes=[pltpu.VMEM((rows, d), BF16), pltpu.VMEM((rows, d), F32)]),
        out_shape=jax.ShapeDtypeStruct((plan["n_sup"] * rows, d), BF16),
        compiler_params=_cparams(),
        name="moe_experts",
    )(plan["src"], plan["sup_exp"], plan["sup_valid"], *([xw] * MOE_SUPER), w_gate, w_up, w_down)


def _combine_kernel(cw_ref, ce_ref, cb_ref, ct_ref, cv_ref, cf_ref, cl_ref, *refs, alpha):
    tile_refs = refs[:MOE_CTILES]
    pos_ref, comb_ref, x_ref, gate_ref, g_ref, b_ref, o_ref = refs[MOE_CTILES:]
    s = pl.program_id(0)
    blk_rows = MOE_CTILES * MOE_TILE

    @pl.when(cf_ref[s] > 0)
    def _():
        o_ref[...] = jnp.zeros_like(o_ref)

    @pl.when(cv_ref[s] > 0)
    def _():
        comb = comb_ref[...]
        lane = lax.broadcasted_iota(jnp.int32, comb.shape, 1)
        pick = lane == ce_ref[s]
        gcol = jnp.sum(jnp.where(pick, comb, 0.0), axis=-1, keepdims=True)
        pcol = jnp.sum(jnp.where(pick, pos_ref[...], 0.0), axis=-1, keepdims=True)
        rel = pcol.astype(jnp.int32) - cb_ref[s] * blk_rows
        col = lax.broadcasted_iota(jnp.int32, (comb.shape[0], blk_rows), 1)
        q = jnp.where(rel == col, gcol, 0.0).astype(BF16)
        rows = jnp.concatenate([t[...] for t in tile_refs], axis=0)
        o_ref[...] += jnp.dot(q, rows, preferred_element_type=F32)

    @pl.when(cl_ref[s] > 0)
    def _():
        y = alpha * x_ref[...] + gate_ref[0] * o_ref[...]
        o_ref[...] = _layernorm(y, LN_EPS) * g_ref[...] + b_ref[...]


def _combine_call(ys, plan, comb, x2, mod3, mod_row, g, b, alpha):
    n_tok, d = x2.shape
    n_exp = comb.shape[1]
    last_tile = plan["n_rank"] - 1

    def tile_spec(k):
        return pl.BlockSpec((MOE_TILE, d),
                            lambda s, cw, ce, cb, ct, cv, cf, cl: (jnp.minimum(ct[s] + k, last_tile), 0))

    win = lambda width: pl.BlockSpec((MOE_WINDOW, width), lambda s, cw, *_: (cw[s], 0))
    vec = pl.BlockSpec((1, d), lambda s, *_: (0, 0))
    return pl.pallas_call(
        functools.partial(_combine_kernel, alpha=alpha),
        grid_spec=pltpu.PrefetchScalarGridSpec(
            num_scalar_prefetch=7,
            grid=(plan["n_steps"],),
            in_specs=[tile_spec(k) for k in range(MOE_CTILES)] + [win(n_exp), win(n_exp), win(d),
                      pl.BlockSpec((1, 1, d), lambda s, cw, *_: (mod_row(cw[s]) * 6 + 5, 0, 0)),
                      vec, vec],
            out_specs=win(d)),
        out_shape=jax.ShapeDtypeStruct((n_tok, d), F32),
        compiler_params=_cparams(),
        name="moe_combine_deepnorm",
    )(plan["c_w"], plan["c_e"], plan["c_b"], plan["c_t0"], plan["c_valid"], plan["c_first"], plan["c_last"],
      *([ys] * MOE_CTILES), plan["pos"], comb, x2, mod3, g.reshape(1, d), b.reshape(1, d))


def _moe_call(x2, mod3, tok_row, router_w, w_gate, w_up, w_down, layer_base, g, b, alpha, fc):
    n_tok = x2.shape[0]
    u, comb, pos = _router_call(x2, mod3, lambda i: tok_row(i * MOE_WINDOW), router_w)
    plan = _moe_plan(comb, pos, n_tok)
    xw = _dispatch_call(u, plan)
    ys = _expert_call(xw, plan, w_gate, w_up, w_down, layer_base, fc)
    return _combine_call(ys, plan, comb, x2, mod3, lambda w: tok_row(w * MOE_WINDOW), g, b, alpha)


def _pick_tile(n, pref):
    t = min(n, pref)
    while n % t:
        t //= 2
    return t


def kernel(x, c, ctx, c_ctx, w_mod, b_mod, w_in, q_gain, k_gain, hy_conv_w, hy_conv_b, hy_f_w1, hy_f_b1, hy_f_freq, hy_f_w2, hy_f_b2, hy_f_w3, hy_d, pool_w, pool_scale, w_out, ln1_g, ln1_b, ln2_g, ln2_b, ffn_w_gate, ffn_w_up, ffn_w_down, router_w, moe_w_gate, moe_w_up, moe_w_down):
    bsz, seq, d = x.shape
    clen = ctx.shape[1]
    depth = w_mod.shape[0]
    alpha = (2.0 * depth) ** 0.25
    assert seq % MOE_WINDOW == 0 and bsz % 2 == 0
    n_exp = router_w.shape[2]
    moe_wg = moe_w_gate.reshape((-1,) + moe_w_gate.shape[2:])
    moe_wu = moe_w_up.reshape((-1,) + moe_w_up.shape[2:])
    moe_wd = moe_w_down.reshape((-1,) + moe_w_down.shape[2:])

    n_rows = -(-(bsz + 1) // 8) * 8
    cc = jnp.zeros((n_rows, d), F32).at[:bsz].set(c).at[bsz].set(c_ctx)
    mod = _mod_call(cc, w_mod, b_mod)

    tm_lat = _pick_tile(seq, 512)
    tm_ctx = _pick_tile(clen, 512)
    tm_ffn = _pick_tile(seq, 1024)
    tm_ffn_ctx = _pick_tile(bsz * clen, 1024)
    lat_row = lambda tm: (lambda i: i // (seq // tm))
    ctx_row = lambda i: bsz
    rope_lat = tuple(jnp.asarray(t) for t in _rope_tables(seq, True))
    rope_ctx = tuple(jnp.asarray(t) for t in _rope_tables(clen, False))
    head_avg = np.kron(np.eye(QK_W // HEAD_DIM), np.full((HEAD_DIM, HEAD_DIM), 1.0 / HEAD_DIM))
    head_avg = jnp.asarray(head_avg, BF16)

    xl = x.reshape(bsz * seq, d)
    xc = ctx.reshape(bsz * clen, d)
    for l in range(depth):
        last = l == depth - 1
        mod3 = mod[l].reshape(n_rows * 6, 1, d)
        w_in_bf = w_in[l].astype(BF16)
        w_out_bf = w_out[l].astype(BF16)
        gain = jnp.concatenate([jnp.tile(q_gain[l] * (ATTN_SCALE * LOG2E), N_Q_HEADS),
                                jnp.tile(k_gain[l], N_KV_HEADS)]).reshape(1, QK_W)
        shift = (1.02 * HEAD_DIM * ATTN_SCALE * LOG2E) * jnp.max(jnp.abs(q_gain[l])) * jnp.max(jnp.abs(k_gain[l]))
        lane64 = (jnp.arange(LANES) == HEAD_DIM).astype(F32)
        ext = jnp.stack([lane64, -shift * lane64, lane64])
        fast = (shift <= ATTN_SAFE_SHIFT).astype(jnp.int32).reshape(1)
        filt_w = (hy_f_w1[l], hy_f_b1[l], hy_f_freq[l], hy_f_w2[l], hy_f_b2[l], hy_f_w3[l])
        j = l // 2
        routed = l % 2 == 1
        if routed:
            ffn_w = (router_w[j], moe_wg, moe_wu, moe_wd, j * n_exp, n_exp)
        else:
            ffn_w = (None, ffn_w_gate, ffn_w_up, ffn_w_down, j, 1)
        fc = 256

        q_c, k_c, v_c, hp_c = _proj_call(xc, mod3, ctx_row, clen, w_in_bf, gain, head_avg, rope_ctx, ext, tm_ctx)
        k_c3 = k_c.reshape(bsz, clen, -1)
        v_c3 = v_c.reshape(bsz, clen, -1)
        if not last:
            hp_c3 = hp_c.reshape(bsz, clen, -1)
            att_c = _attn_call(fast, q_c.reshape(bsz, clen, -1), (k_c3, v_c3), _pick_tile(clen, 256))
            spec_c = _ctx_spectrum(_filter_call(clen, *filt_w), clen)
            hy_c = _ctx_hyena_call(hp_c3, hy_conv_w[l], hy_conv_b[l], spec_c, hy_d[l])
            pl_c = _pool_call(hp_c3, pool_w[l], pool_scale[l])
            xc_mid = _outproj_call(xc, att_c.reshape(bsz * clen, D_ATTN), hy_c.reshape(bsz * clen, D_HYENA),
                                   pl_c.reshape(bsz * clen, D_POOL), mod3, ctx_row, w_out_bf,
                                   ln1_g[l], ln1_b[l], alpha, tm_ctx)

        q_l, k_l, v_l, hp_l = _proj_call(xl, mod3, lat_row(tm_lat), seq, w_in_bf, gain, head_avg, rope_lat, ext,
                                         tm_lat)
        hp_l3 = hp_l.reshape(bsz, seq, -1)
        att = _attn_call(fast, q_l.reshape(bsz, seq, -1),
                         (k_l.reshape(bsz, seq, -1), v_l.reshape(bsz, seq, -1), k_c3, v_c3), _pick_tile(seq, 256))
        spec = _hyena_spectrum(_filter_call(seq, *filt_w), seq)
        hyo = _hyena_latent(hp_l3, hy_conv_w[l], hy_conv_b[l], spec, hy_d[l])
        plo = _pool_call(hp_l3, pool_w[l], pool_scale[l])
        xl = _outproj_call(xl, att.reshape(bsz * seq, D_ATTN), hyo.reshape(bsz * seq, D_HYENA),
                           plo.reshape(bsz * seq, D_POOL), mod3, lat_row(tm_lat), w_out_bf,
                           ln1_g[l], ln1_b[l], alpha, tm_lat)
        if routed:
            xl = _moe_call(xl, mod3, lambda t: t // seq, router_w[j], moe_wg, moe_wu, moe_wd, j * n_exp,
                           ln2_g[l], ln2_b[l], alpha, MOE_FC)
        else:
            xl = _ffn_call(xl, mod3, lat_row(tm_ffn), *ffn_w, ln2_g[l], ln2_b[l], alpha, tm_ffn, fc)

        if not last:
            xc = _ffn_call(xc_mid, mod3, ctx_row, *ffn_w, ln2_g[l], ln2_b[l], alpha, tm_ffn_ctx, fc)
    return xl.reshape(bsz, seq, d)
```

```python
import functools
import math

import numpy as np
import jax
import jax.numpy as jnp
from jax import lax
from jax.experimental import pallas as pl
from jax.experimental.pallas import tpu as pltpu

F32 = jnp.float32
BF16 = jnp.bfloat16
HIGHEST = lax.Precision.HIGHEST

GRID_W = 64
HEAD_DIM = 64
N_Q_HEADS = 8
N_KV_HEADS = 2
GQA_GROUP = N_Q_HEADS // N_KV_HEADS
D_ATTN = N_Q_HEADS * HEAD_DIM
KV_W = N_KV_HEADS * HEAD_DIM
QK_W = D_ATTN + KV_W
ATTN_SCALE = HEAD_DIM ** -0.5
ROPE_THETA = 10000.0
ROPE_AXIS_DIM = HEAD_DIM // 2
QK_EPS = 1e-6
D_HYENA = 256
HYENA_ORDER = 2
HYENA_BANDS = 16
HYENA_EMB = 1 + 2 * HYENA_BANDS
HYENA_DECAY_TARGET = 1e-2
HYENA_FAST_DECAY_PCT = 0.3
HYENA_SLOW_DECAY_PCT = 1.5
D_POOL = 256
POOL_WINDOWS = (2, 4, 8, 16)
POOL_GROUP = D_POOL // len(POOL_WINDOWS)
POOL_PAD = 8
TOP_K = 2
LN_EPS = 1e-5
ADALN_EPS = 1e-6
LOG2E = math.log2(math.e)

LANES = 128
FFT_N2 = 64
FFT_UNROLL = 4
VMEM_LIMIT = 56 * 1024 * 1024


def _cparams():
    return pltpu.CompilerParams(vmem_limit_bytes=VMEM_LIMIT)


def _layernorm(x, eps):
    mu = jnp.mean(x, axis=-1, keepdims=True)
    xc = x - mu
    var = jnp.mean(xc * xc, axis=-1, keepdims=True)
    return xc * lax.rsqrt(var + eps)


def _mod_kernel(c_ref, w_ref, b_ref, o_ref):
    c = c_ref[...]
    s = c * jax.nn.sigmoid(c)
    o_ref[0] = jnp.dot(s, w_ref[0], preferred_element_type=F32, precision=HIGHEST) + b_ref[0]


def _mod_call(cc, w_mod, b_mod):
    depth, d, d6 = w_mod.shape
    r = cc.shape[0]
    tn = 1536
    return pl.pallas_call(
        _mod_kernel,
        grid=(depth, d6 // tn),
        in_specs=[pl.BlockSpec((r, d), lambda l, j: (0, 0)),
                  pl.BlockSpec((1, d, tn), lambda l, j: (l, 0, j)),
                  pl.BlockSpec((1, 1, tn), lambda l, j: (l, 0, j))],
        out_specs=pl.BlockSpec((1, r, tn), lambda l, j: (l, 0, j)),
        out_shape=jax.ShapeDtypeStruct((depth, r, d6), F32),
        compiler_params=_cparams(),
        name="adaln_mod",
    )(cc, w_mod, b_mod.reshape(depth, 1, d6))


def _rope_tables(seq_len, rope):
    lane = np.arange(LANES)
    d = lane % HEAD_DIM
    if not rope:
        one = np.ones((seq_len, LANES), np.float32)
        zero = np.zeros((seq_len, LANES), np.float32)
        return one, zero, zero
    t = np.arange(seq_len)
    row = (t // GRID_W).astype(np.float64)
    col = (t % GRID_W).astype(np.float64)
    half = ROPE_AXIS_DIM // 2
    inv = ROPE_THETA ** (-np.arange(0, ROPE_AXIS_DIM, 2, dtype=np.float64) / ROPE_AXIS_DIM)
    pos = np.where((d // ROPE_AXIS_DIM)[None, :] == 0, row[:, None], col[:, None])
    ang = pos * inv[d % half][None, :]
    is_b = ((d % ROPE_AXIS_DIM) >= half)[None, :]
    cos = np.cos(ang)
    sin = np.sin(ang)
    s_up = np.where(is_b, sin, 0.0)
    s_dn = np.where(is_b, 0.0, -sin)
    return cos.astype(np.float32), s_up.astype(np.float32), s_dn.astype(np.float32)


def _store_padded_heads(blk, ext, out_ref, col0):
    lo = lax.broadcasted_iota(jnp.int32, blk.shape, 1) < HEAD_DIM
    out_ref[:, col0:col0 + LANES] = jnp.where(lo, blk, ext).astype(out_ref.dtype)
    out_ref[:, col0 + LANES:col0 + 2 * LANES] = jnp.where(lo, pltpu.roll(blk, HEAD_DIM, axis=1),
                                                          ext).astype(out_ref.dtype)


def _proj_kernel(x_ref, sh_ref, sc_ref, w_ref, gain_ref, bd_ref, cos_ref, sup_ref, sdn_ref, ext_ref,
                 q_ref, k_ref, v_ref, hp_ref):
    u = _layernorm(x_ref[...], ADALN_EPS) * (1.0 + sc_ref[0]) + sh_ref[0]
    p = jnp.dot(u.astype(BF16), w_ref[...], preferred_element_type=F32)
    qk = p[:, :QK_W]
    ms = jnp.dot((qk * qk).astype(BF16), bd_ref[...], preferred_element_type=F32)
    qn = qk * lax.rsqrt(ms + QK_EPS) * gain_ref[...]
    cos = cos_ref[...]
    sup = sup_ref[...]
    sdn = sdn_ref[...]
    n_qblk = D_ATTN // LANES
    for j in range(QK_W // LANES):
        blk = qn[:, j * LANES:(j + 1) * LANES]
        up = pltpu.roll(blk, ROPE_AXIS_DIM // 2, axis=1)
        dn = pltpu.roll(blk, LANES - ROPE_AXIS_DIM // 2, axis=1)
        rot = blk * cos + up * sup + dn * sdn
        if j < n_qblk:
            _store_padded_heads(rot, ext_ref[0:1, :], q_ref, 2 * j * LANES)
        else:
            _store_padded_heads(rot, ext_ref[1:2, :], k_ref, 2 * (j - n_qblk) * LANES)
    _store_padded_heads(p[:, QK_W:QK_W + KV_W], ext_ref[2:3, :], v_ref, 0)
    hp_ref[...] = p[:, QK_W + KV_W:]


def _proj_call(x2, mod3, mod_row, seq_len, w_in_bf, gain, bd, tables, ext, tm):
    r, d = x2.shape
    d_in = w_in_bf.shape[1]
    d_hp = d_in - QK_W - KV_W
    nseq = seq_len // tm
    cos, sup, sdn = tables
    tab_spec = pl.BlockSpec((tm, LANES), lambda i: (i % nseq, 0))
    return pl.pallas_call(
        _proj_kernel,
        grid=(r // tm,),
        in_specs=[pl.BlockSpec((tm, d), lambda i: (i, 0)),
                  pl.BlockSpec((1, 1, d), lambda i: (mod_row(i) * 6 + 0, 0, 0)),
                  pl.BlockSpec((1, 1, d), lambda i: (mod_row(i) * 6 + 1, 0, 0)),
                  pl.BlockSpec((d, d_in), lambda i: (0, 0)),
                  pl.BlockSpec((1, QK_W), lambda i: (0, 0)),
                  pl.BlockSpec((QK_W, QK_W), lambda i: (0, 0)),
                  tab_spec, tab_spec, tab_spec,
                  pl.BlockSpec((3, LANES), lambda i: (0, 0))],
        out_specs=[pl.BlockSpec((tm, N_Q_HEADS * LANES), lambda i: (i, 0)),
                   pl.BlockSpec((tm, N_KV_HEADS * LANES), lambda i: (i, 0)),
                   pl.BlockSpec((tm, N_KV_HEADS * LANES), lambda i: (i, 0)),
                   pl.BlockSpec((tm, d_hp), lambda i: (i, 0))],
        out_shape=[jax.ShapeDtypeStruct((r, N_Q_HEADS * LANES), BF16),
                   jax.ShapeDtypeStruct((r, N_KV_HEADS * LANES), BF16),
                   jax.ShapeDtypeStruct((r, N_KV_HEADS * LANES), BF16),
                   jax.ShapeDtypeStruct((r, d_hp), F32)],
        compiler_params=_cparams(),
        name="ln_mod_in_proj",
    )(x2, mod3, mod3, w_in_bf, gain, bd, cos, sup, sdn, ext)


ATTN_KEY_CHUNK = 1024
ATTN_SAFE_SHIFT = 60.0


def _attn_kernel(fast_ref, q_ref, *refs):
    o_ref = refs[-1]
    sources = [(refs[i], refs[i + 1]) for i in range(0, len(refs) - 1, 2)]
    tq = q_ref.shape[0]
    chunks = [(k_ref, v_ref, c0, min(ATTN_KEY_CHUNK, k_ref.shape[0] - c0))
              for k_ref, v_ref in sources for c0 in range(0, k_ref.shape[0], ATTN_KEY_CHUNK)]

    def run(running_max):
        for h in range(N_KV_HEADS):
            qs = jnp.concatenate([q_ref[:, (h * GQA_GROUP + g) * LANES:(h * GQA_GROUP + g + 1) * LANES]
                                  for g in range(GQA_GROUP)], axis=0)
            acc = jnp.zeros((GQA_GROUP * tq, LANES), F32)
            m = jnp.full((GQA_GROUP * tq, 1), -jnp.inf, F32)
            for k_ref, v_ref, c0, kw in chunks:
                kc = k_ref[c0:c0 + kw, h * LANES:(h + 1) * LANES]
                vc = v_ref[c0:c0 + kw, h * LANES:(h + 1) * LANES]
                s = lax.dot_general(qs, kc, (((1,), (1,)), ((), ())), preferred_element_type=F32)
                if running_max:
                    m_new = jnp.maximum(m, jnp.max(s, axis=-1, keepdims=True))
                    acc = acc * jnp.exp2(m - m_new)
                    s = s - m_new
                    m = m_new
                acc = acc + jnp.dot(jnp.exp2(s).astype(BF16), vc, preferred_element_type=F32)
            o = acc[:, :HEAD_DIM] / acc[:, HEAD_DIM:HEAD_DIM + 1]
            for g in range(GQA_GROUP):
                c0 = (h * GQA_GROUP + g) * HEAD_DIM
                o_ref[:, c0:c0 + HEAD_DIM] = o[g * tq:(g + 1) * tq].astype(o_ref.dtype)

    @pl.when(fast_ref[0] > 0)
    def _():
        run(False)

    @pl.when(fast_ref[0] == 0)
    def _():
        run(True)


def _attn_call(fast, q, kv, tq):
    b, lq, qw = q.shape
    kv_specs = [pl.BlockSpec((None,) + a.shape[1:], lambda i, j, f: (i, 0, 0)) for a in kv]
    return pl.pallas_call(
        _attn_kernel,
        grid_spec=pltpu.PrefetchScalarGridSpec(
            num_scalar_prefetch=1,
            grid=(b, lq // tq),
            in_specs=[pl.BlockSpec((None, tq, qw), lambda i, j, f: (i, j, 0))] + kv_specs,
            out_specs=pl.BlockSpec((None, tq, D_ATTN), lambda i, j, f: (i, j, 0))),
        out_shape=jax.ShapeDtypeStruct((b, lq, D_ATTN), BF16),
        compiler_params=_cparams(),
        name="attention",
    )(fast, q, *kv)


def _filter_consts(seq_len):
    t = np.linspace(0.0, 1.0, seq_len, dtype=np.float32).astype(np.float64)[:, None]
    omega = 2.0 * math.pi * np.arange(seq_len, dtype=np.float64)[:, None] / seq_len
    bands = np.linspace(1e-4, HYENA_BANDS - 1, HYENA_BANDS, dtype=np.float32).astype(np.float64)[None, :]
    feats = np.concatenate([t, np.cos(omega * bands), -np.sin(omega * bands)], axis=-1)
    feats = np.pad(feats, ((0, 0), (0, LANES - HYENA_EMB)))
    max_decay = math.log(HYENA_DECAY_TARGET) / HYENA_FAST_DECAY_PCT
    min_decay = math.log(HYENA_DECAY_TARGET) / HYENA_SLOW_DECAY_PCT
    deltas = np.linspace(min_decay, max_decay, D_HYENA, dtype=np.float32).astype(np.float64)
    decay = np.exp(-t * np.abs(deltas)[None, :])
    decay = np.concatenate([decay, decay], axis=1)
    return feats.astype(np.float32), decay.astype(np.float32)


def _filter_kernel(feat_ref, w1_ref, b1_ref, fr_ref, w2_ref, b2_ref, w3_ref, dec_ref, o_ref):
    fr = fr_ref[...]
    h = jnp.sin(fr * (jnp.dot(feat_ref[...], w1_ref[...], preferred_element_type=F32, precision=HIGHEST)
                      + b1_ref[...]))
    h = jnp.sin(fr * (jnp.dot(h, w2_ref[...], preferred_element_type=F32, precision=HIGHEST) + b2_ref[...]))
    h = jnp.dot(h, w3_ref[...], preferred_element_type=F32, precision=HIGHEST) * dec_ref[...]
    tot = jnp.sum(jnp.abs(h), axis=0, keepdims=True)
    tot = tot[:, :D_HYENA] + tot[:, D_HYENA:]
    inv = 1.0 / tot
    o_ref[...] = h * jnp.concatenate([inv, inv], axis=1)


def _filter_call(seq_len, w1, b1, freq, w2, b2, w3):
    feats, decay = _filter_consts(seq_len)
    hid = w2.shape[0]
    w1p = jnp.pad(w1, ((0, LANES - HYENA_EMB), (0, 0)))
    blk = 2 * D_HYENA
    full = lambda shape: pl.BlockSpec(shape, lambda o: (0,) * len(shape))
    return pl.pallas_call(
        _filter_kernel,
        grid=(HYENA_ORDER,),
        in_specs=[full((seq_len, LANES)), full((LANES, hid)), full((1, hid)), full((1, hid)),
                  full((hid, hid)), full((1, hid)),
                  pl.BlockSpec((hid, blk), lambda o: (0, o)),
                  full((seq_len, blk))],
        out_specs=pl.BlockSpec((seq_len, blk), lambda o: (0, o)),
        out_shape=jax.ShapeDtypeStruct((seq_len, HYENA_ORDER * blk), F32),
        compiler_params=_cparams(),
        name="hyena_filter",
    )(jnp.asarray(feats), w1p, b1.reshape(1, hid), freq.reshape(1, hid), w2, b2.reshape(1, hid), w3,
      jnp.asarray(decay))


def _filter_buffer(filt, seq_len):
    f = filt.reshape(seq_len, HYENA_ORDER, 2, D_HYENA)
    fwd, bwd = f[:, :, 0], f[:, :, 1]
    buf = jnp.concatenate([fwd, jnp.zeros((1,) + fwd.shape[1:], F32), bwd[:0:-1]], axis=0)
    return buf.reshape(2 * seq_len, HYENA_ORDER * D_HYENA)


@functools.lru_cache(maxsize=None)
def _fft_consts(seq_len):
    n = 2 * seq_len
    n2 = FFT_N2
    n1 = n // n2
    h1 = n1 // 2
    k1 = np.arange(n1)
    w1 = np.exp(-2j * np.pi * np.outer(k1, np.arange(n1)) / n1)
    f1 = np.zeros((n1, 2, 2 * h1))
    f1[:, 0, :h1], f1[:, 0, h1:] = w1[:, :h1].real, -w1[:, :h1].imag
    f1[:, 1, :h1], f1[:, 1, h1:] = w1[:, :h1].imag, w1[:, :h1].real
    f1 = f1.reshape(2 * n1, 2 * h1)
    f1r = np.stack([w1.real, w1.imag], axis=1).reshape(2 * n1, n1)
    a2 = np.arange(n2)
    tw = np.exp(-2j * np.pi * np.outer(k1, a2) / n)
    w2 = np.exp(-2j * np.pi * np.outer(a2, a2) / n2)
    m = w2[None, :, :] * tw[:, None, :]
    ff = np.concatenate([np.concatenate([m.real, -m.imag], axis=2),
                         np.concatenate([m.imag, m.real], axis=2)], axis=1)
    g = np.conj(w2.T)[None, :, :] * np.conj(tw)[:, :, None]
    gi = np.concatenate([np.concatenate([g.real, -g.imag], axis=2),
                         np.concatenate([g.imag, g.real], axis=2)], axis=1)
    v = np.conj(w1[:, :h1]).T
    f1i = np.zeros((2, h1, n1, 2))
    f1i[0, :, :, 0], f1i[0, :, :, 1] = v.real, -v.imag
    f1i[1, :, :, 0], f1i[1, :, :, 1] = v.imag, v.real
    f1i = f1i.reshape(2 * h1, 2 * n1)
    il = np.arange(2 * n2).reshape(2, n2).T.reshape(-1)
    ff_il = ff[:, :, il]
    gi_il = gi[:, il, :]
    f32 = lambda a: np.asarray(a, np.float32)
    return dict(n=n, n1=n1, h1=h1, f1=f32(f1), f1r=f32(f1r), ff=f32(ff), ff_il=f32(ff_il), gi_il=f32(gi_il),
                f1i=f32(f1i))


def _fftr_kernel(x_ref, f_ref, o_ref):
    n1 = f_ref.shape[1]

    def body(n2, carry):
        x = x_ref[pl.ds(n2, n1, stride=FFT_N2), :]
        o_ref[pl.ds(n2, 2 * n1, stride=FFT_N2), :] = jnp.dot(f_ref[...], x, preferred_element_type=F32,
                                                            precision=HIGHEST)
        return carry

    lax.fori_loop(0, FFT_N2, body, 0, unroll=FFT_UNROLL)


def _fftr_call(buf, f1r):
    n, cols = buf.shape
    rows = f1r.shape[0] * FFT_N2
    return pl.pallas_call(
        _fftr_kernel,
        grid=(cols // LANES,),
        in_specs=[pl.BlockSpec((n, LANES), lambda j: (0, j)),
                  pl.BlockSpec(f1r.shape, lambda j: (0, 0))],
        out_specs=pl.BlockSpec((None, rows, LANES), lambda j: (j, 0, 0)),
        out_shape=jax.ShapeDtypeStruct((cols // LANES, rows, LANES), F32),
        compiler_params=_cparams(),
        name="fft_filter_outer_dft",
    )(buf, f1r)


def _spec_kernel(a_ref, ff_ref, o_ref, *, scale):
    for kk in range(a_ref.shape[1]):
        for cb in range(a_ref.shape[0]):
            o_ref[kk, :, cb * LANES:(cb + 1) * LANES] = jnp.dot(
                ff_ref[kk], a_ref[cb, kk], preferred_element_type=F32, precision=HIGHEST) * scale


def _spec_call(a4, ff, scale, kt):
    nblk, n1, r, _ = a4.shape
    return pl.pallas_call(
        functools.partial(_spec_kernel, scale=scale),
        grid=(n1 // kt,),
        in_specs=[pl.BlockSpec((nblk, kt, r, LANES), lambda i: (0, i, 0, 0)),
                  pl.BlockSpec((kt, r, r), lambda i: (i, 0, 0))],
        out_specs=pl.BlockSpec((kt, r, nblk * LANES), lambda i: (i, 0, 0)),
        out_shape=jax.ShapeDtypeStruct((n1, r, nblk * LANES), F32),
        compiler_params=_cparams(),
        name="fft_filter_spectrum",
    )(a4, ff)


def _ffta_kernel(z_ref, f_ref, o_ref):
    n1 = f_ref.shape[0] // 2
    h1 = f_ref.shape[1] // 2

    def body(n2, carry):
        rows = pl.ds(n2, h1, stride=FFT_N2)
        x = jnp.concatenate([z_ref[0, rows, :], z_ref[1, rows, :]], axis=0).astype(BF16)
        a = jnp.dot(f_ref[...], x, preferred_element_type=F32).astype(BF16)
        o_ref[pl.ds(n2, n1, stride=FFT_N2), :] = pltpu.bitcast(a, jnp.uint32)
        return carry

    lax.fori_loop(0, FFT_N2, body, 0, unroll=FFT_UNROLL)


def _ffta_call(z4, base, groups, f1_bf):
    _, _, l, c = z4.shape
    halves = c // LANES
    n1 = f1_bf.shape[0] // 2
    return pl.pallas_call(
        _ffta_kernel,
        grid=(groups, halves),
        in_specs=[pl.BlockSpec((None, 2, l, LANES), lambda i, j: (base + i, 0, 0, j)),
                  pl.BlockSpec(f1_bf.shape, lambda i, j: (0, 0))],
        out_specs=pl.BlockSpec((None, None, n1 * FFT_N2, LANES), lambda i, j: (i, j, 0, 0)),
        out_shape=jax.ShapeDtypeStruct((groups, halves, n1 * FFT_N2, LANES), jnp.uint32),
        compiler_params=_cparams(),
        name="fft_outer_dft",
    )(z4, f1_bf)


def _fftb_kernel(a_ref, ff_ref, gi_ref, h_ref, o_ref):
    groups, halves, kt = a_ref.shape[:3]
    half = FFT_N2

    def body(kk, carry):
        ff = ff_ref[kk]
        gi = gi_ref[kk]
        hr = h_ref[kk, :half, :]
        hi = h_ref[kk, half:, :]
        for g in range(groups):
            a = jnp.concatenate([pltpu.bitcast(a_ref[g, hh, kk], BF16) for hh in range(halves)], axis=1)
            xh = jnp.dot(ff, a, preferred_element_type=F32)
            xr, xi = xh[:half], xh[half:]
            y = jnp.concatenate([xr * hr - xi * hi, xr * hi + xi * hr], axis=0).astype(BF16)
            b = jnp.dot(gi, y, preferred_element_type=F32).astype(BF16)
            for hh in range(halves):
                o_ref[g, hh, kk] = pltpu.bitcast(b[:, hh * LANES:(hh + 1) * LANES], jnp.uint32)
        return carry

    lax.fori_loop(0, kt, body, 0)


def _fftb_call(a5, ff_bf, gi_bf, spec, order, kt):
    g, halves, n1, n2, _ = a5.shape
    r = 2 * n2
    return pl.pallas_call(
        _fftb_kernel,
        grid=(n1 // kt,),
        in_specs=[pl.BlockSpec((g, halves, kt, n2, LANES), lambda i: (0, 0, i, 0, 0)),
                  pl.BlockSpec((kt, r, r), lambda i: (i, 0, 0)),
                  pl.BlockSpec((kt, r, r), lambda i: (i, 0, 0)),
                  pl.BlockSpec((kt, r, halves * LANES), lambda i: (i, 0, order))],
        out_specs=pl.BlockSpec((g, halves, kt, n2, LANES), lambda i: (0, 0, i, 0, 0)),
        out_shape=jax.ShapeDtypeStruct(a5.shape, jnp.uint32),
        compiler_params=_cparams(),
        name="fft_inner_conv",
    )(a5, ff_bf, gi_bf, spec)


def _fftc_kernel(b_ref, f_ref, z_ref, g_ref, d_ref, o_ref):
    n1 = f_ref.shape[1] // 2
    h1 = f_ref.shape[0] // 2
    d = d_ref[...]

    def body(n2, carry):
        b = pltpu.bitcast(b_ref[pl.ds(n2, n1, stride=FFT_N2), :], BF16)
        y = jnp.dot(f_ref[...], b, preferred_element_type=F32)
        rows = pl.ds(n2, h1, stride=FFT_N2)
        for m in range(2):
            o_ref[m, rows, :] = y[m * h1:(m + 1) * h1]
        return carry

    lax.fori_loop(0, FFT_N2, body, 0, unroll=FFT_UNROLL)
    o_ref[...] = g_ref[...] * (o_ref[...] + d * z_ref[...])


def _fftc_call(b4, f1i_bf, z4, z_base, g4, g_base, d2, order):
    groups, halves, rows, _ = b4.shape
    _, _, l, c = z4.shape
    pair = lambda base: pl.BlockSpec((None, 2, l, LANES), lambda i, j: (base + i, 0, 0, j))
    return pl.pallas_call(
        _fftc_kernel,
        grid=(groups, halves),
        in_specs=[pl.BlockSpec((None, None, rows, LANES), lambda i, j: (i, j, 0, 0)),
                  pl.BlockSpec(f1i_bf.shape, lambda i, j: (0, 0)),
                  pair(z_base), pair(g_base),
                  pl.BlockSpec((None, 1, LANES), lambda i, j: (order, 0, j))],
        out_specs=pair(0),
        out_shape=jax.ShapeDtypeStruct((groups, 2, l, c), F32),
        compiler_params=_cparams(),
        name="fft_outer_inverse_gate",
    )(b4, f1i_bf, z4, g4, d2.reshape(d2.shape[0], 1, d2.shape[1]))


def _sconv_kernel(z_ref, w_ref, b_ref, o_ref):
    z = z_ref[...]
    n = z.shape[0]
    row = lax.broadcasted_iota(jnp.int32, z.shape, 0)
    prev = jnp.where(row == 0, 0.0, pltpu.roll(z, 1, axis=0))
    nxt = jnp.where(row == n - 1, 0.0, pltpu.roll(z, n - 1, axis=0))
    o_ref[...] = prev * w_ref[0:1, :] + z * w_ref[1:2, :] + nxt * w_ref[2:3, :] + b_ref[...]


def _sconv_call(hp3, conv_w, conv_b):
    b, l, _ = hp3.shape
    per = D_HYENA // LANES
    nblk = 3 * per
    return pl.pallas_call(
        _sconv_kernel,
        grid=(b, nblk),
        in_specs=[pl.BlockSpec((None, l, LANES), lambda i, j: (i, 0, j)),
                  pl.BlockSpec((3, LANES), lambda i, j: (0, j)),
                  pl.BlockSpec((1, LANES), lambda i, j: (0, j))],
        out_specs=pl.BlockSpec((None, None, l, LANES), lambda i, j: (j // per, i, 0, j % per)),
        out_shape=jax.ShapeDtypeStruct((3, b, l, D_HYENA), F32),
        compiler_params=_cparams(),
        name="hyena_short_conv",
    )(hp3, conv_w, conv_b.reshape(1, -1))


def _hyena_spectrum(filt, seq_len):
    cst = _fft_consts(seq_len)
    n, n1 = cst["n"], cst["n1"]
    buf = _filter_buffer(filt, seq_len)
    a = _fftr_call(buf, jnp.asarray(cst["f1r"]))
    a4 = a.reshape(a.shape[0], n1, 2 * FFT_N2, LANES)
    return _spec_call(a4, jnp.asarray(cst["ff"]), 1.0 / n, min(n1, 8))


def _hyena_latent(hp3, conv_w, conv_b, spec, d):
    b, l, _ = hp3.shape
    cst = _fft_consts(l)
    n1, h1 = cst["n1"], cst["h1"]
    groups = b // 2
    halves = D_HYENA // LANES
    kt = min(n1, 8)
    f1 = jnp.asarray(cst["f1"], BF16)
    ff = jnp.asarray(cst["ff_il"], BF16)
    gi = jnp.asarray(cst["gi_il"], BF16)
    f1i = jnp.asarray(cst["f1i"], BF16)
    zc = _sconv_call(hp3, conv_w, conv_b).reshape(3 * groups, 2, l, D_HYENA)
    z, z_base = zc, 0
    for o in range(HYENA_ORDER):
        a = _ffta_call(z, z_base, groups, f1)
        bb = _fftb_call(a.reshape(groups, halves, n1, FFT_N2, LANES), ff, gi, spec, o, kt)
        z = _fftc_call(bb.reshape(groups, halves, n1 * FFT_N2, LANES), f1i, z, z_base, zc, (1 + o) * groups, d, o)
        z_base = 0
    return z.reshape(b, l, D_HYENA)


@functools.lru_cache(maxsize=None)
def _dft_consts(seq_len):
    n = 2 * seq_len
    w = np.exp(-2j * np.pi * np.outer(np.arange(n), np.arange(n)) / n)
    fwd_full = np.concatenate([w.real, w.imag], axis=0)
    fwd = fwd_full[:, :seq_len]
    wi = np.conj(w[:, :seq_len]).T
    inv = np.concatenate([wi.real, -wi.imag], axis=1)
    f32 = lambda a: np.asarray(a, np.float32)
    return dict(n=n, fwd_full=f32(fwd_full), fwd=f32(fwd), inv=f32(inv))


def _cspec_kernel(buf_ref, f_ref, o_ref, *, scale):
    o_ref[...] = jnp.dot(f_ref[...], buf_ref[...], preferred_element_type=F32, precision=HIGHEST) * scale


def _ctx_spectrum(filt, seq_len):
    cst = _dft_consts(seq_len)
    buf = _filter_buffer(filt, seq_len)
    n = cst["n"]
    return pl.pallas_call(
        functools.partial(_cspec_kernel, scale=1.0 / n),
        out_shape=jax.ShapeDtypeStruct((2 * n, buf.shape[1]), F32),
        compiler_params=_cparams(),
        name="ctx_filter_spectrum",
    )(buf, jnp.asarray(cst["fwd_full"]))


def _ctx_hyena_kernel(z_ref, w_ref, b_ref, f_ref, fi_ref, h_ref, d_ref, o_ref):
    z = z_ref[...]
    l = z.shape[0]
    n = 2 * l
    row = lax.broadcasted_iota(jnp.int32, z.shape, 0)
    prev = jnp.where(row == 0, 0.0, pltpu.roll(z, 1, axis=0))
    nxt = jnp.where(row == l - 1, 0.0, pltpu.roll(z, l - 1, axis=0))
    zc = prev * w_ref[0:1, :] + z * w_ref[1:2, :] + nxt * w_ref[2:3, :] + b_ref[...]
    cur = zc[:, :D_HYENA]
    for o in range(HYENA_ORDER):
        gate = zc[:, (1 + o) * D_HYENA:(2 + o) * D_HYENA]
        xh = jnp.dot(f_ref[...], cur.astype(BF16), preferred_element_type=F32)
        xr, xi = xh[:n], xh[n:]
        hr = h_ref[:n, o * D_HYENA:(o + 1) * D_HYENA]
        hi = h_ref[n:, o * D_HYENA:(o + 1) * D_HYENA]
        y = jnp.concatenate([xr * hr - xi * hi, xr * hi + xi * hr], axis=0).astype(BF16)
        conv = jnp.dot(fi_ref[...], y, preferred_element_type=F32)
        cur = gate * (conv + d_ref[o:o + 1, :] * cur)
    o_ref[...] = cur


def _ctx_hyena_call(hp3, conv_w, conv_b, spec, d):
    b, l, _ = hp3.shape
    cst = _dft_consts(l)
    n = cst["n"]
    w3 = 3 * D_HYENA
    full = lambda shape: pl.BlockSpec(shape, lambda i: (0,) * len(shape))
    return pl.pallas_call(
        _ctx_hyena_kernel,
        grid=(b,),
        in_specs=[pl.BlockSpec((None, l, w3), lambda i: (i, 0, 0)),
                  full((3, w3)), full((1, w3)), full((2 * n, l)), full((l, 2 * n)),
                  full((2 * n, HYENA_ORDER * D_HYENA)), full((HYENA_ORDER, D_HYENA))],
        out_specs=pl.BlockSpec((None, l, D_HYENA), lambda i: (i, 0, 0)),
        out_shape=jax.ShapeDtypeStruct((b, l, D_HYENA), F32),
        compiler_params=_cparams(),
        name="ctx_hyena",
    )(hp3, conv_w, conv_b.reshape(1, -1), jnp.asarray(cst["fwd"], BF16), jnp.asarray(cst["inv"], BF16),
      spec, d)


def _pool_kernel(p_ref, w_ref, sc_ref, o_ref):
    x = p_ref[...]
    l, c = x.shape
    half = pl.program_id(1)
    zpad = jnp.zeros((POOL_PAD, c), F32)
    xp = jnp.concatenate([zpad, x, zpad], axis=0)
    tot = l + 2 * POOL_PAD
    sums = []
    f = xp
    for win in POOL_WINDOWS:
        f = f + pltpu.roll(f, tot - win // 2, axis=0)
        sums.append(pltpu.roll(f, win // 2, axis=0)[POOL_PAD:POOL_PAD + l])
    t = lax.broadcasted_iota(jnp.int32, (l, c), 0)
    grp = lax.broadcasted_iota(jnp.int32, (l, c), 1) // POOL_GROUP + half * (LANES // POOL_GROUP)
    total = sums[-1]
    cnt = None
    for gi in range(len(POOL_WINDOWS) - 1, -1, -1):
        win = POOL_WINDOWS[gi]
        cw = (jnp.minimum(t + (win - win // 2), l) - jnp.maximum(t - win // 2, 0)).astype(F32)
        if cnt is None:
            cnt = cw
        else:
            sel = grp == gi
            total = jnp.where(sel, sums[gi], total)
            cnt = jnp.where(sel, cw, cnt)
    dlt = total / cnt - x
    y = jnp.dot(dlt.astype(BF16), w_ref[0], preferred_element_type=F32)
    o_ref[...] = y * sc_ref[...]


def _pool_call(hp3, pool_w, pool_scale):
    b, l, w = hp3.shape
    nh = D_POOL // LANES
    first = (w - D_POOL) // LANES
    per = LANES // POOL_GROUP
    wbd = jnp.zeros((nh, LANES, LANES), F32)
    for g in range(len(POOL_WINDOWS)):
        r0 = (g % per) * POOL_GROUP
        wbd = wbd.at[g // per, r0:r0 + POOL_GROUP, r0:r0 + POOL_GROUP].set(pool_w[g])
    return pl.pallas_call(
        _pool_kernel,
        grid=(b, nh),
        in_specs=[pl.BlockSpec((None, l, LANES), lambda i, j: (i, 0, first + j)),
                  pl.BlockSpec((1, LANES, LANES), lambda i, j: (j, 0, 0)),
                  pl.BlockSpec((1, LANES), lambda i, j: (0, j))],
        out_specs=pl.BlockSpec((None, l, LANES), lambda i, j: (i, 0, j)),
        out_shape=jax.ShapeDtypeStruct((b, l, D_POOL), F32),
        compiler_params=_cparams(),
        name="pool_mixer",
    )(hp3, wbd.astype(BF16), pool_scale.reshape(1, D_POOL))


def _outproj_kernel(x_ref, att_ref, hy_ref, pl_ref, gate_ref, w_ref, g_ref, b_ref, o_ref, *, alpha):
    mix = jnp.dot(att_ref[...], w_ref[0:D_ATTN, :], preferred_element_type=F32)
    mix += jnp.dot(hy_ref[...].astype(BF16), w_ref[D_ATTN:D_ATTN + D_HYENA, :], preferred_element_type=F32)
    mix += jnp.dot(pl_ref[...].astype(BF16), w_ref[D_ATTN + D_HYENA:, :], preferred_element_type=F32)
    y = alpha * x_ref[...] + gate_ref[0] * mix
    o_ref[...] = _layernorm(y, LN_EPS) * g_ref[...] + b_ref[...]


def _outproj_call(x2, att2, hy2, pl2, mod3, mod_row, w_out_bf, g, b, alpha, tm):
    r, d = x2.shape
    row = lambda w: pl.BlockSpec((tm, w), lambda i: (i, 0))
    vec = pl.BlockSpec((1, d), lambda i: (0, 0))
    return pl.pallas_call(
        functools.partial(_outproj_kernel, alpha=alpha),
        grid=(r // tm,),
        in_specs=[row(d), row(D_ATTN), row(D_HYENA), row(D_POOL),
                  pl.BlockSpec((1, 1, d), lambda i: (mod_row(i) * 6 + 2, 0, 0)),
                  pl.BlockSpec(w_out_bf.shape, lambda i: (0, 0)),
                  vec, vec],
        out_specs=row(d),
        out_shape=jax.ShapeDtypeStruct((r, d), F32),
        compiler_params=_cparams(),
        name="out_proj_deepnorm",
    )(x2, att2, hy2, pl2, mod3, w_out_bf, g.reshape(1, d), b.reshape(1, d))


def _ffn_kernel(*refs, alpha, n_exp, n_chunk):
    if n_exp > 1:
        (x_ref, sh_ref, sc_ref, gate_ref, rw_ref, wg_ref, wu_ref, wd_ref, g_ref, b_ref,
         o_ref, u_ref, acc_ref, comb_ref) = refs
    else:
        (x_ref, sh_ref, sc_ref, gate_ref, wg_ref, wu_ref, wd_ref, g_ref, b_ref,
         o_ref, u_ref, acc_ref) = refs
    j = pl.program_id(1)

    @pl.when(j == 0)
    def _():
        u = (_layernorm(x_ref[...], ADALN_EPS) * (1.0 + sc_ref[0]) + sh_ref[0]).astype(BF16)
        u_ref[...] = u
        acc_ref[...] = jnp.zeros_like(acc_ref)
        if n_exp > 1:
            logits = jnp.dot(u, rw_ref[...].astype(BF16), preferred_element_type=F32)
            lane = lax.broadcasted_iota(jnp.int32, logits.shape, 1)
            m1 = jnp.max(logits, axis=-1, keepdims=True)
            i1 = jnp.min(jnp.where(logits == m1, lane, n_exp), axis=-1, keepdims=True)
            rest = jnp.where(lane == i1, -jnp.inf, logits)
            m2 = jnp.max(rest, axis=-1, keepdims=True)
            i2 = jnp.min(jnp.where(rest == m2, lane, n_exp), axis=-1, keepdims=True)
            e2 = jnp.exp(m2 - m1)
            den = 1.0 + e2
            comb_ref[...] = jnp.where(lane == i1, 1.0 / den, 0.0) + jnp.where(lane == i2, e2 / den, 0.0)

    u = u_ref[...]
    a = jnp.dot(u, wg_ref[0].astype(BF16), preferred_element_type=F32)
    bb = jnp.dot(u, wu_ref[0].astype(BF16), preferred_element_type=F32)
    h = a * jax.nn.sigmoid(a) * bb
    if n_exp > 1:
        comb = comb_ref[...]
        lane = lax.broadcasted_iota(jnp.int32, comb.shape, 1)
        h = h * jnp.sum(jnp.where(lane == j // n_chunk, comb, 0.0), axis=-1, keepdims=True)
    acc_ref[...] += jnp.dot(h.astype(BF16), wd_ref[0].astype(BF16), preferred_element_type=F32)

    @pl.when(j == n_exp * n_chunk - 1)
    def _():
        y = alpha * x_ref[...] + gate_ref[0] * acc_ref[...]
        o_ref[...] = _layernorm(y, LN_EPS) * g_ref[...] + b_ref[...]


def _ffn_call(x2, mod3, mod_row, router_w, w_gate, w_up, w_down, w_base, n_exp, g, b, alpha, tm, fc):
    r, d = x2.shape
    dff = w_gate.shape[2]
    n_chunk = dff // fc
    steps = n_exp * n_chunk
    vec = pl.BlockSpec((1, d), lambda i, j: (0, 0))
    mod = lambda which: pl.BlockSpec((1, 1, d), lambda i, j: (mod_row(i) * 6 + which, 0, 0))
    in_specs = [pl.BlockSpec((tm, d), lambda i, j: (i, 0)), mod(3), mod(4), mod(5)]
    args = [x2, mod3, mod3, mod3]
    scratch = [pltpu.VMEM((tm, d), BF16), pltpu.VMEM((tm, d), F32)]
    if n_exp > 1:
        in_specs.append(pl.BlockSpec(router_w.shape, lambda i, j: (0, 0)))
        args.append(router_w)
        scratch.append(pltpu.VMEM((tm, n_exp), F32))
    in_specs += [pl.BlockSpec((1, d, fc), lambda i, j: (w_base + j // n_chunk, 0, j % n_chunk)),
                 pl.BlockSpec((1, d, fc), lambda i, j: (w_base + j // n_chunk, 0, j % n_chunk)),
                 pl.BlockSpec((1, fc, d), lambda i, j: (w_base + j // n_chunk, j % n_chunk, 0)),
                 vec, vec]
    args += [w_gate, w_up, w_down, g.reshape(1, d), b.reshape(1, d)]
    return pl.pallas_call(
        functools.partial(_ffn_kernel, alpha=alpha, n_exp=n_exp, n_chunk=n_chunk),
        grid=(r // tm, steps),
        in_specs=in_specs,
        out_specs=pl.BlockSpec((tm, d), lambda i, j: (i, 0)),
        out_shape=jax.ShapeDtypeStruct((r, d), F32),
        scratch_shapes=scratch,
        compiler_params=_cparams(),
        name="channel_mixer",
    )(*args)


MOE_WINDOW = 1024
MOE_TILE = 64
MOE_SUPER = 16
MOE_FC = 512
MOE_GCHUNK = 512
MOE_SUB = 256


def _router_kernel(x_ref, sh_ref, sc_ref, rw_ref, tri_ref, u_ref, comb_ref, pos_ref):
    u = (_layernorm(x_ref[...], ADALN_EPS) * (1.0 + sc_ref[0]) + sh_ref[0]).astype(BF16)
    u_ref[...] = u
    logits = jnp.dot(u, rw_ref[...].astype(BF16), preferred_element_type=F32)
    n_exp = logits.shape[1]
    lane = lax.broadcasted_iota(jnp.int32, logits.shape, 1)
    m1 = jnp.max(logits, axis=-1, keepdims=True)
    i1 = jnp.min(jnp.where(logits == m1, lane, n_exp), axis=-1, keepdims=True)
    rest = jnp.where(lane == i1, -jnp.inf, logits)
    m2 = jnp.max(rest, axis=-1, keepdims=True)
    i2 = jnp.min(jnp.where(rest == m2, lane, n_exp), axis=-1, keepdims=True)
    e2 = jnp.exp(m2 - m1)
    den = 1.0 + e2
    comb = jnp.where(lane == i1, 1.0 / den, 0.0) + jnp.where(lane == i2, e2 / den, 0.0)
    comb_ref[...] = comb
    routed = jnp.where(comb > 0.0, 1.0, 0.0).astype(BF16)
    pos_ref[...] = jnp.dot(tri_ref[...], routed, preferred_element_type=F32) - 1.0


def _router_call(x2, mod3, mod_row, router_w):
    r, d = x2.shape
    n_exp = router_w.shape[1]
    tm = MOE_WINDOW
    tri = jnp.asarray(np.tril(np.ones((tm, tm), np.float32)), BF16)
    mod = lambda which: pl.BlockSpec((1, 1, d), lambda i: (mod_row(i) * 6 + which, 0, 0))
    small = pl.BlockSpec((tm, n_exp), lambda i: (i, 0))
    return pl.pallas_call(
        _router_kernel,
        grid=(r // tm,),
        in_specs=[pl.BlockSpec((tm, d), lambda i: (i, 0)), mod(3), mod(4),
                  pl.BlockSpec(router_w.shape, lambda i: (0, 0)),
                  pl.BlockSpec((tm, tm), lambda i: (0, 0))],
        out_specs=[pl.BlockSpec((tm, d), lambda i: (i, 0)), small, small],
        out_shape=[jax.ShapeDtypeStruct((r, d), BF16), jax.ShapeDtypeStruct((r, n_exp), F32),
                   jax.ShapeDtypeStruct((r, n_exp), F32)],
        compiler_params=_cparams(),
        name="moe_router",
    )(x2, mod3, mod3, router_w, tri)


def _moe_plan(comb, pos_f, n_tok):
    n_exp = comb.shape[1]
    win, tile = MOE_WINDOW, MOE_TILE
    n_win = n_tok // win
    cap_tiles = 2 * win // tile + n_exp
    i32 = jnp.int32
    mask = (comb > 0.0).reshape(n_win, win, n_exp)
    pos = pos_f.astype(i32).reshape(n_win, win, n_exp)
    cnt = pos[:, -1, :] + 1
    ntile = (cnt + tile - 1) // tile
    woff = jnp.cumsum(ntile, axis=1) - ntile
    rows = jnp.where(mask, tile * woff[:, None, :] + pos, -1)
    row_a = jnp.max(rows, axis=-1)
    row_b = jnp.max(jnp.where(rows == row_a[..., None], -1, rows), axis=-1)
    rowsel = jnp.stack([row_a, row_b], axis=1)
    win_rows = tile * jnp.sum(ntile, axis=1)

    etiles = jnp.sum(ntile, axis=0)
    nsup = (etiles + MOE_SUPER - 1) // MOE_SUPER
    ebase = MOE_SUPER * (jnp.cumsum(nsup) - nsup)
    erank = ebase[None, :] + jnp.cumsum(ntile, axis=0) - ntile
    n_sup = -(-(n_win * cap_tiles) // MOE_SUPER) + n_exp
    n_rank = n_sup * MOE_SUPER
    slots = win // tile
    s = jnp.arange(slots, dtype=i32)[None, None, :]
    ok = s < ntile[:, :, None]
    rank = jnp.where(ok, erank[:, :, None] + s, n_rank)
    wm_tile = (jnp.arange(n_win, dtype=i32) * cap_tiles)[:, None, None] + woff[:, :, None] + s
    src = jnp.zeros((n_rank + 1,), i32).at[rank.reshape(-1)].set(wm_tile.reshape(-1))[:n_rank]
    sup_total = jnp.sum(nsup)
    g = jnp.arange(n_sup, dtype=i32)
    sup_valid = (g < sup_total).astype(i32)
    sup_exp = jnp.clip(jnp.sum((g[:, None] >= jnp.cumsum(nsup)[None, :]).astype(i32), axis=1), 0, n_exp - 1)
    sup_exp = jnp.where(sup_valid > 0, sup_exp, sup_exp[jnp.maximum(sup_total - 1, 0)])

    tile_rank = jnp.zeros((n_win * cap_tiles + 1,), i32).at[
        jnp.where(ok, wm_tile, n_win * cap_tiles).reshape(-1)].set(
        jnp.minimum(rank, n_rank - 1).reshape(-1))[:n_win * cap_tiles]
    comb3 = comb.reshape(n_win, win, n_exp)
    gate_a = jnp.sum(jnp.where(rows == row_a[..., None], comb3, 0.0), axis=-1)
    gate_b = jnp.sum(jnp.where((rows == row_b[..., None]) & mask, comb3, 0.0), axis=-1)
    tok_sel = jnp.stack([row_a.astype(F32), row_b.astype(F32), gate_a, gate_b], axis=-1).reshape(n_tok, 4)
    return dict(n_win=n_win, cap_tiles=cap_tiles, n_sup=n_sup, n_rank=n_rank,
                rowsel=rowsel, win_rows=win_rows.astype(i32),
                src=src, sup_exp=sup_exp.astype(i32), sup_valid=sup_valid,
                tile_rank=tile_rank, tok_sel=tok_sel)


def _dispatch_kernel(nrows_ref, u_ref, sel_ref, o_ref):
    w = pl.program_id(0)
    r1 = sel_ref[0:1, :]
    r2 = sel_ref[1:2, :]
    for cidx in range(o_ref.shape[0] // MOE_GCHUNK):
        base = cidx * MOE_GCHUNK
        rows = pl.ds(base, MOE_GCHUNK)

        @pl.when(base < nrows_ref[w])
        def _():
            rid = lax.broadcasted_iota(jnp.int32, (MOE_GCHUNK, r1.shape[1]), 0) + base
            onehot = jnp.where(rid == r1, 1.0, jnp.where(rid == r2, 1.0, 0.0)).astype(BF16)
            o_ref[rows, :] = jnp.dot(onehot, u_ref[...], preferred_element_type=F32).astype(o_ref.dtype)

        @pl.when(base >= nrows_ref[w])
        def _():
            o_ref[rows, :] = jnp.zeros((MOE_GCHUNK, o_ref.shape[1]), o_ref.dtype)


def _dispatch_call(u, plan):
    n_tok, d = u.shape
    cap_rows = plan["cap_tiles"] * MOE_TILE
    n_win = plan["n_win"]
    return pl.pallas_call(
        _dispatch_kernel,
        grid_spec=pltpu.PrefetchScalarGridSpec(
            num_scalar_prefetch=1,
            grid=(n_win,),
            in_specs=[pl.BlockSpec((MOE_WINDOW, d), lambda w, n: (w, 0)),
                      pl.BlockSpec((None, TOP_K, MOE_WINDOW), lambda w, n: (w, 0, 0))],
            out_specs=pl.BlockSpec((cap_rows, d), lambda w, n: (w, 0))),
        out_shape=jax.ShapeDtypeStruct((n_win * cap_rows, d), BF16),
        compiler_params=_cparams(),
        name="moe_dispatch",
    )(plan["win_rows"], u, plan["rowsel"])


def _expert_kernel(src_ref, exp_ref, valid_ref, *refs, n_chunk):
    tiles = refs[:MOE_SUPER]
    wg_ref, wu_ref, wd_ref, o_ref, u_ref, acc_ref = refs[MOE_SUPER:]
    g = pl.program_id(0)
    j = pl.program_id(1)
    ok = valid_ref[g] > 0

    @pl.when(j == 0)
    def _():
        for k in range(MOE_SUPER):
            u_ref[k * MOE_TILE:(k + 1) * MOE_TILE, :] = tiles[k][...]
        acc_ref[...] = jnp.zeros_like(acc_ref)

    @pl.when(ok)
    def _():
        u = u_ref[...]
        part = None
        for c0 in range(0, wg_ref.shape[2], MOE_SUB):
            a = jnp.dot(u, wg_ref[0, :, c0:c0 + MOE_SUB].astype(BF16), preferred_element_type=F32)
            b = jnp.dot(u, wu_ref[0, :, c0:c0 + MOE_SUB].astype(BF16), preferred_element_type=F32)
            h = (a * jax.nn.sigmoid(a) * b).astype(BF16)
            y = jnp.dot(h, wd_ref[0, c0:c0 + MOE_SUB, :].astype(BF16), preferred_element_type=F32)
            part = y if part is None else part + y
        acc_ref[...] += part

    @pl.when(j == n_chunk - 1)
    def _():
        o_ref[...] = acc_ref[...].astype(o_ref.dtype)


def _expert_call(xw, plan, w_gate, w_up, w_down, layer_base, fc):
    d = xw.shape[1]
    dff = w_gate.shape[2]
    n_chunk = dff // fc
    rows = MOE_SUPER * MOE_TILE

    def tile_spec(k):
        return pl.BlockSpec((MOE_TILE, d), lambda g, j, src, ex, va: (src[g * MOE_SUPER + k], 0))

    def chunk(j, va, g):
        return jnp.where(va[g] > 0, j, n_chunk - 1)

    in_specs = [tile_spec(k) for k in range(MOE_SUPER)] + [
        pl.BlockSpec((1, d, fc), lambda g, j, src, ex, va: (layer_base + ex[g], 0, chunk(j, va, g))),
        pl.BlockSpec((1, d, fc), lambda g, j, src, ex, va: (layer_base + ex[g], 0, chunk(j, va, g))),
        pl.BlockSpec((1, fc, d), lambda g, j, src, ex, va: (layer_base + ex[g], chunk(j, va, g), 0))]
    return pl.pallas_call(
        functools.partial(_expert_kernel, n_chunk=n_chunk),
        grid_spec=pltpu.PrefetchScalarGridSpec(
            num_scalar_prefetch=3,
            grid=(plan["n_sup"], n_chunk),
            in_specs=in_specs,
            out_specs=pl.BlockSpec((rows, d), lambda g, j, src, ex, va: (g, 0)),
            scratch_shapes=[pltpu.VMEM((rows, d), BF16), pltpu.VMEM((rows, d), F32)]),
        out_shape=jax.ShapeDtypeStruct((plan["n_sup"] * rows, d), BF16),
        compiler_params=_cparams(),
        name="moe_experts",
    )(plan["src"], plan["sup_exp"], plan["sup_valid"], *([xw] * MOE_SUPER), w_gate, w_up, w_down)


def _combine_kernel(rank_ref, *refs, alpha, n_tiles):
    tile_refs = refs[:n_tiles]
    sel_ref, x_ref, gate_ref, g_ref, b_ref, o_ref, rows_ref, acc_ref = refs[n_tiles:]
    for k in range(n_tiles):
        rows_ref[k * MOE_TILE:(k + 1) * MOE_TILE, :] = tile_refs[k][...]
    sel = sel_ref[...]
    lane = lax.broadcasted_iota(jnp.int32, sel.shape, 1)
    col = lambda k: jnp.sum(jnp.where(lane == k, sel, 0.0), axis=-1, keepdims=True)
    row_a = col(0).astype(jnp.int32)
    row_b = col(1).astype(jnp.int32)
    w_a = col(2)
    w_b = col(3)
    n_rows = rows_ref.shape[0]
    for c0 in range(0, n_rows, MOE_GCHUNK):
        rid = lax.broadcasted_iota(jnp.int32, (sel.shape[0], MOE_GCHUNK), 1) + c0
        q = jnp.where(rid == row_a, w_a, jnp.where(rid == row_b, w_b, 0.0)).astype(BF16)
        part = jnp.dot(q, rows_ref[c0:c0 + MOE_GCHUNK, :], preferred_element_type=F32)
        if c0 == 0:
            acc_ref[...] = part
        else:
            acc_ref[...] += part
    y = alpha * x_ref[...] + gate_ref[0] * acc_ref[...]
    o_ref[...] = _layernorm(y, LN_EPS) * g_ref[...] + b_ref[...]


def _combine_call(ys, plan, x2, mod3, mod_row, g, b, alpha):
    n_tok, d = x2.shape
    n_tiles = plan["cap_tiles"]

    def tile_spec(k):
        return pl.BlockSpec((MOE_TILE, d), lambda w, rk: (rk[w * n_tiles + k], 0))

    win = lambda width: pl.BlockSpec((MOE_WINDOW, width), lambda w, rk: (w, 0))
    vec = pl.BlockSpec((1, d), lambda w, rk: (0, 0))
    return pl.pallas_call(
        functools.partial(_combine_kernel, alpha=alpha, n_tiles=n_tiles),
        grid_spec=pltpu.PrefetchScalarGridSpec(
            num_scalar_prefetch=1,
            grid=(plan["n_win"],),
            in_specs=[tile_spec(k) for k in range(n_tiles)] + [
                win(4), win(d),
                pl.BlockSpec((1, 1, d), lambda w, rk: (mod_row(w) * 6 + 5, 0, 0)),
                vec, vec],
            out_specs=win(d),
            scratch_shapes=[pltpu.VMEM((n_tiles * MOE_TILE, d), BF16), pltpu.VMEM((MOE_WINDOW, d), F32)]),
        out_shape=jax.ShapeDtypeStruct((n_tok, d), F32),
        compiler_params=_cparams(),
        name="moe_combine_deepnorm",
    )(plan["tile_rank"], *([ys] * n_tiles), plan["tok_sel"], x2, mod3, g.reshape(1, d), b.reshape(1, d))


def _moe_call(x2, mod3, tok_row, router_w, w_gate, w_up, w_down, layer_base, g, b, alpha, fc):
    n_tok = x2.shape[0]
    u, comb, pos = _router_call(x2, mod3, lambda i: tok_row(i * MOE_WINDOW), router_w)
    plan = _moe_plan(comb, pos, n_tok)
    xw = _dispatch_call(u, plan)
    ys = _expert_call(xw, plan, w_gate, w_up, w_down, layer_base, fc)
    return _combine_call(ys, plan, x2, mod3, lambda w: tok_row(w * MOE_WINDOW), g, b, alpha)


def _pick_tile(n, pref):
    t = min(n, pref)
    while n % t:
        t //= 2
    return t


def kernel(x, c, ctx, c_ctx, w_mod, b_mod, w_in, q_gain, k_gain, hy_conv_w, hy_conv_b, hy_f_w1, hy_f_b1, hy_f_freq, hy_f_w2, hy_f_b2, hy_f_w3, hy_d, pool_w, pool_scale, w_out, ln1_g, ln1_b, ln2_g, ln2_b, ffn_w_gate, ffn_w_up, ffn_w_down, router_w, moe_w_gate, moe_w_up, moe_w_down):
    bsz, seq, d = x.shape
    clen = ctx.shape[1]
    depth = w_mod.shape[0]
    alpha = (2.0 * depth) ** 0.25
    assert seq % MOE_WINDOW == 0 and bsz % 2 == 0
    n_exp = router_w.shape[2]
    moe_wg = moe_w_gate.reshape((-1,) + moe_w_gate.shape[2:])
    moe_wu = moe_w_up.reshape((-1,) + moe_w_up.shape[2:])
    moe_wd = moe_w_down.reshape((-1,) + moe_w_down.shape[2:])

    n_rows = -(-(bsz + 1) // 8) * 8
    cc = jnp.zeros((n_rows, d), F32).at[:bsz].set(c).at[bsz].set(c_ctx)
    mod = _mod_call(cc, w_mod, b_mod)

    tm_lat = _pick_tile(seq, 512)
    tm_ctx = _pick_tile(clen, 512)
    tm_ffn = _pick_tile(seq, 1024)
    tm_ffn_ctx = _pick_tile(bsz * clen, 1024)
    lat_row = lambda tm: (lambda i: i // (seq // tm))
    ctx_row = lambda i: bsz
    rope_lat = tuple(jnp.asarray(t) for t in _rope_tables(seq, True))
    rope_ctx = tuple(jnp.asarray(t) for t in _rope_tables(clen, False))
    head_avg = np.kron(np.eye(QK_W // HEAD_DIM), np.full((HEAD_DIM, HEAD_DIM), 1.0 / HEAD_DIM))
    head_avg = jnp.asarray(head_avg, BF16)

    xl = x.reshape(bsz * seq, d)
    xc = ctx.reshape(bsz * clen, d)
    for l in range(depth):
        last = l == depth - 1
        mod3 = mod[l].reshape(n_rows * 6, 1, d)
        w_in_bf = w_in[l].astype(BF16)
        w_out_bf = w_out[l].astype(BF16)
        gain = jnp.concatenate([jnp.tile(q_gain[l] * (ATTN_SCALE * LOG2E), N_Q_HEADS),
                                jnp.tile(k_gain[l], N_KV_HEADS)]).reshape(1, QK_W)
        shift = (1.02 * HEAD_DIM * ATTN_SCALE * LOG2E) * jnp.max(jnp.abs(q_gain[l])) * jnp.max(jnp.abs(k_gain[l]))
        lane64 = (jnp.arange(LANES) == HEAD_DIM).astype(F32)
        ext = jnp.stack([lane64, -shift * lane64, lane64])
        fast = (shift <= ATTN_SAFE_SHIFT).astype(jnp.int32).reshape(1)
        filt_w = (hy_f_w1[l], hy_f_b1[l], hy_f_freq[l], hy_f_w2[l], hy_f_b2[l], hy_f_w3[l])
        j = l // 2
        routed = l % 2 == 1
        if routed:
            ffn_w = (router_w[j], moe_wg, moe_wu, moe_wd, j * n_exp, n_exp)
        else:
            ffn_w = (None, ffn_w_gate, ffn_w_up, ffn_w_down, j, 1)
        fc = 256

        q_c, k_c, v_c, hp_c = _proj_call(xc, mod3, ctx_row, clen, w_in_bf, gain, head_avg, rope_ctx, ext, tm_ctx)
        k_c3 = k_c.reshape(bsz, clen, -1)
        v_c3 = v_c.reshape(bsz, clen, -1)
        if not last:
            hp_c3 = hp_c.reshape(bsz, clen, -1)
            att_c = _attn_call(fast, q_c.reshape(bsz, clen, -1), (k_c3, v_c3), _pick_tile(clen, 256))
            spec_c = _ctx_spectrum(_filter_call(clen, *filt_w), clen)
            hy_c = _ctx_hyena_call(hp_c3, hy_conv_w[l], hy_conv_b[l], spec_c, hy_d[l])
            pl_c = _pool_call(hp_c3, pool_w[l], pool_scale[l])
            xc_mid = _outproj_call(xc, att_c.reshape(bsz * clen, D_ATTN), hy_c.reshape(bsz * clen, D_HYENA),
                                   pl_c.reshape(bsz * clen, D_POOL), mod3, ctx_row, w_out_bf,
                                   ln1_g[l], ln1_b[l], alpha, tm_ctx)

        q_l, k_l, v_l, hp_l = _proj_call(xl, mod3, lat_row(tm_lat), seq, w_in_bf, gain, head_avg, rope_lat, ext,
                                         tm_lat)
        hp_l3 = hp_l.reshape(bsz, seq, -1)
        att = _attn_call(fast, q_l.reshape(bsz, seq, -1),
                         (k_l.reshape(bsz, seq, -1), v_l.reshape(bsz, seq, -1), k_c3, v_c3), _pick_tile(seq, 256))
        spec = _hyena_spectrum(_filter_call(seq, *filt_w), seq)
        hyo = _hyena_latent(hp_l3, hy_conv_w[l], hy_conv_b[l], spec, hy_d[l])
        plo = _pool_call(hp_l3, pool_w[l], pool_scale[l])
        xl = _outproj_call(xl, att.reshape(bsz * seq, D_ATTN), hyo.reshape(bsz * seq, D_HYENA),
                           plo.reshape(bsz * seq, D_POOL), mod3, lat_row(tm_lat), w_out_bf,
                           ln1_g[l], ln1_b[l], alpha, tm_lat)
        if routed:
            xl = _moe_call(xl, mod3, lambda t: t // seq, router_w[j], moe_wg, moe_wu, moe_wd, j * n_exp,
                           ln2_g[l], ln2_b[l], alpha, MOE_FC)
        else:
            xl = _ffn_call(xl, mod3, lat_row(tm_ffn), *ffn_w, ln2_g[l], ln2_b[l], alpha, tm_ffn, fc)

        if not last:
            xc = _ffn_call(xc_mid, mod3, ctx_row, *ffn_w, ln2_g[l], ln2_b[l], alpha, tm_ffn_ctx, fc)
    return xl.reshape(bsz, seq, d)
```

```python
import functools
import math

import numpy as np
import jax
import jax.numpy as jnp
from jax import lax
from jax.experimental import pallas as pl
from jax.experimental.pallas import tpu as pltpu

F32 = jnp.float32
BF16 = jnp.bfloat16
HIGHEST = lax.Precision.HIGHEST

GRID_W = 64
HEAD_DIM = 64
N_Q_HEADS = 8
N_KV_HEADS = 2
GQA_GROUP = N_Q_HEADS // N_KV_HEADS
D_ATTN = N_Q_HEADS * HEAD_DIM
KV_W = N_KV_HEADS * HEAD_DIM
QK_W = D_ATTN + KV_W
ATTN_SCALE = HEAD_DIM ** -0.5
ROPE_THETA = 10000.0
ROPE_AXIS_DIM = HEAD_DIM // 2
QK_EPS = 1e-6
D_HYENA = 256
HYENA_ORDER = 2
HYENA_BANDS = 16
HYENA_EMB = 1 + 2 * HYENA_BANDS
HYENA_DECAY_TARGET = 1e-2
HYENA_FAST_DECAY_PCT = 0.3
HYENA_SLOW_DECAY_PCT = 1.5
D_POOL = 256
POOL_WINDOWS = (2, 4, 8, 16)
POOL_GROUP = D_POOL // len(POOL_WINDOWS)
POOL_PAD = 8
TOP_K = 2
LN_EPS = 1e-5
ADALN_EPS = 1e-6
LOG2E = math.log2(math.e)

LANES = 128
FFT_N2 = 64
FFT_UNROLL = 4
VMEM_LIMIT = 56 * 1024 * 1024


def _cparams():
    return pltpu.CompilerParams(vmem_limit_bytes=VMEM_LIMIT)


def _layernorm(x, eps):
    mu = jnp.mean(x, axis=-1, keepdims=True)
    xc = x - mu
    var = jnp.mean(xc * xc, axis=-1, keepdims=True)
    return xc * lax.rsqrt(var + eps)


def _mod_kernel(c_ref, w_ref, b_ref, o_ref):
    c = c_ref[...]
    s = c * jax.nn.sigmoid(c)
    o_ref[0] = jnp.dot(s, w_ref[0], preferred_element_type=F32, precision=HIGHEST) + b_ref[0]


def _mod_call(cc, w_mod, b_mod):
    depth, d, d6 = w_mod.shape
    r = cc.shape[0]
    tn = 1536
    return pl.pallas_call(
        _mod_kernel,
        grid=(depth, d6 // tn),
        in_specs=[pl.BlockSpec((r, d), lambda l, j: (0, 0)),
                  pl.BlockSpec((1, d, tn), lambda l, j: (l, 0, j)),
                  pl.BlockSpec((1, 1, tn), lambda l, j: (l, 0, j))],
        out_specs=pl.BlockSpec((1, r, tn), lambda l, j: (l, 0, j)),
        out_shape=jax.ShapeDtypeStruct((depth, r, d6), F32),
        compiler_params=_cparams(),
        name="adaln_mod",
    )(cc, w_mod, b_mod.reshape(depth, 1, d6))


def _rope_tables(seq_len, rope):
    lane = np.arange(LANES)
    d = lane % HEAD_DIM
    if not rope:
        one = np.ones((seq_len, LANES), np.float32)
        zero = np.zeros((seq_len, LANES), np.float32)
        return one, zero, zero
    t = np.arange(seq_len)
    row = (t // GRID_W).astype(np.float64)
    col = (t % GRID_W).astype(np.float64)
    half = ROPE_AXIS_DIM // 2
    inv = ROPE_THETA ** (-np.arange(0, ROPE_AXIS_DIM, 2, dtype=np.float64) / ROPE_AXIS_DIM)
    pos = np.where((d // ROPE_AXIS_DIM)[None, :] == 0, row[:, None], col[:, None])
    ang = pos * inv[d % half][None, :]
    is_b = ((d % ROPE_AXIS_DIM) >= half)[None, :]
    cos = np.cos(ang)
    sin = np.sin(ang)
    s_up = np.where(is_b, sin, 0.0)
    s_dn = np.where(is_b, 0.0, -sin)
    return cos.astype(np.float32), s_up.astype(np.float32), s_dn.astype(np.float32)


def _store_padded_heads(blk, ext, out_ref, col0):
    lo = lax.broadcasted_iota(jnp.int32, blk.shape, 1) < HEAD_DIM
    out_ref[:, col0:col0 + LANES] = jnp.where(lo, blk, ext).astype(out_ref.dtype)
    out_ref[:, col0 + LANES:col0 + 2 * LANES] = jnp.where(lo, pltpu.roll(blk, HEAD_DIM, axis=1),
                                                          ext).astype(out_ref.dtype)


def _proj_kernel(x_ref, sh_ref, sc_ref, w_ref, gain_ref, bd_ref, cos_ref, sup_ref, sdn_ref, ext_ref,
                 q_ref, k_ref, v_ref, hp_ref):
    u = _layernorm(x_ref[...], ADALN_EPS) * (1.0 + sc_ref[0]) + sh_ref[0]
    p = jnp.dot(u.astype(BF16), w_ref[...], preferred_element_type=F32)
    qk = p[:, :QK_W]
    ms = jnp.dot((qk * qk).astype(BF16), bd_ref[...], preferred_element_type=F32)
    qn = qk * lax.rsqrt(ms + QK_EPS) * gain_ref[...]
    cos = cos_ref[...]
    sup = sup_ref[...]
    sdn = sdn_ref[...]
    n_qblk = D_ATTN // LANES
    for j in range(QK_W // LANES):
        blk = qn[:, j * LANES:(j + 1) * LANES]
        up = pltpu.roll(blk, ROPE_AXIS_DIM // 2, axis=1)
        dn = pltpu.roll(blk, LANES - ROPE_AXIS_DIM // 2, axis=1)
        rot = blk * cos + up * sup + dn * sdn
        if j < n_qblk:
            _store_padded_heads(rot, ext_ref[0:1, :], q_ref, 2 * j * LANES)
        else:
            _store_padded_heads(rot, ext_ref[1:2, :], k_ref, 2 * (j - n_qblk) * LANES)
    _store_padded_heads(p[:, QK_W:QK_W + KV_W], ext_ref[2:3, :], v_ref, 0)
    hp_ref[...] = p[:, QK_W + KV_W:]


def _proj_call(x2, mod3, mod_row, seq_len, w_in_bf, gain, bd, tables, ext, tm):
    r, d = x2.shape
    d_in = w_in_bf.shape[1]
    d_hp = d_in - QK_W - KV_W
    nseq = seq_len // tm
    cos, sup, sdn = tables
    tab_spec = pl.BlockSpec((tm, LANES), lambda i: (i % nseq, 0))
    return pl.pallas_call(
        _proj_kernel,
        grid=(r // tm,),
        in_specs=[pl.BlockSpec((tm, d), lambda i: (i, 0)),
                  pl.BlockSpec((1, 1, d), lambda i: (mod_row(i) * 6 + 0, 0, 0)),
                  pl.BlockSpec((1, 1, d), lambda i: (mod_row(i) * 6 + 1, 0, 0)),
                  pl.BlockSpec((d, d_in), lambda i: (0, 0)),
                  pl.BlockSpec((1, QK_W), lambda i: (0, 0)),
                  pl.BlockSpec((QK_W, QK_W), lambda i: (0, 0)),
                  tab_spec, tab_spec, tab_spec,
                  pl.BlockSpec((3, LANES), lambda i: (0, 0))],
        out_specs=[pl.BlockSpec((tm, N_Q_HEADS * LANES), lambda i: (i, 0)),
                   pl.BlockSpec((tm, N_KV_HEADS * LANES), lambda i: (i, 0)),
                   pl.BlockSpec((tm, N_KV_HEADS * LANES), lambda i: (i, 0)),
                   pl.BlockSpec((tm, d_hp), lambda i: (i, 0))],
        out_shape=[jax.ShapeDtypeStruct((r, N_Q_HEADS * LANES), BF16),
                   jax.ShapeDtypeStruct((r, N_KV_HEADS * LANES), BF16),
                   jax.ShapeDtypeStruct((r, N_KV_HEADS * LANES), BF16),
                   jax.ShapeDtypeStruct((r, d_hp), F32)],
        compiler_params=_cparams(),
        name="ln_mod_in_proj",
    )(x2, mod3, mod3, w_in_bf, gain, bd, cos, sup, sdn, ext)


ATTN_KEY_CHUNK = 1024
ATTN_SAFE_SHIFT = 60.0


def _attn_kernel(fast_ref, q_ref, *refs):
    o_ref = refs[-1]
    sources = [(refs[i], refs[i + 1]) for i in range(0, len(refs) - 1, 2)]
    tq = q_ref.shape[0]
    chunks = [(k_ref, v_ref, c0, min(ATTN_KEY_CHUNK, k_ref.shape[0] - c0))
              for k_ref, v_ref in sources for c0 in range(0, k_ref.shape[0], ATTN_KEY_CHUNK)]

    def run(running_max):
        for h in range(N_KV_HEADS):
            qs = jnp.concatenate([q_ref[:, (h * GQA_GROUP + g) * LANES:(h * GQA_GROUP + g + 1) * LANES]
                                  for g in range(GQA_GROUP)], axis=0)
            acc = jnp.zeros((GQA_GROUP * tq, LANES), F32)
            m = jnp.full((GQA_GROUP * tq, 1), -jnp.inf, F32)
            for k_ref, v_ref, c0, kw in chunks:
                kc = k_ref[c0:c0 + kw, h * LANES:(h + 1) * LANES]
                vc = v_ref[c0:c0 + kw, h * LANES:(h + 1) * LANES]
                s = lax.dot_general(qs, kc, (((1,), (1,)), ((), ())), preferred_element_type=F32)
                if running_max:
                    m_new = jnp.maximum(m, jnp.max(s, axis=-1, keepdims=True))
                    acc = acc * jnp.exp2(m - m_new)
                    s = s - m_new
                    m = m_new
                acc = acc + jnp.dot(jnp.exp2(s).astype(BF16), vc, preferred_element_type=F32)
            o = acc[:, :HEAD_DIM] / acc[:, HEAD_DIM:HEAD_DIM + 1]
            for g in range(GQA_GROUP):
                c0 = (h * GQA_GROUP + g) * HEAD_DIM
                o_ref[:, c0:c0 + HEAD_DIM] = o[g * tq:(g + 1) * tq].astype(o_ref.dtype)

    @pl.when(fast_ref[0] > 0)
    def _():
        run(False)

    @pl.when(fast_ref[0] == 0)
    def _():
        run(True)


def _attn_call(fast, q, kv, tq):
    b, lq, qw = q.shape
    kv_specs = [pl.BlockSpec((None,) + a.shape[1:], lambda i, j, f: (i, 0, 0)) for a in kv]
    return pl.pallas_call(
        _attn_kernel,
        grid_spec=pltpu.PrefetchScalarGridSpec(
            num_scalar_prefetch=1,
            grid=(b, lq // tq),
            in_specs=[pl.BlockSpec((None, tq, qw), lambda i, j, f: (i, j, 0))] + kv_specs,
            out_specs=pl.BlockSpec((None, tq, D_ATTN), lambda i, j, f: (i, j, 0))),
        out_shape=jax.ShapeDtypeStruct((b, lq, D_ATTN), BF16),
        compiler_params=_cparams(),
        name="attention",
    )(fast, q, *kv)


def _filter_consts(seq_len):
    t = np.linspace(0.0, 1.0, seq_len, dtype=np.float32).astype(np.float64)[:, None]
    omega = 2.0 * math.pi * np.arange(seq_len, dtype=np.float64)[:, None] / seq_len
    bands = np.linspace(1e-4, HYENA_BANDS - 1, HYENA_BANDS, dtype=np.float32).astype(np.float64)[None, :]
    feats = np.concatenate([t, np.cos(omega * bands), -np.sin(omega * bands)], axis=-1)
    feats = np.pad(feats, ((0, 0), (0, LANES - HYENA_EMB)))
    max_decay = math.log(HYENA_DECAY_TARGET) / HYENA_FAST_DECAY_PCT
    min_decay = math.log(HYENA_DECAY_TARGET) / HYENA_SLOW_DECAY_PCT
    deltas = np.linspace(min_decay, max_decay, D_HYENA, dtype=np.float32).astype(np.float64)
    decay = np.exp(-t * np.abs(deltas)[None, :])
    rev = (seq_len - np.arange(seq_len)) % seq_len
    f32 = lambda a: jnp.asarray(a.astype(np.float32))
    return f32(feats), f32(feats[rev]), f32(decay), f32(decay[rev])


def _filter_kernel(feat_ref, featr_ref, w1_ref, b1_ref, fr_ref, w2_ref, b2_ref, w3_ref, dec_ref, decr_ref, o_ref):
    fr = fr_ref[...]
    seq_len = feat_ref.shape[0]
    dot = functools.partial(jnp.dot, preferred_element_type=F32, precision=HIGHEST)

    def hidden(feat):
        h = jnp.sin(fr * (dot(feat, w1_ref[...]) + b1_ref[...]))
        return jnp.sin(fr * (dot(h, w2_ref[...]) + b2_ref[...]))

    fwd = dot(hidden(feat_ref[...]), w3_ref[:, :D_HYENA]) * dec_ref[...]
    bwd = dot(hidden(featr_ref[...]), w3_ref[:, D_HYENA:]) * decr_ref[...]
    tot = jnp.sum(jnp.abs(fwd), axis=0, keepdims=True) + jnp.sum(jnp.abs(bwd), axis=0, keepdims=True)
    inv = 1.0 / tot
    o_ref[:seq_len, :] = fwd * inv
    row = lax.broadcasted_iota(jnp.int32, bwd.shape, 0)
    o_ref[seq_len:, :] = jnp.where(row == 0, 0.0, bwd * inv)


def _filter_call(seq_len, w1, b1, freq, w2, b2, w3):
    feats, feats_rev, decay, decay_rev = _filter_consts(seq_len)
    hid = w2.shape[0]
    w1p = jnp.pad(w1, ((0, LANES - HYENA_EMB), (0, 0)))
    full = lambda shape: pl.BlockSpec(shape, lambda o: (0,) * len(shape))
    return pl.pallas_call(
        _filter_kernel,
        grid=(HYENA_ORDER,),
        in_specs=[full((seq_len, LANES)), full((seq_len, LANES)), full((LANES, hid)), full((1, hid)),
                  full((1, hid)), full((hid, hid)), full((1, hid)),
                  pl.BlockSpec((hid, 2 * D_HYENA), lambda o: (0, o)),
                  full((seq_len, D_HYENA)), full((seq_len, D_HYENA))],
        out_specs=pl.BlockSpec((2 * seq_len, D_HYENA), lambda o: (0, o)),
        out_shape=jax.ShapeDtypeStruct((2 * seq_len, HYENA_ORDER * D_HYENA), F32),
        compiler_params=_cparams(),
        name="hyena_filter",
    )(feats, feats_rev, w1p, b1.reshape(1, hid), freq.reshape(1, hid), w2, b2.reshape(1, hid), w3,
      decay, decay_rev)


@functools.lru_cache(maxsize=None)
def _fft_consts(seq_len):
    n = 2 * seq_len
    n2 = FFT_N2
    n1 = n // n2
    h1 = n1 // 2
    k1 = np.arange(n1)
    w1 = np.exp(-2j * np.pi * np.outer(k1, np.arange(n1)) / n1)
    f1 = np.zeros((n1, 2, 2 * h1))
    f1[:, 0, :h1], f1[:, 0, h1:] = w1[:, :h1].real, -w1[:, :h1].imag
    f1[:, 1, :h1], f1[:, 1, h1:] = w1[:, :h1].imag, w1[:, :h1].real
    f1 = f1.reshape(2 * n1, 2 * h1)
    f1r = np.stack([w1.real, w1.imag], axis=1).reshape(2 * n1, n1)
    a2 = np.arange(n2)
    tw = np.exp(-2j * np.pi * np.outer(k1, a2) / n)
    w2 = np.exp(-2j * np.pi * np.outer(a2, a2) / n2)
    m = w2[None, :, :] * tw[:, None, :]
    ff = np.concatenate([np.concatenate([m.real, -m.imag], axis=2),
                         np.concatenate([m.imag, m.real], axis=2)], axis=1)
    g = np.conj(w2.T)[None, :, :] * np.conj(tw)[:, :, None]
    gi = np.concatenate([np.concatenate([g.real, -g.imag], axis=2),
                         np.concatenate([g.imag, g.real], axis=2)], axis=1)
    v = np.conj(w1[:, :h1]).T
    f1i = np.zeros((2, h1, n1, 2))
    f1i[0, :, :, 0], f1i[0, :, :, 1] = v.real, -v.imag
    f1i[1, :, :, 0], f1i[1, :, :, 1] = v.imag, v.real
    f1i = f1i.reshape(2 * h1, 2 * n1)
    il = np.arange(2 * n2).reshape(2, n2).T.reshape(-1)
    ff_il = ff[:, :, il]
    gi_il = gi[:, il, :]
    f32 = lambda a: np.asarray(a, np.float32)
    return dict(n=n, n1=n1, h1=h1, f1=f32(f1), f1r=f32(f1r), ff=f32(ff), ff_il=f32(ff_il), gi_il=f32(gi_il),
                f1i=f32(f1i))


def _fftr_kernel(x_ref, f_ref, o_ref):
    n1 = f_ref.shape[1]

    def body(n2, carry):
        x = x_ref[pl.ds(n2, n1, stride=FFT_N2), :]
        o_ref[pl.ds(n2, 2 * n1, stride=FFT_N2), :] = jnp.dot(f_ref[...], x, preferred_element_type=F32,
                                                            precision=HIGHEST)
        return carry

    lax.fori_loop(0, FFT_N2, body, 0, unroll=FFT_UNROLL)


def _fftr_call(buf, f1r):
    n, cols = buf.shape
    rows = f1r.shape[0] * FFT_N2
    return pl.pallas_call(
        _fftr_kernel,
        grid=(cols // LANES,),
        in_specs=[pl.BlockSpec((n, LANES), lambda j: (0, j)),
                  pl.BlockSpec(f1r.shape, lambda j: (0, 0))],
        out_specs=pl.BlockSpec((None, rows, LANES), lambda j: (j, 0, 0)),
        out_shape=jax.ShapeDtypeStruct((cols // LANES, rows, LANES), F32),
        compiler_params=_cparams(),
        name="fft_filter_outer_dft",
    )(buf, f1r)


def _spec_kernel(a_ref, ff_ref, o_ref, *, scale):
    for kk in range(a_ref.shape[1]):
        for cb in range(a_ref.shape[0]):
            o_ref[kk, :, cb * LANES:(cb + 1) * LANES] = jnp.dot(
                ff_ref[kk], a_ref[cb, kk], preferred_element_type=F32, precision=HIGHEST) * scale


def _spec_call(a4, ff, scale, kt):
    nblk, n1, r, _ = a4.shape
    return pl.pallas_call(
        functools.partial(_spec_kernel, scale=scale),
        grid=(n1 // kt,),
        in_specs=[pl.BlockSpec((nblk, kt, r, LANES), lambda i: (0, i, 0, 0)),
                  pl.BlockSpec((kt, r, r), lambda i: (i, 0, 0))],
        out_specs=pl.BlockSpec((kt, r, nblk * LANES), lambda i: (i, 0, 0)),
        out_shape=jax.ShapeDtypeStruct((n1, r, nblk * LANES), F32),
        compiler_params=_cparams(),
        name="fft_filter_spectrum",
    )(a4, ff)


def _short_conv(z, w, b):
    n = z.shape[0]
    row = lax.broadcasted_iota(jnp.int32, z.shape, 0)
    prev = jnp.where(row == 0, 0.0, pltpu.roll(z, 1, axis=0))
    nxt = jnp.where(row == n - 1, 0.0, pltpu.roll(z, n - 1, axis=0))
    return prev * w[0:1, :] + z * w[1:2, :] + nxt * w[2:3, :] + b


def _ffta_kernel(*refs, conv):
    if conv:
        z_ref, w_ref, cb_ref, f_ref, o_ref, x_ref = refs
        for m in range(2):
            x_ref[m] = _short_conv(z_ref[m], w_ref[...], cb_ref[...])
    else:
        z_ref, f_ref, o_ref = refs
        x_ref = z_ref
    n1 = f_ref.shape[0] // 2
    h1 = f_ref.shape[1] // 2

    def body(n2, carry):
        rows = pl.ds(n2, h1, stride=FFT_N2)
        x = jnp.concatenate([x_ref[0, rows, :], x_ref[1, rows, :]], axis=0).astype(BF16)
        a = jnp.dot(f_ref[...], x, preferred_element_type=F32).astype(BF16)
        o_ref[pl.ds(n2, n1, stride=FFT_N2), :] = pltpu.bitcast(a, jnp.uint32)
        return carry

    lax.fori_loop(0, FFT_N2, body, 0, unroll=FFT_UNROLL)


def _ffta_call(z4, col0, f1_bf, conv_w=None, conv_b=None):
    groups, _, l, _ = z4.shape
    halves = D_HYENA // LANES
    n1 = f1_bf.shape[0] // 2
    conv = conv_w is not None
    in_specs = [pl.BlockSpec((None, 2, l, LANES), lambda i, j: (i, 0, 0, col0 + j))]
    args = [z4]
    scratch = []
    if conv:
        in_specs += [pl.BlockSpec((3, LANES), lambda i, j: (0, col0 + j)),
                     pl.BlockSpec((1, LANES), lambda i, j: (0, col0 + j))]
        args += [conv_w, conv_b]
        scratch = [pltpu.VMEM((2, l, LANES), F32)]
    in_specs.append(pl.BlockSpec(f1_bf.shape, lambda i, j: (0, 0)))
    args.append(f1_bf)
    return pl.pallas_call(
        functools.partial(_ffta_kernel, conv=conv),
        grid=(groups, halves),
        in_specs=in_specs,
        out_specs=pl.BlockSpec((None, None, n1 * FFT_N2, LANES), lambda i, j: (i, j, 0, 0)),
        out_shape=jax.ShapeDtypeStruct((groups, halves, n1 * FFT_N2, LANES), jnp.uint32),
        scratch_shapes=scratch,
        compiler_params=_cparams(),
        name="fft_outer_dft",
    )(*args)


def _fftb_kernel(a_ref, ff_ref, gi_ref, h_ref, o_ref):
    groups, halves, kt = a_ref.shape[:3]
    half = FFT_N2

    def body(kk, carry):
        ff = ff_ref[kk]
        gi = gi_ref[kk]
        hr = h_ref[kk, :half, :]
        hi = h_ref[kk, half:, :]
        for g in range(groups):
            a = jnp.concatenate([pltpu.bitcast(a_ref[g, hh, kk], BF16) for hh in range(halves)], axis=1)
            xh = jnp.dot(ff, a, preferred_element_type=F32)
            xr, xi = xh[:half], xh[half:]
            y = jnp.concatenate([xr * hr - xi * hi, xr * hi + xi * hr], axis=0).astype(BF16)
            b = jnp.dot(gi, y, preferred_element_type=F32).astype(BF16)
            for hh in range(halves):
                o_ref[g, hh, kk] = pltpu.bitcast(b[:, hh * LANES:(hh + 1) * LANES], jnp.uint32)
        return carry

    lax.fori_loop(0, kt, body, 0)


def _fftb_call(a5, ff_bf, gi_bf, spec, order, kt):
    g, halves, n1, n2, _ = a5.shape
    r = 2 * n2
    return pl.pallas_call(
        _fftb_kernel,
        grid=(n1 // kt,),
        in_specs=[pl.BlockSpec((g, halves, kt, n2, LANES), lambda i: (0, 0, i, 0, 0)),
                  pl.BlockSpec((kt, r, r), lambda i: (i, 0, 0)),
                  pl.BlockSpec((kt, r, r), lambda i: (i, 0, 0)),
                  pl.BlockSpec((kt, r, halves * LANES), lambda i: (i, 0, order))],
        out_specs=pl.BlockSpec((g, halves, kt, n2, LANES), lambda i: (0, 0, i, 0, 0)),
        out_shape=jax.ShapeDtypeStruct(a5.shape, jnp.uint32),
        compiler_params=_cparams(),
        name="fft_inner_conv",
    )(a5, ff_bf, gi_bf, spec)


def _fftc_kernel(*refs, z_conv):
    if z_conv:
        b_ref, f_ref, z_ref, zw_ref, zb_ref, g_ref, gw_ref, gb_ref, d_ref, o_ref = refs
    else:
        b_ref, f_ref, z_ref, g_ref, gw_ref, gb_ref, d_ref, o_ref = refs
    n1 = f_ref.shape[1] // 2
    h1 = f_ref.shape[0] // 2
    d = d_ref[...]

    def body(n2, carry):
        b = pltpu.bitcast(b_ref[pl.ds(n2, n1, stride=FFT_N2), :], BF16)
        y = jnp.dot(f_ref[...], b, preferred_element_type=F32)
        rows = pl.ds(n2, h1, stride=FFT_N2)
        for m in range(2):
            o_ref[m, rows, :] = y[m * h1:(m + 1) * h1]
        return carry

    lax.fori_loop(0, FFT_N2, body, 0, unroll=FFT_UNROLL)
    for m in range(2):
        z = _short_conv(z_ref[m], zw_ref[...], zb_ref[...]) if z_conv else z_ref[m]
        gate = _short_conv(g_ref[m], gw_ref[...], gb_ref[...])
        o_ref[m] = gate * (o_ref[m] + d * z)


def _fftc_call(b4, f1i_bf, z4, z_col0, z_conv, hp4, g_col0, conv_w, conv_b, d2, order):
    groups, halves, rows, _ = b4.shape
    l = z4.shape[2]
    pair = lambda col0: pl.BlockSpec((None, 2, l, LANES), lambda i, j: (i, 0, 0, col0 + j))
    taps = lambda col0: [pl.BlockSpec((3, LANES), lambda i, j: (0, col0 + j)),
                         pl.BlockSpec((1, LANES), lambda i, j: (0, col0 + j))]
    in_specs = [pl.BlockSpec((None, None, rows, LANES), lambda i, j: (i, j, 0, 0)),
                pl.BlockSpec(f1i_bf.shape, lambda i, j: (0, 0)),
                pair(z_col0)]
    args = [b4, f1i_bf, z4]
    if z_conv:
        in_specs += taps(z_col0)
        args += [conv_w, conv_b]
    in_specs += [pair(g_col0)] + taps(g_col0) + [pl.BlockSpec((None, 1, LANES), lambda i, j: (order, 0, j))]
    args += [hp4, conv_w, conv_b, d2.reshape(d2.shape[0], 1, d2.shape[1])]
    return pl.pallas_call(
        functools.partial(_fftc_kernel, z_conv=z_conv),
        grid=(groups, halves),
        in_specs=in_specs,
        out_specs=pair(0),
        out_shape=jax.ShapeDtypeStruct((groups, 2, l, D_HYENA), F32),
        compiler_params=_cparams(),
        name="fft_outer_inverse_gate",
    )(*args)


def _hyena_spectrum(buf, seq_len):
    cst = _fft_consts(seq_len)
    n, n1 = cst["n"], cst["n1"]
    a = _fftr_call(buf, jnp.asarray(cst["f1r"]))
    a4 = a.reshape(a.shape[0], n1, 2 * FFT_N2, LANES)
    return _spec_call(a4, jnp.asarray(cst["ff"]), 1.0 / n, min(n1, 8))


def _hyena_latent(hp3, conv_w, conv_b, spec, d):
    b, l, _ = hp3.shape
    cst = _fft_consts(l)
    n1, h1 = cst["n1"], cst["h1"]
    groups = b // 2
    halves = D_HYENA // LANES
    kt = min(n1, 8)
    f1 = jnp.asarray(cst["f1"], BF16)
    ff = jnp.asarray(cst["ff_il"], BF16)
    gi = jnp.asarray(cst["gi_il"], BF16)
    f1i = jnp.asarray(cst["f1i"], BF16)
    hp4 = hp3.reshape(groups, 2, l, hp3.shape[2])
    cb = conv_b.reshape(1, -1)
    z = hp4
    for o in range(HYENA_ORDER):
        first = o == 0
        a = _ffta_call(z, 0, f1, conv_w, cb) if first else _ffta_call(z, 0, f1)
        bb = _fftb_call(a.reshape(groups, halves, n1, FFT_N2, LANES), ff, gi, spec, o, kt)
        z = _fftc_call(bb.reshape(groups, halves, n1 * FFT_N2, LANES), f1i, z, 0, first, hp4, (1 + o) * halves,
                       conv_w, cb, d, o)
    return z.reshape(b, l, D_HYENA)


@functools.lru_cache(maxsize=None)
def _dft_consts(seq_len):
    n = 2 * seq_len
    w = np.exp(-2j * np.pi * np.outer(np.arange(n), np.arange(n)) / n)
    fwd_full = np.concatenate([w.real, w.imag], axis=0)
    fwd = fwd_full[:, :seq_len]
    wi = np.conj(w[:, :seq_len]).T
    inv = np.concatenate([wi.real, -wi.imag], axis=1)
    f32 = lambda a: np.asarray(a, np.float32)
    return dict(n=n, fwd_full=f32(fwd_full), fwd=f32(fwd), inv=f32(inv))


def _cspec_kernel(buf_ref, f_ref, o_ref, *, scale):
    o_ref[...] = jnp.dot(f_ref[...], buf_ref[...], preferred_element_type=F32, precision=HIGHEST) * scale


def _ctx_spectrum(buf, seq_len):
    cst = _dft_consts(seq_len)
    n = cst["n"]
    return pl.pallas_call(
        functools.partial(_cspec_kernel, scale=1.0 / n),
        out_shape=jax.ShapeDtypeStruct((2 * n, buf.shape[1]), F32),
        compiler_params=_cparams(),
        name="ctx_filter_spectrum",
    )(buf, jnp.asarray(cst["fwd_full"]))


def _ctx_hyena_kernel(z_ref, w_ref, b_ref, f_ref, fi_ref, h_ref, d_ref, o_ref):
    n = 2 * z_ref.shape[0]
    zc = _short_conv(z_ref[...], w_ref[...], b_ref[...])
    cur = zc[:, :D_HYENA]
    for o in range(HYENA_ORDER):
        gate = zc[:, (1 + o) * D_HYENA:(2 + o) * D_HYENA]
        xh = jnp.dot(f_ref[...], cur.astype(BF16), preferred_element_type=F32)
        xr, xi = xh[:n], xh[n:]
        hr = h_ref[:n, o * D_HYENA:(o + 1) * D_HYENA]
        hi = h_ref[n:, o * D_HYENA:(o + 1) * D_HYENA]
        y = jnp.concatenate([xr * hr - xi * hi, xr * hi + xi * hr], axis=0).astype(BF16)
        conv = jnp.dot(fi_ref[...], y, preferred_element_type=F32)
        cur = gate * (conv + d_ref[o:o + 1, :] * cur)
    o_ref[...] = cur


def _ctx_hyena_call(hp3, conv_w, conv_b, spec, d):
    b, l, _ = hp3.shape
    cst = _dft_consts(l)
    n = cst["n"]
    w3 = 3 * D_HYENA
    full = lambda shape: pl.BlockSpec(shape, lambda i: (0,) * len(shape))
    return pl.pallas_call(
        _ctx_hyena_kernel,
        grid=(b,),
        in_specs=[pl.BlockSpec((None, l, w3), lambda i: (i, 0, 0)),
                  full((3, w3)), full((1, w3)), full((2 * n, l)), full((l, 2 * n)),
                  full((2 * n, HYENA_ORDER * D_HYENA)), full((HYENA_ORDER, D_HYENA))],
        out_specs=pl.BlockSpec((None, l, D_HYENA), lambda i: (i, 0, 0)),
        out_shape=jax.ShapeDtypeStruct((b, l, D_HYENA), F32),
        compiler_params=_cparams(),
        name="ctx_hyena",
    )(hp3, conv_w, conv_b.reshape(1, -1), jnp.asarray(cst["fwd"], BF16), jnp.asarray(cst["inv"], BF16),
      spec, d)


def _pool_kernel(p_ref, w_ref, sc_ref, o_ref):
    x = p_ref[...]
    l, c = x.shape
    half = pl.program_id(1)
    zpad = jnp.zeros((POOL_PAD, c), F32)
    xp = jnp.concatenate([zpad, x, zpad], axis=0)
    tot = l + 2 * POOL_PAD
    sums = []
    f = xp
    for win in POOL_WINDOWS:
        f = f + pltpu.roll(f, tot - win // 2, axis=0)
        sums.append(pltpu.roll(f, win // 2, axis=0)[POOL_PAD:POOL_PAD + l])
    t = lax.broadcasted_iota(jnp.int32, (l, c), 0)
    grp = lax.broadcasted_iota(jnp.int32, (l, c), 1) // POOL_GROUP + half * (LANES // POOL_GROUP)
    total = sums[-1]
    cnt = None
    for gi in range(len(POOL_WINDOWS) - 1, -1, -1):
        win = POOL_WINDOWS[gi]
        cw = (jnp.minimum(t + (win - win // 2), l) - jnp.maximum(t - win // 2, 0)).astype(F32)
        if cnt is None:
            cnt = cw
        else:
            sel = grp == gi
            total = jnp.where(sel, sums[gi], total)
            cnt = jnp.where(sel, cw, cnt)
    dlt = total / cnt - x
    y = jnp.dot(dlt.astype(BF16), w_ref[0], preferred_element_type=F32)
    o_ref[...] = y * sc_ref[...]


def _pool_call(hp3, pool_w, pool_scale):
    b, l, w = hp3.shape
    nh = D_POOL // LANES
    first = (w - D_POOL) // LANES
    per = LANES // POOL_GROUP
    wbd = jnp.zeros((nh, LANES, LANES), F32)
    for g in range(len(POOL_WINDOWS)):
        r0 = (g % per) * POOL_GROUP
        wbd = wbd.at[g // per, r0:r0 + POOL_GROUP, r0:r0 + POOL_GROUP].set(pool_w[g])
    return pl.pallas_call(
        _pool_kernel,
        grid=(b, nh),
        in_specs=[pl.BlockSpec((None, l, LANES), lambda i, j: (i, 0, first + j)),
                  pl.BlockSpec((1, LANES, LANES), lambda i, j: (j, 0, 0)),
                  pl.BlockSpec((1, LANES), lambda i, j: (0, j))],
        out_specs=pl.BlockSpec((None, l, LANES), lambda i, j: (i, 0, j)),
        out_shape=jax.ShapeDtypeStruct((b, l, D_POOL), F32),
        compiler_params=_cparams(),
        name="pool_mixer",
    )(hp3, wbd.astype(BF16), pool_scale.reshape(1, D_POOL))


def _outproj_kernel(x_ref, att_ref, hy_ref, pl_ref, gate_ref, w_ref, g_ref, b_ref, o_ref, *, alpha):
    mix = jnp.dot(att_ref[...], w_ref[0:D_ATTN, :], preferred_element_type=F32)
    mix += jnp.dot(hy_ref[...].astype(BF16), w_ref[D_ATTN:D_ATTN + D_HYENA, :], preferred_element_type=F32)
    mix += jnp.dot(pl_ref[...].astype(BF16), w_ref[D_ATTN + D_HYENA:, :], preferred_element_type=F32)
    y = alpha * x_ref[...] + gate_ref[0] * mix
    o_ref[...] = _layernorm(y, LN_EPS) * g_ref[...] + b_ref[...]


def _outproj_call(x2, att2, hy2, pl2, mod3, mod_row, w_out_bf, g, b, alpha, tm):
    r, d = x2.shape
    row = lambda w: pl.BlockSpec((tm, w), lambda i: (i, 0))
    vec = pl.BlockSpec((1, d), lambda i: (0, 0))
    return pl.pallas_call(
        functools.partial(_outproj_kernel, alpha=alpha),
        grid=(r // tm,),
        in_specs=[row(d), row(D_ATTN), row(D_HYENA), row(D_POOL),
                  pl.BlockSpec((1, 1, d), lambda i: (mod_row(i) * 6 + 2, 0, 0)),
                  pl.BlockSpec(w_out_bf.shape, lambda i: (0, 0)),
                  vec, vec],
        out_specs=row(d),
        out_shape=jax.ShapeDtypeStruct((r, d), F32),
        compiler_params=_cparams(),
        name="out_proj_deepnorm",
    )(x2, att2, hy2, pl2, mod3, w_out_bf, g.reshape(1, d), b.reshape(1, d))


def _ffn_kernel(*refs, alpha, n_exp, n_chunk):
    if n_exp > 1:
        (x_ref, sh_ref, sc_ref, gate_ref, rw_ref, wg_ref, wu_ref, wd_ref, g_ref, b_ref,
         o_ref, u_ref, acc_ref, comb_ref) = refs
    else:
        (x_ref, sh_ref, sc_ref, gate_ref, wg_ref, wu_ref, wd_ref, g_ref, b_ref,
         o_ref, u_ref, acc_ref) = refs
    j = pl.program_id(1)

    @pl.when(j == 0)
    def _():
        u = (_layernorm(x_ref[...], ADALN_EPS) * (1.0 + sc_ref[0]) + sh_ref[0]).astype(BF16)
        u_ref[...] = u
        acc_ref[...] = jnp.zeros_like(acc_ref)
        if n_exp > 1:
            logits = jnp.dot(u, rw_ref[...].astype(BF16), preferred_element_type=F32)
            lane = lax.broadcasted_iota(jnp.int32, logits.shape, 1)
            m1 = jnp.max(logits, axis=-1, keepdims=True)
            i1 = jnp.min(jnp.where(logits == m1, lane, n_exp), axis=-1, keepdims=True)
            rest = jnp.where(lane == i1, -jnp.inf, logits)
            m2 = jnp.max(rest, axis=-1, keepdims=True)
            i2 = jnp.min(jnp.where(rest == m2, lane, n_exp), axis=-1, keepdims=True)
            e2 = jnp.exp(m2 - m1)
            den = 1.0 + e2
            comb_ref[...] = jnp.where(lane == i1, 1.0 / den, 0.0) + jnp.where(lane == i2, e2 / den, 0.0)

    u = u_ref[...]
    a = jnp.dot(u, wg_ref[0].astype(BF16), preferred_element_type=F32)
    bb = jnp.dot(u, wu_ref[0].astype(BF16), preferred_element_type=F32)
    h = a * jax.nn.sigmoid(a) * bb
    if n_exp > 1:
        comb = comb_ref[...]
        lane = lax.broadcasted_iota(jnp.int32, comb.shape, 1)
        h = h * jnp.sum(jnp.where(lane == j // n_chunk, comb, 0.0), axis=-1, keepdims=True)
    acc_ref[...] += jnp.dot(h.astype(BF16), wd_ref[0].astype(BF16), preferred_element_type=F32)

    @pl.when(j == n_exp * n_chunk - 1)
    def _():
        y = alpha * x_ref[...] + gate_ref[0] * acc_ref[...]
        o_ref[...] = _layernorm(y, LN_EPS) * g_ref[...] + b_ref[...]


def _ffn_call(x2, mod3, mod_row, router_w, w_gate, w_up, w_down, w_base, n_exp, g, b, alpha, tm, fc):
    r, d = x2.shape
    dff = w_gate.shape[2]
    n_chunk = dff // fc
    steps = n_exp * n_chunk
    vec = pl.BlockSpec((1, d), lambda i, j: (0, 0))
    mod = lambda which: pl.BlockSpec((1, 1, d), lambda i, j: (mod_row(i) * 6 + which, 0, 0))
    in_specs = [pl.BlockSpec((tm, d), lambda i, j: (i, 0)), mod(3), mod(4), mod(5)]
    args = [x2, mod3, mod3, mod3]
    scratch = [pltpu.VMEM((tm, d), BF16), pltpu.VMEM((tm, d), F32)]
    if n_exp > 1:
        in_specs.append(pl.BlockSpec(router_w.shape, lambda i, j: (0, 0)))
        args.append(router_w)
        scratch.append(pltpu.VMEM((tm, n_exp), F32))
    in_specs += [pl.BlockSpec((1, d, fc), lambda i, j: (w_base + j // n_chunk, 0, j % n_chunk)),
                 pl.BlockSpec((1, d, fc), lambda i, j: (w_base + j // n_chunk, 0, j % n_chunk)),
                 pl.BlockSpec((1, fc, d), lambda i, j: (w_base + j // n_chunk, j % n_chunk, 0)),
                 vec, vec]
    args += [w_gate, w_up, w_down, g.reshape(1, d), b.reshape(1, d)]
    return pl.pallas_call(
        functools.partial(_ffn_kernel, alpha=alpha, n_exp=n_exp, n_chunk=n_chunk),
        grid=(r // tm, steps),
        in_specs=in_specs,
        out_specs=pl.BlockSpec((tm, d), lambda i, j: (i, 0)),
        out_shape=jax.ShapeDtypeStruct((r, d), F32),
        scratch_shapes=scratch,
        compiler_params=_cparams(),
        name="channel_mixer",
    )(*args)


MOE_WINDOW = 1024
MOE_TILE = 64
MOE_SUPER = 16
MOE_FC = 512
MOE_GCHUNK = 512
MOE_SUB = 256


def _router_kernel(x_ref, sh_ref, sc_ref, rw_ref, tri_ref, u_ref, comb_ref, pos_ref):
    u = (_layernorm(x_ref[...], ADALN_EPS) * (1.0 + sc_ref[0]) + sh_ref[0]).astype(BF16)
    u_ref[...] = u
    logits = jnp.dot(u, rw_ref[...].astype(BF16), preferred_element_type=F32)
    n_exp = logits.shape[1]
    lane = lax.broadcasted_iota(jnp.int32, logits.shape, 1)
    m1 = jnp.max(logits, axis=-1, keepdims=True)
    i1 = jnp.min(jnp.where(logits == m1, lane, n_exp), axis=-1, keepdims=True)
    rest = jnp.where(lane == i1, -jnp.inf, logits)
    m2 = jnp.max(rest, axis=-1, keepdims=True)
    i2 = jnp.min(jnp.where(rest == m2, lane, n_exp), axis=-1, keepdims=True)
    e2 = jnp.exp(m2 - m1)
    den = 1.0 + e2
    comb = jnp.where(lane == i1, 1.0 / den, 0.0) + jnp.where(lane == i2, e2 / den, 0.0)
    comb_ref[...] = comb
    routed = jnp.where(comb > 0.0, 1.0, 0.0).astype(BF16)
    pos_ref[...] = jnp.dot(tri_ref[...], routed, preferred_element_type=F32) - 1.0


def _router_call(x2, mod3, mod_row, router_w):
    r, d = x2.shape
    n_exp = router_w.shape[1]
    tm = MOE_WINDOW
    tri = jnp.asarray(np.tril(np.ones((tm, tm), np.float32)), BF16)
    mod = lambda which: pl.BlockSpec((1, 1, d), lambda i: (mod_row(i) * 6 + which, 0, 0))
    small = pl.BlockSpec((tm, n_exp), lambda i: (i, 0))
    return pl.pallas_call(
        _router_kernel,
        grid=(r // tm,),
        in_specs=[pl.BlockSpec((tm, d), lambda i: (i, 0)), mod(3), mod(4),
                  pl.BlockSpec(router_w.shape, lambda i: (0, 0)),
                  pl.BlockSpec((tm, tm), lambda i: (0, 0))],
        out_specs=[pl.BlockSpec((tm, d), lambda i: (i, 0)), small, small],
        out_shape=[jax.ShapeDtypeStruct((r, d), BF16), jax.ShapeDtypeStruct((r, n_exp), F32),
                   jax.ShapeDtypeStruct((r, n_exp), F32)],
        compiler_params=_cparams(),
        name="moe_router",
    )(x2, mod3, mod3, router_w, tri)


def _moe_plan(comb, pos_f, n_tok):
    n_exp = comb.shape[1]
    win, tile = MOE_WINDOW, MOE_TILE
    n_win = n_tok // win
    cap_tiles = 2 * win // tile + n_exp
    i32 = jnp.int32
    mask = (comb > 0.0).reshape(n_win, win, n_exp)
    pos = pos_f.astype(i32).reshape(n_win, win, n_exp)
    cnt = pos[:, -1, :] + 1
    ntile = (cnt + tile - 1) // tile
    woff = jnp.cumsum(ntile, axis=1) - ntile
    rows = jnp.where(mask, tile * woff[:, None, :] + pos, -1)
    row_a = jnp.max(rows, axis=-1)
    row_b = jnp.max(jnp.where(rows == row_a[..., None], -1, rows), axis=-1)
    rowsel = jnp.stack([row_a, row_b], axis=1)
    win_rows = tile * jnp.sum(ntile, axis=1)

    etiles = jnp.sum(ntile, axis=0)
    nsup = (etiles + MOE_SUPER - 1) // MOE_SUPER
    ebase = MOE_SUPER * (jnp.cumsum(nsup) - nsup)
    erank = ebase[None, :] + jnp.cumsum(ntile, axis=0) - ntile
    n_sup = -(-(n_win * cap_tiles) // MOE_SUPER) + n_exp
    n_rank = n_sup * MOE_SUPER
    slots = win // tile
    s = jnp.arange(slots, dtype=i32)[None, None, :]
    ok = s < ntile[:, :, None]
    rank = jnp.where(ok, erank[:, :, None] + s, n_rank)
    wm_tile = (jnp.arange(n_win, dtype=i32) * cap_tiles)[:, None, None] + woff[:, :, None] + s
    src = jnp.zeros((n_rank + 1,), i32).at[rank.reshape(-1)].set(wm_tile.reshape(-1))[:n_rank]
    sup_total = jnp.sum(nsup)
    g = jnp.arange(n_sup, dtype=i32)
    sup_valid = (g < sup_total).astype(i32)
    sup_exp = jnp.clip(jnp.sum((g[:, None] >= jnp.cumsum(nsup)[None, :]).astype(i32), axis=1), 0, n_exp - 1)
    sup_exp = jnp.where(sup_valid > 0, sup_exp, sup_exp[jnp.maximum(sup_total - 1, 0)])

    tile_rank = jnp.zeros((n_win * cap_tiles + 1,), i32).at[
        jnp.where(ok, wm_tile, n_win * cap_tiles).reshape(-1)].set(
        jnp.minimum(rank, n_rank - 1).reshape(-1))[:n_win * cap_tiles]
    comb3 = comb.reshape(n_win, win, n_exp)
    gate_a = jnp.sum(jnp.where(rows == row_a[..., None], comb3, 0.0), axis=-1)
    gate_b = jnp.sum(jnp.where((rows == row_b[..., None]) & mask, comb3, 0.0), axis=-1)
    tok_sel = jnp.stack([row_a.astype(F32), row_b.astype(F32), gate_a, gate_b], axis=-1).reshape(n_tok, 4)
    return dict(n_win=n_win, cap_tiles=cap_tiles, n_sup=n_sup, n_rank=n_rank,
                rowsel=rowsel, win_rows=win_rows.astype(i32),
                src=src, sup_exp=sup_exp.astype(i32), sup_valid=sup_valid,
                tile_rank=tile_rank, tok_sel=tok_sel)


def _dispatch_kernel(nrows_ref, u_ref, sel_ref, o_ref):
    w = pl.program_id(0)
    r1 = sel_ref[0:1, :]
    r2 = sel_ref[1:2, :]
    for cidx in range(o_ref.shape[0] // MOE_GCHUNK):
        base = cidx * MOE_GCHUNK
        rows = pl.ds(base, MOE_GCHUNK)

        @pl.when(base < nrows_ref[w])
        def _():
            rid = lax.broadcasted_iota(jnp.int32, (MOE_GCHUNK, r1.shape[1]), 0) + base
            onehot = jnp.where(rid == r1, 1.0, jnp.where(rid == r2, 1.0, 0.0)).astype(BF16)
            o_ref[rows, :] = jnp.dot(onehot, u_ref[...], preferred_element_type=F32).astype(o_ref.dtype)

        @pl.when(base >= nrows_ref[w])
        def _():
            o_ref[rows, :] = jnp.zeros((MOE_GCHUNK, o_ref.shape[1]), o_ref.dtype)


def _dispatch_call(u, plan):
    n_tok, d = u.shape
    cap_rows = plan["cap_tiles"] * MOE_TILE
    n_win = plan["n_win"]
    return pl.pallas_call(
        _dispatch_kernel,
        grid_spec=pltpu.PrefetchScalarGridSpec(
            num_scalar_prefetch=1,
            grid=(n_win,),
            in_specs=[pl.BlockSpec((MOE_WINDOW, d), lambda w, n: (w, 0)),
                      pl.BlockSpec((None, TOP_K, MOE_WINDOW), lambda w, n: (w, 0, 0))],
            out_specs=pl.BlockSpec((cap_rows, d), lambda w, n: (w, 0))),
        out_shape=jax.ShapeDtypeStruct((n_win * cap_rows, d), BF16),
        compiler_params=_cparams(),
        name="moe_dispatch",
    )(plan["win_rows"], u, plan["rowsel"])


def _expert_kernel(src_ref, exp_ref, valid_ref, *refs, n_chunk):
    tiles = refs[:MOE_SUPER]
    wg_ref, wu_ref, wd_ref, o_ref, u_ref, acc_ref = refs[MOE_SUPER:]
    g = pl.program_id(0)
    j = pl.program_id(1)
    ok = valid_ref[g] > 0

    @pl.when(j == 0)
    def _():
        for k in range(MOE_SUPER):
            u_ref[k * MOE_TILE:(k + 1) * MOE_TILE, :] = tiles[k][...]
        acc_ref[...] = jnp.zeros_like(acc_ref)

    @pl.when(ok)
    def _():
        u = u_ref[...]
        part = None
        for c0 in range(0, wg_ref.shape[2], MOE_SUB):
            a = jnp.dot(u, wg_ref[0, :, c0:c0 + MOE_SUB].astype(BF16), preferred_element_type=F32)
            b = jnp.dot(u, wu_ref[0, :, c0:c0 + MOE_SUB].astype(BF16), preferred_element_type=F32)
            h = (a * jax.nn.sigmoid(a) * b).astype(BF16)
            y = jnp.dot(h, wd_ref[0, c0:c0 + MOE_SUB, :].astype(BF16), preferred_element_type=F32)
            part = y if part is None else part + y
        acc_ref[...] += part

    @pl.when(j == n_chunk - 1)
    def _():
        o_ref[...] = acc_ref[...].astype(o_ref.dtype)


def _expert_call(xw, plan, w_gate, w_up, w_down, layer_base, fc):
    d = xw.shape[1]
    dff = w_gate.shape[2]
    n_chunk = dff // fc
    rows = MOE_SUPER * MOE_TILE

    def tile_spec(k):
        return pl.BlockSpec((MOE_TILE, d), lambda g, j, src, ex, va: (src[g * MOE_SUPER + k], 0))

    def chunk(j, va, g):
        return jnp.where(va[g] > 0, j, n_chunk - 1)

    in_specs = [tile_spec(k) for k in range(MOE_SUPER)] + [
        pl.BlockSpec((1, d, fc), lambda g, j, src, ex, va: (layer_base + ex[g], 0, chunk(j, va, g))),
        pl.BlockSpec((1, d, fc), lambda g, j, src, ex, va: (layer_base + ex[g], 0, chunk(j, va, g))),
        pl.BlockSpec((1, fc, d), lambda g, j, src, ex, va: (layer_base + ex[g], chunk(j, va, g), 0))]
    return pl.pallas_call(
        functools.partial(_expert_kernel, n_chunk=n_chunk),
        grid_spec=pltpu.PrefetchScalarGridSpec(
            num_scalar_prefetch=3,
            grid=(plan["n_sup"], n_chunk),
            in_specs=in_specs,
            out_specs=pl.BlockSpec((rows, d), lambda g, j, src, ex, va: (g, 0)),
            scratch_shapes=[pltpu.VMEM((rows, d), BF16), pltpu.VMEM((rows, d), F32)]),
        out_shape=jax.ShapeDtypeStruct((plan["n_sup"] * rows, d), BF16),
        compiler_params=_cparams(),
        name="moe_experts",
    )(plan["src"], plan["sup_exp"], plan["sup_valid"], *([xw] * MOE_SUPER), w_gate, w_up, w_down)


def _combine_kernel(rank_ref, *refs, alpha, n_tiles):
    tile_refs = refs[:n_tiles]
    sel_ref, x_ref, gate_ref, g_ref, b_ref, o_ref, rows_ref, acc_ref = refs[n_tiles:]
    for k in range(n_tiles):
        rows_ref[k * MOE_TILE:(k + 1) * MOE_TILE, :] = tile_refs[k][...]
    sel = sel_ref[...]
    lane = lax.broadcasted_iota(jnp.int32, sel.shape, 1)
    col = lambda k: jnp.sum(jnp.where(lane == k, sel, 0.0), axis=-1, keepdims=True)
    row_a = col(0).astype(jnp.int32)
    row_b = col(1).astype(jnp.int32)
    w_a = col(2)
    w_b = col(3)
    n_rows = rows_ref.shape[0]
    for c0 in range(0, n_rows, MOE_GCHUNK):
        rid = lax.broadcasted_iota(jnp.int32, (sel.shape[0], MOE_GCHUNK), 1) + c0
        q = jnp.where(rid == row_a, w_a, jnp.where(rid == row_b, w_b, 0.0)).astype(BF16)
        part = jnp.dot(q, rows_ref[c0:c0 + MOE_GCHUNK, :], preferred_element_type=F32)
        if c0 == 0:
            acc_ref[...] = part
        else:
            acc_ref[...] += part
    y = alpha * x_ref[...] + gate_ref[0] * acc_ref[...]
    o_ref[...] = _layernorm(y, LN_EPS) * g_ref[...] + b_ref[...]


def _combine_call(ys, plan, x2, mod3, mod_row, g, b, alpha):
    n_tok, d = x2.shape
    n_tiles = plan["cap_tiles"]

    def tile_spec(k):
        return pl.BlockSpec((MOE_TILE, d), lambda w, rk: (rk[w * n_tiles + k], 0))

    win = lambda width: pl.BlockSpec((MOE_WINDOW, width), lambda w, rk: (w, 0))
    vec = pl.BlockSpec((1, d), lambda w, rk: (0, 0))
    return pl.pallas_call(
        functools.partial(_combine_kernel, alpha=alpha, n_tiles=n_tiles),
        grid_spec=pltpu.PrefetchScalarGridSpec(
            num_scalar_prefetch=1,
            grid=(plan["n_win"],),
            in_specs=[tile_spec(k) for k in range(n_tiles)] + [
                win(4), win(d),
                pl.BlockSpec((1, 1, d), lambda w, rk: (mod_row(w) * 6 + 5, 0, 0)),
                vec, vec],
            out_specs=win(d),
            scratch_shapes=[pltpu.VMEM((n_tiles * MOE_TILE, d), BF16), pltpu.VMEM((MOE_WINDOW, d), F32)]),
        out_shape=jax.ShapeDtypeStruct((n_tok, d), F32),
        compiler_params=_cparams(),
        name="moe_combine_deepnorm",
    )(plan["tile_rank"], *([ys] * n_tiles), plan["tok_sel"], x2, mod3, g.reshape(1, d), b.reshape(1, d))


def _moe_call(x2, mod3, tok_row, router_w, w_gate, w_up, w_down, layer_base, g, b, alpha, fc):
    n_tok = x2.shape[0]
    u, comb, pos = _router_call(x2, mod3, lambda i: tok_row(i * MOE_WINDOW), router_w)
    plan = _moe_plan(comb, pos, n_tok)
    xw = _dispatch_call(u, plan)
    ys = _expert_call(xw, plan, w_gate, w_up, w_down, layer_base, fc)
    return _combine_call(ys, plan, x2, mod3, lambda w: tok_row(w * MOE_WINDOW), g, b, alpha)


def _pick_tile(n, pref):
    t = min(n, pref)
    while n % t:
        t //= 2
    return t


def kernel(x, c, ctx, c_ctx, w_mod, b_mod, w_in, q_gain, k_gain, hy_conv_w, hy_conv_b, hy_f_w1, hy_f_b1, hy_f_freq, hy_f_w2, hy_f_b2, hy_f_w3, hy_d, pool_w, pool_scale, w_out, ln1_g, ln1_b, ln2_g, ln2_b, ffn_w_gate, ffn_w_up, ffn_w_down, router_w, moe_w_gate, moe_w_up, moe_w_down):
    bsz, seq, d = x.shape
    clen = ctx.shape[1]
    depth = w_mod.shape[0]
    alpha = (2.0 * depth) ** 0.25
    assert seq % MOE_WINDOW == 0 and bsz % 2 == 0
    n_exp = router_w.shape[2]
    moe_wg = moe_w_gate.reshape((-1,) + moe_w_gate.shape[2:])
    moe_wu = moe_w_up.reshape((-1,) + moe_w_up.shape[2:])
    moe_wd = moe_w_down.reshape((-1,) + moe_w_down.shape[2:])

    n_rows = -(-(bsz + 1) // 8) * 8
    cc = jnp.zeros((n_rows, d), F32).at[:bsz].set(c).at[bsz].set(c_ctx)
    mod = _mod_call(cc, w_mod, b_mod)

    tm_lat = _pick_tile(seq, 512)
    tm_ctx = _pick_tile(clen, 512)
    tm_ffn = _pick_tile(seq, 1024)
    tm_ffn_ctx = _pick_tile(bsz * clen, 1024)
    lat_row = lambda tm: (lambda i: i // (seq // tm))
    ctx_row = lambda i: bsz
    rope_lat = tuple(jnp.asarray(t) for t in _rope_tables(seq, True))
    rope_ctx = tuple(jnp.asarray(t) for t in _rope_tables(clen, False))
    head_avg = np.kron(np.eye(QK_W // HEAD_DIM), np.full((HEAD_DIM, HEAD_DIM), 1.0 / HEAD_DIM))
    head_avg = jnp.asarray(head_avg, BF16)

    xl = x.reshape(bsz * seq, d)
    xc = ctx.reshape(bsz * clen, d)
    for l in range(depth):
        last = l == depth - 1
        mod3 = mod[l].reshape(n_rows * 6, 1, d)
        w_in_bf = w_in[l].astype(BF16)
        w_out_bf = w_out[l].astype(BF16)
        gain = jnp.concatenate([jnp.tile(q_gain[l] * (ATTN_SCALE * LOG2E), N_Q_HEADS),
                                jnp.tile(k_gain[l], N_KV_HEADS)]).reshape(1, QK_W)
        shift = (1.02 * HEAD_DIM * ATTN_SCALE * LOG2E) * jnp.max(jnp.abs(q_gain[l])) * jnp.max(jnp.abs(k_gain[l]))
        lane64 = (jnp.arange(LANES) == HEAD_DIM).astype(F32)
        ext = jnp.stack([lane64, -shift * lane64, lane64])
        fast = (shift <= ATTN_SAFE_SHIFT).astype(jnp.int32).reshape(1)
        filt_w = (hy_f_w1[l], hy_f_b1[l], hy_f_freq[l], hy_f_w2[l], hy_f_b2[l], hy_f_w3[l])
        j = l // 2
        routed = l % 2 == 1
        if routed:
            ffn_w = (router_w[j], moe_wg, moe_wu, moe_wd, j * n_exp, n_exp)
        else:
            ffn_w = (None, ffn_w_gate, ffn_w_up, ffn_w_down, j, 1)
        fc = 256

        q_c, k_c, v_c, hp_c = _proj_call(xc, mod3, ctx_row, clen, w_in_bf, gain, head_avg, rope_ctx, ext, tm_ctx)
        k_c3 = k_c.reshape(bsz, clen, -1)
        v_c3 = v_c.reshape(bsz, clen, -1)
        if not last:
            hp_c3 = hp_c.reshape(bsz, clen, -1)
            att_c = _attn_call(fast, q_c.reshape(bsz, clen, -1), (k_c3, v_c3), _pick_tile(clen, 256))
            spec_c = _ctx_spectrum(_filter_call(clen, *filt_w), clen)
            hy_c = _ctx_hyena_call(hp_c3, hy_conv_w[l], hy_conv_b[l], spec_c, hy_d[l])
            pl_c = _pool_call(hp_c3, pool_w[l], pool_scale[l])
            xc_mid = _outproj_call(xc, att_c.reshape(bsz * clen, D_ATTN), hy_c.reshape(bsz * clen, D_HYENA),
                                   pl_c.reshape(bsz * clen, D_POOL), mod3, ctx_row, w_out_bf,
                                   ln1_g[l], ln1_b[l], alpha, tm_ctx)

        q_l, k_l, v_l, hp_l = _proj_call(xl, mod3, lat_row(tm_lat), seq, w_in_bf, gain, head_avg, rope_lat, ext,
                                         tm_lat)
        hp_l3 = hp_l.reshape(bsz, seq, -1)
        att = _attn_call(fast, q_l.reshape(bsz, seq, -1),
                         (k_l.reshape(bsz, seq, -1), v_l.reshape(bsz, seq, -1), k_c3, v_c3), _pick_tile(seq, 256))
        spec = _hyena_spectrum(_filter_call(seq, *filt_w), seq)
        hyo = _hyena_latent(hp_l3, hy_conv_w[l], hy_conv_b[l], spec, hy_d[l])
        plo = _pool_call(hp_l3, pool_w[l], pool_scale[l])
        xl = _outproj_call(xl, att.reshape(bsz * seq, D_ATTN), hyo.reshape(bsz * seq, D_HYENA),
                           plo.reshape(bsz * seq, D_POOL), mod3, lat_row(tm_lat), w_out_bf,
                           ln1_g[l], ln1_b[l], alpha, tm_lat)
        if routed:
            xl = _moe_call(xl, mod3, lambda t: t // seq, router_w[j], moe_wg, moe_wu, moe_wd, j * n_exp,
                           ln2_g[l], ln2_b[l], alpha, MOE_FC)
        else:
            xl = _ffn_call(xl, mod3, lat_row(tm_ffn), *ffn_w, ln2_g[l], ln2_b[l], alpha, tm_ffn, fc)

        if not last:
            xc = _ffn_call(xc_mid, mod3, ctx_row, *ffn_w, ln2_g[l], ln2_b[l], alpha, tm_ffn_ctx, fc)
    return xl.reshape(bsz, seq, d)
```

```python
import functools
import math

import numpy as np
import jax
import jax.numpy as jnp
from jax import lax
from jax.experimental import pallas as pl
from jax.experimental.pallas import tpu as pltpu

F32 = jnp.float32
BF16 = jnp.bfloat16
HIGHEST = lax.Precision.HIGHEST

GRID_W = 64
HEAD_DIM = 64
N_Q_HEADS = 8
N_KV_HEADS = 2
GQA_GROUP = N_Q_HEADS // N_KV_HEADS
D_ATTN = N_Q_HEADS * HEAD_DIM
KV_W = N_KV_HEADS * HEAD_DIM
QK_W = D_ATTN + KV_W
ATTN_SCALE = HEAD_DIM ** -0.5
ROPE_THETA = 10000.0
ROPE_AXIS_DIM = HEAD_DIM // 2
QK_EPS = 1e-6
D_HYENA = 256
HYENA_ORDER = 2
HYENA_BANDS = 16
HYENA_EMB = 1 + 2 * HYENA_BANDS
HYENA_DECAY_TARGET = 1e-2
HYENA_FAST_DECAY_PCT = 0.3
HYENA_SLOW_DECAY_PCT = 1.5
D_POOL = 256
POOL_WINDOWS = (2, 4, 8, 16)
POOL_GROUP = D_POOL // len(POOL_WINDOWS)
POOL_PAD = 8
TOP_K = 2
LN_EPS = 1e-5
ADALN_EPS = 1e-6
LOG2E = math.log2(math.e)

LANES = 128
FFT_N2 = 64
FFT_UNROLL = 4
VMEM_LIMIT = 56 * 1024 * 1024
FFN_FC = 256


def _cparams():
    return pltpu.CompilerParams(vmem_limit_bytes=VMEM_LIMIT)


def _layernorm(x, eps):
    mu = jnp.mean(x, axis=-1, keepdims=True)
    xc = x - mu
    var = jnp.mean(xc * xc, axis=-1, keepdims=True)
    return xc * lax.rsqrt(var + eps)


def _mod_kernel(c_ref, w_ref, b_ref, o_ref):
    c = c_ref[...]
    s = c * jax.nn.sigmoid(c)
    o_ref[0] = jnp.dot(s, w_ref[0], preferred_element_type=F32, precision=HIGHEST) + b_ref[0]


def _mod_call(cc, w_mod, b_mod):
    depth, d, d6 = w_mod.shape
    r = cc.shape[0]
    tn = 1536
    return pl.pallas_call(
        _mod_kernel,
        grid=(depth, d6 // tn),
        in_specs=[pl.BlockSpec((r, d), lambda l, j: (0, 0)),
                  pl.BlockSpec((1, d, tn), lambda l, j: (l, 0, j)),
                  pl.BlockSpec((1, 1, tn), lambda l, j: (l, 0, j))],
        out_specs=pl.BlockSpec((1, r, tn), lambda l, j: (l, 0, j)),
        out_shape=jax.ShapeDtypeStruct((depth, r, d6), F32),
        compiler_params=_cparams(),
        name="adaln_mod",
    )(cc, w_mod, b_mod.reshape(depth, 1, d6))


def _rope_tables(seq_len, rope):
    lane = np.arange(LANES)
    d = lane % HEAD_DIM
    if not rope:
        one = np.ones((seq_len, LANES), np.float32)
        zero = np.zeros((seq_len, LANES), np.float32)
        return one, zero, zero
    t = np.arange(seq_len)
    row = (t // GRID_W).astype(np.float64)
    col = (t % GRID_W).astype(np.float64)
    half = ROPE_AXIS_DIM // 2
    inv = ROPE_THETA ** (-np.arange(0, ROPE_AXIS_DIM, 2, dtype=np.float64) / ROPE_AXIS_DIM)
    pos = np.where((d // ROPE_AXIS_DIM)[None, :] == 0, row[:, None], col[:, None])
    ang = pos * inv[d % half][None, :]
    is_b = ((d % ROPE_AXIS_DIM) >= half)[None, :]
    cos = np.cos(ang)
    sin = np.sin(ang)
    s_up = np.where(is_b, sin, 0.0)
    s_dn = np.where(is_b, 0.0, -sin)
    return cos.astype(np.float32), s_up.astype(np.float32), s_dn.astype(np.float32)


def _store_padded_heads(blk, ext, out_ref, col0):
    lo = lax.broadcasted_iota(jnp.int32, blk.shape, 1) < HEAD_DIM
    out_ref[:, col0:col0 + LANES] = jnp.where(lo, blk, ext).astype(out_ref.dtype)
    out_ref[:, col0 + LANES:col0 + 2 * LANES] = jnp.where(lo, pltpu.roll(blk, HEAD_DIM, axis=1),
                                                          ext).astype(out_ref.dtype)


def _proj_kernel(x_ref, sh_ref, sc_ref, w_ref, gain_ref, bd_ref, cos_ref, sup_ref, sdn_ref, ext_ref,
                 q_ref, k_ref, v_ref, hp_ref):
    u = _layernorm(x_ref[...], ADALN_EPS) * (1.0 + sc_ref[0]) + sh_ref[0]
    p = jnp.dot(u.astype(BF16), w_ref[...], preferred_element_type=F32)
    qk = p[:, :QK_W]
    ms = jnp.dot((qk * qk).astype(BF16), bd_ref[...], preferred_element_type=F32)
    qn = qk * lax.rsqrt(ms + QK_EPS) * gain_ref[...]
    cos = cos_ref[...]
    sup = sup_ref[...]
    sdn = sdn_ref[...]
    n_qblk = D_ATTN // LANES
    for j in range(QK_W // LANES):
        blk = qn[:, j * LANES:(j + 1) * LANES]
        up = pltpu.roll(blk, ROPE_AXIS_DIM // 2, axis=1)
        dn = pltpu.roll(blk, LANES - ROPE_AXIS_DIM // 2, axis=1)
        rot = blk * cos + up * sup + dn * sdn
        if j < n_qblk:
            _store_padded_heads(rot, ext_ref[0:1, :], q_ref, 2 * j * LANES)
        else:
            _store_padded_heads(rot, ext_ref[1:2, :], k_ref, 2 * (j - n_qblk) * LANES)
    _store_padded_heads(p[:, QK_W:QK_W + KV_W], ext_ref[2:3, :], v_ref, 0)
    hp_ref[...] = p[:, QK_W + KV_W:]


def _proj_call(x2, mod3, mod_row, seq_len, w_in_bf, gain, bd, tables, ext, tm):
    r, d = x2.shape
    d_in = w_in_bf.shape[1]
    d_hp = d_in - QK_W - KV_W
    nseq = seq_len // tm
    cos, sup, sdn = tables
    tab_spec = pl.BlockSpec((tm, LANES), lambda i: (i % nseq, 0))
    return pl.pallas_call(
        _proj_kernel,
        grid=(r // tm,),
        in_specs=[pl.BlockSpec((tm, d), lambda i: (i, 0)),
                  pl.BlockSpec((1, 1, d), lambda i: (mod_row(i) * 6 + 0, 0, 0)),
                  pl.BlockSpec((1, 1, d), lambda i: (mod_row(i) * 6 + 1, 0, 0)),
                  pl.BlockSpec((d, d_in), lambda i: (0, 0)),
                  pl.BlockSpec((1, QK_W), lambda i: (0, 0)),
                  pl.BlockSpec((QK_W, QK_W), lambda i: (0, 0)),
                  tab_spec, tab_spec, tab_spec,
                  pl.BlockSpec((3, LANES), lambda i: (0, 0))],
        out_specs=[pl.BlockSpec((tm, N_Q_HEADS * LANES), lambda i: (i, 0)),
                   pl.BlockSpec((tm, N_KV_HEADS * LANES), lambda i: (i, 0)),
                   pl.BlockSpec((tm, N_KV_HEADS * LANES), lambda i: (i, 0)),
                   pl.BlockSpec((tm, d_hp), lambda i: (i, 0))],
        out_shape=[jax.ShapeDtypeStruct((r, N_Q_HEADS * LANES), BF16),
                   jax.ShapeDtypeStruct((r, N_KV_HEADS * LANES), BF16),
                   jax.ShapeDtypeStruct((r, N_KV_HEADS * LANES), BF16),
                   jax.ShapeDtypeStruct((r, d_hp), F32)],
        compiler_params=_cparams(),
        name="ln_mod_in_proj",
    )(x2, mod3, mod3, w_in_bf, gain, bd, cos, sup, sdn, ext)


ATTN_KEY_CHUNK = 1024
ATTN_SAFE_SHIFT = 60.0


def _attn_kernel(fast_ref, q_ref, *refs):
    o_ref = refs[-1]
    sources = [(refs[i], refs[i + 1]) for i in range(0, len(refs) - 1, 2)]
    tq = q_ref.shape[0]
    chunks = [(k_ref, v_ref, c0, min(ATTN_KEY_CHUNK, k_ref.shape[0] - c0))
              for k_ref, v_ref in sources for c0 in range(0, k_ref.shape[0], ATTN_KEY_CHUNK)]

    def run(running_max):
        for h in range(N_KV_HEADS):
            qs = jnp.concatenate([q_ref[:, (h * GQA_GROUP + g) * LANES:(h * GQA_GROUP + g + 1) * LANES]
                                  for g in range(GQA_GROUP)], axis=0)
            acc = jnp.zeros((GQA_GROUP * tq, LANES), F32)
            m = jnp.full((GQA_GROUP * tq, 1), -jnp.inf, F32)
            for k_ref, v_ref, c0, kw in chunks:
                kc = k_ref[c0:c0 + kw, h * LANES:(h + 1) * LANES]
                vc = v_ref[c0:c0 + kw, h * LANES:(h + 1) * LANES]
                s = lax.dot_general(qs, kc, (((1,), (1,)), ((), ())), preferred_element_type=F32)
                if running_max:
                    m_new = jnp.maximum(m, jnp.max(s, axis=-1, keepdims=True))
                    acc = acc * jnp.exp2(m - m_new)
                    s = s - m_new
                    m = m_new
                acc = acc + jnp.dot(jnp.exp2(s).astype(BF16), vc, preferred_element_type=F32)
            o = acc[:, :HEAD_DIM] / acc[:, HEAD_DIM:HEAD_DIM + 1]
            for g in range(GQA_GROUP):
                c0 = (h * GQA_GROUP + g) * HEAD_DIM
                o_ref[:, c0:c0 + HEAD_DIM] = o[g * tq:(g + 1) * tq].astype(o_ref.dtype)

    @pl.when(fast_ref[0] > 0)
    def _():
        run(False)

    @pl.when(fast_ref[0] == 0)
    def _():
        run(True)


def _attn_call(fast, q, kv, tq):
    b, lq, qw = q.shape
    kv_specs = [pl.BlockSpec((None,) + a.shape[1:], lambda i, j, f: (i, 0, 0)) for a in kv]
    return pl.pallas_call(
        _attn_kernel,
        grid_spec=pltpu.PrefetchScalarGridSpec(
            num_scalar_prefetch=1,
            grid=(b, lq // tq),
            in_specs=[pl.BlockSpec((None, tq, qw), lambda i, j, f: (i, j, 0))] + kv_specs,
            out_specs=pl.BlockSpec((None, tq, D_ATTN), lambda i, j, f: (i, j, 0))),
        out_shape=jax.ShapeDtypeStruct((b, lq, D_ATTN), BF16),
        compiler_params=_cparams(),
        name="attention",
    )(fast, q, *kv)


def _filter_consts(seq_len):
    t = np.linspace(0.0, 1.0, seq_len, dtype=np.float32).astype(np.float64)[:, None]
    omega = 2.0 * math.pi * np.arange(seq_len, dtype=np.float64)[:, None] / seq_len
    bands = np.linspace(1e-4, HYENA_BANDS - 1, HYENA_BANDS, dtype=np.float32).astype(np.float64)[None, :]
    feats = np.concatenate([t, np.cos(omega * bands), -np.sin(omega * bands)], axis=-1)
    feats = np.pad(feats, ((0, 0), (0, LANES - HYENA_EMB)))
    max_decay = math.log(HYENA_DECAY_TARGET) / HYENA_FAST_DECAY_PCT
    min_decay = math.log(HYENA_DECAY_TARGET) / HYENA_SLOW_DECAY_PCT
    deltas = np.linspace(min_decay, max_decay, D_HYENA, dtype=np.float32).astype(np.float64)
    decay = np.exp(-t * np.abs(deltas)[None, :])
    rev = (seq_len - np.arange(seq_len)) % seq_len
    f32 = lambda a: jnp.asarray(a.astype(np.float32))
    return f32(feats), f32(feats[rev]), f32(decay), f32(decay[rev])


def _filter_kernel(feat_ref, featr_ref, w1_ref, b1_ref, fr_ref, w2_ref, b2_ref, w3_ref, dec_ref, decr_ref, o_ref,
                   hid_ref):
    seq_len = feat_ref.shape[0]
    dot = functools.partial(jnp.dot, preferred_element_type=F32, precision=HIGHEST)

    @pl.when(pl.program_id(0) == 0)
    def _():
        fr = fr_ref[...]
        for i, feat in enumerate((feat_ref, featr_ref)):
            h = jnp.sin(fr * (dot(feat[...], w1_ref[...]) + b1_ref[...]))
            hid_ref[i] = jnp.sin(fr * (dot(h, w2_ref[...]) + b2_ref[...]))

    fwd = dot(hid_ref[0], w3_ref[:, :D_HYENA]) * dec_ref[...]
    bwd = dot(hid_ref[1], w3_ref[:, D_HYENA:]) * decr_ref[...]
    tot = jnp.sum(jnp.abs(fwd), axis=0, keepdims=True) + jnp.sum(jnp.abs(bwd), axis=0, keepdims=True)
    inv = 1.0 / tot
    o_ref[:seq_len, :] = fwd * inv
    row = lax.broadcasted_iota(jnp.int32, bwd.shape, 0)
    o_ref[seq_len:, :] = jnp.where(row == 0, 0.0, bwd * inv)


def _filter_call(seq_len, w1, b1, freq, w2, b2, w3):
    feats, feats_rev, decay, decay_rev = _filter_consts(seq_len)
    hid = w2.shape[0]
    w1p = jnp.pad(w1, ((0, LANES - HYENA_EMB), (0, 0)))
    full = lambda shape: pl.BlockSpec(shape, lambda o: (0,) * len(shape))
    return pl.pallas_call(
        _filter_kernel,
        grid=(HYENA_ORDER,),
        in_specs=[full((seq_len, LANES)), full((seq_len, LANES)), full((LANES, hid)), full((1, hid)),
                  full((1, hid)), full((hid, hid)), full((1, hid)),
                  pl.BlockSpec((hid, 2 * D_HYENA), lambda o: (0, o)),
                  full((seq_len, D_HYENA)), full((seq_len, D_HYENA))],
        out_specs=pl.BlockSpec((2 * seq_len, D_HYENA), lambda o: (0, o)),
        out_shape=jax.ShapeDtypeStruct((2 * seq_len, HYENA_ORDER * D_HYENA), F32),
        scratch_shapes=[pltpu.VMEM((2, seq_len, hid), F32)],
        compiler_params=_cparams(),
        name="hyena_filter",
    )(feats, feats_rev, w1p, b1.reshape(1, hid), freq.reshape(1, hid), w2, b2.reshape(1, hid), w3,
      decay, decay_rev)


@functools.lru_cache(maxsize=None)
def _fft_consts(seq_len):
    n = 2 * seq_len
    n2 = FFT_N2
    n1 = n // n2
    h1 = n1 // 2
    k1 = np.arange(n1)
    w1 = np.exp(-2j * np.pi * np.outer(k1, np.arange(n1)) / n1)
    f1 = np.zeros((n1, 2, 2 * h1))
    f1[:, 0, :h1], f1[:, 0, h1:] = w1[:, :h1].real, -w1[:, :h1].imag
    f1[:, 1, :h1], f1[:, 1, h1:] = w1[:, :h1].imag, w1[:, :h1].real
    f1 = f1.reshape(2 * n1, 2 * h1)
    f1r = np.stack([w1.real, w1.imag], axis=1).reshape(2 * n1, n1)
    a2 = np.arange(n2)
    tw = np.exp(-2j * np.pi * np.outer(k1, a2) / n)
    w2 = np.exp(-2j * np.pi * np.outer(a2, a2) / n2)
    m = w2[None, :, :] * tw[:, None, :]
    ff = np.concatenate([np.concatenate([m.real, -m.imag], axis=2),
                         np.concatenate([m.imag, m.real], axis=2)], axis=1)
    g = np.conj(w2.T)[None, :, :] * np.conj(tw)[:, :, None]
    gi = np.concatenate([np.concatenate([g.real, -g.imag], axis=2),
                         np.concatenate([g.imag, g.real], axis=2)], axis=1)
    v = np.conj(w1[:, :h1]).T
    f1i = np.zeros((2, h1, n1, 2))
    f1i[0, :, :, 0], f1i[0, :, :, 1] = v.real, -v.imag
    f1i[1, :, :, 0], f1i[1, :, :, 1] = v.imag, v.real
    f1i = f1i.reshape(2 * h1, 2 * n1)
    il = np.arange(2 * n2).reshape(2, n2).T.reshape(-1)
    ff_il = ff[:, :, il]
    gi_il = gi[:, il, :]
    f32 = lambda a: np.asarray(a, np.float32)
    return dict(n=n, n1=n1, h1=h1, f1=f32(f1), f1r=f32(f1r), ff=f32(ff), ff_il=f32(ff_il), gi_il=f32(gi_il),
                f1i=f32(f1i))


def _fftr_kernel(x_ref, f_ref, o_ref):
    n1 = f_ref.shape[1]

    def body(n2, carry):
        x = x_ref[pl.ds(n2, n1, stride=FFT_N2), :]
        o_ref[pl.ds(n2, 2 * n1, stride=FFT_N2), :] = jnp.dot(f_ref[...], x, preferred_element_type=F32,
                                                            precision=HIGHEST)
        return carry

    lax.fori_loop(0, FFT_N2, body, 0, unroll=FFT_UNROLL)


def _fftr_call(buf, f1r):
    n, cols = buf.shape
    rows = f1r.shape[0] * FFT_N2
    return pl.pallas_call(
        _fftr_kernel,
        grid=(cols // LANES,),
        in_specs=[pl.BlockSpec((n, LANES), lambda j: (0, j)),
                  pl.BlockSpec(f1r.shape, lambda j: (0, 0))],
        out_specs=pl.BlockSpec((None, rows, LANES), lambda j: (j, 0, 0)),
        out_shape=jax.ShapeDtypeStruct((cols // LANES, rows, LANES), F32),
        compiler_params=_cparams(),
        name="fft_filter_outer_dft",
    )(buf, f1r)


def _spec_kernel(a_ref, ff_ref, o_ref, *, scale):
    for kk in range(a_ref.shape[1]):
        for cb in range(a_ref.shape[0]):
            o_ref[kk, :, cb * LANES:(cb + 1) * LANES] = jnp.dot(
                ff_ref[kk], a_ref[cb, kk], preferred_element_type=F32, precision=HIGHEST) * scale


def _spec_call(a4, ff, scale, kt):
    nblk, n1, r, _ = a4.shape
    return pl.pallas_call(
        functools.partial(_spec_kernel, scale=scale),
        grid=(n1 // kt,),
        in_specs=[pl.BlockSpec((nblk, kt, r, LANES), lambda i: (0, i, 0, 0)),
                  pl.BlockSpec((kt, r, r), lambda i: (i, 0, 0))],
        out_specs=pl.BlockSpec((kt, r, nblk * LANES), lambda i: (i, 0, 0)),
        out_shape=jax.ShapeDtypeStruct((n1, r, nblk * LANES), F32),
        compiler_params=_cparams(),
        name="fft_filter_spectrum",
    )(a4, ff)


def _short_conv(z, w, b):
    n = z.shape[0]
    row = lax.broadcasted_iota(jnp.int32, z.shape, 0)
    prev = jnp.where(row == 0, 0.0, pltpu.roll(z, 1, axis=0))
    nxt = jnp.where(row == n - 1, 0.0, pltpu.roll(z, n - 1, axis=0))
    return prev * w[0:1, :] + z * w[1:2, :] + nxt * w[2:3, :] + b


def _ffta_kernel(*refs, conv):
    if conv:
        z_ref, w_ref, cb_ref, f_ref, o_ref, x_ref = refs
        for m in range(2):
            x_ref[m] = _short_conv(z_ref[m], w_ref[...], cb_ref[...])
    else:
        z_ref, f_ref, o_ref = refs
        x_ref = z_ref
    n1 = f_ref.shape[0] // 2
    h1 = f_ref.shape[1] // 2

    def body(n2, carry):
        rows = pl.ds(n2, h1, stride=FFT_N2)
        x = jnp.concatenate([x_ref[0, rows, :], x_ref[1, rows, :]], axis=0).astype(BF16)
        a = jnp.dot(f_ref[...], x, preferred_element_type=F32).astype(BF16)
        o_ref[pl.ds(n2, n1, stride=FFT_N2), :] = pltpu.bitcast(a, jnp.uint32)
        return carry

    lax.fori_loop(0, FFT_N2, body, 0, unroll=FFT_UNROLL)


def _ffta_call(z4, col0, f1_bf, conv_w=None, conv_b=None):
    groups, _, l, _ = z4.shape
    halves = D_HYENA // LANES
    n1 = f1_bf.shape[0] // 2
    conv = conv_w is not None
    in_specs = [pl.BlockSpec((None, 2, l, LANES), lambda i, j: (i, 0, 0, col0 + j))]
    args = [z4]
    scratch = []
    if conv:
        in_specs += [pl.BlockSpec((3, LANES), lambda i, j: (0, col0 + j)),
                     pl.BlockSpec((1, LANES), lambda i, j: (0, col0 + j))]
        args += [conv_w, conv_b]
        scratch = [pltpu.VMEM((2, l, LANES), F32)]
    in_specs.append(pl.BlockSpec(f1_bf.shape, lambda i, j: (0, 0)))
    args.append(f1_bf)
    return pl.pallas_call(
        functools.partial(_ffta_kernel, conv=conv),
        grid=(groups, halves),
        in_specs=in_specs,
        out_specs=pl.BlockSpec((None, None, n1 * FFT_N2, LANES), lambda i, j: (i, j, 0, 0)),
        out_shape=jax.ShapeDtypeStruct((groups, halves, n1 * FFT_N2, LANES), jnp.uint32),
        scratch_shapes=scratch,
        compiler_params=_cparams(),
        name="fft_outer_dft",
    )(*args)


def _fftb_kernel(a_ref, ff_ref, gi_ref, h_ref, o_ref):
    groups, halves, kt = a_ref.shape[:3]
    half = FFT_N2

    def body(kk, carry):
        ff = ff_ref[kk]
        gi = gi_ref[kk]
        hr = h_ref[kk, :half, :]
        hi = h_ref[kk, half:, :]
        for g in range(groups):
            a = jnp.concatenate([pltpu.bitcast(a_ref[g, hh, kk], BF16) for hh in range(halves)], axis=1)
            xh = jnp.dot(ff, a, preferred_element_type=F32)
            xr, xi = xh[:half], xh[half:]
            y = jnp.concatenate([xr * hr - xi * hi, xr * hi + xi * hr], axis=0).astype(BF16)
            b = jnp.dot(gi, y, preferred_element_type=F32).astype(BF16)
            for hh in range(halves):
                o_ref[g, hh, kk] = pltpu.bitcast(b[:, hh * LANES:(hh + 1) * LANES], jnp.uint32)
        return carry

    lax.fori_loop(0, kt, body, 0)


def _fftb_call(a5, ff_bf, gi_bf, spec, order, kt):
    g, halves, n1, n2, _ = a5.shape
    r = 2 * n2
    return pl.pallas_call(
        _fftb_kernel,
        grid=(n1 // kt,),
        in_specs=[pl.BlockSpec((g, halves, kt, n2, LANES), lambda i: (0, 0, i, 0, 0)),
                  pl.BlockSpec((kt, r, r), lambda i: (i, 0, 0)),
                  pl.BlockSpec((kt, r, r), lambda i: (i, 0, 0)),
                  pl.BlockSpec((kt, r, halves * LANES), lambda i: (i, 0, order))],
        out_specs=pl.BlockSpec((g, halves, kt, n2, LANES), lambda i: (0, 0, i, 0, 0)),
        out_shape=jax.ShapeDtypeStruct(a5.shape, jnp.uint32),
        compiler_params=_cparams(),
        name="fft_inner_conv",
    )(a5, ff_bf, gi_bf, spec)


def _fftc_kernel(*refs, z_conv):
    if z_conv:
        b_ref, f_ref, z_ref, zw_ref, zb_ref, g_ref, gw_ref, gb_ref, d_ref, o_ref = refs
    else:
        b_ref, f_ref, z_ref, g_ref, gw_ref, gb_ref, d_ref, o_ref = refs
    n1 = f_ref.shape[1] // 2
    h1 = f_ref.shape[0] // 2
    d = d_ref[...]

    def body(n2, carry):
        b = pltpu.bitcast(b_ref[pl.ds(n2, n1, stride=FFT_N2), :], BF16)
        y = jnp.dot(f_ref[...], b, preferred_element_type=F32)
        rows = pl.ds(n2, h1, stride=FFT_N2)
        for m in range(2):
            o_ref[m, rows, :] = y[m * h1:(m + 1) * h1]
        return carry

    lax.fori_loop(0, FFT_N2, body, 0, unroll=FFT_UNROLL)
    for m in range(2):
        z = _short_conv(z_ref[m], zw_ref[...], zb_ref[...]) if z_conv else z_ref[m]
        gate = _short_conv(g_ref[m], gw_ref[...], gb_ref[...])
        o_ref[m] = gate * (o_ref[m] + d * z)


def _fftc_call(b4, f1i_bf, z4, z_col0, z_conv, hp4, g_col0, conv_w, conv_b, d2, order):
    groups, halves, rows, _ = b4.shape
    l = z4.shape[2]
    pair = lambda col0: pl.BlockSpec((None, 2, l, LANES), lambda i, j: (i, 0, 0, col0 + j))
    taps = lambda col0: [pl.BlockSpec((3, LANES), lambda i, j: (0, col0 + j)),
                         pl.BlockSpec((1, LANES), lambda i, j: (0, col0 + j))]
    in_specs = [pl.BlockSpec((None, None, rows, LANES), lambda i, j: (i, j, 0, 0)),
                pl.BlockSpec(f1i_bf.shape, lambda i, j: (0, 0)),
                pair(z_col0)]
    args = [b4, f1i_bf, z4]
    if z_conv:
        in_specs += taps(z_col0)
        args += [conv_w, conv_b]
    in_specs += [pair(g_col0)] + taps(g_col0) + [pl.BlockSpec((None, 1, LANES), lambda i, j: (order, 0, j))]
    args += [hp4, conv_w, conv_b, d2.reshape(d2.shape[0], 1, d2.shape[1])]
    return pl.pallas_call(
        functools.partial(_fftc_kernel, z_conv=z_conv),
        grid=(groups, halves),
        in_specs=in_specs,
        out_specs=pair(0),
        out_shape=jax.ShapeDtypeStruct((groups, 2, l, D_HYENA), F32),
        compiler_params=_cparams(),
        name="fft_outer_inverse_gate",
    )(*args)


def _hyena_spectrum(buf, seq_len):
    cst = _fft_consts(seq_len)
    n, n1 = cst["n"], cst["n1"]
    a = _fftr_call(buf, jnp.asarray(cst["f1r"]))
    a4 = a.reshape(a.shape[0], n1, 2 * FFT_N2, LANES)
    return _spec_call(a4, jnp.asarray(cst["ff"]), 1.0 / n, min(n1, 8))


def _hyena_latent(hp3, conv_w, conv_b, spec, d):
    b, l, _ = hp3.shape
    cst = _fft_consts(l)
    n1, h1 = cst["n1"], cst["h1"]
    groups = b // 2
    halves = D_HYENA // LANES
    kt = min(n1, 8)
    f1 = jnp.asarray(cst["f1"], BF16)
    ff = jnp.asarray(cst["ff_il"], BF16)
    gi = jnp.asarray(cst["gi_il"], BF16)
    f1i = jnp.asarray(cst["f1i"], BF16)
    hp4 = hp3.reshape(groups, 2, l, hp3.shape[2])
    cb = conv_b.reshape(1, -1)
    z = hp4
    for o in range(HYENA_ORDER):
        first = o == 0
        a = _ffta_call(z, 0, f1, conv_w, cb) if first else _ffta_call(z, 0, f1)
        bb = _fftb_call(a.reshape(groups, halves, n1, FFT_N2, LANES), ff, gi, spec, o, kt)
        z = _fftc_call(bb.reshape(groups, halves, n1 * FFT_N2, LANES), f1i, z, 0, first, hp4, (1 + o) * halves,
                       conv_w, cb, d, o)
    return z.reshape(b, l, D_HYENA)


@functools.lru_cache(maxsize=None)
def _dft_consts(seq_len):
    n = 2 * seq_len
    w = np.exp(-2j * np.pi * np.outer(np.arange(n), np.arange(n)) / n)
    fwd_full = np.concatenate([w.real, w.imag], axis=0)
    fwd = fwd_full[:, :seq_len]
    wi = np.conj(w[:, :seq_len]).T
    inv = np.concatenate([wi.real, -wi.imag], axis=1)
    f32 = lambda a: np.asarray(a, np.float32)
    return dict(n=n, fwd_full=f32(fwd_full), fwd=f32(fwd), inv=f32(inv))


def _cspec_kernel(buf_ref, f_ref, o_ref, *, scale):
    o_ref[...] = jnp.dot(f_ref[...], buf_ref[...], preferred_element_type=F32, precision=HIGHEST) * scale


def _ctx_spectrum(buf, seq_len):
    cst = _dft_consts(seq_len)
    n = cst["n"]
    return pl.pallas_call(
        functools.partial(_cspec_kernel, scale=1.0 / n),
        out_shape=jax.ShapeDtypeStruct((2 * n, buf.shape[1]), F32),
        compiler_params=_cparams(),
        name="ctx_filter_spectrum",
    )(buf, jnp.asarray(cst["fwd_full"]))


def _ctx_hyena_kernel(z_ref, w_ref, b_ref, f_ref, fi_ref, h_ref, d_ref, o_ref):
    n = 2 * z_ref.shape[0]
    zc = _short_conv(z_ref[...], w_ref[...], b_ref[...])
    cur = zc[:, :D_HYENA]
    for o in range(HYENA_ORDER):
        gate = zc[:, (1 + o) * D_HYENA:(2 + o) * D_HYENA]
        xh = jnp.dot(f_ref[...], cur.astype(BF16), preferred_element_type=F32)
        xr, xi = xh[:n], xh[n:]
        hr = h_ref[:n, o * D_HYENA:(o + 1) * D_HYENA]
        hi = h_ref[n:, o * D_HYENA:(o + 1) * D_HYENA]
        y = jnp.concatenate([xr * hr - xi * hi, xr * hi + xi * hr], axis=0).astype(BF16)
        conv = jnp.dot(fi_ref[...], y, preferred_element_type=F32)
        cur = gate * (conv + d_ref[o:o + 1, :] * cur)
    o_ref[...] = cur


def _ctx_hyena_call(hp3, conv_w, conv_b, spec, d):
    b, l, _ = hp3.shape
    cst = _dft_consts(l)
    n = cst["n"]
    w3 = 3 * D_HYENA
    full = lambda shape: pl.BlockSpec(shape, lambda i: (0,) * len(shape))
    return pl.pallas_call(
        _ctx_hyena_kernel,
        grid=(b,),
        in_specs=[pl.BlockSpec((None, l, w3), lambda i: (i, 0, 0)),
                  full((3, w3)), full((1, w3)), full((2 * n, l)), full((l, 2 * n)),
                  full((2 * n, HYENA_ORDER * D_HYENA)), full((HYENA_ORDER, D_HYENA))],
        out_specs=pl.BlockSpec((None, l, D_HYENA), lambda i: (i, 0, 0)),
        out_shape=jax.ShapeDtypeStruct((b, l, D_HYENA), F32),
        compiler_params=_cparams(),
        name="ctx_hyena",
    )(hp3, conv_w, conv_b.reshape(1, -1), jnp.asarray(cst["fwd"], BF16), jnp.asarray(cst["inv"], BF16),
      spec, d)


def _pool_kernel(p_ref, w_ref, sc_ref, o_ref):
    x = p_ref[...]
    l, c = x.shape
    half = pl.program_id(1)
    zpad = jnp.zeros((POOL_PAD, c), F32)
    xp = jnp.concatenate([zpad, x, zpad], axis=0)
    tot = l + 2 * POOL_PAD
    sums = []
    f = xp
    for win in POOL_WINDOWS:
        f = f + pltpu.roll(f, tot - win // 2, axis=0)
        sums.append(pltpu.roll(f, win // 2, axis=0)[POOL_PAD:POOL_PAD + l])
    t = lax.broadcasted_iota(jnp.int32, (l, c), 0)
    grp = lax.broadcasted_iota(jnp.int32, (l, c), 1) // POOL_GROUP + half * (LANES // POOL_GROUP)
    total = sums[-1]
    cnt = None
    for gi in range(len(POOL_WINDOWS) - 1, -1, -1):
        win = POOL_WINDOWS[gi]
        cw = (jnp.minimum(t + (win - win // 2), l) - jnp.maximum(t - win // 2, 0)).astype(F32)
        if cnt is None:
            cnt = cw
        else:
            sel = grp == gi
            total = jnp.where(sel, sums[gi], total)
            cnt = jnp.where(sel, cw, cnt)
    dlt = total / cnt - x
    y = jnp.dot(dlt.astype(BF16), w_ref[0], preferred_element_type=F32)
    o_ref[...] = y * sc_ref[...]


def _pool_call(hp3, pool_w, pool_scale):
    b, l, w = hp3.shape
    nh = D_POOL // LANES
    first = (w - D_POOL) // LANES
    per = LANES // POOL_GROUP
    wbd = jnp.zeros((nh, LANES, LANES), F32)
    for g in range(len(POOL_WINDOWS)):
        r0 = (g % per) * POOL_GROUP
        wbd = wbd.at[g // per, r0:r0 + POOL_GROUP, r0:r0 + POOL_GROUP].set(pool_w[g])
    return pl.pallas_call(
        _pool_kernel,
        grid=(b, nh),
        in_specs=[pl.BlockSpec((None, l, LANES), lambda i, j: (i, 0, first + j)),
                  pl.BlockSpec((1, LANES, LANES), lambda i, j: (j, 0, 0)),
                  pl.BlockSpec((1, LANES), lambda i, j: (0, j))],
        out_specs=pl.BlockSpec((None, l, LANES), lambda i, j: (i, 0, j)),
        out_shape=jax.ShapeDtypeStruct((b, l, D_POOL), F32),
        compiler_params=_cparams(),
        name="pool_mixer",
    )(hp3, wbd.astype(BF16), pool_scale.reshape(1, D_POOL))


def _outproj_kernel(x_ref, att_ref, hy_ref, pl_ref, gate_ref, w_ref, g_ref, b_ref, o_ref, *, alpha):
    mix = jnp.dot(att_ref[...], w_ref[0:D_ATTN, :], preferred_element_type=F32)
    mix += jnp.dot(hy_ref[...].astype(BF16), w_ref[D_ATTN:D_ATTN + D_HYENA, :], preferred_element_type=F32)
    mix += jnp.dot(pl_ref[...].astype(BF16), w_ref[D_ATTN + D_HYENA:, :], preferred_element_type=F32)
    y = alpha * x_ref[...] + gate_ref[0] * mix
    o_ref[...] = _layernorm(y, LN_EPS) * g_ref[...] + b_ref[...]


def _outproj_call(x2, att2, hy2, pl2, mod3, mod_row, w_out_bf, g, b, alpha, tm):
    r, d = x2.shape
    row = lambda w: pl.BlockSpec((tm, w), lambda i: (i, 0))
    vec = pl.BlockSpec((1, d), lambda i: (0, 0))
    return pl.pallas_call(
        functools.partial(_outproj_kernel, alpha=alpha),
        grid=(r // tm,),
        in_specs=[row(d), row(D_ATTN), row(D_HYENA), row(D_POOL),
                  pl.BlockSpec((1, 1, d), lambda i: (mod_row(i) * 6 + 2, 0, 0)),
                  pl.BlockSpec(w_out_bf.shape, lambda i: (0, 0)),
                  vec, vec],
        out_specs=row(d),
        out_shape=jax.ShapeDtypeStruct((r, d), F32),
        compiler_params=_cparams(),
        name="out_proj_deepnorm",
    )(x2, att2, hy2, pl2, mod3, w_out_bf, g.reshape(1, d), b.reshape(1, d))


def _ffn_kernel(x_ref, sh_ref, sc_ref, gate_ref, wg_ref, wu_ref, wd_ref, g_ref, b_ref, o_ref, u_ref, acc_ref, *,
                alpha, n_chunk):
    j = pl.program_id(1)

    @pl.when(j == 0)
    def _():
        u_ref[...] = (_layernorm(x_ref[...], ADALN_EPS) * (1.0 + sc_ref[0]) + sh_ref[0]).astype(BF16)
        acc_ref[...] = jnp.zeros_like(acc_ref)

    u = u_ref[...]
    a = jnp.dot(u, wg_ref[0].astype(BF16), preferred_element_type=F32)
    bb = jnp.dot(u, wu_ref[0].astype(BF16), preferred_element_type=F32)
    h = a * jax.nn.sigmoid(a) * bb
    acc_ref[...] += jnp.dot(h.astype(BF16), wd_ref[0].astype(BF16), preferred_element_type=F32)

    @pl.when(j == n_chunk - 1)
    def _():
        y = alpha * x_ref[...] + gate_ref[0] * acc_ref[...]
        o_ref[...] = _layernorm(y, LN_EPS) * g_ref[...] + b_ref[...]


def _ffn_call(x2, mod3, mod_row, w_gate, w_up, w_down, w_idx, g, b, alpha, tm, fc):
    r, d = x2.shape
    dff = w_gate.shape[2]
    n_chunk = dff // fc
    vec = pl.BlockSpec((1, d), lambda i, j: (0, 0))
    mod = lambda which: pl.BlockSpec((1, 1, d), lambda i, j: (mod_row(i) * 6 + which, 0, 0))
    return pl.pallas_call(
        functools.partial(_ffn_kernel, alpha=alpha, n_chunk=n_chunk),
        grid=(r // tm, n_chunk),
        in_specs=[pl.BlockSpec((tm, d), lambda i, j: (i, 0)), mod(3), mod(4), mod(5),
                  pl.BlockSpec((1, d, fc), lambda i, j: (w_idx, 0, j)),
                  pl.BlockSpec((1, d, fc), lambda i, j: (w_idx, 0, j)),
                  pl.BlockSpec((1, fc, d), lambda i, j: (w_idx, j, 0)),
                  vec, vec],
        out_specs=pl.BlockSpec((tm, d), lambda i, j: (i, 0)),
        out_shape=jax.ShapeDtypeStruct((r, d), F32),
        scratch_shapes=[pltpu.VMEM((tm, d), BF16), pltpu.VMEM((tm, d), F32)],
        compiler_params=_cparams(),
        name="channel_mixer",
    )(x2, mod3, mod3, mod3, w_gate, w_up, w_down, g.reshape(1, d), b.reshape(1, d))


MOE_WINDOW = 1024
MOE_TILE = 64
MOE_SUPER = 16
MOE_FC = 512
MOE_GCHUNK = 512
MOE_SUB = 256


def _router_kernel(x_ref, sh_ref, sc_ref, rw_ref, tri_ref, u_ref, comb_ref, pos_ref):
    u = (_layernorm(x_ref[...], ADALN_EPS) * (1.0 + sc_ref[0]) + sh_ref[0]).astype(BF16)
    u_ref[...] = u
    logits = jnp.dot(u, rw_ref[...].astype(BF16), preferred_element_type=F32)
    n_exp = logits.shape[1]
    lane = lax.broadcasted_iota(jnp.int32, logits.shape, 1)
    m1 = jnp.max(logits, axis=-1, keepdims=True)
    i1 = jnp.min(jnp.where(logits == m1, lane, n_exp), axis=-1, keepdims=True)
    rest = jnp.where(lane == i1, -jnp.inf, logits)
    m2 = jnp.max(rest, axis=-1, keepdims=True)
    i2 = jnp.min(jnp.where(rest == m2, lane, n_exp), axis=-1, keepdims=True)
    e2 = jnp.exp(m2 - m1)
    den = 1.0 + e2
    comb = jnp.where(lane == i1, 1.0 / den, 0.0) + jnp.where(lane == i2, e2 / den, 0.0)
    comb_ref[...] = comb
    routed = jnp.where(comb > 0.0, 1.0, 0.0).astype(BF16)
    pos_ref[...] = jnp.dot(tri_ref[...], routed, preferred_element_type=F32) - 1.0


def _router_call(x2, mod3, mod_row, router_w):
    r, d = x2.shape
    n_exp = router_w.shape[1]
    tm = MOE_WINDOW
    tri = jnp.asarray(np.tril(np.ones((tm, tm), np.float32)), BF16)
    mod = lambda which: pl.BlockSpec((1, 1, d), lambda i: (mod_row(i) * 6 + which, 0, 0))
    small = pl.BlockSpec((tm, n_exp), lambda i: (i, 0))
    return pl.pallas_call(
        _router_kernel,
        grid=(r // tm,),
        in_specs=[pl.BlockSpec((tm, d), lambda i: (i, 0)), mod(3), mod(4),
                  pl.BlockSpec(router_w.shape, lambda i: (0, 0)),
                  pl.BlockSpec((tm, tm), lambda i: (0, 0))],
        out_specs=[pl.BlockSpec((tm, d), lambda i: (i, 0)), small, small],
        out_shape=[jax.ShapeDtypeStruct((r, d), BF16), jax.ShapeDtypeStruct((r, n_exp), F32),
                   jax.ShapeDtypeStruct((r, n_exp), F32)],
        compiler_params=_cparams(),
        name="moe_router",
    )(x2, mod3, mod3, router_w, tri)


def _moe_plan(comb, pos_f, n_tok):
    n_exp = comb.shape[1]
    win, tile = MOE_WINDOW, MOE_TILE
    n_win = n_tok // win
    cap_tiles = 2 * win // tile + n_exp
    i32 = jnp.int32
    mask = (comb > 0.0).reshape(n_win, win, n_exp)
    pos = pos_f.astype(i32).reshape(n_win, win, n_exp)
    cnt = pos[:, -1, :] + 1
    ntile = (cnt + tile - 1) // tile
    woff = jnp.cumsum(ntile, axis=1) - ntile
    rows = jnp.where(mask, tile * woff[:, None, :] + pos, -1)
    row_a = jnp.max(rows, axis=-1)
    row_b = jnp.max(jnp.where(rows == row_a[..., None], -1, rows), axis=-1)
    rowsel = jnp.stack([row_a, row_b], axis=1)
    win_rows = tile * jnp.sum(ntile, axis=1)

    etiles = jnp.sum(ntile, axis=0)
    nsup = (etiles + MOE_SUPER - 1) // MOE_SUPER
    ebase = MOE_SUPER * (jnp.cumsum(nsup) - nsup)
    erank = ebase[None, :] + jnp.cumsum(ntile, axis=0) - ntile
    n_sup = -(-(n_win * cap_tiles) // MOE_SUPER) + n_exp
    n_rank = n_sup * MOE_SUPER
    slots = win // tile
    s = jnp.arange(slots, dtype=i32)[None, None, :]
    ok = s < ntile[:, :, None]
    rank = jnp.where(ok, erank[:, :, None] + s, n_rank)
    wm_tile = (jnp.arange(n_win, dtype=i32) * cap_tiles)[:, None, None] + woff[:, :, None] + s
    src = jnp.zeros((n_rank + 1,), i32).at[rank.reshape(-1)].set(wm_tile.reshape(-1))[:n_rank]
    sup_total = jnp.sum(nsup)
    g = jnp.arange(n_sup, dtype=i32)
    sup_valid = (g < sup_total).astype(i32)
    sup_exp = jnp.clip(jnp.sum((g[:, None] >= jnp.cumsum(nsup)[None, :]).astype(i32), axis=1), 0, n_exp - 1)
    sup_exp = jnp.where(sup_valid > 0, sup_exp, sup_exp[jnp.maximum(sup_total - 1, 0)])

    tile_rank = jnp.zeros((n_win * cap_tiles + 1,), i32).at[
        jnp.where(ok, wm_tile, n_win * cap_tiles).reshape(-1)].set(
        jnp.minimum(rank, n_rank - 1).reshape(-1))[:n_win * cap_tiles]
    comb3 = comb.reshape(n_win, win, n_exp)
    gate_a = jnp.sum(jnp.where(rows == row_a[..., None], comb3, 0.0), axis=-1)
    gate_b = jnp.sum(jnp.where((rows == row_b[..., None]) & mask, comb3, 0.0), axis=-1)
    tok_sel = jnp.stack([row_a.astype(F32), row_b.astype(F32), gate_a, gate_b], axis=-1).reshape(n_tok, 4)
    return dict(n_win=n_win, cap_tiles=cap_tiles, n_sup=n_sup, n_rank=n_rank,
                rowsel=rowsel, win_rows=win_rows.astype(i32),
                src=src, sup_exp=sup_exp.astype(i32), sup_valid=sup_valid,
                tile_rank=tile_rank, tok_sel=tok_sel)


def _dispatch_kernel(nrows_ref, *refs, offsets):
    u_refs = refs[:len(offsets)]
    sel_ref, o_ref = refs[len(offsets):]
    w = pl.program_id(0)
    u_ref = u_refs[0]
    if len(offsets) > 1:
        tokens = u_refs[0][...]
        for s in range(1, len(offsets)):
            tokens = jnp.where(w >= offsets[s], u_refs[s][...], tokens)
    r1 = sel_ref[0:1, :]
    r2 = sel_ref[1:2, :]
    for cidx in range(o_ref.shape[0] // MOE_GCHUNK):
        base = cidx * MOE_GCHUNK
        rows = pl.ds(base, MOE_GCHUNK)

        @pl.when(base < nrows_ref[w])
        def _():
            rid = lax.broadcasted_iota(jnp.int32, (MOE_GCHUNK, r1.shape[1]), 0) + base
            onehot = jnp.where(rid == r1, 1.0, jnp.where(rid == r2, 1.0, 0.0)).astype(BF16)
            src = tokens if len(offsets) > 1 else u_ref[...]
            o_ref[rows, :] = jnp.dot(onehot, src, preferred_element_type=F32).astype(o_ref.dtype)

        @pl.when(base >= nrows_ref[w])
        def _():
            o_ref[rows, :] = jnp.zeros((MOE_GCHUNK, o_ref.shape[1]), o_ref.dtype)


def _dispatch_call(us, plan):
    d = us[0].shape[1]
    cap_rows = plan["cap_tiles"] * MOE_TILE
    n_win = plan["n_win"]
    counts = [u.shape[0] // MOE_WINDOW for u in us]
    offsets = tuple(sum(counts[:s]) for s in range(len(us)))

    def u_spec(s):
        return pl.BlockSpec((MOE_WINDOW, d), lambda w, n: (jnp.clip(w - offsets[s], 0, counts[s] - 1), 0))

    return pl.pallas_call(
        functools.partial(_dispatch_kernel, offsets=offsets),
        grid_spec=pltpu.PrefetchScalarGridSpec(
            num_scalar_prefetch=1,
            grid=(n_win,),
            in_specs=[u_spec(s) for s in range(len(us))] + [
                pl.BlockSpec((None, TOP_K, MOE_WINDOW), lambda w, n: (w, 0, 0))],
            out_specs=pl.BlockSpec((cap_rows, d), lambda w, n: (w, 0))),
        out_shape=jax.ShapeDtypeStruct((n_win * cap_rows, d), BF16),
        compiler_params=_cparams(),
        name="moe_dispatch",
    )(plan["win_rows"], *us, plan["rowsel"])


def _expert_kernel(src_ref, exp_ref, valid_ref, *refs, n_chunk):
    tiles = refs[:MOE_SUPER]
    wg_ref, wu_ref, wd_ref, o_ref, u_ref, acc_ref = refs[MOE_SUPER:]
    g = pl.program_id(0)
    j = pl.program_id(1)
    ok = valid_ref[g] > 0

    @pl.when(j == 0)
    def _():
        for k in range(MOE_SUPER):
            u_ref[k * MOE_TILE:(k + 1) * MOE_TILE, :] = tiles[k][...]
        acc_ref[...] = jnp.zeros_like(acc_ref)

    @pl.when(ok)
    def _():
        u = u_ref[...]
        part = None
        for c0 in range(0, wg_ref.shape[2], MOE_SUB):
            a = jnp.dot(u, wg_ref[0, :, c0:c0 + MOE_SUB].astype(BF16), preferred_element_type=F32)
            b = jnp.dot(u, wu_ref[0, :, c0:c0 + MOE_SUB].astype(BF16), preferred_element_type=F32)
            h = (a * jax.nn.sigmoid(a) * b).astype(BF16)
            y = jnp.dot(h, wd_ref[0, c0:c0 + MOE_SUB, :].astype(BF16), preferred_element_type=F32)
            part = y if part is None else part + y
        acc_ref[...] += part

    @pl.when(j == n_chunk - 1)
    def _():
        o_ref[...] = acc_ref[...].astype(o_ref.dtype)


def _expert_call(xw, plan, w_gate, w_up, w_down, layer_base, fc):
    d = xw.shape[1]
    dff = w_gate.shape[2]
    n_chunk = dff // fc
    rows = MOE_SUPER * MOE_TILE

    def tile_spec(k):
        return pl.BlockSpec((MOE_TILE, d), lambda g, j, src, ex, va: (src[g * MOE_SUPER + k], 0))

    def chunk(j, va, g):
        return jnp.where(va[g] > 0, j, n_chunk - 1)

    in_specs = [tile_spec(k) for k in range(MOE_SUPER)] + [
        pl.BlockSpec((1, d, fc), lambda g, j, src, ex, va: (layer_base + ex[g], 0, chunk(j, va, g))),
        pl.BlockSpec((1, d, fc), lambda g, j, src, ex, va: (layer_base + ex[g], 0, chunk(j, va, g))),
        pl.BlockSpec((1, fc, d), lambda g, j, src, ex, va: (layer_base + ex[g], chunk(j, va, g), 0))]
    return pl.pallas_call(
        functools.partial(_expert_kernel, n_chunk=n_chunk),
        grid_spec=pltpu.PrefetchScalarGridSpec(
            num_scalar_prefetch=3,
            grid=(plan["n_sup"], n_chunk),
            in_specs=in_specs,
            out_specs=pl.BlockSpec((rows, d), lambda g, j, src, ex, va: (g, 0)),
            scratch_shapes=[pltpu.VMEM((rows, d), BF16), pltpu.VMEM((rows, d), F32)]),
        out_shape=jax.ShapeDtypeStruct((plan["n_sup"] * rows, d), BF16),
        compiler_params=_cparams(),
        name="moe_experts",
    )(plan["src"], plan["sup_exp"], plan["sup_valid"], *([xw] * MOE_SUPER), w_gate, w_up, w_down)


def _combine_kernel(rank_ref, *refs, alpha, n_tiles):
    tile_refs = refs[:n_tiles]
    sel_ref, x_ref, gate_ref, g_ref, b_ref, o_ref, rows_ref, acc_ref = refs[n_tiles:]
    for k in range(n_tiles):
        rows_ref[k * MOE_TILE:(k + 1) * MOE_TILE, :] = tile_refs[k][...]
    sel = sel_ref[...]
    lane = lax.broadcasted_iota(jnp.int32, sel.shape, 1)
    col = lambda k: jnp.sum(jnp.where(lane == k, sel, 0.0), axis=-1, keepdims=True)
    row_a = col(0).astype(jnp.int32)
    row_b = col(1).astype(jnp.int32)
    w_a = col(2)
    w_b = col(3)
    n_rows = rows_ref.shape[0]
    for c0 in range(0, n_rows, MOE_GCHUNK):
        rid = lax.broadcasted_iota(jnp.int32, (sel.shape[0], MOE_GCHUNK), 1) + c0
        q = jnp.where(rid == row_a, w_a, jnp.where(rid == row_b, w_b, 0.0)).astype(BF16)
        part = jnp.dot(q, rows_ref[c0:c0 + MOE_GCHUNK, :], preferred_element_type=F32)
        if c0 == 0:
            acc_ref[...] = part
        else:
            acc_ref[...] += part
    y = alpha * x_ref[...] + gate_ref[0] * acc_ref[...]
    o_ref[...] = _layernorm(y, LN_EPS) * g_ref[...] + b_ref[...]


def _combine_call(ys, plan, x2, win_base, mod3, mod_row, g, b, alpha):
    n_tok, d = x2.shape
    n_tiles = plan["cap_tiles"]

    def tile_spec(k):
        return pl.BlockSpec((MOE_TILE, d), lambda w, rk: (rk[(win_base + w) * n_tiles + k], 0))

    win = lambda width: pl.BlockSpec((MOE_WINDOW, width), lambda w, rk: (w, 0))
    vec = pl.BlockSpec((1, d), lambda w, rk: (0, 0))
    return pl.pallas_call(
        functools.partial(_combine_kernel, alpha=alpha, n_tiles=n_tiles),
        grid_spec=pltpu.PrefetchScalarGridSpec(
            num_scalar_prefetch=1,
            grid=(n_tok // MOE_WINDOW,),
            in_specs=[tile_spec(k) for k in range(n_tiles)] + [
                pl.BlockSpec((MOE_WINDOW, 4), lambda w, rk: (win_base + w, 0)), win(d),
                pl.BlockSpec((1, 1, d), lambda w, rk: (mod_row(w) * 6 + 5, 0, 0)),
                vec, vec],
            out_specs=win(d),
            scratch_shapes=[pltpu.VMEM((n_tiles * MOE_TILE, d), BF16), pltpu.VMEM((MOE_WINDOW, d), F32)]),
        out_shape=jax.ShapeDtypeStruct((n_tok, d), F32),
        compiler_params=_cparams(),
        name="moe_combine_deepnorm",
    )(plan["tile_rank"], *([ys] * n_tiles), plan["tok_sel"], x2, mod3, g.reshape(1, d), b.reshape(1, d))


def _moe_call(parts, mod3, router_w, w_gate, w_up, w_down, layer_base, g, b, alpha, fc):
    routed = [_router_call(x2, mod3, lambda i, r=tok_row: r(i * MOE_WINDOW), router_w) for x2, tok_row in parts]
    comb = jnp.concatenate([r[1] for r in routed], axis=0)
    pos = jnp.concatenate([r[2] for r in routed], axis=0)
    plan = _moe_plan(comb, pos, comb.shape[0])
    xw = _dispatch_call([r[0] for r in routed], plan)
    ys = _expert_call(xw, plan, w_gate, w_up, w_down, layer_base, fc)
    outs = []
    win_base = 0
    for x2, tok_row in parts:
        outs.append(_combine_call(ys, plan, x2, win_base, mod3, lambda w, r=tok_row: r(w * MOE_WINDOW),
                                  g, b, alpha))
        win_base += x2.shape[0] // MOE_WINDOW
    return outs


def _pick_tile(n, pref):
    t = min(n, pref)
    while n % t:
        t //= 2
    return t


def kernel(x, c, ctx, c_ctx, w_mod, b_mod, w_in, q_gain, k_gain, hy_conv_w, hy_conv_b, hy_f_w1, hy_f_b1, hy_f_freq, hy_f_w2, hy_f_b2, hy_f_w3, hy_d, pool_w, pool_scale, w_out, ln1_g, ln1_b, ln2_g, ln2_b, ffn_w_gate, ffn_w_up, ffn_w_down, router_w, moe_w_gate, moe_w_up, moe_w_down):
    bsz, seq, d = x.shape
    clen = ctx.shape[1]
    depth = w_mod.shape[0]
    alpha = (2.0 * depth) ** 0.25
    assert seq % MOE_WINDOW == 0 and (bsz * clen) % MOE_WINDOW == 0 and bsz % 2 == 0
    n_exp = router_w.shape[2]
    moe_wg = moe_w_gate.reshape((-1,) + moe_w_gate.shape[2:])
    moe_wu = moe_w_up.reshape((-1,) + moe_w_up.shape[2:])
    moe_wd = moe_w_down.reshape((-1,) + moe_w_down.shape[2:])

    n_rows = -(-(bsz + 1) // 8) * 8
    cc = jnp.zeros((n_rows, d), F32).at[:bsz].set(c).at[bsz].set(c_ctx)
    mod = _mod_call(cc, w_mod, b_mod)

    tm_lat = _pick_tile(seq, 512)
    tm_ctx = _pick_tile(clen, 512)
    tm_ffn = _pick_tile(seq, 1024)
    tm_ffn_ctx = _pick_tile(bsz * clen, 1024)
    lat_row = lambda tm: (lambda i: i // (seq // tm))
    ctx_row = lambda i: bsz
    rope_lat = tuple(jnp.asarray(t) for t in _rope_tables(seq, True))
    rope_ctx = tuple(jnp.asarray(t) for t in _rope_tables(clen, False))
    head_avg = np.kron(np.eye(QK_W // HEAD_DIM), np.full((HEAD_DIM, HEAD_DIM), 1.0 / HEAD_DIM))
    head_avg = jnp.asarray(head_avg, BF16)

    xl = x.reshape(bsz * seq, d)
    xc = ctx.reshape(bsz * clen, d)
    for l in range(depth):
        last = l == depth - 1
        mod3 = mod[l].reshape(n_rows * 6, 1, d)
        w_in_bf = w_in[l].astype(BF16)
        w_out_bf = w_out[l].astype(BF16)
        gain = jnp.concatenate([jnp.tile(q_gain[l] * (ATTN_SCALE * LOG2E), N_Q_HEADS),
                                jnp.tile(k_gain[l], N_KV_HEADS)]).reshape(1, QK_W)
        shift = (1.02 * HEAD_DIM * ATTN_SCALE * LOG2E) * jnp.max(jnp.abs(q_gain[l])) * jnp.max(jnp.abs(k_gain[l]))
        lane64 = (jnp.arange(LANES) == HEAD_DIM).astype(F32)
        ext = jnp.stack([lane64, -shift * lane64, lane64])
        fast = (shift <= ATTN_SAFE_SHIFT).astype(jnp.int32).reshape(1)
        filt_w = (hy_f_w1[l], hy_f_b1[l], hy_f_freq[l], hy_f_w2[l], hy_f_b2[l], hy_f_w3[l])
        j = l // 2
        routed = l % 2 == 1
        ffn_w = (ffn_w_gate, ffn_w_up, ffn_w_down, j)

        q_c, k_c, v_c, hp_c = _proj_call(xc, mod3, ctx_row, clen, w_in_bf, gain, head_avg, rope_ctx, ext, tm_ctx)
        k_c3 = k_c.reshape(bsz, clen, -1)
        v_c3 = v_c.reshape(bsz, clen, -1)
        if not last:
            hp_c3 = hp_c.reshape(bsz, clen, -1)
            att_c = _attn_call(fast, q_c.reshape(bsz, clen, -1), (k_c3, v_c3), _pick_tile(clen, 256))
            spec_c = _ctx_spectrum(_filter_call(clen, *filt_w), clen)
            hy_c = _ctx_hyena_call(hp_c3, hy_conv_w[l], hy_conv_b[l], spec_c, hy_d[l])
            pl_c = _pool_call(hp_c3, pool_w[l], pool_scale[l])
            xc_mid = _outproj_call(xc, att_c.reshape(bsz * clen, D_ATTN), hy_c.reshape(bsz * clen, D_HYENA),
                                   pl_c.reshape(bsz * clen, D_POOL), mod3, ctx_row, w_out_bf,
                                   ln1_g[l], ln1_b[l], alpha, tm_ctx)

        q_l, k_l, v_l, hp_l = _proj_call(xl, mod3, lat_row(tm_lat), seq, w_in_bf, gain, head_avg, rope_lat, ext,
                                         tm_lat)
        hp_l3 = hp_l.reshape(bsz, seq, -1)
        att = _attn_call(fast, q_l.reshape(bsz, seq, -1),
                         (k_l.reshape(bsz, seq, -1), v_l.reshape(bsz, seq, -1), k_c3, v_c3), _pick_tile(seq, 256))
        spec = _hyena_spectrum(_filter_call(seq, *filt_w), seq)
        hyo = _hyena_latent(hp_l3, hy_conv_w[l], hy_conv_b[l], spec, hy_d[l])
        plo = _pool_call(hp_l3, pool_w[l], pool_scale[l])
        xl = _outproj_call(xl, att.reshape(bsz * seq, D_ATTN), hyo.reshape(bsz * seq, D_HYENA),
                           plo.reshape(bsz * seq, D_POOL), mod3, lat_row(tm_lat), w_out_bf,
                           ln1_g[l], ln1_b[l], alpha, tm_lat)
        if routed:
            parts = [(xl, lambda t: t // seq)] + ([] if last else [(xc_mid, lambda t: bsz)])
            outs = _moe_call(parts, mod3, router_w[j], moe_wg, moe_wu, moe_wd, j * n_exp,
                             ln2_g[l], ln2_b[l], alpha, MOE_FC)
            xl = outs[0]
            if not last:
                xc = outs[1]
        else:
            xl = _ffn_call(xl, mod3, lat_row(tm_ffn), *ffn_w, ln2_g[l], ln2_b[l], alpha, tm_ffn, FFN_FC)
            if not last:
                xc = _ffn_call(xc_mid, mod3, ctx_row, *ffn_w, ln2_g[l], ln2_b[l], alpha, tm_ffn_ctx, FFN_FC)
    return xl.reshape(bsz, seq, d)
```

```python
import functools
import math

import numpy as np
import jax
import jax.numpy as jnp
from jax import lax
from jax.experimental import pallas as pl
from jax.experimental.pallas import tpu as pltpu

F32 = jnp.float32
BF16 = jnp.bfloat16
HIGHEST = lax.Precision.HIGHEST

GRID_W = 64
HEAD_DIM = 64
N_Q_HEADS = 8
N_KV_HEADS = 2
GQA_GROUP = N_Q_HEADS // N_KV_HEADS
D_ATTN = N_Q_HEADS * HEAD_DIM
KV_W = N_KV_HEADS * HEAD_DIM
QK_W = D_ATTN + KV_W
ATTN_SCALE = HEAD_DIM ** -0.5
ROPE_THETA = 10000.0
ROPE_AXIS_DIM = HEAD_DIM // 2
QK_EPS = 1e-6
D_HYENA = 256
HYENA_ORDER = 2
HYENA_BANDS = 16
HYENA_EMB = 1 + 2 * HYENA_BANDS
HYENA_DECAY_TARGET = 1e-2
HYENA_FAST_DECAY_PCT = 0.3
HYENA_SLOW_DECAY_PCT = 1.5
D_POOL = 256
POOL_WINDOWS = (2, 4, 8, 16)
POOL_GROUP = D_POOL // len(POOL_WINDOWS)
POOL_PAD = 8
TOP_K = 2
LN_EPS = 1e-5
ADALN_EPS = 1e-6
LOG2E = math.log2(math.e)

LANES = 128
FFT_N2 = 64
FFT_UNROLL = 4
VMEM_LIMIT = 56 * 1024 * 1024
FFN_FC = 2816
FFN_SUB = 256


def _cparams():
    return pltpu.CompilerParams(vmem_limit_bytes=VMEM_LIMIT)


def _layernorm(x, eps):
    mu = jnp.mean(x, axis=-1, keepdims=True)
    xc = x - mu
    var = jnp.mean(xc * xc, axis=-1, keepdims=True)
    return xc * lax.rsqrt(var + eps)


def _mod_kernel(c_ref, w_ref, b_ref, o_ref):
    c = c_ref[...]
    s = c * jax.nn.sigmoid(c)
    o_ref[0] = jnp.dot(s, w_ref[0], preferred_element_type=F32, precision=HIGHEST) + b_ref[0]


def _mod_call(cc, w_mod, b_mod):
    depth, d, d6 = w_mod.shape
    r = cc.shape[0]
    tn = 1536
    return pl.pallas_call(
        _mod_kernel,
        grid=(depth, d6 // tn),
        in_specs=[pl.BlockSpec((r, d), lambda l, j: (0, 0)),
                  pl.BlockSpec((1, d, tn), lambda l, j: (l, 0, j)),
                  pl.BlockSpec((1, 1, tn), lambda l, j: (l, 0, j))],
        out_specs=pl.BlockSpec((1, r, tn), lambda l, j: (l, 0, j)),
        out_shape=jax.ShapeDtypeStruct((depth, r, d6), F32),
        compiler_params=_cparams(),
        name="adaln_mod",
    )(cc, w_mod, b_mod.reshape(depth, 1, d6))


def _rope_tables(seq_len, rope):
    lane = np.arange(LANES)
    d = lane % HEAD_DIM
    if not rope:
        one = np.ones((seq_len, LANES), np.float32)
        zero = np.zeros((seq_len, LANES), np.float32)
        return one, zero, zero
    t = np.arange(seq_len)
    row = (t // GRID_W).astype(np.float64)
    col = (t % GRID_W).astype(np.float64)
    half = ROPE_AXIS_DIM // 2
    inv = ROPE_THETA ** (-np.arange(0, ROPE_AXIS_DIM, 2, dtype=np.float64) / ROPE_AXIS_DIM)
    pos = np.where((d // ROPE_AXIS_DIM)[None, :] == 0, row[:, None], col[:, None])
    ang = pos * inv[d % half][None, :]
    is_b = ((d % ROPE_AXIS_DIM) >= half)[None, :]
    cos = np.cos(ang)
    sin = np.sin(ang)
    s_up = np.where(is_b, sin, 0.0)
    s_dn = np.where(is_b, 0.0, -sin)
    return cos.astype(np.float32), s_up.astype(np.float32), s_dn.astype(np.float32)


def _store_padded_heads(blk, ext, out_ref, col0):
    lo = lax.broadcasted_iota(jnp.int32, blk.shape, 1) < HEAD_DIM
    out_ref[:, col0:col0 + LANES] = jnp.where(lo, blk, ext).astype(out_ref.dtype)
    out_ref[:, col0 + LANES:col0 + 2 * LANES] = jnp.where(lo, pltpu.roll(blk, HEAD_DIM, axis=1),
                                                          ext).astype(out_ref.dtype)


def _proj_kernel(x_ref, sh_ref, sc_ref, w_ref, gain_ref, bd_ref, cos_ref, sup_ref, sdn_ref, ext_ref,
                 q_ref, k_ref, v_ref, hp_ref):
    u = _layernorm(x_ref[...], ADALN_EPS) * (1.0 + sc_ref[0]) + sh_ref[0]
    p = jnp.dot(u.astype(BF16), w_ref[...], preferred_element_type=F32)
    cos = cos_ref[...]
    sup = sup_ref[...]
    sdn = sdn_ref[...]
    n_qblk = D_ATTN // LANES
    for j in range(QK_W // LANES):
        blk = p[:, j * LANES:(j + 1) * LANES]
        ms = jnp.dot((blk * blk).astype(BF16), bd_ref[...], preferred_element_type=F32)
        blk = blk * lax.rsqrt(ms + QK_EPS) * gain_ref[:, j * LANES:(j + 1) * LANES]
        up = pltpu.roll(blk, ROPE_AXIS_DIM // 2, axis=1)
        dn = pltpu.roll(blk, LANES - ROPE_AXIS_DIM // 2, axis=1)
        rot = blk * cos + up * sup + dn * sdn
        if j < n_qblk:
            _store_padded_heads(rot, ext_ref[0:1, :], q_ref, 2 * j * LANES)
        else:
            _store_padded_heads(rot, ext_ref[1:2, :], k_ref, 2 * (j - n_qblk) * LANES)
    _store_padded_heads(p[:, QK_W:QK_W + KV_W], ext_ref[2:3, :], v_ref, 0)
    hp_ref[...] = p[:, QK_W + KV_W:]


def _proj_call(x2, mod3, mod_row, seq_len, w_in_bf, gain, bd, tables, ext, tm):
    r, d = x2.shape
    d_in = w_in_bf.shape[1]
    d_hp = d_in - QK_W - KV_W
    nseq = seq_len // tm
    cos, sup, sdn = tables
    tab_spec = pl.BlockSpec((tm, LANES), lambda i: (i % nseq, 0))
    return pl.pallas_call(
        _proj_kernel,
        grid=(r // tm,),
        in_specs=[pl.BlockSpec((tm, d), lambda i: (i, 0)),
                  pl.BlockSpec((1, 1, d), lambda i: (mod_row(i) * 6 + 0, 0, 0)),
                  pl.BlockSpec((1, 1, d), lambda i: (mod_row(i) * 6 + 1, 0, 0)),
                  pl.BlockSpec((d, d_in), lambda i: (0, 0)),
                  pl.BlockSpec((1, QK_W), lambda i: (0, 0)),
                  pl.BlockSpec((LANES, LANES), lambda i: (0, 0)),
                  tab_spec, tab_spec, tab_spec,
                  pl.BlockSpec((3, LANES), lambda i: (0, 0))],
        out_specs=[pl.BlockSpec((tm, N_Q_HEADS * LANES), lambda i: (i, 0)),
                   pl.BlockSpec((tm, N_KV_HEADS * LANES), lambda i: (i, 0)),
                   pl.BlockSpec((tm, N_KV_HEADS * LANES), lambda i: (i, 0)),
                   pl.BlockSpec((tm, d_hp), lambda i: (i, 0))],
        out_shape=[jax.ShapeDtypeStruct((r, N_Q_HEADS * LANES), BF16),
                   jax.ShapeDtypeStruct((r, N_KV_HEADS * LANES), BF16),
                   jax.ShapeDtypeStruct((r, N_KV_HEADS * LANES), BF16),
                   jax.ShapeDtypeStruct((r, d_hp), F32)],
        compiler_params=_cparams(),
        name="ln_mod_in_proj",
    )(x2, mod3, mod3, w_in_bf, gain, bd, cos, sup, sdn, ext)


ATTN_KEY_CHUNK = 1024
ATTN_SAFE_SHIFT = 60.0


def _attn_kernel(fast_ref, q_ref, *refs):
    o_ref = refs[-1]
    sources = [(refs[i], refs[i + 1]) for i in range(0, len(refs) - 1, 2)]
    tq = q_ref.shape[0]
    chunks = [(k_ref, v_ref, c0, min(ATTN_KEY_CHUNK, k_ref.shape[0] - c0))
              for k_ref, v_ref in sources for c0 in range(0, k_ref.shape[0], ATTN_KEY_CHUNK)]

    def run(running_max):
        for h in range(N_KV_HEADS):
            qs = jnp.concatenate([q_ref[:, (h * GQA_GROUP + g) * LANES:(h * GQA_GROUP + g + 1) * LANES]
                                  for g in range(GQA_GROUP)], axis=0)
            acc = jnp.zeros((GQA_GROUP * tq, LANES), F32)
            m = jnp.full((GQA_GROUP * tq, 1), -jnp.inf, F32)
            for k_ref, v_ref, c0, kw in chunks:
                kc = k_ref[c0:c0 + kw, h * LANES:(h + 1) * LANES]
                vc = v_ref[c0:c0 + kw, h * LANES:(h + 1) * LANES]
                s = lax.dot_general(qs, kc, (((1,), (1,)), ((), ())), preferred_element_type=F32)
                if running_max:
                    m_new = jnp.maximum(m, jnp.max(s, axis=-1, keepdims=True))
                    acc = acc * jnp.exp2(m - m_new)
                    s = s - m_new
                    m = m_new
                acc = acc + jnp.dot(jnp.exp2(s).astype(BF16), vc, preferred_element_type=F32)
            o = acc[:, :HEAD_DIM] / acc[:, HEAD_DIM:HEAD_DIM + 1]
            for g in range(GQA_GROUP):
                c0 = (h * GQA_GROUP + g) * HEAD_DIM
                o_ref[:, c0:c0 + HEAD_DIM] = o[g * tq:(g + 1) * tq].astype(o_ref.dtype)

    @pl.when(fast_ref[0] > 0)
    def _():
        run(False)

    @pl.when(fast_ref[0] == 0)
    def _():
        run(True)


def _attn_call(fast, q, kv, tq):
    b, lq, qw = q.shape
    kv_specs = [pl.BlockSpec((None,) + a.shape[1:], lambda i, j, f: (i, 0, 0)) for a in kv]
    return pl.pallas_call(
        _attn_kernel,
        grid_spec=pltpu.PrefetchScalarGridSpec(
            num_scalar_prefetch=1,
            grid=(b, lq // tq),
            in_specs=[pl.BlockSpec((None, tq, qw), lambda i, j, f: (i, j, 0))] + kv_specs,
            out_specs=pl.BlockSpec((None, tq, D_ATTN), lambda i, j, f: (i, j, 0))),
        out_shape=jax.ShapeDtypeStruct((b, lq, D_ATTN), BF16),
        compiler_params=_cparams(),
        name="attention",
    )(fast, q, *kv)


def _filter_consts(seq_len):
    t = np.linspace(0.0, 1.0, seq_len, dtype=np.float32).astype(np.float64)[:, None]
    omega = 2.0 * math.pi * np.arange(seq_len, dtype=np.float64)[:, None] / seq_len
    bands = np.linspace(1e-4, HYENA_BANDS - 1, HYENA_BANDS, dtype=np.float32).astype(np.float64)[None, :]
    feats = np.concatenate([t, np.cos(omega * bands), -np.sin(omega * bands)], axis=-1)
    feats = np.pad(feats, ((0, 0), (0, LANES - HYENA_EMB)))
    max_decay = math.log(HYENA_DECAY_TARGET) / HYENA_FAST_DECAY_PCT
    min_decay = math.log(HYENA_DECAY_TARGET) / HYENA_SLOW_DECAY_PCT
    deltas = np.linspace(min_decay, max_decay, D_HYENA, dtype=np.float32).astype(np.float64)
    decay = np.exp(-t * np.abs(deltas)[None, :])
    rev = (seq_len - np.arange(seq_len)) % seq_len
    f32 = lambda a: jnp.asarray(a.astype(np.float32))
    return f32(feats), f32(feats[rev]), f32(decay), f32(decay[rev])


def _filter_kernel(feat_ref, featr_ref, w1_ref, b1_ref, fr_ref, w2_ref, b2_ref, w3_ref, dec_ref, decr_ref, o_ref,
                   hid_ref):
    seq_len = feat_ref.shape[0]
    dot = functools.partial(jnp.dot, preferred_element_type=F32, precision=HIGHEST)

    @pl.when(pl.program_id(0) == 0)
    def _():
        fr = fr_ref[...]
        for i, feat in enumerate((feat_ref, featr_ref)):
            h = jnp.sin(fr * (dot(feat[...], w1_ref[...]) + b1_ref[...]))
            hid_ref[i] = jnp.sin(fr * (dot(h, w2_ref[...]) + b2_ref[...]))

    fwd = dot(hid_ref[0], w3_ref[:, :D_HYENA]) * dec_ref[...]
    bwd = dot(hid_ref[1], w3_ref[:, D_HYENA:]) * decr_ref[...]
    tot = jnp.sum(jnp.abs(fwd), axis=0, keepdims=True) + jnp.sum(jnp.abs(bwd), axis=0, keepdims=True)
    inv = 1.0 / tot
    o_ref[:seq_len, :] = fwd * inv
    row = lax.broadcasted_iota(jnp.int32, bwd.shape, 0)
    o_ref[seq_len:, :] = jnp.where(row == 0, 0.0, bwd * inv)


def _filter_call(seq_len, w1, b1, freq, w2, b2, w3):
    feats, feats_rev, decay, decay_rev = _filter_consts(seq_len)
    hid = w2.shape[0]
    w1p = jnp.pad(w1, ((0, LANES - HYENA_EMB), (0, 0)))
    full = lambda shape: pl.BlockSpec(shape, lambda o: (0,) * len(shape))
    return pl.pallas_call(
        _filter_kernel,
        grid=(HYENA_ORDER,),
        in_specs=[full((seq_len, LANES)), full((seq_len, LANES)), full((LANES, hid)), full((1, hid)),
                  full((1, hid)), full((hid, hid)), full((1, hid)),
                  pl.BlockSpec((hid, 2 * D_HYENA), lambda o: (0, o)),
                  full((seq_len, D_HYENA)), full((seq_len, D_HYENA))],
        out_specs=pl.BlockSpec((2 * seq_len, D_HYENA), lambda o: (0, o)),
        out_shape=jax.ShapeDtypeStruct((2 * seq_len, HYENA_ORDER * D_HYENA), F32),
        scratch_shapes=[pltpu.VMEM((2, seq_len, hid), F32)],
        compiler_params=_cparams(),
        name="hyena_filter",
    )(feats, feats_rev, w1p, b1.reshape(1, hid), freq.reshape(1, hid), w2, b2.reshape(1, hid), w3,
      decay, decay_rev)


@functools.lru_cache(maxsize=None)
def _fft_consts(seq_len):
    n = 2 * seq_len
    n2 = FFT_N2
    n1 = n // n2
    h1 = n1 // 2
    k1 = np.arange(n1)
    w1 = np.exp(-2j * np.pi * np.outer(k1, np.arange(n1)) / n1)
    f1 = np.zeros((n1, 2, 2 * h1))
    f1[:, 0, :h1], f1[:, 0, h1:] = w1[:, :h1].real, -w1[:, :h1].imag
    f1[:, 1, :h1], f1[:, 1, h1:] = w1[:, :h1].imag, w1[:, :h1].real
    f1 = f1.reshape(2 * n1, 2 * h1)
    f1r = np.stack([w1.real, w1.imag], axis=1).reshape(2 * n1, n1)
    a2 = np.arange(n2)
    tw = np.exp(-2j * np.pi * np.outer(k1, a2) / n)
    w2 = np.exp(-2j * np.pi * np.outer(a2, a2) / n2)
    m = w2[None, :, :] * tw[:, None, :]
    ff = np.concatenate([np.concatenate([m.real, -m.imag], axis=2),
                         np.concatenate([m.imag, m.real], axis=2)], axis=1)
    g = np.conj(w2.T)[None, :, :] * np.conj(tw)[:, :, None]
    gi = np.concatenate([np.concatenate([g.real, -g.imag], axis=2),
                         np.concatenate([g.imag, g.real], axis=2)], axis=1)
    v = np.conj(w1[:, :h1]).T
    f1i = np.zeros((2, h1, n1, 2))
    f1i[0, :, :, 0], f1i[0, :, :, 1] = v.real, -v.imag
    f1i[1, :, :, 0], f1i[1, :, :, 1] = v.imag, v.real
    f1i = f1i.reshape(2 * h1, 2 * n1)
    il = np.arange(2 * n2).reshape(2, n2).T.reshape(-1)
    ff_il = ff[:, :, il]
    gi_il = gi[:, il, :]
    f32 = lambda a: np.asarray(a, np.float32)
    return dict(n=n, n1=n1, h1=h1, f1=f32(f1), f1r=f32(f1r), ff=f32(ff), ff_il=f32(ff_il), gi_il=f32(gi_il),
                f1i=f32(f1i))


def _fftr_kernel(x_ref, f_ref, o_ref):
    n1 = f_ref.shape[1]

    def body(n2, carry):
        x = x_ref[pl.ds(n2, n1, stride=FFT_N2), :]
        o_ref[pl.ds(n2, 2 * n1, stride=FFT_N2), :] = jnp.dot(f_ref[...], x, preferred_element_type=F32,
                                                            precision=HIGHEST)
        return carry

    lax.fori_loop(0, FFT_N2, body, 0, unroll=FFT_UNROLL)


def _fftr_call(buf, f1r):
    n, cols = buf.shape
    rows = f1r.shape[0] * FFT_N2
    return pl.pallas_call(
        _fftr_kernel,
        grid=(cols // LANES,),
        in_specs=[pl.BlockSpec((n, LANES), lambda j: (0, j)),
                  pl.BlockSpec(f1r.shape, lambda j: (0, 0))],
        out_specs=pl.BlockSpec((None, rows, LANES), lambda j: (j, 0, 0)),
        out_shape=jax.ShapeDtypeStruct((cols // LANES, rows, LANES), F32),
        compiler_params=_cparams(),
        name="fft_filter_outer_dft",
    )(buf, f1r)


def _spec_kernel(a_ref, ff_ref, o_ref, *, scale):
    for kk in range(a_ref.shape[1]):
        for cb in range(a_ref.shape[0]):
            o_ref[kk, :, cb * LANES:(cb + 1) * LANES] = jnp.dot(
                ff_ref[kk], a_ref[cb, kk], preferred_element_type=F32, precision=HIGHEST) * scale


def _spec_call(a4, ff, scale, kt):
    nblk, n1, r, _ = a4.shape
    return pl.pallas_call(
        functools.partial(_spec_kernel, scale=scale),
        grid=(n1 // kt,),
        in_specs=[pl.BlockSpec((nblk, kt, r, LANES), lambda i: (0, i, 0, 0)),
                  pl.BlockSpec((kt, r, r), lambda i: (i, 0, 0))],
        out_specs=pl.BlockSpec((kt, r, nblk * LANES), lambda i: (i, 0, 0)),
        out_shape=jax.ShapeDtypeStruct((n1, r, nblk * LANES), F32),
        compiler_params=_cparams(),
        name="fft_filter_spectrum",
    )(a4, ff)


def _short_conv(z, w, b):
    n = z.shape[0]
    row = lax.broadcasted_iota(jnp.int32, z.shape, 0)
    prev = jnp.where(row == 0, 0.0, pltpu.roll(z, 1, axis=0))
    nxt = jnp.where(row == n - 1, 0.0, pltpu.roll(z, n - 1, axis=0))
    return prev * w[0:1, :] + z * w[1:2, :] + nxt * w[2:3, :] + b


def _ffta_kernel(*refs, conv):
    if conv:
        z_ref, w_ref, cb_ref, f_ref, o_ref, x_ref = refs
        for m in range(2):
            x_ref[m] = _short_conv(z_ref[m], w_ref[...], cb_ref[...])
    else:
        z_ref, f_ref, o_ref = refs
        x_ref = z_ref
    n1 = f_ref.shape[0] // 2
    h1 = f_ref.shape[1] // 2

    def body(n2, carry):
        rows = pl.ds(n2, h1, stride=FFT_N2)
        x = jnp.concatenate([x_ref[0, rows, :], x_ref[1, rows, :]], axis=0).astype(BF16)
        a = jnp.dot(f_ref[...], x, preferred_element_type=F32).astype(BF16)
        o_ref[pl.ds(n2, n1, stride=FFT_N2), :] = pltpu.bitcast(a, jnp.uint32)
        return carry

    lax.fori_loop(0, FFT_N2, body, 0, unroll=FFT_UNROLL)


def _ffta_call(z4, col0, f1_bf, conv_w=None, conv_b=None):
    groups, _, l, _ = z4.shape
    halves = D_HYENA // LANES
    n1 = f1_bf.shape[0] // 2
    conv = conv_w is not None
    in_specs = [pl.BlockSpec((None, 2, l, LANES), lambda i, j: (i, 0, 0, col0 + j))]
    args = [z4]
    scratch = []
    if conv:
        in_specs += [pl.BlockSpec((3, LANES), lambda i, j: (0, col0 + j)),
                     pl.BlockSpec((1, LANES), lambda i, j: (0, col0 + j))]
        args += [conv_w, conv_b]
        scratch = [pltpu.VMEM((2, l, LANES), F32)]
    in_specs.append(pl.BlockSpec(f1_bf.shape, lambda i, j: (0, 0)))
    args.append(f1_bf)
    return pl.pallas_call(
        functools.partial(_ffta_kernel, conv=conv),
        grid=(groups, halves),
        in_specs=in_specs,
        out_specs=pl.BlockSpec((None, None, n1 * FFT_N2, LANES), lambda i, j: (i, j, 0, 0)),
        out_shape=jax.ShapeDtypeStruct((groups, halves, n1 * FFT_N2, LANES), jnp.uint32),
        scratch_shapes=scratch,
        compiler_params=_cparams(),
        name="fft_outer_dft",
    )(*args)


def _fftb_kernel(a_ref, ff_ref, gi_ref, h_ref, o_ref):
    groups, halves, kt = a_ref.shape[:3]
    half = FFT_N2

    def body(kk, carry):
        ff = ff_ref[kk]
        gi = gi_ref[kk]
        hr = h_ref[kk, :half, :]
        hi = h_ref[kk, half:, :]
        for g in range(groups):
            a = jnp.concatenate([pltpu.bitcast(a_ref[g, hh, kk], BF16) for hh in range(halves)], axis=1)
            xh = jnp.dot(ff, a, preferred_element_type=F32)
            xr, xi = xh[:half], xh[half:]
            y = jnp.concatenate([xr * hr - xi * hi, xr * hi + xi * hr], axis=0).astype(BF16)
            b = jnp.dot(gi, y, preferred_element_type=F32).astype(BF16)
            for hh in range(halves):
                o_ref[g, hh, kk] = pltpu.bitcast(b[:, hh * LANES:(hh + 1) * LANES], jnp.uint32)
        return carry

    lax.fori_loop(0, kt, body, 0)


def _fftb_call(a5, ff_bf, gi_bf, spec, order, kt):
    g, halves, n1, n2, _ = a5.shape
    r = 2 * n2
    return pl.pallas_call(
        _fftb_kernel,
        grid=(n1 // kt,),
        in_specs=[pl.BlockSpec((g, halves, kt, n2, LANES), lambda i: (0, 0, i, 0, 0)),
                  pl.BlockSpec((kt, r, r), lambda i: (i, 0, 0)),
                  pl.BlockSpec((kt, r, r), lambda i: (i, 0, 0)),
                  pl.BlockSpec((kt, r, halves * LANES), lambda i: (i, 0, order))],
        out_specs=pl.BlockSpec((g, halves, kt, n2, LANES), lambda i: (0, 0, i, 0, 0)),
        out_shape=jax.ShapeDtypeStruct(a5.shape, jnp.uint32),
        compiler_params=_cparams(),
        name="fft_inner_conv",
    )(a5, ff_bf, gi_bf, spec)


def _fftc_kernel(*refs, z_conv):
    if z_conv:
        b_ref, f_ref, z_ref, zw_ref, zb_ref, g_ref, gw_ref, gb_ref, d_ref, o_ref = refs
    else:
        b_ref, f_ref, z_ref, g_ref, gw_ref, gb_ref, d_ref, o_ref = refs
    n1 = f_ref.shape[1] // 2
    h1 = f_ref.shape[0] // 2
    d = d_ref[...]

    def body(n2, carry):
        b = pltpu.bitcast(b_ref[pl.ds(n2, n1, stride=FFT_N2), :], BF16)
        y = jnp.dot(f_ref[...], b, preferred_element_type=F32)
        rows = pl.ds(n2, h1, stride=FFT_N2)
        for m in range(2):
            o_ref[m, rows, :] = y[m * h1:(m + 1) * h1]
        return carry

    lax.fori_loop(0, FFT_N2, body, 0, unroll=FFT_UNROLL)
    for m in range(2):
        z = _short_conv(z_ref[m], zw_ref[...], zb_ref[...]) if z_conv else z_ref[m]
        gate = _short_conv(g_ref[m], gw_ref[...], gb_ref[...])
        o_ref[m] = gate * (o_ref[m] + d * z)


def _fftc_call(b4, f1i_bf, z4, z_col0, z_conv, hp4, g_col0, conv_w, conv_b, d2, order):
    groups, halves, rows, _ = b4.shape
    l = z4.shape[2]
    pair = lambda col0: pl.BlockSpec((None, 2, l, LANES), lambda i, j: (i, 0, 0, col0 + j))
    taps = lambda col0: [pl.BlockSpec((3, LANES), lambda i, j: (0, col0 + j)),
                         pl.BlockSpec((1, LANES), lambda i, j: (0, col0 + j))]
    in_specs = [pl.BlockSpec((None, None, rows, LANES), lambda i, j: (i, j, 0, 0)),
                pl.BlockSpec(f1i_bf.shape, lambda i, j: (0, 0)),
                pair(z_col0)]
    args = [b4, f1i_bf, z4]
    if z_conv:
        in_specs += taps(z_col0)
        args += [conv_w, conv_b]
    in_specs += [pair(g_col0)] + taps(g_col0) + [pl.BlockSpec((None, 1, LANES), lambda i, j: (order, 0, j))]
    args += [hp4, conv_w, conv_b, d2.reshape(d2.shape[0], 1, d2.shape[1])]
    return pl.pallas_call(
        functools.partial(_fftc_kernel, z_conv=z_conv),
        grid=(groups, halves),
        in_specs=in_specs,
        out_specs=pair(0),
        out_shape=jax.ShapeDtypeStruct((groups, 2, l, D_HYENA), F32),
        compiler_params=_cparams(),
        name="fft_outer_inverse_gate",
    )(*args)


def _hyena_spectrum(buf, seq_len):
    cst = _fft_consts(seq_len)
    n, n1 = cst["n"], cst["n1"]
    a = _fftr_call(buf, jnp.asarray(cst["f1r"]))
    a4 = a.reshape(a.shape[0], n1, 2 * FFT_N2, LANES)
    return _spec_call(a4, jnp.asarray(cst["ff"]), 1.0 / n, min(n1, 8))


def _hyena_latent(hp3, conv_w, conv_b, spec, d):
    b, l, _ = hp3.shape
    cst = _fft_consts(l)
    n1, h1 = cst["n1"], cst["h1"]
    groups = b // 2
    halves = D_HYENA // LANES
    kt = min(n1, 8)
    f1 = jnp.asarray(cst["f1"], BF16)
    ff = jnp.asarray(cst["ff_il"], BF16)
    gi = jnp.asarray(cst["gi_il"], BF16)
    f1i = jnp.asarray(cst["f1i"], BF16)
    hp4 = hp3.reshape(groups, 2, l, hp3.shape[2])
    cb = conv_b.reshape(1, -1)
    z = hp4
    for o in range(HYENA_ORDER):
        first = o == 0
        a = _ffta_call(z, 0, f1, conv_w, cb) if first else _ffta_call(z, 0, f1)
        bb = _fftb_call(a.reshape(groups, halves, n1, FFT_N2, LANES), ff, gi, spec, o, kt)
        z = _fftc_call(bb.reshape(groups, halves, n1 * FFT_N2, LANES), f1i, z, 0, first, hp4, (1 + o) * halves,
                       conv_w, cb, d, o)
    return z.reshape(b, l, D_HYENA)


@functools.lru_cache(maxsize=None)
def _dft_consts(seq_len):
    n = 2 * seq_len
    w = np.exp(-2j * np.pi * np.outer(np.arange(n), np.arange(n)) / n)
    fwd_full = np.concatenate([w.real, w.imag], axis=0)
    fwd = fwd_full[:, :seq_len]
    wi = np.conj(w[:, :seq_len]).T
    inv = np.concatenate([wi.real, -wi.imag], axis=1)
    f32 = lambda a: np.asarray(a, np.float32)
    return dict(n=n, fwd_full=f32(fwd_full), fwd=f32(fwd), inv=f32(inv))


def _cspec_kernel(buf_ref, f_ref, o_ref, *, scale):
    o_ref[...] = jnp.dot(f_ref[...], buf_ref[...], preferred_element_type=F32, precision=HIGHEST) * scale


def _ctx_spectrum(buf, seq_len):
    cst = _dft_consts(seq_len)
    n = cst["n"]
    return pl.pallas_call(
        functools.partial(_cspec_kernel, scale=1.0 / n),
        out_shape=jax.ShapeDtypeStruct((2 * n, buf.shape[1]), F32),
        compiler_params=_cparams(),
        name="ctx_filter_spectrum",
    )(buf, jnp.asarray(cst["fwd_full"]))


def _ctx_hyena_kernel(z_ref, w_ref, b_ref, f_ref, fi_ref, h_ref, d_ref, o_ref):
    n = 2 * z_ref.shape[0]
    zc = _short_conv(z_ref[...], w_ref[...], b_ref[...])
    cur = zc[:, :D_HYENA]
    for o in range(HYENA_ORDER):
        gate = zc[:, (1 + o) * D_HYENA:(2 + o) * D_HYENA]
        xh = jnp.dot(f_ref[...], cur.astype(BF16), preferred_element_type=F32)
        xr, xi = xh[:n], xh[n:]
        hr = h_ref[:n, o * D_HYENA:(o + 1) * D_HYENA]
        hi = h_ref[n:, o * D_HYENA:(o + 1) * D_HYENA]
        y = jnp.concatenate([xr * hr - xi * hi, xr * hi + xi * hr], axis=0).astype(BF16)
        conv = jnp.dot(fi_ref[...], y, preferred_element_type=F32)
        cur = gate * (conv + d_ref[o:o + 1, :] * cur)
    o_ref[...] = cur


def _ctx_hyena_call(hp3, conv_w, conv_b, spec, d):
    b, l, _ = hp3.shape
    cst = _dft_consts(l)
    n = cst["n"]
    w3 = 3 * D_HYENA
    full = lambda shape: pl.BlockSpec(shape, lambda i: (0,) * len(shape))
    return pl.pallas_call(
        _ctx_hyena_kernel,
        grid=(b,),
        in_specs=[pl.BlockSpec((None, l, w3), lambda i: (i, 0, 0)),
                  full((3, w3)), full((1, w3)), full((2 * n, l)), full((l, 2 * n)),
                  full((2 * n, HYENA_ORDER * D_HYENA)), full((HYENA_ORDER, D_HYENA))],
        out_specs=pl.BlockSpec((None, l, D_HYENA), lambda i: (i, 0, 0)),
        out_shape=jax.ShapeDtypeStruct((b, l, D_HYENA), F32),
        compiler_params=_cparams(),
        name="ctx_hyena",
    )(hp3, conv_w, conv_b.reshape(1, -1), jnp.asarray(cst["fwd"], BF16), jnp.asarray(cst["inv"], BF16),
      spec, d)


def _pool_kernel(p_ref, w_ref, sc_ref, o_ref):
    x = p_ref[...]
    l, c = x.shape
    half = pl.program_id(1)
    zpad = jnp.zeros((POOL_PAD, c), F32)
    xp = jnp.concatenate([zpad, x, zpad], axis=0)
    tot = l + 2 * POOL_PAD
    sums = []
    f = xp
    for win in POOL_WINDOWS:
        f = f + pltpu.roll(f, tot - win // 2, axis=0)
        sums.append(pltpu.roll(f, win // 2, axis=0)[POOL_PAD:POOL_PAD + l])
    t = lax.broadcasted_iota(jnp.int32, (l, c), 0)
    grp = lax.broadcasted_iota(jnp.int32, (l, c), 1) // POOL_GROUP + half * (LANES // POOL_GROUP)
    total = sums[-1]
    cnt = None
    for gi in range(len(POOL_WINDOWS) - 1, -1, -1):
        win = POOL_WINDOWS[gi]
        cw = (jnp.minimum(t + (win - win // 2), l) - jnp.maximum(t - win // 2, 0)).astype(F32)
        if cnt is None:
            cnt = cw
        else:
            sel = grp == gi
            total = jnp.where(sel, sums[gi], total)
            cnt = jnp.where(sel, cw, cnt)
    dlt = total / cnt - x
    y = jnp.dot(dlt.astype(BF16), w_ref[0], preferred_element_type=F32)
    o_ref[...] = y * sc_ref[...]


def _pool_call(hp3, pool_w, pool_scale):
    b, l, w = hp3.shape
    nh = D_POOL // LANES
    first = (w - D_POOL) // LANES
    per = LANES // POOL_GROUP
    wbd = jnp.zeros((nh, LANES, LANES), F32)
    for g in range(len(POOL_WINDOWS)):
        r0 = (g % per) * POOL_GROUP
        wbd = wbd.at[g // per, r0:r0 + POOL_GROUP, r0:r0 + POOL_GROUP].set(pool_w[g])
    return pl.pallas_call(
        _pool_kernel,
        grid=(b, nh),
        in_specs=[pl.BlockSpec((None, l, LANES), lambda i, j: (i, 0, first + j)),
                  pl.BlockSpec((1, LANES, LANES), lambda i, j: (j, 0, 0)),
                  pl.BlockSpec((1, LANES), lambda i, j: (0, j))],
        out_specs=pl.BlockSpec((None, l, LANES), lambda i, j: (i, 0, j)),
        out_shape=jax.ShapeDtypeStruct((b, l, D_POOL), F32),
        compiler_params=_cparams(),
        name="pool_mixer",
    )(hp3, wbd.astype(BF16), pool_scale.reshape(1, D_POOL))


def _outproj_kernel(x_ref, att_ref, hy_ref, pl_ref, gate_ref, w_ref, g_ref, b_ref, o_ref, *, alpha):
    mix = jnp.dot(att_ref[...], w_ref[0:D_ATTN, :], preferred_element_type=F32)
    mix += jnp.dot(hy_ref[...].astype(BF16), w_ref[D_ATTN:D_ATTN + D_HYENA, :], preferred_element_type=F32)
    mix += jnp.dot(pl_ref[...].astype(BF16), w_ref[D_ATTN + D_HYENA:, :], preferred_element_type=F32)
    y = alpha * x_ref[...] + gate_ref[0] * mix
    o_ref[...] = _layernorm(y, LN_EPS) * g_ref[...] + b_ref[...]


def _outproj_call(x2, att2, hy2, pl2, mod3, mod_row, w_out_bf, g, b, alpha, tm):
    r, d = x2.shape
    row = lambda w: pl.BlockSpec((tm, w), lambda i: (i, 0))
    vec = pl.BlockSpec((1, d), lambda i: (0, 0))
    return pl.pallas_call(
        functools.partial(_outproj_kernel, alpha=alpha),
        grid=(r // tm,),
        in_specs=[row(d), row(D_ATTN), row(D_HYENA), row(D_POOL),
                  pl.BlockSpec((1, 1, d), lambda i: (mod_row(i) * 6 + 2, 0, 0)),
                  pl.BlockSpec(w_out_bf.shape, lambda i: (0, 0)),
                  vec, vec],
        out_specs=row(d),
        out_shape=jax.ShapeDtypeStruct((r, d), F32),
        compiler_params=_cparams(),
        name="out_proj_deepnorm",
    )(x2, att2, hy2, pl2, mod3, w_out_bf, g.reshape(1, d), b.reshape(1, d))


def _ffn_kernel(x_ref, sh_ref, sc_ref, gate_ref, wg_ref, wu_ref, wd_ref, g_ref, b_ref, o_ref, u_ref, acc_ref, *,
                alpha, n_chunk):
    j = pl.program_id(1)

    @pl.when(j == 0)
    def _():
        u_ref[...] = (_layernorm(x_ref[...], ADALN_EPS) * (1.0 + sc_ref[0]) + sh_ref[0]).astype(BF16)
        acc_ref[...] = jnp.zeros_like(acc_ref)

    u = u_ref[...]
    part = None
    for c0 in range(0, wg_ref.shape[2], FFN_SUB):
        c1 = min(c0 + FFN_SUB, wg_ref.shape[2])
        a = jnp.dot(u, wg_ref[0, :, c0:c1], preferred_element_type=F32)
        bb = jnp.dot(u, wu_ref[0, :, c0:c1], preferred_element_type=F32)
        h = (a * jax.nn.sigmoid(a) * bb).astype(BF16)
        y = jnp.dot(h, wd_ref[0, c0:c1, :], preferred_element_type=F32)
        part = y if part is None else part + y
    acc_ref[...] += part

    @pl.when(j == n_chunk - 1)
    def _():
        y = alpha * x_ref[...] + gate_ref[0] * acc_ref[...]
        o_ref[...] = _layernorm(y, LN_EPS) * g_ref[...] + b_ref[...]


def _ffn_call(x2, mod3, mod_row, w_gate, w_up, w_down, w_idx, g, b, alpha, tm, fc):
    r, d = x2.shape
    dff = w_gate.shape[2]
    n_chunk = dff // fc
    wmode = pl.Buffered(1) if n_chunk == 1 else None
    vec = pl.BlockSpec((1, d), lambda i, j: (0, 0))
    mod = lambda which: pl.BlockSpec((1, 1, d), lambda i, j: (mod_row(i) * 6 + which, 0, 0))
    return pl.pallas_call(
        functools.partial(_ffn_kernel, alpha=alpha, n_chunk=n_chunk),
        grid=(r // tm, n_chunk),
        in_specs=[pl.BlockSpec((tm, d), lambda i, j: (i, 0)), mod(3), mod(4), mod(5),
                  pl.BlockSpec((1, d, fc), lambda i, j: (w_idx, 0, j), pipeline_mode=wmode),
                  pl.BlockSpec((1, d, fc), lambda i, j: (w_idx, 0, j), pipeline_mode=wmode),
                  pl.BlockSpec((1, fc, d), lambda i, j: (w_idx, j, 0), pipeline_mode=wmode),
                  vec, vec],
        out_specs=pl.BlockSpec((tm, d), lambda i, j: (i, 0)),
        out_shape=jax.ShapeDtypeStruct((r, d), F32),
        scratch_shapes=[pltpu.VMEM((tm, d), BF16), pltpu.VMEM((tm, d), F32)],
        compiler_params=_cparams(),
        name="channel_mixer",
    )(x2, mod3, mod3, mod3, w_gate, w_up, w_down, g.reshape(1, d), b.reshape(1, d))


MOE_WINDOW = 1024
MOE_TILE = 64
MOE_SUPER = 16
MOE_FC = 512
MOE_GCHUNK = 512
MOE_SUB = 256


def _router_kernel(x_ref, sh_ref, sc_ref, rw_ref, tri_ref, u_ref, comb_ref, pos_ref):
    u = (_layernorm(x_ref[...], ADALN_EPS) * (1.0 + sc_ref[0]) + sh_ref[0]).astype(BF16)
    u_ref[...] = u
    logits = jnp.dot(u, rw_ref[...].astype(BF16), preferred_element_type=F32)
    n_exp = logits.shape[1]
    lane = lax.broadcasted_iota(jnp.int32, logits.shape, 1)
    m1 = jnp.max(logits, axis=-1, keepdims=True)
    i1 = jnp.min(jnp.where(logits == m1, lane, n_exp), axis=-1, keepdims=True)
    rest = jnp.where(lane == i1, -jnp.inf, logits)
    m2 = jnp.max(rest, axis=-1, keepdims=True)
    i2 = jnp.min(jnp.where(rest == m2, lane, n_exp), axis=-1, keepdims=True)
    e2 = jnp.exp(m2 - m1)
    den = 1.0 + e2
    comb = jnp.where(lane == i1, 1.0 / den, 0.0) + jnp.where(lane == i2, e2 / den, 0.0)
    comb_ref[...] = comb
    routed = jnp.where(comb > 0.0, 1.0, 0.0).astype(BF16)
    pos_ref[...] = jnp.dot(tri_ref[...], routed, preferred_element_type=F32) - 1.0


def _router_call(x2, mod3, mod_row, router_w):
    r, d = x2.shape
    n_exp = router_w.shape[1]
    tm = MOE_WINDOW
    tri = jnp.asarray(np.tril(np.ones((tm, tm), np.float32)), BF16)
    mod = lambda which: pl.BlockSpec((1, 1, d), lambda i: (mod_row(i) * 6 + which, 0, 0))
    small = pl.BlockSpec((tm, n_exp), lambda i: (i, 0))
    return pl.pallas_call(
        _router_kernel,
        grid=(r // tm,),
        in_specs=[pl.BlockSpec((tm, d), lambda i: (i, 0)), mod(3), mod(4),
                  pl.BlockSpec(router_w.shape, lambda i: (0, 0)),
                  pl.BlockSpec((tm, tm), lambda i: (0, 0))],
        out_specs=[pl.BlockSpec((tm, d), lambda i: (i, 0)), small, small],
        out_shape=[jax.ShapeDtypeStruct((r, d), BF16), jax.ShapeDtypeStruct((r, n_exp), F32),
                   jax.ShapeDtypeStruct((r, n_exp), F32)],
        compiler_params=_cparams(),
        name="moe_router",
    )(x2, mod3, mod3, router_w, tri)


def _moe_plan(comb, pos_f, n_tok):
    n_exp = comb.shape[1]
    win, tile = MOE_WINDOW, MOE_TILE
    n_win = n_tok // win
    cap_tiles = 2 * win // tile + n_exp
    i32 = jnp.int32
    mask = (comb > 0.0).reshape(n_win, win, n_exp)
    pos = pos_f.astype(i32).reshape(n_win, win, n_exp)
    cnt = pos[:, -1, :] + 1
    ntile = (cnt + tile - 1) // tile
    woff = jnp.cumsum(ntile, axis=1) - ntile
    rows = jnp.where(mask, tile * woff[:, None, :] + pos, -1)
    row_a = jnp.max(rows, axis=-1)
    row_b = jnp.max(jnp.where(rows == row_a[..., None], -1, rows), axis=-1)
    rowsel = jnp.stack([row_a, row_b], axis=1)
    win_rows = tile * jnp.sum(ntile, axis=1)

    etiles = jnp.sum(ntile, axis=0)
    nsup = (etiles + MOE_SUPER - 1) // MOE_SUPER
    ebase = MOE_SUPER * (jnp.cumsum(nsup) - nsup)
    erank = ebase[None, :] + jnp.cumsum(ntile, axis=0) - ntile
    n_sup = -(-(n_win * cap_tiles) // MOE_SUPER) + n_exp
    n_rank = n_sup * MOE_SUPER
    slots = win // tile
    s = jnp.arange(slots, dtype=i32)[None, None, :]
    ok = s < ntile[:, :, None]
    rank = jnp.where(ok, erank[:, :, None] + s, n_rank)
    wm_tile = (jnp.arange(n_win, dtype=i32) * cap_tiles)[:, None, None] + woff[:, :, None] + s
    src = jnp.zeros((n_rank + 1,), i32).at[rank.reshape(-1)].set(wm_tile.reshape(-1))[:n_rank]
    sup_total = jnp.sum(nsup)
    g = jnp.arange(n_sup, dtype=i32)
    sup_valid = (g < sup_total).astype(i32)
    sup_exp = jnp.clip(jnp.sum((g[:, None] >= jnp.cumsum(nsup)[None, :]).astype(i32), axis=1), 0, n_exp - 1)
    sup_exp = jnp.where(sup_valid > 0, sup_exp, sup_exp[jnp.maximum(sup_total - 1, 0)])

    tile_rank = jnp.zeros((n_win * cap_tiles + 1,), i32).at[
        jnp.where(ok, wm_tile, n_win * cap_tiles).reshape(-1)].set(
        jnp.minimum(rank, n_rank - 1).reshape(-1))[:n_win * cap_tiles]
    comb3 = comb.reshape(n_win, win, n_exp)
    gate_a = jnp.sum(jnp.where(rows == row_a[..., None], comb3, 0.0), axis=-1)
    gate_b = jnp.sum(jnp.where((rows == row_b[..., None]) & mask, comb3, 0.0), axis=-1)
    tok_sel = jnp.stack([row_a.astype(F32), row_b.astype(F32), gate_a, gate_b], axis=-1).reshape(n_tok, 4)
    return dict(n_win=n_win, cap_tiles=cap_tiles, n_sup=n_sup, n_rank=n_rank,
                rowsel=rowsel, win_rows=win_rows.astype(i32),
                src=src, sup_exp=sup_exp.astype(i32), sup_valid=sup_valid,
                tile_rank=tile_rank, tok_sel=tok_sel)


def _dispatch_kernel(nrows_ref, *refs, offsets):
    u_refs = refs[:len(offsets)]
    sel_ref, o_ref = refs[len(offsets):]
    w = pl.program_id(0)
    u_ref = u_refs[0]
    if len(offsets) > 1:
        tokens = u_refs[0][...]
        for s in range(1, len(offsets)):
            tokens = jnp.where(w >= offsets[s], u_refs[s][...], tokens)
    r1 = sel_ref[0:1, :]
    r2 = sel_ref[1:2, :]
    for cidx in range(o_ref.shape[0] // MOE_GCHUNK):
        base = cidx * MOE_GCHUNK
        rows = pl.ds(base, MOE_GCHUNK)

        @pl.when(base < nrows_ref[w])
        def _():
            rid = lax.broadcasted_iota(jnp.int32, (MOE_GCHUNK, r1.shape[1]), 0) + base
            onehot = jnp.where(rid == r1, 1.0, jnp.where(rid == r2, 1.0, 0.0)).astype(BF16)
            src = tokens if len(offsets) > 1 else u_ref[...]
            o_ref[rows, :] = jnp.dot(onehot, src, preferred_element_type=F32).astype(o_ref.dtype)

        @pl.when(base >= nrows_ref[w])
        def _():
            o_ref[rows, :] = jnp.zeros((MOE_GCHUNK, o_ref.shape[1]), o_ref.dtype)


def _dispatch_call(us, plan):
    d = us[0].shape[1]
    cap_rows = plan["cap_tiles"] * MOE_TILE
    n_win = plan["n_win"]
    counts = [u.shape[0] // MOE_WINDOW for u in us]
    offsets = tuple(sum(counts[:s]) for s in range(len(us)))

    def u_spec(s):
        return pl.BlockSpec((MOE_WINDOW, d), lambda w, n: (jnp.clip(w - offsets[s], 0, counts[s] - 1), 0))

    return pl.pallas_call(
        functools.partial(_dispatch_kernel, offsets=offsets),
        grid_spec=pltpu.PrefetchScalarGridSpec(
            num_scalar_prefetch=1,
            grid=(n_win,),
            in_specs=[u_spec(s) for s in range(len(us))] + [
                pl.BlockSpec((None, TOP_K, MOE_WINDOW), lambda w, n: (w, 0, 0))],
            out_specs=pl.BlockSpec((cap_rows, d), lambda w, n: (w, 0))),
        out_shape=jax.ShapeDtypeStruct((n_win * cap_rows, d), BF16),
        compiler_params=_cparams(),
        name="moe_dispatch",
    )(plan["win_rows"], *us, plan["rowsel"])


def _expert_kernel(src_ref, exp_ref, valid_ref, *refs, n_chunk):
    tiles = refs[:MOE_SUPER]
    wg_ref, wu_ref, wd_ref, o_ref, u_ref, acc_ref = refs[MOE_SUPER:]
    g = pl.program_id(0)
    j = pl.program_id(1)
    ok = valid_ref[g] > 0

    @pl.when(j == 0)
    def _():
        for k in range(MOE_SUPER):
            u_ref[k * MOE_TILE:(k + 1) * MOE_TILE, :] = tiles[k][...]
        acc_ref[...] = jnp.zeros_like(acc_ref)

    @pl.when(ok)
    def _():
        u = u_ref[...]
        part = None
        for c0 in range(0, wg_ref.shape[2], MOE_SUB):
            a = jnp.dot(u, wg_ref[0, :, c0:c0 + MOE_SUB].astype(BF16), preferred_element_type=F32)
            b = jnp.dot(u, wu_ref[0, :, c0:c0 + MOE_SUB].astype(BF16), preferred_element_type=F32)
            h = (a * jax.nn.sigmoid(a) * b).astype(BF16)
            y = jnp.dot(h, wd_ref[0, c0:c0 + MOE_SUB, :].astype(BF16), preferred_element_type=F32)
            part = y if part is None else part + y
        acc_ref[...] += part

    @pl.when(j == n_chunk - 1)
    def _():
        o_ref[...] = acc_ref[...].astype(o_ref.dtype)


def _expert_call(xw, plan, w_gate, w_up, w_down, layer_base, fc):
    d = xw.shape[1]
    dff = w_gate.shape[2]
    n_chunk = dff // fc
    rows = MOE_SUPER * MOE_TILE

    def tile_spec(k):
        return pl.BlockSpec((MOE_TILE, d), lambda g, j, src, ex, va: (src[g * MOE_SUPER + k], 0))

    def chunk(j, va, g):
        return jnp.where(va[g] > 0, j, n_chunk - 1)

    in_specs = [tile_spec(k) for k in range(MOE_SUPER)] + [
        pl.BlockSpec((1, d, fc), lambda g, j, src, ex, va: (layer_base + ex[g], 0, chunk(j, va, g))),
        pl.BlockSpec((1, d, fc), lambda g, j, src, ex, va: (layer_base + ex[g], 0, chunk(j, va, g))),
        pl.BlockSpec((1, fc, d), lambda g, j, src, ex, va: (layer_base + ex[g], chunk(j, va, g), 0))]
    return pl.pallas_call(
        functools.partial(_expert_kernel, n_chunk=n_chunk),
        grid_spec=pltpu.PrefetchScalarGridSpec(
            num_scalar_prefetch=3,
            grid=(plan["n_sup"], n_chunk),
            in_specs=in_specs,
            out_specs=pl.BlockSpec((rows, d), lambda g, j, src, ex, va: (g, 0)),
            scratch_shapes=[pltpu.VMEM((rows, d), BF16), pltpu.VMEM((rows, d), F32)]),
        out_shape=jax.ShapeDtypeStruct((plan["n_sup"] * rows, d), BF16),
        compiler_params=_cparams(),
        name="moe_experts",
    )(plan["src"], plan["sup_exp"], plan["sup_valid"], *([xw] * MOE_SUPER), w_gate, w_up, w_down)


def _combine_kernel(rank_ref, *refs, alpha, n_tiles):
    tile_refs = refs[:n_tiles]
    sel_ref, x_ref, gate_ref, g_ref, b_ref, o_ref, rows_ref, acc_ref = refs[n_tiles:]
    for k in range(n_tiles):
        rows_ref[k * MOE_TILE:(k + 1) * MOE_TILE, :] = tile_refs[k][...]
    sel = sel_ref[...]
    lane = lax.broadcasted_iota(jnp.int32, sel.shape, 1)
    col = lambda k: jnp.sum(jnp.where(lane == k, sel, 0.0), axis=-1, keepdims=True)
    row_a = col(0).astype(jnp.int32)
    row_b = col(1).astype(jnp.int32)
    w_a = col(2)
    w_b = col(3)
    n_rows = rows_ref.shape[0]
    for c0 in range(0, n_rows, MOE_GCHUNK):
        rid = lax.broadcasted_iota(jnp.int32, (sel.shape[0], MOE_GCHUNK), 1) + c0
        q = jnp.where(rid == row_a, w_a, jnp.where(rid == row_b, w_b, 0.0)).astype(BF16)
        part = jnp.dot(q, rows_ref[c0:c0 + MOE_GCHUNK, :], preferred_element_type=F32)
        if c0 == 0:
            acc_ref[...] = part
        else:
            acc_ref[...] += part
    y = alpha * x_ref[...] + gate_ref[0] * acc_ref[...]
    o_ref[...] = _layernorm(y, LN_EPS) * g_ref[...] + b_ref[...]


def _combine_call(ys, plan, x2, win_base, mod3, mod_row, g, b, alpha):
    n_tok, d = x2.shape
    n_tiles = plan["cap_tiles"]

    def tile_spec(k):
        return pl.BlockSpec((MOE_TILE, d), lambda w, rk: (rk[(win_base + w) * n_tiles + k], 0))

    win = lambda width: pl.BlockSpec((MOE_WINDOW, width), lambda w, rk: (w, 0))
    vec = pl.BlockSpec((1, d), lambda w, rk: (0, 0))
    return pl.pallas_call(
        functools.partial(_combine_kernel, alpha=alpha, n_tiles=n_tiles),
        grid_spec=pltpu.PrefetchScalarGridSpec(
            num_scalar_prefetch=1,
            grid=(n_tok // MOE_WINDOW,),
            in_specs=[tile_spec(k) for k in range(n_tiles)] + [
                pl.BlockSpec((MOE_WINDOW, 4), lambda w, rk: (win_base + w, 0)), win(d),
                pl.BlockSpec((1, 1, d), lambda w, rk: (mod_row(w) * 6 + 5, 0, 0)),
                vec, vec],
            out_specs=win(d),
            scratch_shapes=[pltpu.VMEM((n_tiles * MOE_TILE, d), BF16), pltpu.VMEM((MOE_WINDOW, d), F32)]),
        out_shape=jax.ShapeDtypeStruct((n_tok, d), F32),
        compiler_params=_cparams(),
        name="moe_combine_deepnorm",
    )(plan["tile_rank"], *([ys] * n_tiles), plan["tok_sel"], x2, mod3, g.reshape(1, d), b.reshape(1, d))


def _moe_call(parts, mod3, router_w, w_gate, w_up, w_down, layer_base, g, b, alpha, fc):
    routed = [_router_call(x2, mod3, lambda i, r=tok_row: r(i * MOE_WINDOW), router_w) for x2, tok_row in parts]
    comb = jnp.concatenate([r[1] for r in routed], axis=0)
    pos = jnp.concatenate([r[2] for r in routed], axis=0)
    plan = _moe_plan(comb, pos, comb.shape[0])
    xw = _dispatch_call([r[0] for r in routed], plan)
    ys = _expert_call(xw, plan, w_gate, w_up, w_down, layer_base, fc)
    outs = []
    win_base = 0
    for x2, tok_row in parts:
        outs.append(_combine_call(ys, plan, x2, win_base, mod3, lambda w, r=tok_row: r(w * MOE_WINDOW),
                                  g, b, alpha))
        win_base += x2.shape[0] // MOE_WINDOW
    return outs


def _pick_tile(n, pref):
    t = min(n, pref)
    while n % t:
        t //= 2
    return t


def kernel(x, c, ctx, c_ctx, w_mod, b_mod, w_in, q_gain, k_gain, hy_conv_w, hy_conv_b, hy_f_w1, hy_f_b1, hy_f_freq, hy_f_w2, hy_f_b2, hy_f_w3, hy_d, pool_w, pool_scale, w_out, ln1_g, ln1_b, ln2_g, ln2_b, ffn_w_gate, ffn_w_up, ffn_w_down, router_w, moe_w_gate, moe_w_up, moe_w_down):
    bsz, seq, d = x.shape
    clen = ctx.shape[1]
    depth = w_mod.shape[0]
    alpha = (2.0 * depth) ** 0.25
    assert seq % MOE_WINDOW == 0 and (bsz * clen) % MOE_WINDOW == 0 and bsz % 2 == 0
    n_exp = router_w.shape[2]
    moe_wg = moe_w_gate.reshape((-1,) + moe_w_gate.shape[2:])
    moe_wu = moe_w_up.reshape((-1,) + moe_w_up.shape[2:])
    moe_wd = moe_w_down.reshape((-1,) + moe_w_down.shape[2:])
    ffn_wg, ffn_wu, ffn_wd = (w.astype(BF16) for w in (ffn_w_gate, ffn_w_up, ffn_w_down))

    n_rows = -(-(bsz + 1) // 8) * 8
    cc = jnp.zeros((n_rows, d), F32).at[:bsz].set(c).at[bsz].set(c_ctx)
    mod = _mod_call(cc, w_mod, b_mod)

    tm_lat = _pick_tile(seq, 512)
    tm_ctx = _pick_tile(clen, 512)
    tm_ffn = _pick_tile(seq, 1024)
    tm_ffn_ctx = _pick_tile(bsz * clen, 1024)
    lat_row = lambda tm: (lambda i: i // (seq // tm))
    ctx_row = lambda i: bsz
    rope_lat = tuple(jnp.asarray(t) for t in _rope_tables(seq, True))
    rope_ctx = tuple(jnp.asarray(t) for t in _rope_tables(clen, False))
    head_avg = np.kron(np.eye(LANES // HEAD_DIM), np.full((HEAD_DIM, HEAD_DIM), 1.0 / HEAD_DIM))
    head_avg = jnp.asarray(head_avg, BF16)

    xl = x.reshape(bsz * seq, d)
    xc = ctx.reshape(bsz * clen, d)
    for l in range(depth):
        last = l == depth - 1
        mod3 = mod[l].reshape(n_rows * 6, 1, d)
        w_in_bf = w_in[l].astype(BF16)
        w_out_bf = w_out[l].astype(BF16)
        gain = jnp.concatenate([jnp.tile(q_gain[l] * (ATTN_SCALE * LOG2E), N_Q_HEADS),
                                jnp.tile(k_gain[l], N_KV_HEADS)]).reshape(1, QK_W)
        shift = (1.02 * HEAD_DIM * ATTN_SCALE * LOG2E) * jnp.max(jnp.abs(q_gain[l])) * jnp.max(jnp.abs(k_gain[l]))
        lane64 = (jnp.arange(LANES) == HEAD_DIM).astype(F32)
        ext = jnp.stack([lane64, -shift * lane64, lane64])
        fast = (shift <= ATTN_SAFE_SHIFT).astype(jnp.int32).reshape(1)
        filt_w = (hy_f_w1[l], hy_f_b1[l], hy_f_freq[l], hy_f_w2[l], hy_f_b2[l], hy_f_w3[l])
        j = l // 2
        routed = l % 2 == 1
        ffn_w = (ffn_wg, ffn_wu, ffn_wd, j)

        q_c, k_c, v_c, hp_c = _proj_call(xc, mod3, ctx_row, clen, w_in_bf, gain, head_avg, rope_ctx, ext, tm_ctx)
        k_c3 = k_c.reshape(bsz, clen, -1)
        v_c3 = v_c.reshape(bsz, clen, -1)
        if not last:
            hp_c3 = hp_c.reshape(bsz, clen, -1)
            att_c = _attn_call(fast, q_c.reshape(bsz, clen, -1), (k_c3, v_c3), _pick_tile(clen, 256))
            spec_c = _ctx_spectrum(_filter_call(clen, *filt_w), clen)
            hy_c = _ctx_hyena_call(hp_c3, hy_conv_w[l], hy_conv_b[l], spec_c, hy_d[l])
            pl_c = _pool_call(hp_c3, pool_w[l], pool_scale[l])
            xc_mid = _outproj_call(xc, att_c.reshape(bsz * clen, D_ATTN), hy_c.reshape(bsz * clen, D_HYENA),
                                   pl_c.reshape(bsz * clen, D_POOL), mod3, ctx_row, w_out_bf,
                                   ln1_g[l], ln1_b[l], alpha, tm_ctx)

        q_l, k_l, v_l, hp_l = _proj_call(xl, mod3, lat_row(tm_ffn), seq, w_in_bf, gain, head_avg, rope_lat, ext,
                                         tm_ffn)
        hp_l3 = hp_l.reshape(bsz, seq, -1)
        att = _attn_call(fast, q_l.reshape(bsz, seq, -1),
                         (k_l.reshape(bsz, seq, -1), v_l.reshape(bsz, seq, -1), k_c3, v_c3), _pick_tile(seq, 256))
        spec = _hyena_spectrum(_filter_call(seq, *filt_w), seq)
        hyo = _hyena_latent(hp_l3, hy_conv_w[l], hy_conv_b[l], spec, hy_d[l])
        plo = _pool_call(hp_l3, pool_w[l], pool_scale[l])
        xl = _outproj_call(xl, att.reshape(bsz * seq, D_ATTN), hyo.reshape(bsz * seq, D_HYENA),
                           plo.reshape(bsz * seq, D_POOL), mod3, lat_row(tm_lat), w_out_bf,
                           ln1_g[l], ln1_b[l], alpha, tm_lat)
        if routed:
            parts = [(xl, lambda t: t // seq)] + ([] if last else [(xc_mid, lambda t: bsz)])
            outs = _moe_call(parts, mod3, router_w[j], moe_wg, moe_wu, moe_wd, j * n_exp,
                             ln2_g[l], ln2_b[l], alpha, MOE_FC)
            xl = outs[0]
            if not last:
                xc = outs[1]
        else:
            xl = _ffn_call(xl, mod3, lat_row(tm_ffn), *ffn_w, ln2_g[l], ln2_b[l], alpha, tm_ffn, FFN_FC)
            if not last:
                xc = _ffn_call(xc_mid, mod3, ctx_row, *ffn_w, ln2_g[l], ln2_b[l], alpha, tm_ffn_ctx, FFN_FC)
    return xl.reshape(bsz, seq, d)
```

```python
import functools
import math

import numpy as np
import jax
import jax.numpy as jnp
from jax import lax
from jax.experimental import pallas as pl
from jax.experimental.pallas import tpu as pltpu

F32 = jnp.float32
BF16 = jnp.bfloat16
HIGHEST = lax.Precision.HIGHEST

GRID_W = 64
HEAD_DIM = 64
N_Q_HEADS = 8
N_KV_HEADS = 2
GQA_GROUP = N_Q_HEADS // N_KV_HEADS
D_ATTN = N_Q_HEADS * HEAD_DIM
KV_W = N_KV_HEADS * HEAD_DIM
QK_W = D_ATTN + KV_W
ATTN_SCALE = HEAD_DIM ** -0.5
ROPE_THETA = 10000.0
ROPE_AXIS_DIM = HEAD_DIM // 2
QK_EPS = 1e-6
D_HYENA = 256
HYENA_ORDER = 2
HYENA_BANDS = 16
HYENA_EMB = 1 + 2 * HYENA_BANDS
HYENA_DECAY_TARGET = 1e-2
HYENA_FAST_DECAY_PCT = 0.3
HYENA_SLOW_DECAY_PCT = 1.5
D_POOL = 256
POOL_WINDOWS = (2, 4, 8, 16)
POOL_GROUP = D_POOL // len(POOL_WINDOWS)
POOL_PAD = 8
TOP_K = 2
LN_EPS = 1e-5
ADALN_EPS = 1e-6
LOG2E = math.log2(math.e)

LANES = 128
FFT_N2 = 64
FFT_UNROLL = 4
VMEM_LIMIT = 56 * 1024 * 1024
FFN_FC = 2816
FFN_SUB = 256


def _cparams():
    return pltpu.CompilerParams(vmem_limit_bytes=VMEM_LIMIT)


def _layernorm(x, eps):
    mu = jnp.mean(x, axis=-1, keepdims=True)
    xc = x - mu
    var = jnp.mean(xc * xc, axis=-1, keepdims=True)
    return xc * lax.rsqrt(var + eps)


def _mod_kernel(c_ref, w_ref, b_ref, o_ref):
    c = c_ref[...]
    s = c * jax.nn.sigmoid(c)
    o_ref[0] = jnp.dot(s, w_ref[0], preferred_element_type=F32, precision=HIGHEST) + b_ref[0]


def _mod_call(cc, w_mod, b_mod):
    depth, d, d6 = w_mod.shape
    r = cc.shape[0]
    tn = 1536
    return pl.pallas_call(
        _mod_kernel,
        grid=(depth, d6 // tn),
        in_specs=[pl.BlockSpec((r, d), lambda l, j: (0, 0)),
                  pl.BlockSpec((1, d, tn), lambda l, j: (l, 0, j)),
                  pl.BlockSpec((1, 1, tn), lambda l, j: (l, 0, j))],
        out_specs=pl.BlockSpec((1, r, tn), lambda l, j: (l, 0, j)),
        out_shape=jax.ShapeDtypeStruct((depth, r, d6), F32),
        compiler_params=_cparams(),
        name="adaln_mod",
    )(cc, w_mod, b_mod.reshape(depth, 1, d6))


def _rope_tables(seq_len, rope):
    lane = np.arange(LANES)
    d = lane % HEAD_DIM
    if not rope:
        one = np.ones((seq_len, LANES), np.float32)
        zero = np.zeros((seq_len, LANES), np.float32)
        return one, zero, zero
    t = np.arange(seq_len)
    row = (t // GRID_W).astype(np.float64)
    col = (t % GRID_W).astype(np.float64)
    half = ROPE_AXIS_DIM // 2
    inv = ROPE_THETA ** (-np.arange(0, ROPE_AXIS_DIM, 2, dtype=np.float64) / ROPE_AXIS_DIM)
    pos = np.where((d // ROPE_AXIS_DIM)[None, :] == 0, row[:, None], col[:, None])
    ang = pos * inv[d % half][None, :]
    is_b = ((d % ROPE_AXIS_DIM) >= half)[None, :]
    cos = np.cos(ang)
    sin = np.sin(ang)
    s_up = np.where(is_b, sin, 0.0)
    s_dn = np.where(is_b, 0.0, -sin)
    return cos.astype(np.float32), s_up.astype(np.float32), s_dn.astype(np.float32)


def _store_padded_heads(blk, ext, out_ref, col0):
    lo = lax.broadcasted_iota(jnp.int32, blk.shape, 1) < HEAD_DIM
    out_ref[:, col0:col0 + LANES] = jnp.where(lo, blk, ext).astype(out_ref.dtype)
    out_ref[:, col0 + LANES:col0 + 2 * LANES] = jnp.where(lo, pltpu.roll(blk, HEAD_DIM, axis=1),
                                                          ext).astype(out_ref.dtype)


def _proj_kernel(x_ref, sh_ref, sc_ref, w_ref, gain_ref, bd_ref, cos_ref, sup_ref, sdn_ref, ext_ref,
                 q_ref, k_ref, v_ref, hp_ref):
    u = _layernorm(x_ref[...], ADALN_EPS) * (1.0 + sc_ref[0]) + sh_ref[0]
    p = jnp.dot(u.astype(BF16), w_ref[...], preferred_element_type=F32)
    cos = cos_ref[...]
    sup = sup_ref[...]
    sdn = sdn_ref[...]
    n_qblk = D_ATTN // LANES
    for j in range(QK_W // LANES):
        blk = p[:, j * LANES:(j + 1) * LANES]
        ms = jnp.dot((blk * blk).astype(BF16), bd_ref[...], preferred_element_type=F32)
        blk = blk * lax.rsqrt(ms + QK_EPS) * gain_ref[:, j * LANES:(j + 1) * LANES]
        up = pltpu.roll(blk, ROPE_AXIS_DIM // 2, axis=1)
        dn = pltpu.roll(blk, LANES - ROPE_AXIS_DIM // 2, axis=1)
        rot = blk * cos + up * sup + dn * sdn
        if j < n_qblk:
            _store_padded_heads(rot, ext_ref[0:1, :], q_ref, 2 * j * LANES)
        else:
            _store_padded_heads(rot, ext_ref[1:2, :], k_ref, 2 * (j - n_qblk) * LANES)
    _store_padded_heads(p[:, QK_W:QK_W + KV_W], ext_ref[2:3, :], v_ref, 0)
    hp_ref[...] = p[:, QK_W + KV_W:]


def _proj_call(x2, mod3, mod_row, seq_len, w_in_bf, gain, bd, tables, ext, tm):
    r, d = x2.shape
    d_in = w_in_bf.shape[1]
    d_hp = d_in - QK_W - KV_W
    nseq = seq_len // tm
    cos, sup, sdn = tables
    tab_spec = pl.BlockSpec((tm, LANES), lambda i: (i % nseq, 0))
    return pl.pallas_call(
        _proj_kernel,
        grid=(r // tm,),
        in_specs=[pl.BlockSpec((tm, d), lambda i: (i, 0)),
                  pl.BlockSpec((1, 1, d), lambda i: (mod_row(i) * 6 + 0, 0, 0)),
                  pl.BlockSpec((1, 1, d), lambda i: (mod_row(i) * 6 + 1, 0, 0)),
                  pl.BlockSpec((d, d_in), lambda i: (0, 0)),
                  pl.BlockSpec((1, QK_W), lambda i: (0, 0)),
                  pl.BlockSpec((LANES, LANES), lambda i: (0, 0)),
                  tab_spec, tab_spec, tab_spec,
                  pl.BlockSpec((3, LANES), lambda i: (0, 0))],
        out_specs=[pl.BlockSpec((tm, N_Q_HEADS * LANES), lambda i: (i, 0)),
                   pl.BlockSpec((tm, N_KV_HEADS * LANES), lambda i: (i, 0)),
                   pl.BlockSpec((tm, N_KV_HEADS * LANES), lambda i: (i, 0)),
                   pl.BlockSpec((tm, d_hp), lambda i: (i, 0))],
        out_shape=[jax.ShapeDtypeStruct((r, N_Q_HEADS * LANES), BF16),
                   jax.ShapeDtypeStruct((r, N_KV_HEADS * LANES), BF16),
                   jax.ShapeDtypeStruct((r, N_KV_HEADS * LANES), BF16),
                   jax.ShapeDtypeStruct((r, d_hp), F32)],
        compiler_params=_cparams(),
        name="ln_mod_in_proj",
    )(x2, mod3, mod3, w_in_bf, gain, bd, cos, sup, sdn, ext)


ATTN_KEY_CHUNK = 1024
ATTN_SAFE_SHIFT = 60.0


def _attn_kernel(fast_ref, q_ref, *refs):
    o_ref = refs[-1]
    sources = [(refs[i], refs[i + 1]) for i in range(0, len(refs) - 1, 2)]
    tq = q_ref.shape[0]
    chunks = [(k_ref, v_ref, c0, min(ATTN_KEY_CHUNK, k_ref.shape[0] - c0))
              for k_ref, v_ref in sources for c0 in range(0, k_ref.shape[0], ATTN_KEY_CHUNK)]

    def run(running_max):
        for h in range(N_KV_HEADS):
            qs = jnp.concatenate([q_ref[:, (h * GQA_GROUP + g) * LANES:(h * GQA_GROUP + g + 1) * LANES]
                                  for g in range(GQA_GROUP)], axis=0)
            acc = jnp.zeros((GQA_GROUP * tq, LANES), F32)
            m = jnp.full((GQA_GROUP * tq, 1), -jnp.inf, F32)
            for k_ref, v_ref, c0, kw in chunks:
                kc = k_ref[c0:c0 + kw, h * LANES:(h + 1) * LANES]
                vc = v_ref[c0:c0 + kw, h * LANES:(h + 1) * LANES]
                s = lax.dot_general(qs, kc, (((1,), (1,)), ((), ())), preferred_element_type=F32)
                if running_max:
                    m_new = jnp.maximum(m, jnp.max(s, axis=-1, keepdims=True))
                    acc = acc * jnp.exp2(m - m_new)
                    s = s - m_new
                    m = m_new
                acc = acc + jnp.dot(jnp.exp2(s).astype(BF16), vc, preferred_element_type=F32)
            o = acc[:, :HEAD_DIM] / acc[:, HEAD_DIM:HEAD_DIM + 1]
            for g in range(GQA_GROUP):
                c0 = (h * GQA_GROUP + g) * HEAD_DIM
                o_ref[:, c0:c0 + HEAD_DIM] = o[g * tq:(g + 1) * tq].astype(o_ref.dtype)

    @pl.when(fast_ref[0] > 0)
    def _():
        run(False)

    @pl.when(fast_ref[0] == 0)
    def _():
        run(True)


def _attn_call(fast, q, kv, tq):
    b, lq, qw = q.shape
    kv_specs = [pl.BlockSpec((None,) + a.shape[1:], lambda i, j, f: (i, 0, 0)) for a in kv]
    return pl.pallas_call(
        _attn_kernel,
        grid_spec=pltpu.PrefetchScalarGridSpec(
            num_scalar_prefetch=1,
            grid=(b, lq // tq),
            in_specs=[pl.BlockSpec((None, tq, qw), lambda i, j, f: (i, j, 0))] + kv_specs,
            out_specs=pl.BlockSpec((None, tq, D_ATTN), lambda i, j, f: (i, j, 0))),
        out_shape=jax.ShapeDtypeStruct((b, lq, D_ATTN), BF16),
        compiler_params=_cparams(),
        name="attention",
    )(fast, q, *kv)


def _filter_consts(seq_len):
    t = np.linspace(0.0, 1.0, seq_len, dtype=np.float32).astype(np.float64)[:, None]
    omega = 2.0 * math.pi * np.arange(seq_len, dtype=np.float64)[:, None] / seq_len
    bands = np.linspace(1e-4, HYENA_BANDS - 1, HYENA_BANDS, dtype=np.float32).astype(np.float64)[None, :]
    feats = np.concatenate([t, np.cos(omega * bands), -np.sin(omega * bands)], axis=-1)
    feats = np.pad(feats, ((0, 0), (0, LANES - HYENA_EMB)))
    max_decay = math.log(HYENA_DECAY_TARGET) / HYENA_FAST_DECAY_PCT
    min_decay = math.log(HYENA_DECAY_TARGET) / HYENA_SLOW_DECAY_PCT
    deltas = np.linspace(min_decay, max_decay, D_HYENA, dtype=np.float32).astype(np.float64)
    decay = np.exp(-t * np.abs(deltas)[None, :])
    rev = (seq_len - np.arange(seq_len)) % seq_len
    f32 = lambda a: jnp.asarray(a.astype(np.float32))
    return f32(feats), f32(feats[rev]), f32(decay), f32(decay[rev])


def _filter_kernel(feat_ref, featr_ref, w1_ref, b1_ref, fr_ref, w2_ref, b2_ref, w3_ref, dec_ref, decr_ref, o_ref,
                   hid_ref):
    seq_len = feat_ref.shape[0]
    dot = functools.partial(jnp.dot, preferred_element_type=F32, precision=HIGHEST)

    @pl.when(pl.program_id(0) == 0)
    def _():
        fr = fr_ref[...]
        for i, feat in enumerate((feat_ref, featr_ref)):
            h = jnp.sin(fr * (dot(feat[...], w1_ref[...]) + b1_ref[...]))
            hid_ref[i] = jnp.sin(fr * (dot(h, w2_ref[...]) + b2_ref[...]))

    fwd = dot(hid_ref[0], w3_ref[:, :D_HYENA]) * dec_ref[...]
    bwd = dot(hid_ref[1], w3_ref[:, D_HYENA:]) * decr_ref[...]
    tot = jnp.sum(jnp.abs(fwd), axis=0, keepdims=True) + jnp.sum(jnp.abs(bwd), axis=0, keepdims=True)
    inv = 1.0 / tot
    o_ref[:seq_len, :] = fwd * inv
    row = lax.broadcasted_iota(jnp.int32, bwd.shape, 0)
    o_ref[seq_len:, :] = jnp.where(row == 0, 0.0, bwd * inv)


def _filter_call(seq_len, w1, b1, freq, w2, b2, w3):
    feats, feats_rev, decay, decay_rev = _filter_consts(seq_len)
    hid = w2.shape[0]
    w1p = jnp.pad(w1, ((0, LANES - HYENA_EMB), (0, 0)))
    full = lambda shape: pl.BlockSpec(shape, lambda o: (0,) * len(shape))
    return pl.pallas_call(
        _filter_kernel,
        grid=(HYENA_ORDER,),
        in_specs=[full((seq_len, LANES)), full((seq_len, LANES)), full((LANES, hid)), full((1, hid)),
                  full((1, hid)), full((hid, hid)), full((1, hid)),
                  pl.BlockSpec((hid, 2 * D_HYENA), lambda o: (0, o)),
                  full((seq_len, D_HYENA)), full((seq_len, D_HYENA))],
        out_specs=pl.BlockSpec((2 * seq_len, D_HYENA), lambda o: (0, o)),
        out_shape=jax.ShapeDtypeStruct((2 * seq_len, HYENA_ORDER * D_HYENA), F32),
        scratch_shapes=[pltpu.VMEM((2, seq_len, hid), F32)],
        compiler_params=_cparams(),
        name="hyena_filter",
    )(feats, feats_rev, w1p, b1.reshape(1, hid), freq.reshape(1, hid), w2, b2.reshape(1, hid), w3,
      decay, decay_rev)


@functools.lru_cache(maxsize=None)
def _fft_consts(seq_len):
    n = 2 * seq_len
    n2 = FFT_N2
    n1 = n // n2
    h1 = n1 // 2
    k1 = np.arange(n1)
    w1 = np.exp(-2j * np.pi * np.outer(k1, np.arange(n1)) / n1)
    f1 = np.zeros((n1, 2, 2 * h1))
    f1[:, 0, :h1], f1[:, 0, h1:] = w1[:, :h1].real, -w1[:, :h1].imag
    f1[:, 1, :h1], f1[:, 1, h1:] = w1[:, :h1].imag, w1[:, :h1].real
    f1 = f1.reshape(2 * n1, 2 * h1)
    f1r = np.stack([w1.real, w1.imag], axis=1).reshape(2 * n1, n1)
    a2 = np.arange(n2)
    tw = np.exp(-2j * np.pi * np.outer(k1, a2) / n)
    w2 = np.exp(-2j * np.pi * np.outer(a2, a2) / n2)
    m = w2[None, :, :] * tw[:, None, :]
    ff = np.concatenate([np.concatenate([m.real, -m.imag], axis=2),
                         np.concatenate([m.imag, m.real], axis=2)], axis=1)
    g = np.conj(w2.T)[None, :, :] * np.conj(tw)[:, :, None]
    gi = np.concatenate([np.concatenate([g.real, -g.imag], axis=2),
                         np.concatenate([g.imag, g.real], axis=2)], axis=1)
    v = np.conj(w1[:, :h1]).T
    f1i = np.zeros((2, h1, n1, 2))
    f1i[0, :, :, 0], f1i[0, :, :, 1] = v.real, -v.imag
    f1i[1, :, :, 0], f1i[1, :, :, 1] = v.imag, v.real
    f1i = f1i.reshape(2 * h1, 2 * n1)
    il = np.arange(2 * n2).reshape(2, n2).T.reshape(-1)
    ff_il = ff[:, :, il]
    gi_il = gi[:, il, :]
    f32 = lambda a: np.asarray(a, np.float32)
    return dict(n=n, n1=n1, h1=h1, f1=f32(f1), f1r=f32(f1r), ff=f32(ff), ff_il=f32(ff_il), gi_il=f32(gi_il),
                f1i=f32(f1i))


def _fftr_kernel(x_ref, f_ref, o_ref):
    n1 = f_ref.shape[1]

    def body(n2, carry):
        x = x_ref[pl.ds(n2, n1, stride=FFT_N2), :]
        o_ref[pl.ds(n2, 2 * n1, stride=FFT_N2), :] = jnp.dot(f_ref[...], x, preferred_element_type=F32,
                                                            precision=HIGHEST)
        return carry

    lax.fori_loop(0, FFT_N2, body, 0, unroll=FFT_UNROLL)


def _fftr_call(buf, f1r):
    n, cols = buf.shape
    rows = f1r.shape[0] * FFT_N2
    return pl.pallas_call(
        _fftr_kernel,
        grid=(cols // LANES,),
        in_specs=[pl.BlockSpec((n, LANES), lambda j: (0, j)),
                  pl.BlockSpec(f1r.shape, lambda j: (0, 0))],
        out_specs=pl.BlockSpec((None, rows, LANES), lambda j: (j, 0, 0)),
        out_shape=jax.ShapeDtypeStruct((cols // LANES, rows, LANES), F32),
        compiler_params=_cparams(),
        name="fft_filter_outer_dft",
    )(buf, f1r)


def _spec_kernel(a_ref, ff_ref, o_ref, *, scale):
    for kk in range(a_ref.shape[1]):
        for cb in range(a_ref.shape[0]):
            o_ref[kk, :, cb * LANES:(cb + 1) * LANES] = jnp.dot(
                ff_ref[kk], a_ref[cb, kk], preferred_element_type=F32, precision=HIGHEST) * scale


def _spec_call(a4, ff, scale, kt):
    nblk, n1, r, _ = a4.shape
    return pl.pallas_call(
        functools.partial(_spec_kernel, scale=scale),
        grid=(n1 // kt,),
        in_specs=[pl.BlockSpec((nblk, kt, r, LANES), lambda i: (0, i, 0, 0)),
                  pl.BlockSpec((kt, r, r), lambda i: (i, 0, 0))],
        out_specs=pl.BlockSpec((kt, r, nblk * LANES), lambda i: (i, 0, 0)),
        out_shape=jax.ShapeDtypeStruct((n1, r, nblk * LANES), F32),
        compiler_params=_cparams(),
        name="fft_filter_spectrum",
    )(a4, ff)


def _short_conv(z, w, b):
    n = z.shape[0]
    row = lax.broadcasted_iota(jnp.int32, z.shape, 0)
    prev = jnp.where(row == 0, 0.0, pltpu.roll(z, 1, axis=0))
    nxt = jnp.where(row == n - 1, 0.0, pltpu.roll(z, n - 1, axis=0))
    return prev * w[0:1, :] + z * w[1:2, :] + nxt * w[2:3, :] + b


def _ffta_kernel(*refs, conv):
    if conv:
        z_ref, w_ref, cb_ref, f_ref, o_ref, x_ref = refs
        for m in range(2):
            x_ref[m] = _short_conv(z_ref[m], w_ref[...], cb_ref[...])
    else:
        z_ref, f_ref, o_ref = refs
        x_ref = z_ref
    n1 = f_ref.shape[0] // 2
    h1 = f_ref.shape[1] // 2

    def body(n2, carry):
        rows = pl.ds(n2, h1, stride=FFT_N2)
        x = jnp.concatenate([x_ref[0, rows, :], x_ref[1, rows, :]], axis=0).astype(BF16)
        a = jnp.dot(f_ref[...], x, preferred_element_type=F32).astype(BF16)
        o_ref[pl.ds(n2, n1, stride=FFT_N2), :] = pltpu.bitcast(a, jnp.uint32)
        return carry

    lax.fori_loop(0, FFT_N2, body, 0, unroll=FFT_UNROLL)


def _ffta_call(z4, col0, f1_bf, conv_w=None, conv_b=None):
    groups, _, l, _ = z4.shape
    halves = D_HYENA // LANES
    n1 = f1_bf.shape[0] // 2
    conv = conv_w is not None
    in_specs = [pl.BlockSpec((None, 2, l, LANES), lambda i, j: (i, 0, 0, col0 + j))]
    args = [z4]
    scratch = []
    if conv:
        in_specs += [pl.BlockSpec((3, LANES), lambda i, j: (0, col0 + j)),
                     pl.BlockSpec((1, LANES), lambda i, j: (0, col0 + j))]
        args += [conv_w, conv_b]
        scratch = [pltpu.VMEM((2, l, LANES), F32)]
    in_specs.append(pl.BlockSpec(f1_bf.shape, lambda i, j: (0, 0)))
    args.append(f1_bf)
    return pl.pallas_call(
        functools.partial(_ffta_kernel, conv=conv),
        grid=(groups, halves),
        in_specs=in_specs,
        out_specs=pl.BlockSpec((None, None, n1 * FFT_N2, LANES), lambda i, j: (i, j, 0, 0)),
        out_shape=jax.ShapeDtypeStruct((groups, halves, n1 * FFT_N2, LANES), jnp.uint32),
        scratch_shapes=scratch,
        compiler_params=_cparams(),
        name="fft_outer_dft",
    )(*args)


def _fftb_kernel(a_ref, ff_ref, gi_ref, h_ref, o_ref):
    groups, halves, kt = a_ref.shape[:3]
    half = FFT_N2

    def body(kk, carry):
        ff = ff_ref[kk]
        gi = gi_ref[kk]
        hr = h_ref[kk, :half, :]
        hi = h_ref[kk, half:, :]
        for g in range(groups):
            a = jnp.concatenate([pltpu.bitcast(a_ref[g, hh, kk], BF16) for hh in range(halves)], axis=1)
            xh = jnp.dot(ff, a, preferred_element_type=F32)
            xr, xi = xh[:half], xh[half:]
            y = jnp.concatenate([xr * hr - xi * hi, xr * hi + xi * hr], axis=0).astype(BF16)
            b = jnp.dot(gi, y, preferred_element_type=F32).astype(BF16)
            for hh in range(halves):
                o_ref[g, hh, kk] = pltpu.bitcast(b[:, hh * LANES:(hh + 1) * LANES], jnp.uint32)
        return carry

    lax.fori_loop(0, kt, body, 0)


def _fftb_call(a5, ff_bf, gi_bf, spec, order, kt):
    g, halves, n1, n2, _ = a5.shape
    r = 2 * n2
    return pl.pallas_call(
        _fftb_kernel,
        grid=(n1 // kt,),
        in_specs=[pl.BlockSpec((g, halves, kt, n2, LANES), lambda i: (0, 0, i, 0, 0)),
                  pl.BlockSpec((kt, r, r), lambda i: (i, 0, 0)),
                  pl.BlockSpec((kt, r, r), lambda i: (i, 0, 0)),
                  pl.BlockSpec((kt, r, halves * LANES), lambda i: (i, 0, order))],
        out_specs=pl.BlockSpec((g, halves, kt, n2, LANES), lambda i: (0, 0, i, 0, 0)),
        out_shape=jax.ShapeDtypeStruct(a5.shape, jnp.uint32),
        compiler_params=_cparams(),
        name="fft_inner_conv",
    )(a5, ff_bf, gi_bf, spec)


def _fftc_kernel(*refs, z_conv):
    if z_conv:
        b_ref, f_ref, z_ref, zw_ref, zb_ref, g_ref, gw_ref, gb_ref, d_ref, o_ref = refs
    else:
        b_ref, f_ref, z_ref, g_ref, gw_ref, gb_ref, d_ref, o_ref = refs
    n1 = f_ref.shape[1] // 2
    h1 = f_ref.shape[0] // 2
    d = d_ref[...]

    def body(n2, carry):
        b = pltpu.bitcast(b_ref[pl.ds(n2, n1, stride=FFT_N2), :], BF16)
        y = jnp.dot(f_ref[...], b, preferred_element_type=F32)
        rows = pl.ds(n2, h1, stride=FFT_N2)
        for m in range(2):
            o_ref[m, rows, :] = y[m * h1:(m + 1) * h1]
        return carry

    lax.fori_loop(0, FFT_N2, body, 0, unroll=FFT_UNROLL)
    for m in range(2):
        z = _short_conv(z_ref[m], zw_ref[...], zb_ref[...]) if z_conv else z_ref[m]
        gate = _short_conv(g_ref[m], gw_ref[...], gb_ref[...])
        o_ref[m] = gate * (o_ref[m] + d * z)


def _fftc_call(b4, f1i_bf, z4, z_col0, z_conv, hp4, g_col0, conv_w, conv_b, d2, order):
    groups, halves, rows, _ = b4.shape
    l = z4.shape[2]
    pair = lambda col0: pl.BlockSpec((None, 2, l, LANES), lambda i, j: (i, 0, 0, col0 + j))
    taps = lambda col0: [pl.BlockSpec((3, LANES), lambda i, j: (0, col0 + j)),
                         pl.BlockSpec((1, LANES), lambda i, j: (0, col0 + j))]
    in_specs = [pl.BlockSpec((None, None, rows, LANES), lambda i, j: (i, j, 0, 0)),
                pl.BlockSpec(f1i_bf.shape, lambda i, j: (0, 0)),
                pair(z_col0)]
    args = [b4, f1i_bf, z4]
    if z_conv:
        in_specs += taps(z_col0)
        args += [conv_w, conv_b]
    in_specs += [pair(g_col0)] + taps(g_col0) + [pl.BlockSpec((None, 1, LANES), lambda i, j: (order, 0, j))]
    args += [hp4, conv_w, conv_b, d2.reshape(d2.shape[0], 1, d2.shape[1])]
    return pl.pallas_call(
        functools.partial(_fftc_kernel, z_conv=z_conv),
        grid=(groups, halves),
        in_specs=in_specs,
        out_specs=pair(0),
        out_shape=jax.ShapeDtypeStruct((groups, 2, l, D_HYENA), F32),
        compiler_params=_cparams(),
        name="fft_outer_inverse_gate",
    )(*args)


def _hyena_spectrum(buf, seq_len):
    cst = _fft_consts(seq_len)
    n, n1 = cst["n"], cst["n1"]
    a = _fftr_call(buf, jnp.asarray(cst["f1r"]))
    a4 = a.reshape(a.shape[0], n1, 2 * FFT_N2, LANES)
    return _spec_call(a4, jnp.asarray(cst["ff"]), 1.0 / n, min(n1, 8))


def _hyena_latent(hp3, conv_w, conv_b, spec, d):
    b, l, _ = hp3.shape
    cst = _fft_consts(l)
    n1, h1 = cst["n1"], cst["h1"]
    groups = b // 2
    halves = D_HYENA // LANES
    kt = min(n1, 8)
    f1 = jnp.asarray(cst["f1"], BF16)
    ff = jnp.asarray(cst["ff_il"], BF16)
    gi = jnp.asarray(cst["gi_il"], BF16)
    f1i = jnp.asarray(cst["f1i"], BF16)
    hp4 = hp3.reshape(groups, 2, l, hp3.shape[2])
    cb = conv_b.reshape(1, -1)
    z = hp4
    for o in range(HYENA_ORDER):
        first = o == 0
        a = _ffta_call(z, 0, f1, conv_w, cb) if first else _ffta_call(z, 0, f1)
        bb = _fftb_call(a.reshape(groups, halves, n1, FFT_N2, LANES), ff, gi, spec, o, kt)
        z = _fftc_call(bb.reshape(groups, halves, n1 * FFT_N2, LANES), f1i, z, 0, first, hp4, (1 + o) * halves,
                       conv_w, cb, d, o)
    return z.reshape(b, l, D_HYENA)


@functools.lru_cache(maxsize=None)
def _dft_consts(seq_len):
    n = 2 * seq_len
    w = np.exp(-2j * np.pi * np.outer(np.arange(n), np.arange(n)) / n)
    fwd_full = np.concatenate([w.real, w.imag], axis=0)
    fwd = fwd_full[:, :seq_len]
    wi = np.conj(w[:, :seq_len]).T
    inv = np.concatenate([wi.real, -wi.imag], axis=1)
    f32 = lambda a: np.asarray(a, np.float32)
    return dict(n=n, fwd_full=f32(fwd_full), fwd=f32(fwd), inv=f32(inv))


def _cspec_kernel(buf_ref, f_ref, o_ref, *, scale):
    o_ref[...] = jnp.dot(f_ref[...], buf_ref[...], preferred_element_type=F32, precision=HIGHEST) * scale


def _ctx_spectrum(buf, seq_len):
    cst = _dft_consts(seq_len)
    n = cst["n"]
    return pl.pallas_call(
        functools.partial(_cspec_kernel, scale=1.0 / n),
        out_shape=jax.ShapeDtypeStruct((2 * n, buf.shape[1]), F32),
        compiler_params=_cparams(),
        name="ctx_filter_spectrum",
    )(buf, jnp.asarray(cst["fwd_full"]))


def _ctx_hyena_kernel(z_ref, w_ref, b_ref, f_ref, fi_ref, h_ref, d_ref, o_ref):
    n = 2 * z_ref.shape[0]
    zc = _short_conv(z_ref[...], w_ref[...], b_ref[...])
    cur = zc[:, :D_HYENA]
    for o in range(HYENA_ORDER):
        gate = zc[:, (1 + o) * D_HYENA:(2 + o) * D_HYENA]
        xh = jnp.dot(f_ref[...], cur.astype(BF16), preferred_element_type=F32)
        xr, xi = xh[:n], xh[n:]
        hr = h_ref[:n, o * D_HYENA:(o + 1) * D_HYENA]
        hi = h_ref[n:, o * D_HYENA:(o + 1) * D_HYENA]
        y = jnp.concatenate([xr * hr - xi * hi, xr * hi + xi * hr], axis=0).astype(BF16)
        conv = jnp.dot(fi_ref[...], y, preferred_element_type=F32)
        cur = gate * (conv + d_ref[o:o + 1, :] * cur)
    o_ref[...] = cur


def _ctx_hyena_call(hp3, conv_w, conv_b, spec, d):
    b, l, _ = hp3.shape
    cst = _dft_consts(l)
    n = cst["n"]
    w3 = 3 * D_HYENA
    full = lambda shape: pl.BlockSpec(shape, lambda i: (0,) * len(shape))
    return pl.pallas_call(
        _ctx_hyena_kernel,
        grid=(b,),
        in_specs=[pl.BlockSpec((None, l, w3), lambda i: (i, 0, 0)),
                  full((3, w3)), full((1, w3)), full((2 * n, l)), full((l, 2 * n)),
                  full((2 * n, HYENA_ORDER * D_HYENA)), full((HYENA_ORDER, D_HYENA))],
        out_specs=pl.BlockSpec((None, l, D_HYENA), lambda i: (i, 0, 0)),
        out_shape=jax.ShapeDtypeStruct((b, l, D_HYENA), F32),
        compiler_params=_cparams(),
        name="ctx_hyena",
    )(hp3, conv_w, conv_b.reshape(1, -1), jnp.asarray(cst["fwd"], BF16), jnp.asarray(cst["inv"], BF16),
      spec, d)


def _pool_kernel(p_ref, w_ref, sc_ref, o_ref):
    x = p_ref[...]
    l, c = x.shape
    half = pl.program_id(1)
    zpad = jnp.zeros((POOL_PAD, c), F32)
    xp = jnp.concatenate([zpad, x, zpad], axis=0)
    tot = l + 2 * POOL_PAD
    sums = []
    f = xp
    for win in POOL_WINDOWS:
        f = f + pltpu.roll(f, tot - win // 2, axis=0)
        sums.append(pltpu.roll(f, win // 2, axis=0)[POOL_PAD:POOL_PAD + l])
    t = lax.broadcasted_iota(jnp.int32, (l, c), 0)
    grp = lax.broadcasted_iota(jnp.int32, (l, c), 1) // POOL_GROUP + half * (LANES // POOL_GROUP)
    total = sums[-1]
    cnt = None
    for gi in range(len(POOL_WINDOWS) - 1, -1, -1):
        win = POOL_WINDOWS[gi]
        cw = (jnp.minimum(t + (win - win // 2), l) - jnp.maximum(t - win // 2, 0)).astype(F32)
        if cnt is None:
            cnt = cw
        else:
            sel = grp == gi
            total = jnp.where(sel, sums[gi], total)
            cnt = jnp.where(sel, cw, cnt)
    dlt = total / cnt - x
    y = jnp.dot(dlt.astype(BF16), w_ref[0], preferred_element_type=F32)
    o_ref[...] = y * sc_ref[...]


def _pool_call(hp3, pool_w, pool_scale):
    b, l, w = hp3.shape
    nh = D_POOL // LANES
    first = (w - D_POOL) // LANES
    per = LANES // POOL_GROUP
    wbd = jnp.zeros((nh, LANES, LANES), F32)
    for g in range(len(POOL_WINDOWS)):
        r0 = (g % per) * POOL_GROUP
        wbd = wbd.at[g // per, r0:r0 + POOL_GROUP, r0:r0 + POOL_GROUP].set(pool_w[g])
    return pl.pallas_call(
        _pool_kernel,
        grid=(b, nh),
        in_specs=[pl.BlockSpec((None, l, LANES), lambda i, j: (i, 0, first + j)),
                  pl.BlockSpec((1, LANES, LANES), lambda i, j: (j, 0, 0)),
                  pl.BlockSpec((1, LANES), lambda i, j: (0, j))],
        out_specs=pl.BlockSpec((None, l, LANES), lambda i, j: (i, 0, j)),
        out_shape=jax.ShapeDtypeStruct((b, l, D_POOL), F32),
        compiler_params=_cparams(),
        name="pool_mixer",
    )(hp3, wbd.astype(BF16), pool_scale.reshape(1, D_POOL))


def _outproj_kernel(x_ref, att_ref, hy_ref, pl_ref, gate_ref, w_ref, g_ref, b_ref, o_ref, *, alpha):
    mix = jnp.dot(att_ref[...], w_ref[0:D_ATTN, :], preferred_element_type=F32)
    mix += jnp.dot(hy_ref[...].astype(BF16), w_ref[D_ATTN:D_ATTN + D_HYENA, :], preferred_element_type=F32)
    mix += jnp.dot(pl_ref[...].astype(BF16), w_ref[D_ATTN + D_HYENA:, :], preferred_element_type=F32)
    y = alpha * x_ref[...] + gate_ref[0] * mix
    o_ref[...] = _layernorm(y, LN_EPS) * g_ref[...] + b_ref[...]


def _outproj_call(x2, att2, hy2, pl2, mod3, mod_row, w_out_bf, g, b, alpha, tm):
    r, d = x2.shape
    row = lambda w: pl.BlockSpec((tm, w), lambda i: (i, 0))
    vec = pl.BlockSpec((1, d), lambda i: (0, 0))
    return pl.pallas_call(
        functools.partial(_outproj_kernel, alpha=alpha),
        grid=(r // tm,),
        in_specs=[row(d), row(D_ATTN), row(D_HYENA), row(D_POOL),
                  pl.BlockSpec((1, 1, d), lambda i: (mod_row(i) * 6 + 2, 0, 0)),
                  pl.BlockSpec(w_out_bf.shape, lambda i: (0, 0)),
                  vec, vec],
        out_specs=row(d),
        out_shape=jax.ShapeDtypeStruct((r, d), F32),
        compiler_params=_cparams(),
        name="out_proj_deepnorm",
    )(x2, att2, hy2, pl2, mod3, w_out_bf, g.reshape(1, d), b.reshape(1, d))


def _ffn_kernel(x_ref, sh_ref, sc_ref, gate_ref, wg_ref, wu_ref, wd_ref, g_ref, b_ref, o_ref, u_ref, acc_ref, *,
                alpha, n_chunk):
    j = pl.program_id(1)

    @pl.when(j == 0)
    def _():
        u_ref[...] = (_layernorm(x_ref[...], ADALN_EPS) * (1.0 + sc_ref[0]) + sh_ref[0]).astype(BF16)
        acc_ref[...] = jnp.zeros_like(acc_ref)

    u = u_ref[...]
    part = None
    for c0 in range(0, wg_ref.shape[2], FFN_SUB):
        c1 = min(c0 + FFN_SUB, wg_ref.shape[2])
        a = jnp.dot(u, wg_ref[0, :, c0:c1], preferred_element_type=F32)
        bb = jnp.dot(u, wu_ref[0, :, c0:c1], preferred_element_type=F32)
        h = (a * jax.nn.sigmoid(a) * bb).astype(BF16)
        y = jnp.dot(h, wd_ref[0, c0:c1, :], preferred_element_type=F32)
        part = y if part is None else part + y
    acc_ref[...] += part

    @pl.when(j == n_chunk - 1)
    def _():
        y = alpha * x_ref[...] + gate_ref[0] * acc_ref[...]
        o_ref[...] = _layernorm(y, LN_EPS) * g_ref[...] + b_ref[...]


def _ffn_call(x2, mod3, mod_row, w_gate, w_up, w_down, w_idx, g, b, alpha, tm, fc):
    r, d = x2.shape
    dff = w_gate.shape[2]
    n_chunk = dff // fc
    wmode = pl.Buffered(1) if n_chunk == 1 else None
    vec = pl.BlockSpec((1, d), lambda i, j: (0, 0))
    mod = lambda which: pl.BlockSpec((1, 1, d), lambda i, j: (mod_row(i) * 6 + which, 0, 0))
    return pl.pallas_call(
        functools.partial(_ffn_kernel, alpha=alpha, n_chunk=n_chunk),
        grid=(r // tm, n_chunk),
        in_specs=[pl.BlockSpec((tm, d), lambda i, j: (i, 0)), mod(3), mod(4), mod(5),
                  pl.BlockSpec((1, d, fc), lambda i, j: (w_idx, 0, j), pipeline_mode=wmode),
                  pl.BlockSpec((1, d, fc), lambda i, j: (w_idx, 0, j), pipeline_mode=wmode),
                  pl.BlockSpec((1, fc, d), lambda i, j: (w_idx, j, 0), pipeline_mode=wmode),
                  vec, vec],
        out_specs=pl.BlockSpec((tm, d), lambda i, j: (i, 0)),
        out_shape=jax.ShapeDtypeStruct((r, d), F32),
        scratch_shapes=[pltpu.VMEM((tm, d), BF16), pltpu.VMEM((tm, d), F32)],
        compiler_params=_cparams(),
        name="channel_mixer",
    )(x2, mod3, mod3, mod3, w_gate, w_up, w_down, g.reshape(1, d), b.reshape(1, d))


MOE_WINDOW = 1024
MOE_TILE = 32
MOE_SUPER = 32
MOE_FC = 512
MOE_GCHUNK = 512
MOE_SUB = 256


def _router_kernel(x_ref, sh_ref, sc_ref, rw_ref, tri_ref, u_ref, comb_ref, pos_ref):
    u = (_layernorm(x_ref[...], ADALN_EPS) * (1.0 + sc_ref[0]) + sh_ref[0]).astype(BF16)
    u_ref[...] = u
    logits = jnp.dot(u, rw_ref[...].astype(BF16), preferred_element_type=F32)
    n_exp = logits.shape[1]
    lane = lax.broadcasted_iota(jnp.int32, logits.shape, 1)
    m1 = jnp.max(logits, axis=-1, keepdims=True)
    i1 = jnp.min(jnp.where(logits == m1, lane, n_exp), axis=-1, keepdims=True)
    rest = jnp.where(lane == i1, -jnp.inf, logits)
    m2 = jnp.max(rest, axis=-1, keepdims=True)
    i2 = jnp.min(jnp.where(rest == m2, lane, n_exp), axis=-1, keepdims=True)
    e2 = jnp.exp(m2 - m1)
    den = 1.0 + e2
    comb = jnp.where(lane == i1, 1.0 / den, 0.0) + jnp.where(lane == i2, e2 / den, 0.0)
    comb_ref[...] = comb
    routed = jnp.where(comb > 0.0, 1.0, 0.0).astype(BF16)
    pos_ref[...] = jnp.dot(tri_ref[...], routed, preferred_element_type=F32) - 1.0


def _router_call(x2, mod3, mod_row, router_w):
    r, d = x2.shape
    n_exp = router_w.shape[1]
    tm = MOE_WINDOW
    tri = jnp.asarray(np.tril(np.ones((tm, tm), np.float32)), BF16)
    mod = lambda which: pl.BlockSpec((1, 1, d), lambda i: (mod_row(i) * 6 + which, 0, 0))
    small = pl.BlockSpec((tm, n_exp), lambda i: (i, 0))
    return pl.pallas_call(
        _router_kernel,
        grid=(r // tm,),
        in_specs=[pl.BlockSpec((tm, d), lambda i: (i, 0)), mod(3), mod(4),
                  pl.BlockSpec(router_w.shape, lambda i: (0, 0)),
                  pl.BlockSpec((tm, tm), lambda i: (0, 0))],
        out_specs=[pl.BlockSpec((tm, d), lambda i: (i, 0)), small, small],
        out_shape=[jax.ShapeDtypeStruct((r, d), BF16), jax.ShapeDtypeStruct((r, n_exp), F32),
                   jax.ShapeDtypeStruct((r, n_exp), F32)],
        compiler_params=_cparams(),
        name="moe_router",
    )(x2, mod3, mod3, router_w, tri)


def _moe_plan(comb, pos_f, n_tok):
    n_exp = comb.shape[1]
    win, tile = MOE_WINDOW, MOE_TILE
    n_win = n_tok // win
    cap_tiles = 2 * win // tile + n_exp
    i32 = jnp.int32
    mask = (comb > 0.0).reshape(n_win, win, n_exp)
    pos = pos_f.astype(i32).reshape(n_win, win, n_exp)
    cnt = pos[:, -1, :] + 1
    ntile = (cnt + tile - 1) // tile
    woff = jnp.cumsum(ntile, axis=1) - ntile
    rows = jnp.where(mask, tile * woff[:, None, :] + pos, -1)
    row_a = jnp.max(rows, axis=-1)
    row_b = jnp.max(jnp.where(rows == row_a[..., None], -1, rows), axis=-1)
    rowsel = jnp.stack([row_a, row_b], axis=1)
    win_rows = tile * jnp.sum(ntile, axis=1)

    etiles = jnp.sum(ntile, axis=0)
    nsup = (etiles + MOE_SUPER - 1) // MOE_SUPER
    ebase = MOE_SUPER * (jnp.cumsum(nsup) - nsup)
    erank = ebase[None, :] + jnp.cumsum(ntile, axis=0) - ntile
    n_sup = -(-(n_win * cap_tiles) // MOE_SUPER) + n_exp
    n_rank = n_sup * MOE_SUPER
    slots = win // tile
    s = jnp.arange(slots, dtype=i32)[None, None, :]
    ok = s < ntile[:, :, None]
    rank = jnp.where(ok, erank[:, :, None] + s, n_rank)
    wm_tile = (jnp.arange(n_win, dtype=i32) * cap_tiles)[:, None, None] + woff[:, :, None] + s
    src = jnp.zeros((n_rank + 1,), i32).at[rank.reshape(-1)].set(wm_tile.reshape(-1))[:n_rank]
    sup_total = jnp.sum(nsup)
    g = jnp.arange(n_sup, dtype=i32)
    sup_valid = (g < sup_total).astype(i32)
    sup_exp = jnp.clip(jnp.sum((g[:, None] >= jnp.cumsum(nsup)[None, :]).astype(i32), axis=1), 0, n_exp - 1)
    sup_exp = jnp.where(sup_valid > 0, sup_exp, sup_exp[jnp.maximum(sup_total - 1, 0)])

    tile_rank = jnp.zeros((n_win * cap_tiles + 1,), i32).at[
        jnp.where(ok, wm_tile, n_win * cap_tiles).reshape(-1)].set(
        jnp.minimum(rank, n_rank - 1).reshape(-1))[:n_win * cap_tiles]
    comb3 = comb.reshape(n_win, win, n_exp)
    gate_a = jnp.sum(jnp.where(rows == row_a[..., None], comb3, 0.0), axis=-1)
    gate_b = jnp.sum(jnp.where((rows == row_b[..., None]) & mask, comb3, 0.0), axis=-1)
    tok_sel = jnp.stack([row_a.astype(F32), row_b.astype(F32), gate_a, gate_b], axis=-1).reshape(n_tok, 4)
    return dict(n_win=n_win, cap_tiles=cap_tiles, n_sup=n_sup, n_rank=n_rank,
                rowsel=rowsel, win_rows=win_rows.astype(i32),
                src=src, sup_exp=sup_exp.astype(i32), sup_valid=sup_valid,
                tile_rank=tile_rank, tok_sel=tok_sel)


def _dispatch_kernel(nrows_ref, *refs, offsets):
    u_refs = refs[:len(offsets)]
    sel_ref, o_ref = refs[len(offsets):]
    w = pl.program_id(0)
    u_ref = u_refs[0]
    if len(offsets) > 1:
        tokens = u_refs[0][...]
        for s in range(1, len(offsets)):
            tokens = jnp.where(w >= offsets[s], u_refs[s][...], tokens)
    r1 = sel_ref[0:1, :]
    r2 = sel_ref[1:2, :]
    for base in range(0, o_ref.shape[0], MOE_GCHUNK):
        size = min(MOE_GCHUNK, o_ref.shape[0] - base)
        rows = pl.ds(base, size)

        @pl.when(base < nrows_ref[w])
        def _():
            rid = lax.broadcasted_iota(jnp.int32, (size, r1.shape[1]), 0) + base
            onehot = jnp.where(rid == r1, 1.0, jnp.where(rid == r2, 1.0, 0.0)).astype(BF16)
            src = tokens if len(offsets) > 1 else u_ref[...]
            o_ref[rows, :] = jnp.dot(onehot, src, preferred_element_type=F32).astype(o_ref.dtype)

        @pl.when(base >= nrows_ref[w])
        def _():
            o_ref[rows, :] = jnp.zeros((size, o_ref.shape[1]), o_ref.dtype)


def _dispatch_call(us, plan):
    d = us[0].shape[1]
    cap_rows = plan["cap_tiles"] * MOE_TILE
    n_win = plan["n_win"]
    counts = [u.shape[0] // MOE_WINDOW for u in us]
    offsets = tuple(sum(counts[:s]) for s in range(len(us)))

    def u_spec(s):
        return pl.BlockSpec((MOE_WINDOW, d), lambda w, n: (jnp.clip(w - offsets[s], 0, counts[s] - 1), 0))

    return pl.pallas_call(
        functools.partial(_dispatch_kernel, offsets=offsets),
        grid_spec=pltpu.PrefetchScalarGridSpec(
            num_scalar_prefetch=1,
            grid=(n_win,),
            in_specs=[u_spec(s) for s in range(len(us))] + [
                pl.BlockSpec((None, TOP_K, MOE_WINDOW), lambda w, n: (w, 0, 0))],
            out_specs=pl.BlockSpec((cap_rows, d), lambda w, n: (w, 0))),
        out_shape=jax.ShapeDtypeStruct((n_win * cap_rows, d), BF16),
        compiler_params=_cparams(),
        name="moe_dispatch",
    )(plan["win_rows"], *us, plan["rowsel"])


def _expert_kernel(src_ref, exp_ref, valid_ref, *refs, n_chunk):
    tiles = refs[:MOE_SUPER]
    wg_ref, wu_ref, wd_ref, o_ref, u_ref, acc_ref = refs[MOE_SUPER:]
    g = pl.program_id(0)
    j = pl.program_id(1)
    ok = valid_ref[g] > 0

    @pl.when(j == 0)
    def _():
        for k in range(MOE_SUPER):
            u_ref[k * MOE_TILE:(k + 1) * MOE_TILE, :] = tiles[k][...]
        acc_ref[...] = jnp.zeros_like(acc_ref)

    @pl.when(ok)
    def _():
        u = u_ref[...]
        part = None
        for c0 in range(0, wg_ref.shape[2], MOE_SUB):
            a = jnp.dot(u, wg_ref[0, :, c0:c0 + MOE_SUB].astype(BF16), preferred_element_type=F32)
            b = jnp.dot(u, wu_ref[0, :, c0:c0 + MOE_SUB].astype(BF16), preferred_element_type=F32)
            h = (a * jax.nn.sigmoid(a) * b).astype(BF16)
            y = jnp.dot(h, wd_ref[0, c0:c0 + MOE_SUB, :].astype(BF16), preferred_element_type=F32)
            part = y if part is None else part + y
        acc_ref[...] += part

    @pl.when(j == n_chunk - 1)
    def _():
        o_ref[...] = acc_ref[...].astype(o_ref.dtype)


def _expert_call(xw, plan, w_gate, w_up, w_down, layer_base, fc):
    d = xw.shape[1]
    dff = w_gate.shape[2]
    n_chunk = dff // fc
    rows = MOE_SUPER * MOE_TILE

    def tile_spec(k):
        return pl.BlockSpec((MOE_TILE, d), lambda g, j, src, ex, va: (src[g * MOE_SUPER + k], 0))

    def chunk(j, va, g):
        return jnp.where(va[g] > 0, j, n_chunk - 1)

    in_specs = [tile_spec(k) for k in range(MOE_SUPER)] + [
        pl.BlockSpec((1, d, fc), lambda g, j, src, ex, va: (layer_base + ex[g], 0, chunk(j, va, g))),
        pl.BlockSpec((1, d, fc), lambda g, j, src, ex, va: (layer_base + ex[g], 0, chunk(j, va, g))),
        pl.BlockSpec((1, fc, d), lambda g, j, src, ex, va: (layer_base + ex[g], chunk(j, va, g), 0))]
    return pl.pallas_call(
        functools.partial(_expert_kernel, n_chunk=n_chunk),
        grid_spec=pltpu.PrefetchScalarGridSpec(
            num_scalar_prefetch=3,
            grid=(plan["n_sup"], n_chunk),
            in_specs=in_specs,
            out_specs=pl.BlockSpec((rows, d), lambda g, j, src, ex, va: (g, 0)),
            scratch_shapes=[pltpu.VMEM((rows, d), BF16), pltpu.VMEM((rows, d), F32)]),
        out_shape=jax.ShapeDtypeStruct((plan["n_sup"] * rows, d), BF16),
        compiler_params=_cparams(),
        name="moe_experts",
    )(plan["src"], plan["sup_exp"], plan["sup_valid"], *([xw] * MOE_SUPER), w_gate, w_up, w_down)


def _combine_kernel(rank_ref, *refs, alpha, n_tiles):
    tile_refs = refs[:n_tiles]
    sel_ref, x_ref, gate_ref, g_ref, b_ref, o_ref, rows_ref, acc_ref = refs[n_tiles:]
    for k in range(n_tiles):
        rows_ref[k * MOE_TILE:(k + 1) * MOE_TILE, :] = tile_refs[k][...]
    sel = sel_ref[...]
    lane = lax.broadcasted_iota(jnp.int32, sel.shape, 1)
    col = lambda k: jnp.sum(jnp.where(lane == k, sel, 0.0), axis=-1, keepdims=True)
    row_a = col(0).astype(jnp.int32)
    row_b = col(1).astype(jnp.int32)
    w_a = col(2)
    w_b = col(3)
    n_rows = rows_ref.shape[0]
    for c0 in range(0, n_rows, MOE_GCHUNK):
        size = min(MOE_GCHUNK, n_rows - c0)
        rid = lax.broadcasted_iota(jnp.int32, (sel.shape[0], size), 1) + c0
        q = jnp.where(rid == row_a, w_a, jnp.where(rid == row_b, w_b, 0.0)).astype(BF16)
        part = jnp.dot(q, rows_ref[c0:c0 + size, :], preferred_element_type=F32)
        if c0 == 0:
            acc_ref[...] = part
        else:
            acc_ref[...] += part
    y = alpha * x_ref[...] + gate_ref[0] * acc_ref[...]
    o_ref[...] = _layernorm(y, LN_EPS) * g_ref[...] + b_ref[...]


def _combine_call(ys, plan, x2, win_base, mod3, mod_row, g, b, alpha):
    n_tok, d = x2.shape
    n_tiles = plan["cap_tiles"]

    def tile_spec(k):
        return pl.BlockSpec((MOE_TILE, d), lambda w, rk: (rk[(win_base + w) * n_tiles + k], 0))

    win = lambda width: pl.BlockSpec((MOE_WINDOW, width), lambda w, rk: (w, 0))
    vec = pl.BlockSpec((1, d), lambda w, rk: (0, 0))
    return pl.pallas_call(
        functools.partial(_combine_kernel, alpha=alpha, n_tiles=n_tiles),
        grid_spec=pltpu.PrefetchScalarGridSpec(
            num_scalar_prefetch=1,
            grid=(n_tok // MOE_WINDOW,),
            in_specs=[tile_spec(k) for k in range(n_tiles)] + [
                pl.BlockSpec((MOE_WINDOW, 4), lambda w, rk: (win_base + w, 0)), win(d),
                pl.BlockSpec((1, 1, d), lambda w, rk: (mod_row(w) * 6 + 5, 0, 0)),
                vec, vec],
            out_specs=win(d),
            scratch_shapes=[pltpu.VMEM((n_tiles * MOE_TILE, d), BF16), pltpu.VMEM((MOE_WINDOW, d), F32)]),
        out_shape=jax.ShapeDtypeStruct((n_tok, d), F32),
        compiler_params=_cparams(),
        name="moe_combine_deepnorm",
    )(plan["tile_rank"], *([ys] * n_tiles), plan["tok_sel"], x2, mod3, g.reshape(1, d), b.reshape(1, d))


def _moe_call(parts, mod3, router_w, w_gate, w_up, w_down, layer_base, g, b, alpha, fc):
    routed = [_router_call(x2, mod3, lambda i, r=tok_row: r(i * MOE_WINDOW), router_w) for x2, tok_row in parts]
    comb = jnp.concatenate([r[1] for r in routed], axis=0)
    pos = jnp.concatenate([r[2] for r in routed], axis=0)
    plan = _moe_plan(comb, pos, comb.shape[0])
    xw = _dispatch_call([r[0] for r in routed], plan)
    ys = _expert_call(xw, plan, w_gate, w_up, w_down, layer_base, fc)
    outs = []
    win_base = 0
    for x2, tok_row in parts:
        outs.append(_combine_call(ys, plan, x2, win_base, mod3, lambda w, r=tok_row: r(w * MOE_WINDOW),
                                  g, b, alpha))
        win_base += x2.shape[0] // MOE_WINDOW
    return outs


def _pick_tile(n, pref):
    t = min(n, pref)
    while n % t:
        t //= 2
    return t


def kernel(x, c, ctx, c_ctx, w_mod, b_mod, w_in, q_gain, k_gain, hy_conv_w, hy_conv_b, hy_f_w1, hy_f_b1, hy_f_freq, hy_f_w2, hy_f_b2, hy_f_w3, hy_d, pool_w, pool_scale, w_out, ln1_g, ln1_b, ln2_g, ln2_b, ffn_w_gate, ffn_w_up, ffn_w_down, router_w, moe_w_gate, moe_w_up, moe_w_down):
    bsz, seq, d = x.shape
    clen = ctx.shape[1]
    depth = w_mod.shape[0]
    alpha = (2.0 * depth) ** 0.25
    assert seq % MOE_WINDOW == 0 and (bsz * clen) % MOE_WINDOW == 0 and bsz % 2 == 0
    n_exp = router_w.shape[2]
    moe_wg = moe_w_gate.reshape((-1,) + moe_w_gate.shape[2:])
    moe_wu = moe_w_up.reshape((-1,) + moe_w_up.shape[2:])
    moe_wd = moe_w_down.reshape((-1,) + moe_w_down.shape[2:])
    ffn_wg, ffn_wu, ffn_wd = (w.astype(BF16) for w in (ffn_w_gate, ffn_w_up, ffn_w_down))

    n_rows = -(-(bsz + 1) // 8) * 8
    cc = jnp.zeros((n_rows, d), F32).at[:bsz].set(c).at[bsz].set(c_ctx)
    mod = _mod_call(cc, w_mod, b_mod)

    tm_lat = _pick_tile(seq, 512)
    tm_ctx = _pick_tile(clen, 512)
    tm_ffn = _pick_tile(seq, 1024)
    tm_ffn_ctx = _pick_tile(bsz * clen, 1024)
    lat_row = lambda tm: (lambda i: i // (seq // tm))
    ctx_row = lambda i: bsz
    rope_lat = tuple(jnp.asarray(t) for t in _rope_tables(seq, True))
    rope_ctx = tuple(jnp.asarray(t) for t in _rope_tables(clen, False))
    head_avg = np.kron(np.eye(LANES // HEAD_DIM), np.full((HEAD_DIM, HEAD_DIM), 1.0 / HEAD_DIM))
    head_avg = jnp.asarray(head_avg, BF16)

    xl = x.reshape(bsz * seq, d)
    xc = ctx.reshape(bsz * clen, d)
    for l in range(depth):
        last = l == depth - 1
        mod3 = mod[l].reshape(n_rows * 6, 1, d)
        w_in_bf = w_in[l].astype(BF16)
        w_out_bf = w_out[l].astype(BF16)
        gain = jnp.concatenate([jnp.tile(q_gain[l] * (ATTN_SCALE * LOG2E), N_Q_HEADS),
                                jnp.tile(k_gain[l], N_KV_HEADS)]).reshape(1, QK_W)
        shift = (1.02 * HEAD_DIM * ATTN_SCALE * LOG2E) * jnp.max(jnp.abs(q_gain[l])) * jnp.max(jnp.abs(k_gain[l]))
        lane64 = (jnp.arange(LANES) == HEAD_DIM).astype(F32)
        ext = jnp.stack([lane64, -shift * lane64, lane64])
        fast = (shift <= ATTN_SAFE_SHIFT).astype(jnp.int32).reshape(1)
        filt_w = (hy_f_w1[l], hy_f_b1[l], hy_f_freq[l], hy_f_w2[l], hy_f_b2[l], hy_f_w3[l])
        j = l // 2
        routed = l % 2 == 1
        ffn_w = (ffn_wg, ffn_wu, ffn_wd, j)

        q_c, k_c, v_c, hp_c = _proj_call(xc, mod3, ctx_row, clen, w_in_bf, gain, head_avg, rope_ctx, ext, tm_ctx)
        k_c3 = k_c.reshape(bsz, clen, -1)
        v_c3 = v_c.reshape(bsz, clen, -1)
        if not last:
            hp_c3 = hp_c.reshape(bsz, clen, -1)
            att_c = _attn_call(fast, q_c.reshape(bsz, clen, -1), (k_c3, v_c3), _pick_tile(clen, 256))
            spec_c = _ctx_spectrum(_filter_call(clen, *filt_w), clen)
            hy_c = _ctx_hyena_call(hp_c3, hy_conv_w[l], hy_conv_b[l], spec_c, hy_d[l])
            pl_c = _pool_call(hp_c3, pool_w[l], pool_scale[l])
            xc_mid = _outproj_call(xc, att_c.reshape(bsz * clen, D_ATTN), hy_c.reshape(bsz * clen, D_HYENA),
                                   pl_c.reshape(bsz * clen, D_POOL), mod3, ctx_row, w_out_bf,
                                   ln1_g[l], ln1_b[l], alpha, tm_ctx)

        q_l, k_l, v_l, hp_l = _proj_call(xl, mod3, lat_row(tm_ffn), seq, w_in_bf, gain, head_avg, rope_lat, ext,
                                         tm_ffn)
        hp_l3 = hp_l.reshape(bsz, seq, -1)
        att = _attn_call(fast, q_l.reshape(bsz, seq, -1),
                         (k_l.reshape(bsz, seq, -1), v_l.reshape(bsz, seq, -1), k_c3, v_c3), _pick_tile(seq, 512))
        spec = _hyena_spectrum(_filter_call(seq, *filt_w), seq)
        hyo = _hyena_latent(hp_l3, hy_conv_w[l], hy_conv_b[l], spec, hy_d[l])
        plo = _pool_call(hp_l3, pool_w[l], pool_scale[l])
        xl = _outproj_call(xl, att.reshape(bsz * seq, D_ATTN), hyo.reshape(bsz * seq, D_HYENA),
                           plo.reshape(bsz * seq, D_POOL), mod3, lat_row(tm_lat), w_out_bf,
                           ln1_g[l], ln1_b[l], alpha, tm_lat)
        if routed:
            parts = [(xl, lambda t: t // seq)] + ([] if last else [(xc_mid, lambda t: bsz)])
            outs = _moe_call(parts, mod3, router_w[j], moe_wg, moe_wu, moe_wd, j * n_exp,
                             ln2_g[l], ln2_b[l], alpha, MOE_FC)
            xl = outs[0]
            if not last:
                xc = outs[1]
        else:
            xl = _ffn_call(xl, mod3, lat_row(tm_ffn), *ffn_w, ln2_g[l], ln2_b[l], alpha, tm_ffn, FFN_FC)
            if not last:
                xc = _ffn_call(xc_mid, mod3, ctx_row, *ffn_w, ln2_g[l], ln2_b[l], alpha, tm_ffn_ctx, FFN_FC)
    return xl.reshape(bsz, seq, d)
```

```python
import functools
import math

import numpy as np
import jax
import jax.numpy as jnp
from jax import lax
from jax.experimental import pallas as pl
from jax.experimental.pallas import tpu as pltpu

F32 = jnp.float32
BF16 = jnp.bfloat16
HIGHEST = lax.Precision.HIGHEST

GRID_W = 64
HEAD_DIM = 64
N_Q_HEADS = 8
N_KV_HEADS = 2
GQA_GROUP = N_Q_HEADS // N_KV_HEADS
D_ATTN = N_Q_HEADS * HEAD_DIM
KV_W = N_KV_HEADS * HEAD_DIM
QK_W = D_ATTN + KV_W
ATTN_SCALE = HEAD_DIM ** -0.5
ROPE_THETA = 10000.0
ROPE_AXIS_DIM = HEAD_DIM // 2
QK_EPS = 1e-6
D_HYENA = 256
HYENA_ORDER = 2
HYENA_BANDS = 16
HYENA_EMB = 1 + 2 * HYENA_BANDS
HYENA_DECAY_TARGET = 1e-2
HYENA_FAST_DECAY_PCT = 0.3
HYENA_SLOW_DECAY_PCT = 1.5
D_POOL = 256
POOL_WINDOWS = (2, 4, 8, 16)
POOL_GROUP = D_POOL // len(POOL_WINDOWS)
POOL_PAD = 8
TOP_K = 2
LN_EPS = 1e-5
ADALN_EPS = 1e-6
LOG2E = math.log2(math.e)

LANES = 128
FFT_N2 = 64
FFT_UNROLL = 4
VMEM_LIMIT = 56 * 1024 * 1024
FFN_FC = 2816
FFN_SUB = 256


def _cparams():
    return pltpu.CompilerParams(vmem_limit_bytes=VMEM_LIMIT)


def _layernorm(x, eps):
    mu = jnp.mean(x, axis=-1, keepdims=True)
    xc = x - mu
    var = jnp.mean(xc * xc, axis=-1, keepdims=True)
    return xc * lax.rsqrt(var + eps)


def _mod_kernel(c_ref, w_ref, b_ref, o_ref):
    c = c_ref[...]
    s = c * jax.nn.sigmoid(c)
    o_ref[0] = jnp.dot(s, w_ref[0], preferred_element_type=F32, precision=HIGHEST) + b_ref[0]


def _mod_call(cc, w_mod, b_mod):
    depth, d, d6 = w_mod.shape
    r = cc.shape[0]
    tn = 1536
    return pl.pallas_call(
        _mod_kernel,
        grid=(depth, d6 // tn),
        in_specs=[pl.BlockSpec((r, d), lambda l, j: (0, 0)),
                  pl.BlockSpec((1, d, tn), lambda l, j: (l, 0, j)),
                  pl.BlockSpec((1, 1, tn), lambda l, j: (l, 0, j))],
        out_specs=pl.BlockSpec((1, r, tn), lambda l, j: (l, 0, j)),
        out_shape=jax.ShapeDtypeStruct((depth, r, d6), F32),
        compiler_params=_cparams(),
        name="adaln_mod",
    )(cc, w_mod, b_mod.reshape(depth, 1, d6))


def _rope_tables(seq_len, rope):
    lane = np.arange(LANES)
    d = lane % HEAD_DIM
    if not rope:
        one = np.ones((seq_len, LANES), np.float32)
        zero = np.zeros((seq_len, LANES), np.float32)
        return one, zero, zero
    t = np.arange(seq_len)
    row = (t // GRID_W).astype(np.float64)
    col = (t % GRID_W).astype(np.float64)
    half = ROPE_AXIS_DIM // 2
    inv = ROPE_THETA ** (-np.arange(0, ROPE_AXIS_DIM, 2, dtype=np.float64) / ROPE_AXIS_DIM)
    pos = np.where((d // ROPE_AXIS_DIM)[None, :] == 0, row[:, None], col[:, None])
    ang = pos * inv[d % half][None, :]
    is_b = ((d % ROPE_AXIS_DIM) >= half)[None, :]
    cos = np.cos(ang)
    sin = np.sin(ang)
    s_up = np.where(is_b, sin, 0.0)
    s_dn = np.where(is_b, 0.0, -sin)
    return cos.astype(np.float32), s_up.astype(np.float32), s_dn.astype(np.float32)


def _store_padded_heads(blk, ext, out_ref, col0):
    lo = lax.broadcasted_iota(jnp.int32, blk.shape, 1) < HEAD_DIM
    out_ref[:, col0:col0 + LANES] = jnp.where(lo, blk, ext).astype(out_ref.dtype)
    out_ref[:, col0 + LANES:col0 + 2 * LANES] = jnp.where(lo, pltpu.roll(blk, HEAD_DIM, axis=1),
                                                          ext).astype(out_ref.dtype)


def _proj_kernel(x_ref, sh_ref, sc_ref, w_ref, gain_ref, bd_ref, cos_ref, sup_ref, sdn_ref, ext_ref,
                 q_ref, k_ref, v_ref, hp_ref):
    u = _layernorm(x_ref[...], ADALN_EPS) * (1.0 + sc_ref[0]) + sh_ref[0]
    p = jnp.dot(u.astype(BF16), w_ref[...], preferred_element_type=F32)
    cos = cos_ref[...]
    sup = sup_ref[...]
    sdn = sdn_ref[...]
    n_qblk = D_ATTN // LANES
    for j in range(QK_W // LANES):
        blk = p[:, j * LANES:(j + 1) * LANES]
        ms = jnp.dot((blk * blk).astype(BF16), bd_ref[...], preferred_element_type=F32)
        blk = blk * lax.rsqrt(ms + QK_EPS) * gain_ref[:, j * LANES:(j + 1) * LANES]
        up = pltpu.roll(blk, ROPE_AXIS_DIM // 2, axis=1)
        dn = pltpu.roll(blk, LANES - ROPE_AXIS_DIM // 2, axis=1)
        rot = blk * cos + up * sup + dn * sdn
        if j < n_qblk:
            _store_padded_heads(rot, ext_ref[0:1, :], q_ref, 2 * j * LANES)
        else:
            _store_padded_heads(rot, ext_ref[1:2, :], k_ref, 2 * (j - n_qblk) * LANES)
    _store_padded_heads(p[:, QK_W:QK_W + KV_W], ext_ref[2:3, :], v_ref, 0)
    hp_ref[...] = p[:, QK_W + KV_W:]


def _proj_call(x2, mod3, mod_row, seq_len, w_in_bf, gain, bd, tables, ext, tm):
    r, d = x2.shape
    d_in = w_in_bf.shape[1]
    d_hp = d_in - QK_W - KV_W
    nseq = seq_len // tm
    cos, sup, sdn = tables
    tab_spec = pl.BlockSpec((tm, LANES), lambda i: (i % nseq, 0))
    return pl.pallas_call(
        _proj_kernel,
        grid=(r // tm,),
        in_specs=[pl.BlockSpec((tm, d), lambda i: (i, 0)),
                  pl.BlockSpec((1, 1, d), lambda i: (mod_row(i) * 6 + 0, 0, 0)),
                  pl.BlockSpec((1, 1, d), lambda i: (mod_row(i) * 6 + 1, 0, 0)),
                  pl.BlockSpec((d, d_in), lambda i: (0, 0)),
                  pl.BlockSpec((1, QK_W), lambda i: (0, 0)),
                  pl.BlockSpec((LANES, LANES), lambda i: (0, 0)),
                  tab_spec, tab_spec, tab_spec,
                  pl.BlockSpec((3, LANES), lambda i: (0, 0))],
        out_specs=[pl.BlockSpec((tm, N_Q_HEADS * LANES), lambda i: (i, 0)),
                   pl.BlockSpec((tm, N_KV_HEADS * LANES), lambda i: (i, 0)),
                   pl.BlockSpec((tm, N_KV_HEADS * LANES), lambda i: (i, 0)),
                   pl.BlockSpec((tm, d_hp), lambda i: (i, 0))],
        out_shape=[jax.ShapeDtypeStruct((r, N_Q_HEADS * LANES), BF16),
                   jax.ShapeDtypeStruct((r, N_KV_HEADS * LANES), BF16),
                   jax.ShapeDtypeStruct((r, N_KV_HEADS * LANES), BF16),
                   jax.ShapeDtypeStruct((r, d_hp), F32)],
        compiler_params=_cparams(),
        name="ln_mod_in_proj",
    )(x2, mod3, mod3, w_in_bf, gain, bd, cos, sup, sdn, ext)


ATTN_KEY_CHUNK = 256
ATTN_SAFE_SHIFT = 60.0


def _attn_kernel(fast_ref, q_ref, *refs):
    o_ref = refs[-1]
    sources = [(refs[i], refs[i + 1]) for i in range(0, len(refs) - 1, 2)]
    tq = q_ref.shape[0]
    chunks = [(k_ref, v_ref, c0, min(ATTN_KEY_CHUNK, k_ref.shape[0] - c0))
              for k_ref, v_ref in sources for c0 in range(0, k_ref.shape[0], ATTN_KEY_CHUNK)]

    def run(running_max):
        for h in range(N_KV_HEADS):
            qs = jnp.concatenate([q_ref[:, (h * GQA_GROUP + g) * LANES:(h * GQA_GROUP + g + 1) * LANES]
                                  for g in range(GQA_GROUP)], axis=0)
            acc = jnp.zeros((GQA_GROUP * tq, LANES), F32)
            m = jnp.full((GQA_GROUP * tq, 1), -jnp.inf, F32)
            for k_ref, v_ref, c0, kw in chunks:
                kc = k_ref[c0:c0 + kw, h * LANES:(h + 1) * LANES]
                vc = v_ref[c0:c0 + kw, h * LANES:(h + 1) * LANES]
                s = lax.dot_general(qs, kc, (((1,), (1,)), ((), ())), preferred_element_type=F32)
                if running_max:
                    m_new = jnp.maximum(m, jnp.max(s, axis=-1, keepdims=True))
                    acc = acc * jnp.exp2(m - m_new)
                    s = s - m_new
                    m = m_new
                acc = acc + jnp.dot(jnp.exp2(s).astype(BF16), vc, preferred_element_type=F32)
            o = acc[:, :HEAD_DIM] / acc[:, HEAD_DIM:HEAD_DIM + 1]
            for g in range(GQA_GROUP):
                c0 = (h * GQA_GROUP + g) * HEAD_DIM
                o_ref[:, c0:c0 + HEAD_DIM] = o[g * tq:(g + 1) * tq].astype(o_ref.dtype)

    @pl.when(fast_ref[0] > 0)
    def _():
        run(False)

    @pl.when(fast_ref[0] == 0)
    def _():
        run(True)


def _attn_call(fast, q, kv, tq):
    b, lq, qw = q.shape
    kv_specs = [pl.BlockSpec((None,) + a.shape[1:], lambda i, j, f: (i, 0, 0)) for a in kv]
    return pl.pallas_call(
        _attn_kernel,
        grid_spec=pltpu.PrefetchScalarGridSpec(
            num_scalar_prefetch=1,
            grid=(b, lq // tq),
            in_specs=[pl.BlockSpec((None, tq, qw), lambda i, j, f: (i, j, 0))] + kv_specs,
            out_specs=pl.BlockSpec((None, tq, D_ATTN), lambda i, j, f: (i, j, 0))),
        out_shape=jax.ShapeDtypeStruct((b, lq, D_ATTN), BF16),
        compiler_params=_cparams(),
        name="attention",
    )(fast, q, *kv)


def _filter_consts(seq_len):
    t = np.linspace(0.0, 1.0, seq_len, dtype=np.float32).astype(np.float64)[:, None]
    omega = 2.0 * math.pi * np.arange(seq_len, dtype=np.float64)[:, None] / seq_len
    bands = np.linspace(1e-4, HYENA_BANDS - 1, HYENA_BANDS, dtype=np.float32).astype(np.float64)[None, :]
    feats = np.concatenate([t, np.cos(omega * bands), -np.sin(omega * bands)], axis=-1)
    feats = np.pad(feats, ((0, 0), (0, LANES - HYENA_EMB)))
    max_decay = math.log(HYENA_DECAY_TARGET) / HYENA_FAST_DECAY_PCT
    min_decay = math.log(HYENA_DECAY_TARGET) / HYENA_SLOW_DECAY_PCT
    deltas = np.linspace(min_decay, max_decay, D_HYENA, dtype=np.float32).astype(np.float64)
    decay = np.exp(-t * np.abs(deltas)[None, :])
    rev = (seq_len - np.arange(seq_len)) % seq_len
    f32 = lambda a: jnp.asarray(a.astype(np.float32))
    return f32(feats), f32(feats[rev]), f32(decay), f32(decay[rev])


def _filter_kernel(feat_ref, featr_ref, w1_ref, b1_ref, fr_ref, w2_ref, b2_ref, w3_ref, dec_ref, decr_ref, o_ref,
                   hid_ref):
    seq_len = feat_ref.shape[0]
    dot = functools.partial(jnp.dot, preferred_element_type=F32, precision=HIGHEST)

    @pl.when(pl.program_id(0) == 0)
    def _():
        fr = fr_ref[...]
        for i, feat in enumerate((feat_ref, featr_ref)):
            h = jnp.sin(fr * (dot(feat[...], w1_ref[...]) + b1_ref[...]))
            hid_ref[i] = jnp.sin(fr * (dot(h, w2_ref[...]) + b2_ref[...]))

    fwd = dot(hid_ref[0], w3_ref[:, :D_HYENA]) * dec_ref[...]
    bwd = dot(hid_ref[1], w3_ref[:, D_HYENA:]) * decr_ref[...]
    tot = jnp.sum(jnp.abs(fwd), axis=0, keepdims=True) + jnp.sum(jnp.abs(bwd), axis=0, keepdims=True)
    inv = 1.0 / tot
    o_ref[:seq_len, :] = fwd * inv
    row = lax.broadcasted_iota(jnp.int32, bwd.shape, 0)
    o_ref[seq_len:, :] = jnp.where(row == 0, 0.0, bwd * inv)


def _filter_call(seq_len, w1, b1, freq, w2, b2, w3):
    feats, feats_rev, decay, decay_rev = _filter_consts(seq_len)
    hid = w2.shape[0]
    w1p = jnp.pad(w1, ((0, LANES - HYENA_EMB), (0, 0)))
    full = lambda shape: pl.BlockSpec(shape, lambda o: (0,) * len(shape))
    return pl.pallas_call(
        _filter_kernel,
        grid=(HYENA_ORDER,),
        in_specs=[full((seq_len, LANES)), full((seq_len, LANES)), full((LANES, hid)), full((1, hid)),
                  full((1, hid)), full((hid, hid)), full((1, hid)),
                  pl.BlockSpec((hid, 2 * D_HYENA), lambda o: (0, o)),
                  full((seq_len, D_HYENA)), full((seq_len, D_HYENA))],
        out_specs=pl.BlockSpec((2 * seq_len, D_HYENA), lambda o: (0, o)),
        out_shape=jax.ShapeDtypeStruct((2 * seq_len, HYENA_ORDER * D_HYENA), F32),
        scratch_shapes=[pltpu.VMEM((2, seq_len, hid), F32)],
        compiler_params=_cparams(),
        name="hyena_filter",
    )(feats, feats_rev, w1p, b1.reshape(1, hid), freq.reshape(1, hid), w2, b2.reshape(1, hid), w3,
      decay, decay_rev)


@functools.lru_cache(maxsize=None)
def _fft_consts(seq_len):
    n = 2 * seq_len
    n2 = FFT_N2
    n1 = n // n2
    h1 = n1 // 2
    k1 = np.arange(n1)
    w1 = np.exp(-2j * np.pi * np.outer(k1, np.arange(n1)) / n1)
    f1 = np.zeros((n1, 2, 2 * h1))
    f1[:, 0, :h1], f1[:, 0, h1:] = w1[:, :h1].real, -w1[:, :h1].imag
    f1[:, 1, :h1], f1[:, 1, h1:] = w1[:, :h1].imag, w1[:, :h1].real
    f1 = f1.reshape(2 * n1, 2 * h1)
    f1r = np.stack([w1.real, w1.imag], axis=1).reshape(2 * n1, n1)
    a2 = np.arange(n2)
    tw = np.exp(-2j * np.pi * np.outer(k1, a2) / n)
    w2 = np.exp(-2j * np.pi * np.outer(a2, a2) / n2)
    m = w2[None, :, :] * tw[:, None, :]
    ff = np.concatenate([np.concatenate([m.real, -m.imag], axis=2),
                         np.concatenate([m.imag, m.real], axis=2)], axis=1)
    g = np.conj(w2.T)[None, :, :] * np.conj(tw)[:, :, None]
    gi = np.concatenate([np.concatenate([g.real, -g.imag], axis=2),
                         np.concatenate([g.imag, g.real], axis=2)], axis=1)
    v = np.conj(w1[:, :h1]).T
    f1i = np.zeros((2, h1, n1, 2))
    f1i[0, :, :, 0], f1i[0, :, :, 1] = v.real, -v.imag
    f1i[1, :, :, 0], f1i[1, :, :, 1] = v.imag, v.real
    f1i = f1i.reshape(2 * h1, 2 * n1)
    il = np.arange(2 * n2).reshape(2, n2).T.reshape(-1)
    ff_il = ff[:, :, il]
    gi_il = gi[:, il, :]
    f32 = lambda a: np.asarray(a, np.float32)
    return dict(n=n, n1=n1, h1=h1, f1=f32(f1), f1r=f32(f1r), ff=f32(ff), ff_il=f32(ff_il), gi_il=f32(gi_il),
                f1i=f32(f1i))


def _fftr_kernel(x_ref, f_ref, o_ref):
    n1 = f_ref.shape[1]

    def body(n2, carry):
        x = x_ref[pl.ds(n2, n1, stride=FFT_N2), :]
        o_ref[pl.ds(n2, 2 * n1, stride=FFT_N2), :] = jnp.dot(f_ref[...], x, preferred_element_type=F32,
                                                            precision=HIGHEST)
        return carry

    lax.fori_loop(0, FFT_N2, body, 0, unroll=FFT_UNROLL)


def _fftr_call(buf, f1r):
    n, cols = buf.shape
    rows = f1r.shape[0] * FFT_N2
    return pl.pallas_call(
        _fftr_kernel,
        grid=(cols // LANES,),
        in_specs=[pl.BlockSpec((n, LANES), lambda j: (0, j)),
                  pl.BlockSpec(f1r.shape, lambda j: (0, 0))],
        out_specs=pl.BlockSpec((None, rows, LANES), lambda j: (j, 0, 0)),
        out_shape=jax.ShapeDtypeStruct((cols // LANES, rows, LANES), F32),
        compiler_params=_cparams(),
        name="fft_filter_outer_dft",
    )(buf, f1r)


def _spec_kernel(a_ref, ff_ref, o_ref, *, scale):
    for kk in range(a_ref.shape[1]):
        for cb in range(a_ref.shape[0]):
            o_ref[kk, :, cb * LANES:(cb + 1) * LANES] = jnp.dot(
                ff_ref[kk], a_ref[cb, kk], preferred_element_type=F32, precision=HIGHEST) * scale


def _spec_call(a4, ff, scale, kt):
    nblk, n1, r, _ = a4.shape
    return pl.pallas_call(
        functools.partial(_spec_kernel, scale=scale),
        grid=(n1 // kt,),
        in_specs=[pl.BlockSpec((nblk, kt, r, LANES), lambda i: (0, i, 0, 0)),
                  pl.BlockSpec((kt, r, r), lambda i: (i, 0, 0))],
        out_specs=pl.BlockSpec((kt, r, nblk * LANES), lambda i: (i, 0, 0)),
        out_shape=jax.ShapeDtypeStruct((n1, r, nblk * LANES), F32),
        compiler_params=_cparams(),
        name="fft_filter_spectrum",
    )(a4, ff)


def _short_conv(z, w, b):
    n = z.shape[0]
    row = lax.broadcasted_iota(jnp.int32, z.shape, 0)
    prev = jnp.where(row == 0, 0.0, pltpu.roll(z, 1, axis=0))
    nxt = jnp.where(row == n - 1, 0.0, pltpu.roll(z, n - 1, axis=0))
    return prev * w[0:1, :] + z * w[1:2, :] + nxt * w[2:3, :] + b


def _ffta_kernel(*refs, conv):
    if conv:
        z_ref, w_ref, cb_ref, f_ref, o_ref, x_ref = refs
        for m in range(2):
            x_ref[m] = _short_conv(z_ref[m], w_ref[...], cb_ref[...])
    else:
        z_ref, f_ref, o_ref = refs
        x_ref = z_ref
    n1 = f_ref.shape[0] // 2
    h1 = f_ref.shape[1] // 2

    def body(n2, carry):
        rows = pl.ds(n2, h1, stride=FFT_N2)
        x = jnp.concatenate([x_ref[0, rows, :], x_ref[1, rows, :]], axis=0).astype(BF16)
        a = jnp.dot(f_ref[...], x, preferred_element_type=F32).astype(BF16)
        o_ref[pl.ds(n2, n1, stride=FFT_N2), :] = pltpu.bitcast(a, jnp.uint32)
        return carry

    lax.fori_loop(0, FFT_N2, body, 0, unroll=FFT_UNROLL)


def _ffta_call(z4, col0, f1_bf, conv_w=None, conv_b=None):
    groups, _, l, _ = z4.shape
    halves = D_HYENA // LANES
    n1 = f1_bf.shape[0] // 2
    conv = conv_w is not None
    in_specs = [pl.BlockSpec((None, 2, l, LANES), lambda i, j: (i, 0, 0, col0 + j))]
    args = [z4]
    scratch = []
    if conv:
        in_specs += [pl.BlockSpec((3, LANES), lambda i, j: (0, col0 + j)),
                     pl.BlockSpec((1, LANES), lambda i, j: (0, col0 + j))]
        args += [conv_w, conv_b]
        scratch = [pltpu.VMEM((2, l, LANES), F32)]
    in_specs.append(pl.BlockSpec(f1_bf.shape, lambda i, j: (0, 0)))
    args.append(f1_bf)
    return pl.pallas_call(
        functools.partial(_ffta_kernel, conv=conv),
        grid=(groups, halves),
        in_specs=in_specs,
        out_specs=pl.BlockSpec((None, None, n1 * FFT_N2, LANES), lambda i, j: (i, j, 0, 0)),
        out_shape=jax.ShapeDtypeStruct((groups, halves, n1 * FFT_N2, LANES), jnp.uint32),
        scratch_shapes=scratch,
        compiler_params=_cparams(),
        name="fft_outer_dft",
    )(*args)


def _fftb_kernel(a_ref, ff_ref, gi_ref, h_ref, o_ref):
    groups, halves, kt = a_ref.shape[:3]
    half = FFT_N2

    def body(kk, carry):
        ff = ff_ref[kk]
        gi = gi_ref[kk]
        hr = h_ref[kk, :half, :]
        hi = h_ref[kk, half:, :]
        for g in range(groups):
            a = jnp.concatenate([pltpu.bitcast(a_ref[g, hh, kk], BF16) for hh in range(halves)], axis=1)
            xh = jnp.dot(ff, a, preferred_element_type=F32)
            xr, xi = xh[:half], xh[half:]
            y = jnp.concatenate([xr * hr - xi * hi, xr * hi + xi * hr], axis=0).astype(BF16)
            b = jnp.dot(gi, y, preferred_element_type=F32).astype(BF16)
            for hh in range(halves):
                o_ref[g, hh, kk] = pltpu.bitcast(b[:, hh * LANES:(hh + 1) * LANES], jnp.uint32)
        return carry

    lax.fori_loop(0, kt, body, 0)


def _fftb_call(a5, ff_bf, gi_bf, spec, order, kt):
    g, halves, n1, n2, _ = a5.shape
    r = 2 * n2
    return pl.pallas_call(
        _fftb_kernel,
        grid=(n1 // kt,),
        in_specs=[pl.BlockSpec((g, halves, kt, n2, LANES), lambda i: (0, 0, i, 0, 0)),
                  pl.BlockSpec((kt, r, r), lambda i: (i, 0, 0)),
                  pl.BlockSpec((kt, r, r), lambda i: (i, 0, 0)),
                  pl.BlockSpec((kt, r, halves * LANES), lambda i: (i, 0, order))],
        out_specs=pl.BlockSpec((g, halves, kt, n2, LANES), lambda i: (0, 0, i, 0, 0)),
        out_shape=jax.ShapeDtypeStruct(a5.shape, jnp.uint32),
        compiler_params=_cparams(),
        name="fft_inner_conv",
    )(a5, ff_bf, gi_bf, spec)


def _fftc_kernel(*refs, z_conv):
    if z_conv:
        b_ref, f_ref, z_ref, zw_ref, zb_ref, g_ref, gw_ref, gb_ref, d_ref, o_ref = refs
    else:
        b_ref, f_ref, z_ref, g_ref, gw_ref, gb_ref, d_ref, o_ref = refs
    n1 = f_ref.shape[1] // 2
    h1 = f_ref.shape[0] // 2
    d = d_ref[...]

    def body(n2, carry):
        b = pltpu.bitcast(b_ref[pl.ds(n2, n1, stride=FFT_N2), :], BF16)
        y = jnp.dot(f_ref[...], b, preferred_element_type=F32)
        rows = pl.ds(n2, h1, stride=FFT_N2)
        for m in range(2):
            o_ref[m, rows, :] = y[m * h1:(m + 1) * h1]
        return carry

    lax.fori_loop(0, FFT_N2, body, 0, unroll=FFT_UNROLL)
    for m in range(2):
        z = _short_conv(z_ref[m], zw_ref[...], zb_ref[...]) if z_conv else z_ref[m]
        gate = _short_conv(g_ref[m], gw_ref[...], gb_ref[...])
        o_ref[m] = gate * (o_ref[m] + d * z)


def _fftc_call(b4, f1i_bf, z4, z_col0, z_conv, hp4, g_col0, conv_w, conv_b, d2, order):
    groups, halves, rows, _ = b4.shape
    l = z4.shape[2]
    pair = lambda col0: pl.BlockSpec((None, 2, l, LANES), lambda i, j: (i, 0, 0, col0 + j))
    taps = lambda col0: [pl.BlockSpec((3, LANES), lambda i, j: (0, col0 + j)),
                         pl.BlockSpec((1, LANES), lambda i, j: (0, col0 + j))]
    in_specs = [pl.BlockSpec((None, None, rows, LANES), lambda i, j: (i, j, 0, 0)),
                pl.BlockSpec(f1i_bf.shape, lambda i, j: (0, 0)),
                pair(z_col0)]
    args = [b4, f1i_bf, z4]
    if z_conv:
        in_specs += taps(z_col0)
        args += [conv_w, conv_b]
    in_specs += [pair(g_col0)] + taps(g_col0) + [pl.BlockSpec((None, 1, LANES), lambda i, j: (order, 0, j))]
    args += [hp4, conv_w, conv_b, d2.reshape(d2.shape[0], 1, d2.shape[1])]
    return pl.pallas_call(
        functools.partial(_fftc_kernel, z_conv=z_conv),
        grid=(groups, halves),
        in_specs=in_specs,
        out_specs=pair(0),
        out_shape=jax.ShapeDtypeStruct((groups, 2, l, D_HYENA), F32),
        compiler_params=_cparams(),
        name="fft_outer_inverse_gate",
    )(*args)


def _hyena_spectrum(buf, seq_len):
    cst = _fft_consts(seq_len)
    n, n1 = cst["n"], cst["n1"]
    a = _fftr_call(buf, jnp.asarray(cst["f1r"]))
    a4 = a.reshape(a.shape[0], n1, 2 * FFT_N2, LANES)
    return _spec_call(a4, jnp.asarray(cst["ff"]), 1.0 / n, min(n1, 8))


def _hyena_latent(hp3, conv_w, conv_b, spec, d):
    b, l, _ = hp3.shape
    cst = _fft_consts(l)
    n1, h1 = cst["n1"], cst["h1"]
    groups = b // 2
    halves = D_HYENA // LANES
    kt = min(n1, 8)
    f1 = jnp.asarray(cst["f1"], BF16)
    ff = jnp.asarray(cst["ff_il"], BF16)
    gi = jnp.asarray(cst["gi_il"], BF16)
    f1i = jnp.asarray(cst["f1i"], BF16)
    hp4 = hp3.reshape(groups, 2, l, hp3.shape[2])
    cb = conv_b.reshape(1, -1)
    z = hp4
    for o in range(HYENA_ORDER):
        first = o == 0
        a = _ffta_call(z, 0, f1, conv_w, cb) if first else _ffta_call(z, 0, f1)
        bb = _fftb_call(a.reshape(groups, halves, n1, FFT_N2, LANES), ff, gi, spec, o, kt)
        z = _fftc_call(bb.reshape(groups, halves, n1 * FFT_N2, LANES), f1i, z, 0, first, hp4, (1 + o) * halves,
                       conv_w, cb, d, o)
    return z.reshape(b, l, D_HYENA)


@functools.lru_cache(maxsize=None)
def _dft_consts(seq_len):
    n = 2 * seq_len
    w = np.exp(-2j * np.pi * np.outer(np.arange(n), np.arange(n)) / n)
    fwd_full = np.concatenate([w.real, w.imag], axis=0)
    fwd = fwd_full[:, :seq_len]
    wi = np.conj(w[:, :seq_len]).T
    inv = np.concatenate([wi.real, -wi.imag], axis=1)
    f32 = lambda a: np.asarray(a, np.float32)
    return dict(n=n, fwd_full=f32(fwd_full), fwd=f32(fwd), inv=f32(inv))


def _cspec_kernel(buf_ref, f_ref, o_ref, *, scale):
    o_ref[...] = jnp.dot(f_ref[...], buf_ref[...], preferred_element_type=F32, precision=HIGHEST) * scale


def _ctx_spectrum(buf, seq_len):
    cst = _dft_consts(seq_len)
    n = cst["n"]
    return pl.pallas_call(
        functools.partial(_cspec_kernel, scale=1.0 / n),
        out_shape=jax.ShapeDtypeStruct((2 * n, buf.shape[1]), F32),
        compiler_params=_cparams(),
        name="ctx_filter_spectrum",
    )(buf, jnp.asarray(cst["fwd_full"]))


def _ctx_hyena_kernel(z_ref, w_ref, b_ref, f_ref, fi_ref, h_ref, d_ref, o_ref):
    n = 2 * z_ref.shape[0]
    zc = _short_conv(z_ref[...], w_ref[...], b_ref[...])
    cur = zc[:, :D_HYENA]
    for o in range(HYENA_ORDER):
        gate = zc[:, (1 + o) * D_HYENA:(2 + o) * D_HYENA]
        xh = jnp.dot(f_ref[...], cur.astype(BF16), preferred_element_type=F32)
        xr, xi = xh[:n], xh[n:]
        hr = h_ref[:n, o * D_HYENA:(o + 1) * D_HYENA]
        hi = h_ref[n:, o * D_HYENA:(o + 1) * D_HYENA]
        y = jnp.concatenate([xr * hr - xi * hi, xr * hi + xi * hr], axis=0).astype(BF16)
        conv = jnp.dot(fi_ref[...], y, preferred_element_type=F32)
        cur = gate * (conv + d_ref[o:o + 1, :] * cur)
    o_ref[...] = cur


def _ctx_hyena_call(hp3, conv_w, conv_b, spec, d):
    b, l, _ = hp3.shape
    cst = _dft_consts(l)
    n = cst["n"]
    w3 = 3 * D_HYENA
    full = lambda shape: pl.BlockSpec(shape, lambda i: (0,) * len(shape))
    return pl.pallas_call(
        _ctx_hyena_kernel,
        grid=(b,),
        in_specs=[pl.BlockSpec((None, l, w3), lambda i: (i, 0, 0)),
                  full((3, w3)), full((1, w3)), full((2 * n, l)), full((l, 2 * n)),
                  full((2 * n, HYENA_ORDER * D_HYENA)), full((HYENA_ORDER, D_HYENA))],
        out_specs=pl.BlockSpec((None, l, D_HYENA), lambda i: (i, 0, 0)),
        out_shape=jax.ShapeDtypeStruct((b, l, D_HYENA), F32),
        compiler_params=_cparams(),
        name="ctx_hyena",
    )(hp3, conv_w, conv_b.reshape(1, -1), jnp.asarray(cst["fwd"], BF16), jnp.asarray(cst["inv"], BF16),
      spec, d)


def _pool_kernel(p_ref, w_ref, sc_ref, o_ref):
    x = p_ref[...]
    l, c = x.shape
    half = pl.program_id(1)
    zpad = jnp.zeros((POOL_PAD, c), F32)
    xp = jnp.concatenate([zpad, x, zpad], axis=0)
    tot = l + 2 * POOL_PAD
    sums = []
    f = xp
    for win in POOL_WINDOWS:
        f = f + pltpu.roll(f, tot - win // 2, axis=0)
        sums.append(pltpu.roll(f, win // 2, axis=0)[POOL_PAD:POOL_PAD + l])
    t = lax.broadcasted_iota(jnp.int32, (l, c), 0)
    grp = lax.broadcasted_iota(jnp.int32, (l, c), 1) // POOL_GROUP + half * (LANES // POOL_GROUP)
    total = sums[-1]
    cnt = None
    for gi in range(len(POOL_WINDOWS) - 1, -1, -1):
        win = POOL_WINDOWS[gi]
        cw = (jnp.minimum(t + (win - win // 2), l) - jnp.maximum(t - win // 2, 0)).astype(F32)
        if cnt is None:
            cnt = cw
        else:
            sel = grp == gi
            total = jnp.where(sel, sums[gi], total)
            cnt = jnp.where(sel, cw, cnt)
    dlt = total / cnt - x
    y = jnp.dot(dlt.astype(BF16), w_ref[0], preferred_element_type=F32)
    o_ref[...] = y * sc_ref[...]


def _pool_call(hp3, pool_w, pool_scale):
    b, l, w = hp3.shape
    nh = D_POOL // LANES
    first = (w - D_POOL) // LANES
    per = LANES // POOL_GROUP
    wbd = jnp.zeros((nh, LANES, LANES), F32)
    for g in range(len(POOL_WINDOWS)):
        r0 = (g % per) * POOL_GROUP
        wbd = wbd.at[g // per, r0:r0 + POOL_GROUP, r0:r0 + POOL_GROUP].set(pool_w[g])
    return pl.pallas_call(
        _pool_kernel,
        grid=(b, nh),
        in_specs=[pl.BlockSpec((None, l, LANES), lambda i, j: (i, 0, first + j)),
                  pl.BlockSpec((1, LANES, LANES), lambda i, j: (j, 0, 0)),
                  pl.BlockSpec((1, LANES), lambda i, j: (0, j))],
        out_specs=pl.BlockSpec((None, l, LANES), lambda i, j: (i, 0, j)),
        out_shape=jax.ShapeDtypeStruct((b, l, D_POOL), F32),
        compiler_params=_cparams(),
        name="pool_mixer",
    )(hp3, wbd.astype(BF16), pool_scale.reshape(1, D_POOL))


def _outproj_kernel(x_ref, att_ref, hy_ref, pl_ref, gate_ref, w_ref, g_ref, b_ref, o_ref, *, alpha):
    mix = jnp.dot(att_ref[...], w_ref[0:D_ATTN, :], preferred_element_type=F32)
    mix += jnp.dot(hy_ref[...].astype(BF16), w_ref[D_ATTN:D_ATTN + D_HYENA, :], preferred_element_type=F32)
    mix += jnp.dot(pl_ref[...].astype(BF16), w_ref[D_ATTN + D_HYENA:, :], preferred_element_type=F32)
    y = alpha * x_ref[...] + gate_ref[0] * mix
    o_ref[...] = _layernorm(y, LN_EPS) * g_ref[...] + b_ref[...]


def _outproj_call(x2, att2, hy2, pl2, mod3, mod_row, w_out_bf, g, b, alpha, tm):
    r, d = x2.shape
    row = lambda w: pl.BlockSpec((tm, w), lambda i: (i, 0))
    vec = pl.BlockSpec((1, d), lambda i: (0, 0))
    return pl.pallas_call(
        functools.partial(_outproj_kernel, alpha=alpha),
        grid=(r // tm,),
        in_specs=[row(d), row(D_ATTN), row(D_HYENA), row(D_POOL),
                  pl.BlockSpec((1, 1, d), lambda i: (mod_row(i) * 6 + 2, 0, 0)),
                  pl.BlockSpec(w_out_bf.shape, lambda i: (0, 0)),
                  vec, vec],
        out_specs=row(d),
        out_shape=jax.ShapeDtypeStruct((r, d), F32),
        compiler_params=_cparams(),
        name="out_proj_deepnorm",
    )(x2, att2, hy2, pl2, mod3, w_out_bf, g.reshape(1, d), b.reshape(1, d))


def _ffn_kernel(x_ref, sh_ref, sc_ref, gate_ref, wg_ref, wu_ref, wd_ref, g_ref, b_ref, o_ref, u_ref, acc_ref, *,
                alpha, n_chunk):
    j = pl.program_id(1)

    @pl.when(j == 0)
    def _():
        u_ref[...] = (_layernorm(x_ref[...], ADALN_EPS) * (1.0 + sc_ref[0]) + sh_ref[0]).astype(BF16)
        acc_ref[...] = jnp.zeros_like(acc_ref)

    u = u_ref[...]
    part = None
    for c0 in range(0, wg_ref.shape[2], FFN_SUB):
        c1 = min(c0 + FFN_SUB, wg_ref.shape[2])
        a = jnp.dot(u, wg_ref[0, :, c0:c1], preferred_element_type=F32)
        bb = jnp.dot(u, wu_ref[0, :, c0:c1], preferred_element_type=F32)
        h = (a * jax.nn.sigmoid(a) * bb).astype(BF16)
        y = jnp.dot(h, wd_ref[0, c0:c1, :], preferred_element_type=F32)
        part = y if part is None else part + y
    acc_ref[...] += part

    @pl.when(j == n_chunk - 1)
    def _():
        y = alpha * x_ref[...] + gate_ref[0] * acc_ref[...]
        o_ref[...] = _layernorm(y, LN_EPS) * g_ref[...] + b_ref[...]


def _ffn_call(x2, mod3, mod_row, w_gate, w_up, w_down, w_idx, g, b, alpha, tm, fc):
    r, d = x2.shape
    dff = w_gate.shape[2]
    n_chunk = dff // fc
    wmode = pl.Buffered(1) if n_chunk == 1 else None
    vec = pl.BlockSpec((1, d), lambda i, j: (0, 0))
    mod = lambda which: pl.BlockSpec((1, 1, d), lambda i, j: (mod_row(i) * 6 + which, 0, 0))
    return pl.pallas_call(
        functools.partial(_ffn_kernel, alpha=alpha, n_chunk=n_chunk),
        grid=(r // tm, n_chunk),
        in_specs=[pl.BlockSpec((tm, d), lambda i, j: (i, 0)), mod(3), mod(4), mod(5),
                  pl.BlockSpec((1, d, fc), lambda i, j: (w_idx, 0, j), pipeline_mode=wmode),
                  pl.BlockSpec((1, d, fc), lambda i, j: (w_idx, 0, j), pipeline_mode=wmode),
                  pl.BlockSpec((1, fc, d), lambda i, j: (w_idx, j, 0), pipeline_mode=wmode),
                  vec, vec],
        out_specs=pl.BlockSpec((tm, d), lambda i, j: (i, 0)),
        out_shape=jax.ShapeDtypeStruct((r, d), F32),
        scratch_shapes=[pltpu.VMEM((tm, d), BF16), pltpu.VMEM((tm, d), F32)],
        compiler_params=_cparams(),
        name="channel_mixer",
    )(x2, mod3, mod3, mod3, w_gate, w_up, w_down, g.reshape(1, d), b.reshape(1, d))


MOE_WINDOW = 1024
MOE_TILE = 64
MOE_SUPER = 16
MOE_FC = 512
MOE_GCHUNK = 512
MOE_SUB = 256


def _router_kernel(x_ref, sh_ref, sc_ref, rw_ref, tri_ref, u_ref, comb_ref, pos_ref):
    u = (_layernorm(x_ref[...], ADALN_EPS) * (1.0 + sc_ref[0]) + sh_ref[0]).astype(BF16)
    u_ref[...] = u
    logits = jnp.dot(u, rw_ref[...].astype(BF16), preferred_element_type=F32)
    n_exp = logits.shape[1]
    lane = lax.broadcasted_iota(jnp.int32, logits.shape, 1)
    m1 = jnp.max(logits, axis=-1, keepdims=True)
    i1 = jnp.min(jnp.where(logits == m1, lane, n_exp), axis=-1, keepdims=True)
    rest = jnp.where(lane == i1, -jnp.inf, logits)
    m2 = jnp.max(rest, axis=-1, keepdims=True)
    i2 = jnp.min(jnp.where(rest == m2, lane, n_exp), axis=-1, keepdims=True)
    e2 = jnp.exp(m2 - m1)
    den = 1.0 + e2
    comb = jnp.where(lane == i1, 1.0 / den, 0.0) + jnp.where(lane == i2, e2 / den, 0.0)
    comb_ref[...] = comb
    routed = jnp.where(comb > 0.0, 1.0, 0.0).astype(BF16)
    pos_ref[...] = jnp.dot(tri_ref[...], routed, preferred_element_type=F32) - 1.0


def _router_call(x2, mod3, mod_row, router_w):
    r, d = x2.shape
    n_exp = router_w.shape[1]
    tm = MOE_WINDOW
    tri = jnp.asarray(np.tril(np.ones((tm, tm), np.float32)), BF16)
    mod = lambda which: pl.BlockSpec((1, 1, d), lambda i: (mod_row(i) * 6 + which, 0, 0))
    small = pl.BlockSpec((tm, n_exp), lambda i: (i, 0))
    return pl.pallas_call(
        _router_kernel,
        grid=(r // tm,),
        in_specs=[pl.BlockSpec((tm, d), lambda i: (i, 0)), mod(3), mod(4),
                  pl.BlockSpec(router_w.shape, lambda i: (0, 0)),
                  pl.BlockSpec((tm, tm), lambda i: (0, 0))],
        out_specs=[pl.BlockSpec((tm, d), lambda i: (i, 0)), small, small],
        out_shape=[jax.ShapeDtypeStruct((r, d), BF16), jax.ShapeDtypeStruct((r, n_exp), F32),
                   jax.ShapeDtypeStruct((r, n_exp), F32)],
        compiler_params=_cparams(),
        name="moe_router",
    )(x2, mod3, mod3, router_w, tri)


def _moe_plan(comb, pos_f, n_tok):
    n_exp = comb.shape[1]
    win, tile = MOE_WINDOW, MOE_TILE
    n_win = n_tok // win
    cap_tiles = 2 * win // tile + n_exp
    i32 = jnp.int32
    mask = (comb > 0.0).reshape(n_win, win, n_exp)
    pos = pos_f.astype(i32).reshape(n_win, win, n_exp)
    cnt = pos[:, -1, :] + 1
    ntile = (cnt + tile - 1) // tile
    woff = jnp.cumsum(ntile, axis=1) - ntile
    rows = jnp.where(mask, tile * woff[:, None, :] + pos, -1)
    row_a = jnp.max(rows, axis=-1)
    row_b = jnp.max(jnp.where(rows == row_a[..., None], -1, rows), axis=-1)
    rowsel = jnp.stack([row_a, row_b], axis=1)
    win_rows = tile * jnp.sum(ntile, axis=1)

    etiles = jnp.sum(ntile, axis=0)
    nsup = (etiles + MOE_SUPER - 1) // MOE_SUPER
    ebase = MOE_SUPER * (jnp.cumsum(nsup) - nsup)
    erank = ebase[None, :] + jnp.cumsum(ntile, axis=0) - ntile
    n_sup = -(-(n_win * cap_tiles) // MOE_SUPER) + n_exp
    n_rank = n_sup * MOE_SUPER
    slots = win // tile
    s = jnp.arange(slots, dtype=i32)[None, None, :]
    ok = s < ntile[:, :, None]
    rank = jnp.where(ok, erank[:, :, None] + s, n_rank)
    wm_tile = (jnp.arange(n_win, dtype=i32) * cap_tiles)[:, None, None] + woff[:, :, None] + s
    src = jnp.zeros((n_rank + 1,), i32).at[rank.reshape(-1)].set(wm_tile.reshape(-1))[:n_rank]
    sup_total = jnp.sum(nsup)
    g = jnp.arange(n_sup, dtype=i32)
    sup_valid = (g < sup_total).astype(i32)
    sup_exp = jnp.clip(jnp.sum((g[:, None] >= jnp.cumsum(nsup)[None, :]).astype(i32), axis=1), 0, n_exp - 1)
    sup_exp = jnp.where(sup_valid > 0, sup_exp, sup_exp[jnp.maximum(sup_total - 1, 0)])

    tile_rank = jnp.zeros((n_win * cap_tiles + 1,), i32).at[
        jnp.where(ok, wm_tile, n_win * cap_tiles).reshape(-1)].set(
        jnp.minimum(rank, n_rank - 1).reshape(-1))[:n_win * cap_tiles]
    comb3 = comb.reshape(n_win, win, n_exp)
    gate_a = jnp.sum(jnp.where(rows == row_a[..., None], comb3, 0.0), axis=-1)
    gate_b = jnp.sum(jnp.where((rows == row_b[..., None]) & mask, comb3, 0.0), axis=-1)
    tok_sel = jnp.stack([row_a.astype(F32), row_b.astype(F32), gate_a, gate_b], axis=-1).reshape(n_tok, 4)
    return dict(n_win=n_win, cap_tiles=cap_tiles, n_sup=n_sup, n_rank=n_rank,
                rowsel=rowsel, win_rows=win_rows.astype(i32),
                src=src, sup_exp=sup_exp.astype(i32), sup_valid=sup_valid,
                tile_rank=tile_rank, tok_sel=tok_sel)


def _dispatch_kernel(nrows_ref, *refs, offsets):
    u_refs = refs[:len(offsets)]
    sel_ref, o_ref = refs[len(offsets):]
    w = pl.program_id(0)
    u_ref = u_refs[0]
    if len(offsets) > 1:
        tokens = u_refs[0][...]
        for s in range(1, len(offsets)):
            tokens = jnp.where(w >= offsets[s], u_refs[s][...], tokens)
    r1 = sel_ref[0:1, :]
    r2 = sel_ref[1:2, :]
    for base in range(0, o_ref.shape[0], MOE_GCHUNK):
        size = min(MOE_GCHUNK, o_ref.shape[0] - base)
        rows = pl.ds(base, size)

        @pl.when(base < nrows_ref[w])
        def _():
            rid = lax.broadcasted_iota(jnp.int32, (size, r1.shape[1]), 0) + base
            onehot = jnp.where(rid == r1, 1.0, jnp.where(rid == r2, 1.0, 0.0)).astype(BF16)
            src = tokens if len(offsets) > 1 else u_ref[...]
            o_ref[rows, :] = jnp.dot(onehot, src, preferred_element_type=F32).astype(o_ref.dtype)

        @pl.when(base >= nrows_ref[w])
        def _():
            o_ref[rows, :] = jnp.zeros((size, o_ref.shape[1]), o_ref.dtype)


def _dispatch_call(us, plan):
    d = us[0].shape[1]
    cap_rows = plan["cap_tiles"] * MOE_TILE
    n_win = plan["n_win"]
    counts = [u.shape[0] // MOE_WINDOW for u in us]
    offsets = tuple(sum(counts[:s]) for s in range(len(us)))

    def u_spec(s):
        return pl.BlockSpec((MOE_WINDOW, d), lambda w, n: (jnp.clip(w - offsets[s], 0, counts[s] - 1), 0))

    return pl.pallas_call(
        functools.partial(_dispatch_kernel, offsets=offsets),
        grid_spec=pltpu.PrefetchScalarGridSpec(
            num_scalar_prefetch=1,
            grid=(n_win,),
            in_specs=[u_spec(s) for s in range(len(us))] + [
                pl.BlockSpec((None, TOP_K, MOE_WINDOW), lambda w, n: (w, 0, 0))],
            out_specs=pl.BlockSpec((cap_rows, d), lambda w, n: (w, 0))),
        out_shape=jax.ShapeDtypeStruct((n_win * cap_rows, d), BF16),
        compiler_params=_cparams(),
        name="moe_dispatch",
    )(plan["win_rows"], *us, plan["rowsel"])


def _expert_kernel(src_ref, exp_ref, valid_ref, *refs, n_chunk):
    tiles = refs[:MOE_SUPER]
    wg_ref, wu_ref, wd_ref, o_ref, u_ref, acc_ref = refs[MOE_SUPER:]
    g = pl.program_id(0)
    j = pl.program_id(1)
    ok = valid_ref[g] > 0

    @pl.when(j == 0)
    def _():
        for k in range(MOE_SUPER):
            u_ref[k * MOE_TILE:(k + 1) * MOE_TILE, :] = tiles[k][...]
        acc_ref[...] = jnp.zeros_like(acc_ref)

    @pl.when(ok)
    def _():
        u = u_ref[...]
        part = None
        for c0 in range(0, wg_ref.shape[2], MOE_SUB):
            a = jnp.dot(u, wg_ref[0, :, c0:c0 + MOE_SUB].astype(BF16), preferred_element_type=F32)
            b = jnp.dot(u, wu_ref[0, :, c0:c0 + MOE_SUB].astype(BF16), preferred_element_type=F32)
            h = (a * jax.nn.sigmoid(a) * b).astype(BF16)
            y = jnp.dot(h, wd_ref[0, c0:c0 + MOE_SUB, :].astype(BF16), preferred_element_type=F32)
            part = y if part is None else part + y
        acc_ref[...] += part

    @pl.when(j == n_chunk - 1)
    def _():
        o_ref[...] = acc_ref[...].astype(o_ref.dtype)


def _expert_call(xw, plan, w_gate, w_up, w_down, layer_base, fc):
    d = xw.shape[1]
    dff = w_gate.shape[2]
    n_chunk = dff // fc
    rows = MOE_SUPER * MOE_TILE

    def tile_spec(k):
        return pl.BlockSpec((MOE_TILE, d), lambda g, j, src, ex, va: (src[g * MOE_SUPER + k], 0))

    def chunk(j, va, g):
        return jnp.where(va[g] > 0, j, n_chunk - 1)

    in_specs = [tile_spec(k) for k in range(MOE_SUPER)] + [
        pl.BlockSpec((1, d, fc), lambda g, j, src, ex, va: (layer_base + ex[g], 0, chunk(j, va, g))),
        pl.BlockSpec((1, d, fc), lambda g, j, src, ex, va: (layer_base + ex[g], 0, chunk(j, va, g))),
        pl.BlockSpec((1, fc, d), lambda g, j, src, ex, va: (layer_base + ex[g], chunk(j, va, g), 0))]
    return pl.pallas_call(
        functools.partial(_expert_kernel, n_chunk=n_chunk),
        grid_spec=pltpu.PrefetchScalarGridSpec(
            num_scalar_prefetch=3,
            grid=(plan["n_sup"], n_chunk),
            in_specs=in_specs,
            out_specs=pl.BlockSpec((rows, d), lambda g, j, src, ex, va: (g, 0)),
            scratch_shapes=[pltpu.VMEM((rows, d), BF16), pltpu.VMEM((rows, d), F32)]),
        out_shape=jax.ShapeDtypeStruct((plan["n_sup"] * rows, d), BF16),
        compiler_params=_cparams(),
        name="moe_experts",
    )(plan["src"], plan["sup_exp"], plan["sup_valid"], *([xw] * MOE_SUPER), w_gate, w_up, w_down)


def _combine_kernel(rank_ref, *refs, alpha, n_tiles):
    tile_refs = refs[:n_tiles]
    sel_ref, x_ref, gate_ref, g_ref, b_ref, o_ref, rows_ref, acc_ref = refs[n_tiles:]
    for k in range(n_tiles):
        rows_ref[k * MOE_TILE:(k + 1) * MOE_TILE, :] = tile_refs[k][...]
    sel = sel_ref[...]
    lane = lax.broadcasted_iota(jnp.int32, sel.shape, 1)
    col = lambda k: jnp.sum(jnp.where(lane == k, sel, 0.0), axis=-1, keepdims=True)
    row_a = col(0).astype(jnp.int32)
    row_b = col(1).astype(jnp.int32)
    w_a = col(2)
    w_b = col(3)
    n_rows = rows_ref.shape[0]
    for c0 in range(0, n_rows, MOE_GCHUNK):
        size = min(MOE_GCHUNK, n_rows - c0)
        rid = lax.broadcasted_iota(jnp.int32, (sel.shape[0], size), 1) + c0
        q = jnp.where(rid == row_a, w_a, jnp.where(rid == row_b, w_b, 0.0)).astype(BF16)
        part = jnp.dot(q, rows_ref[c0:c0 + size, :], preferred_element_type=F32)
        if c0 == 0:
            acc_ref[...] = part
        else:
            acc_ref[...] += part
    y = alpha * x_ref[...] + gate_ref[0] * acc_ref[...]
    o_ref[...] = _layernorm(y, LN_EPS) * g_ref[...] + b_ref[...]


def _combine_call(ys, plan, x2, win_base, mod3, mod_row, g, b, alpha):
    n_tok, d = x2.shape
    n_tiles = plan["cap_tiles"]

    def tile_spec(k):
        return pl.BlockSpec((MOE_TILE, d), lambda w, rk: (rk[(win_base + w) * n_tiles + k], 0))

    win = lambda width: pl.BlockSpec((MOE_WINDOW, width), lambda w, rk: (w, 0))
    vec = pl.BlockSpec((1, d), lambda w, rk: (0, 0))
    return pl.pallas_call(
        functools.partial(_combine_kernel, alpha=alpha, n_tiles=n_tiles),
        grid_spec=pltpu.PrefetchScalarGridSpec(
            num_scalar_prefetch=1,
            grid=(n_tok // MOE_WINDOW,),
            in_specs=[tile_spec(k) for k in range(n_tiles)] + [
                pl.BlockSpec((MOE_WINDOW, 4), lambda w, rk: (win_base + w, 0)), win(d),
                pl.BlockSpec((1, 1, d), lambda w, rk: (mod_row(w) * 6 + 5, 0, 0)),
                vec, vec],
            out_specs=win(d),
            scratch_shapes=[pltpu.VMEM((n_tiles * MOE_TILE, d), BF16), pltpu.VMEM((MOE_WINDOW, d), F32)]),
        out_shape=jax.ShapeDtypeStruct((n_tok, d), F32),
        compiler_params=_cparams(),
        name="moe_combine_deepnorm",
    )(plan["tile_rank"], *([ys] * n_tiles), plan["tok_sel"], x2, mod3, g.reshape(1, d), b.reshape(1, d))


def _moe_call(parts, mod3, router_w, w_gate, w_up, w_down, layer_base, g, b, alpha, fc):
    routed = [_router_call(x2, mod3, lambda i, r=tok_row: r(i * MOE_WINDOW), router_w) for x2, tok_row in parts]
    comb = jnp.concatenate([r[1] for r in routed], axis=0)
    pos = jnp.concatenate([r[2] for r in routed], axis=0)
    plan = _moe_plan(comb, pos, comb.shape[0])
    xw = _dispatch_call([r[0] for r in routed], plan)
    ys = _expert_call(xw, plan, w_gate, w_up, w_down, layer_base, fc)
    outs = []
    win_base = 0
    for x2, tok_row in parts:
        outs.append(_combine_call(ys, plan, x2, win_base, mod3, lambda w, r=tok_row: r(w * MOE_WINDOW),
                                  g, b, alpha))
        win_base += x2.shape[0] // MOE_WINDOW
    return outs


def _pick_tile(n, pref):
    t = min(n, pref)
    while n % t:
        t //= 2
    return t


def kernel(x, c, ctx, c_ctx, w_mod, b_mod, w_in, q_gain, k_gain, hy_conv_w, hy_conv_b, hy_f_w1, hy_f_b1, hy_f_freq, hy_f_w2, hy_f_b2, hy_f_w3, hy_d, pool_w, pool_scale, w_out, ln1_g, ln1_b, ln2_g, ln2_b, ffn_w_gate, ffn_w_up, ffn_w_down, router_w, moe_w_gate, moe_w_up, moe_w_down):
    bsz, seq, d = x.shape
    clen = ctx.shape[1]
    depth = w_mod.shape[0]
    alpha = (2.0 * depth) ** 0.25
    assert seq % MOE_WINDOW == 0 and (bsz * clen) % MOE_WINDOW == 0 and bsz % 2 == 0
    n_exp = router_w.shape[2]
    moe_wg = moe_w_gate.reshape((-1,) + moe_w_gate.shape[2:])
    moe_wu = moe_w_up.reshape((-1,) + moe_w_up.shape[2:])
    moe_wd = moe_w_down.reshape((-1,) + moe_w_down.shape[2:])
    ffn_wg, ffn_wu, ffn_wd = (w.astype(BF16) for w in (ffn_w_gate, ffn_w_up, ffn_w_down))

    n_rows = -(-(bsz + 1) // 8) * 8
    cc = jnp.zeros((n_rows, d), F32).at[:bsz].set(c).at[bsz].set(c_ctx)
    mod = _mod_call(cc, w_mod, b_mod)

    tm_lat = _pick_tile(seq, 512)
    tm_ctx = _pick_tile(clen, 512)
    tm_ffn = _pick_tile(seq, 1024)
    tm_ffn_ctx = _pick_tile(bsz * clen, 1024)
    lat_row = lambda tm: (lambda i: i // (seq // tm))
    ctx_row = lambda i: bsz
    rope_lat = tuple(jnp.asarray(t) for t in _rope_tables(seq, True))
    rope_ctx = tuple(jnp.asarray(t) for t in _rope_tables(clen, False))
    head_avg = np.kron(np.eye(LANES // HEAD_DIM), np.full((HEAD_DIM, HEAD_DIM), 1.0 / HEAD_DIM))
    head_avg = jnp.asarray(head_avg, BF16)

    xl = x.reshape(bsz * seq, d)
    xc = ctx.reshape(bsz * clen, d)
    for l in range(depth):
        last = l == depth - 1
        mod3 = mod[l].reshape(n_rows * 6, 1, d)
        w_in_bf = w_in[l].astype(BF16)
        w_out_bf = w_out[l].astype(BF16)
        gain = jnp.concatenate([jnp.tile(q_gain[l] * (ATTN_SCALE * LOG2E), N_Q_HEADS),
                                jnp.tile(k_gain[l], N_KV_HEADS)]).reshape(1, QK_W)
        shift = (1.02 * HEAD_DIM * ATTN_SCALE * LOG2E) * jnp.max(jnp.abs(q_gain[l])) * jnp.max(jnp.abs(k_gain[l]))
        lane64 = (jnp.arange(LANES) == HEAD_DIM).astype(F32)
        ext = jnp.stack([lane64, -shift * lane64, lane64])
        fast = (shift <= ATTN_SAFE_SHIFT).astype(jnp.int32).reshape(1)
        filt_w = (hy_f_w1[l], hy_f_b1[l], hy_f_freq[l], hy_f_w2[l], hy_f_b2[l], hy_f_w3[l])
        j = l // 2
        routed = l % 2 == 1
        ffn_w = (ffn_wg, ffn_wu, ffn_wd, j)

        q_c, k_c, v_c, hp_c = _proj_call(xc, mod3, ctx_row, clen, w_in_bf, gain, head_avg, rope_ctx, ext, tm_ctx)
        k_c3 = k_c.reshape(bsz, clen, -1)
        v_c3 = v_c.reshape(bsz, clen, -1)
        if not last:
            hp_c3 = hp_c.reshape(bsz, clen, -1)
            att_c = _attn_call(fast, q_c.reshape(bsz, clen, -1), (k_c3, v_c3), _pick_tile(clen, 256))
            spec_c = _ctx_spectrum(_filter_call(clen, *filt_w), clen)
            hy_c = _ctx_hyena_call(hp_c3, hy_conv_w[l], hy_conv_b[l], spec_c, hy_d[l])
            pl_c = _pool_call(hp_c3, pool_w[l], pool_scale[l])
            xc_mid = _outproj_call(xc, att_c.reshape(bsz * clen, D_ATTN), hy_c.reshape(bsz * clen, D_HYENA),
                                   pl_c.reshape(bsz * clen, D_POOL), mod3, ctx_row, w_out_bf,
                                   ln1_g[l], ln1_b[l], alpha, tm_ctx)

        q_l, k_l, v_l, hp_l = _proj_call(xl, mod3, lat_row(tm_ffn), seq, w_in_bf, gain, head_avg, rope_lat, ext,
                                         tm_ffn)
        hp_l3 = hp_l.reshape(bsz, seq, -1)
        att = _attn_call(fast, q_l.reshape(bsz, seq, -1),
                         (k_l.reshape(bsz, seq, -1), v_l.reshape(bsz, seq, -1), k_c3, v_c3), _pick_tile(seq, 256))
        spec = _hyena_spectrum(_filter_call(seq, *filt_w), seq)
        hyo = _hyena_latent(hp_l3, hy_conv_w[l], hy_conv_b[l], spec, hy_d[l])
        plo = _pool_call(hp_l3, pool_w[l], pool_scale[l])
        xl = _outproj_call(xl, att.reshape(bsz * seq, D_ATTN), hyo.reshape(bsz * seq, D_HYENA),
                           plo.reshape(bsz * seq, D_POOL), mod3, lat_row(tm_lat), w_out_bf,
                           ln1_g[l], ln1_b[l], alpha, tm_lat)
        if routed:
            parts = [(xl, lambda t: t // seq)] + ([] if last else [(xc_mid, lambda t: bsz)])
            outs = _moe_call(parts, mod3, router_w[j], moe_wg, moe_wu, moe_wd, j * n_exp,
                             ln2_g[l], ln2_b[l], alpha, MOE_FC)
            xl = outs[0]
            if not last:
                xc = outs[1]
        else:
            xl = _ffn_call(xl, mod3, lat_row(tm_ffn), *ffn_w, ln2_g[l], ln2_b[l], alpha, tm_ffn, FFN_FC)
            if not last:
                xc = _ffn_call(xc_mid, mod3, ctx_row, *ffn_w, ln2_g[l], ln2_b[l], alpha, tm_ffn_ctx, FFN_FC)
    return xl.reshape(bsz, seq, d)
```

```python
import functools
import math

import numpy as np
import jax
import jax.numpy as jnp
from jax import lax
from jax.experimental import pallas as pl
from jax.experimental.pallas import tpu as pltpu

F32 = jnp.float32
BF16 = jnp.bfloat16
HIGHEST = lax.Precision.HIGHEST

GRID_W = 64
HEAD_DIM = 64
N_Q_HEADS = 8
N_KV_HEADS = 2
GQA_GROUP = N_Q_HEADS // N_KV_HEADS
D_ATTN = N_Q_HEADS * HEAD_DIM
KV_W = N_KV_HEADS * HEAD_DIM
QK_W = D_ATTN + KV_W
ATTN_SCALE = HEAD_DIM ** -0.5
ROPE_THETA = 10000.0
ROPE_AXIS_DIM = HEAD_DIM // 2
QK_EPS = 1e-6
D_HYENA = 256
HYENA_ORDER = 2
HYENA_BANDS = 16
HYENA_EMB = 1 + 2 * HYENA_BANDS
HYENA_DECAY_TARGET = 1e-2
HYENA_FAST_DECAY_PCT = 0.3
HYENA_SLOW_DECAY_PCT = 1.5
D_POOL = 256
POOL_WINDOWS = (2, 4, 8, 16)
POOL_GROUP = D_POOL // len(POOL_WINDOWS)
POOL_PAD = 8
TOP_K = 2
LN_EPS = 1e-5
ADALN_EPS = 1e-6
LOG2E = math.log2(math.e)

LANES = 128
FFT_N2 = 64
FFT_UNROLL = 4
VMEM_LIMIT = 56 * 1024 * 1024
FFN_FC = 2816
FFN_SUB = 256


def _cparams():
    return pltpu.CompilerParams(vmem_limit_bytes=VMEM_LIMIT)


def _layernorm(x, eps):
    mu = jnp.mean(x, axis=-1, keepdims=True)
    xc = x - mu
    var = jnp.mean(xc * xc, axis=-1, keepdims=True)
    return xc * lax.rsqrt(var + eps)


def _mod_kernel(c_ref, w_ref, b_ref, o_ref):
    c = c_ref[...]
    s = c * jax.nn.sigmoid(c)
    o_ref[0] = jnp.dot(s, w_ref[0], preferred_element_type=F32, precision=HIGHEST) + b_ref[0]


def _mod_call(cc, w_mod, b_mod):
    depth, d, d6 = w_mod.shape
    r = cc.shape[0]
    tn = 1536
    return pl.pallas_call(
        _mod_kernel,
        grid=(depth, d6 // tn),
        in_specs=[pl.BlockSpec((r, d), lambda l, j: (0, 0)),
                  pl.BlockSpec((1, d, tn), lambda l, j: (l, 0, j)),
                  pl.BlockSpec((1, 1, tn), lambda l, j: (l, 0, j))],
        out_specs=pl.BlockSpec((1, r, tn), lambda l, j: (l, 0, j)),
        out_shape=jax.ShapeDtypeStruct((depth, r, d6), F32),
        compiler_params=_cparams(),
        name="adaln_mod",
    )(cc, w_mod, b_mod.reshape(depth, 1, d6))


def _rope_tables(seq_len, rope):
    lane = np.arange(LANES)
    d = lane % HEAD_DIM
    if not rope:
        one = np.ones((seq_len, LANES), np.float32)
        zero = np.zeros((seq_len, LANES), np.float32)
        return one, zero, zero
    t = np.arange(seq_len)
    row = (t // GRID_W).astype(np.float64)
    col = (t % GRID_W).astype(np.float64)
    half = ROPE_AXIS_DIM // 2
    inv = ROPE_THETA ** (-np.arange(0, ROPE_AXIS_DIM, 2, dtype=np.float64) / ROPE_AXIS_DIM)
    pos = np.where((d // ROPE_AXIS_DIM)[None, :] == 0, row[:, None], col[:, None])
    ang = pos * inv[d % half][None, :]
    is_b = ((d % ROPE_AXIS_DIM) >= half)[None, :]
    cos = np.cos(ang)
    sin = np.sin(ang)
    s_up = np.where(is_b, sin, 0.0)
    s_dn = np.where(is_b, 0.0, -sin)
    return cos.astype(np.float32), s_up.astype(np.float32), s_dn.astype(np.float32)


def _store_padded_heads(blk, ext, out_ref, col0):
    lo = lax.broadcasted_iota(jnp.int32, blk.shape, 1) < HEAD_DIM
    out_ref[:, col0:col0 + LANES] = jnp.where(lo, blk, ext).astype(out_ref.dtype)
    out_ref[:, col0 + LANES:col0 + 2 * LANES] = jnp.where(lo, pltpu.roll(blk, HEAD_DIM, axis=1),
                                                          ext).astype(out_ref.dtype)


def _proj_kernel(x_ref, sh_ref, sc_ref, w_ref, gain_ref, bd_ref, cos_ref, sup_ref, sdn_ref, ext_ref,
                 q_ref, k_ref, v_ref, hp_ref):
    u = _layernorm(x_ref[...], ADALN_EPS) * (1.0 + sc_ref[0]) + sh_ref[0]
    p = jnp.dot(u.astype(BF16), w_ref[...], preferred_element_type=F32)
    cos = cos_ref[...]
    sup = sup_ref[...]
    sdn = sdn_ref[...]
    n_qblk = D_ATTN // LANES
    for j in range(QK_W // LANES):
        blk = p[:, j * LANES:(j + 1) * LANES]
        ms = jnp.dot((blk * blk).astype(BF16), bd_ref[...], preferred_element_type=F32)
        blk = blk * lax.rsqrt(ms + QK_EPS) * gain_ref[:, j * LANES:(j + 1) * LANES]
        up = pltpu.roll(blk, ROPE_AXIS_DIM // 2, axis=1)
        dn = pltpu.roll(blk, LANES - ROPE_AXIS_DIM // 2, axis=1)
        rot = blk * cos + up * sup + dn * sdn
        if j < n_qblk:
            _store_padded_heads(rot, ext_ref[0:1, :], q_ref, 2 * j * LANES)
        else:
            _store_padded_heads(rot, ext_ref[1:2, :], k_ref, 2 * (j - n_qblk) * LANES)
    _store_padded_heads(p[:, QK_W:QK_W + KV_W], ext_ref[2:3, :], v_ref, 0)
    hp_ref[...] = p[:, QK_W + KV_W:]


def _proj_call(x2, mod3, mod_row, seq_len, w_in_bf, gain, bd, tables, ext, tm):
    r, d = x2.shape
    d_in = w_in_bf.shape[1]
    d_hp = d_in - QK_W - KV_W
    nseq = seq_len // tm
    cos, sup, sdn = tables
    tab_spec = pl.BlockSpec((tm, LANES), lambda i: (i % nseq, 0))
    return pl.pallas_call(
        _proj_kernel,
        grid=(r // tm,),
        in_specs=[pl.BlockSpec((tm, d), lambda i: (i, 0)),
                  pl.BlockSpec((1, 1, d), lambda i: (mod_row(i) * 6 + 0, 0, 0)),
                  pl.BlockSpec((1, 1, d), lambda i: (mod_row(i) * 6 + 1, 0, 0)),
                  pl.BlockSpec((d, d_in), lambda i: (0, 0)),
                  pl.BlockSpec((1, QK_W), lambda i: (0, 0)),
                  pl.BlockSpec((LANES, LANES), lambda i: (0, 0)),
                  tab_spec, tab_spec, tab_spec,
                  pl.BlockSpec((3, LANES), lambda i: (0, 0))],
        out_specs=[pl.BlockSpec((tm, N_Q_HEADS * LANES), lambda i: (i, 0)),
                   pl.BlockSpec((tm, N_KV_HEADS * LANES), lambda i: (i, 0)),
                   pl.BlockSpec((tm, N_KV_HEADS * LANES), lambda i: (i, 0)),
                   pl.BlockSpec((tm, d_hp), lambda i: (i, 0))],
        out_shape=[jax.ShapeDtypeStruct((r, N_Q_HEADS * LANES), BF16),
                   jax.ShapeDtypeStruct((r, N_KV_HEADS * LANES), BF16),
                   jax.ShapeDtypeStruct((r, N_KV_HEADS * LANES), BF16),
                   jax.ShapeDtypeStruct((r, d_hp), F32)],
        compiler_params=_cparams(),
        name="ln_mod_in_proj",
    )(x2, mod3, mod3, w_in_bf, gain, bd, cos, sup, sdn, ext)


ATTN_KEY_CHUNK = 384
ATTN_SAFE_SHIFT = 60.0


def _attn_kernel(fast_ref, q_ref, *refs):
    o_ref = refs[-1]
    sources = [(refs[i], refs[i + 1]) for i in range(0, len(refs) - 1, 2)]
    tq = q_ref.shape[0]
    chunks = [(k_ref, v_ref, c0, min(ATTN_KEY_CHUNK, k_ref.shape[0] - c0))
              for k_ref, v_ref in sources for c0 in range(0, k_ref.shape[0], ATTN_KEY_CHUNK)]

    def run(running_max):
        for h in range(N_KV_HEADS):
            qs = jnp.concatenate([q_ref[:, (h * GQA_GROUP + g) * LANES:(h * GQA_GROUP + g + 1) * LANES]
                                  for g in range(GQA_GROUP)], axis=0)
            acc = jnp.zeros((GQA_GROUP * tq, LANES), F32)
            m = jnp.full((GQA_GROUP * tq, 1), -jnp.inf, F32)
            for k_ref, v_ref, c0, kw in chunks:
                kc = k_ref[c0:c0 + kw, h * LANES:(h + 1) * LANES]
                vc = v_ref[c0:c0 + kw, h * LANES:(h + 1) * LANES]
                s = lax.dot_general(qs, kc, (((1,), (1,)), ((), ())), preferred_element_type=F32)
                if running_max:
                    m_new = jnp.maximum(m, jnp.max(s, axis=-1, keepdims=True))
                    acc = acc * jnp.exp2(m - m_new)
                    s = s - m_new
                    m = m_new
                acc = acc + jnp.dot(jnp.exp2(s).astype(BF16), vc, preferred_element_type=F32)
            o = acc[:, :HEAD_DIM] / acc[:, HEAD_DIM:HEAD_DIM + 1]
            for g in range(GQA_GROUP):
                c0 = (h * GQA_GROUP + g) * HEAD_DIM
                o_ref[:, c0:c0 + HEAD_DIM] = o[g * tq:(g + 1) * tq].astype(o_ref.dtype)

    @pl.when(fast_ref[0] > 0)
    def _():
        run(False)

    @pl.when(fast_ref[0] == 0)
    def _():
        run(True)


def _attn_call(fast, q, kv, tq):
    b, lq, qw = q.shape
    kv_specs = [pl.BlockSpec((None,) + a.shape[1:], lambda i, j, f: (i, 0, 0)) for a in kv]
    return pl.pallas_call(
        _attn_kernel,
        grid_spec=pltpu.PrefetchScalarGridSpec(
            num_scalar_prefetch=1,
            grid=(b, lq // tq),
            in_specs=[pl.BlockSpec((None, tq, qw), lambda i, j, f: (i, j, 0))] + kv_specs,
            out_specs=pl.BlockSpec((None, tq, D_ATTN), lambda i, j, f: (i, j, 0))),
        out_shape=jax.ShapeDtypeStruct((b, lq, D_ATTN), BF16),
        compiler_params=_cparams(),
        name="attention",
    )(fast, q, *kv)


def _filter_consts(seq_len):
    t = np.linspace(0.0, 1.0, seq_len, dtype=np.float32).astype(np.float64)[:, None]
    omega = 2.0 * math.pi * np.arange(seq_len, dtype=np.float64)[:, None] / seq_len
    bands = np.linspace(1e-4, HYENA_BANDS - 1, HYENA_BANDS, dtype=np.float32).astype(np.float64)[None, :]
    feats = np.concatenate([t, np.cos(omega * bands), -np.sin(omega * bands)], axis=-1)
    feats = np.pad(feats, ((0, 0), (0, LANES - HYENA_EMB)))
    max_decay = math.log(HYENA_DECAY_TARGET) / HYENA_FAST_DECAY_PCT
    min_decay = math.log(HYENA_DECAY_TARGET) / HYENA_SLOW_DECAY_PCT
    deltas = np.linspace(min_decay, max_decay, D_HYENA, dtype=np.float32).astype(np.float64)
    decay = np.exp(-t * np.abs(deltas)[None, :])
    rev = (seq_len - np.arange(seq_len)) % seq_len
    f32 = lambda a: jnp.asarray(a.astype(np.float32))
    return f32(feats), f32(feats[rev]), f32(decay), f32(decay[rev])


def _filter_kernel(feat_ref, featr_ref, w1_ref, b1_ref, fr_ref, w2_ref, b2_ref, w3_ref, dec_ref, decr_ref, o_ref,
                   hid_ref):
    seq_len = feat_ref.shape[0]
    dot = functools.partial(jnp.dot, preferred_element_type=F32, precision=HIGHEST)

    @pl.when(pl.program_id(0) == 0)
    def _():
        fr = fr_ref[...]
        for i, feat in enumerate((feat_ref, featr_ref)):
            h = jnp.sin(fr * (dot(feat[...], w1_ref[...]) + b1_ref[...]))
            hid_ref[i] = jnp.sin(fr * (dot(h, w2_ref[...]) + b2_ref[...]))

    fwd = dot(hid_ref[0], w3_ref[:, :D_HYENA]) * dec_ref[...]
    bwd = dot(hid_ref[1], w3_ref[:, D_HYENA:]) * decr_ref[...]
    tot = jnp.sum(jnp.abs(fwd), axis=0, keepdims=True) + jnp.sum(jnp.abs(bwd), axis=0, keepdims=True)
    inv = 1.0 / tot
    o_ref[:seq_len, :] = fwd * inv
    row = lax.broadcasted_iota(jnp.int32, bwd.shape, 0)
    o_ref[seq_len:, :] = jnp.where(row == 0, 0.0, bwd * inv)


def _filter_call(seq_len, w1, b1, freq, w2, b2, w3):
    feats, feats_rev, decay, decay_rev = _filter_consts(seq_len)
    hid = w2.shape[0]
    w1p = jnp.pad(w1, ((0, LANES - HYENA_EMB), (0, 0)))
    full = lambda shape: pl.BlockSpec(shape, lambda o: (0,) * len(shape))
    return pl.pallas_call(
        _filter_kernel,
        grid=(HYENA_ORDER,),
        in_specs=[full((seq_len, LANES)), full((seq_len, LANES)), full((LANES, hid)), full((1, hid)),
                  full((1, hid)), full((hid, hid)), full((1, hid)),
                  pl.BlockSpec((hid, 2 * D_HYENA), lambda o: (0, o)),
                  full((seq_len, D_HYENA)), full((seq_len, D_HYENA))],
        out_specs=pl.BlockSpec((2 * seq_len, D_HYENA), lambda o: (0, o)),
        out_shape=jax.ShapeDtypeStruct((2 * seq_len, HYENA_ORDER * D_HYENA), F32),
        scratch_shapes=[pltpu.VMEM((2, seq_len, hid), F32)],
        compiler_params=_cparams(),
        name="hyena_filter",
    )(feats, feats_rev, w1p, b1.reshape(1, hid), freq.reshape(1, hid), w2, b2.reshape(1, hid), w3,
      decay, decay_rev)


@functools.lru_cache(maxsize=None)
def _fft_consts(seq_len):
    n = 2 * seq_len
    n2 = FFT_N2
    n1 = n // n2
    h1 = n1 // 2
    k1 = np.arange(n1)
    w1 = np.exp(-2j * np.pi * np.outer(k1, np.arange(n1)) / n1)
    f1 = np.zeros((n1, 2, 2 * h1))
    f1[:, 0, :h1], f1[:, 0, h1:] = w1[:, :h1].real, -w1[:, :h1].imag
    f1[:, 1, :h1], f1[:, 1, h1:] = w1[:, :h1].imag, w1[:, :h1].real
    f1 = f1.reshape(2 * n1, 2 * h1)
    f1r = np.stack([w1.real, w1.imag], axis=1).reshape(2 * n1, n1)
    a2 = np.arange(n2)
    tw = np.exp(-2j * np.pi * np.outer(k1, a2) / n)
    w2 = np.exp(-2j * np.pi * np.outer(a2, a2) / n2)
    m = w2[None, :, :] * tw[:, None, :]
    ff = np.concatenate([np.concatenate([m.real, -m.imag], axis=2),
                         np.concatenate([m.imag, m.real], axis=2)], axis=1)
    g = np.conj(w2.T)[None, :, :] * np.conj(tw)[:, :, None]
    gi = np.concatenate([np.concatenate([g.real, -g.imag], axis=2),
                         np.concatenate([g.imag, g.real], axis=2)], axis=1)
    v = np.conj(w1[:, :h1]).T
    f1i = np.zeros((2, h1, n1, 2))
    f1i[0, :, :, 0], f1i[0, :, :, 1] = v.real, -v.imag
    f1i[1, :, :, 0], f1i[1, :, :, 1] = v.imag, v.real
    f1i = f1i.reshape(2 * h1, 2 * n1)
    il = np.arange(2 * n2).reshape(2, n2).T.reshape(-1)
    ff_il = ff[:, :, il]
    gi_il = gi[:, il, :]
    f32 = lambda a: np.asarray(a, np.float32)
    return dict(n=n, n1=n1, h1=h1, f1=f32(f1), f1r=f32(f1r), ff=f32(ff), ff_il=f32(ff_il), gi_il=f32(gi_il),
                f1i=f32(f1i))


def _fftr_kernel(x_ref, f_ref, o_ref):
    n1 = f_ref.shape[1]

    def body(n2, carry):
        x = x_ref[pl.ds(n2, n1, stride=FFT_N2), :]
        o_ref[pl.ds(n2, 2 * n1, stride=FFT_N2), :] = jnp.dot(f_ref[...], x, preferred_element_type=F32,
                                                            precision=HIGHEST)
        return carry

    lax.fori_loop(0, FFT_N2, body, 0, unroll=FFT_UNROLL)


def _fftr_call(buf, f1r):
    n, cols = buf.shape
    rows = f1r.shape[0] * FFT_N2
    return pl.pallas_call(
        _fftr_kernel,
        grid=(cols // LANES,),
        in_specs=[pl.BlockSpec((n, LANES), lambda j: (0, j)),
                  pl.BlockSpec(f1r.shape, lambda j: (0, 0))],
        out_specs=pl.BlockSpec((None, rows, LANES), lambda j: (j, 0, 0)),
        out_shape=jax.ShapeDtypeStruct((cols // LANES, rows, LANES), F32),
        compiler_params=_cparams(),
        name="fft_filter_outer_dft",
    )(buf, f1r)


def _spec_kernel(a_ref, ff_ref, o_ref, *, scale):
    for kk in range(a_ref.shape[1]):
        for cb in range(a_ref.shape[0]):
            o_ref[kk, :, cb * LANES:(cb + 1) * LANES] = jnp.dot(
                ff_ref[kk], a_ref[cb, kk], preferred_element_type=F32, precision=HIGHEST) * scale


def _spec_call(a4, ff, scale, kt):
    nblk, n1, r, _ = a4.shape
    return pl.pallas_call(
        functools.partial(_spec_kernel, scale=scale),
        grid=(n1 // kt,),
        in_specs=[pl.BlockSpec((nblk, kt, r, LANES), lambda i: (0, i, 0, 0)),
                  pl.BlockSpec((kt, r, r), lambda i: (i, 0, 0))],
        out_specs=pl.BlockSpec((kt, r, nblk * LANES), lambda i: (i, 0, 0)),
        out_shape=jax.ShapeDtypeStruct((n1, r, nblk * LANES), F32),
        compiler_params=_cparams(),
        name="fft_filter_spectrum",
    )(a4, ff)


def _short_conv(z, w, b):
    n = z.shape[0]
    row = lax.broadcasted_iota(jnp.int32, z.shape, 0)
    prev = jnp.where(row == 0, 0.0, pltpu.roll(z, 1, axis=0))
    nxt = jnp.where(row == n - 1, 0.0, pltpu.roll(z, n - 1, axis=0))
    return prev * w[0:1, :] + z * w[1:2, :] + nxt * w[2:3, :] + b


def _ffta_kernel(*refs, conv):
    if conv:
        z_ref, w_ref, cb_ref, f_ref, o_ref, x_ref = refs
        for m in range(2):
            x_ref[m] = _short_conv(z_ref[m], w_ref[...], cb_ref[...])
    else:
        z_ref, f_ref, o_ref = refs
        x_ref = z_ref
    n1 = f_ref.shape[0] // 2
    h1 = f_ref.shape[1] // 2

    def body(n2, carry):
        rows = pl.ds(n2, h1, stride=FFT_N2)
        x = jnp.concatenate([x_ref[0, rows, :], x_ref[1, rows, :]], axis=0).astype(BF16)
        a = jnp.dot(f_ref[...], x, preferred_element_type=F32).astype(BF16)
        o_ref[pl.ds(n2, n1, stride=FFT_N2), :] = pltpu.bitcast(a, jnp.uint32)
        return carry

    lax.fori_loop(0, FFT_N2, body, 0, unroll=FFT_UNROLL)


def _ffta_call(z4, col0, f1_bf, conv_w=None, conv_b=None):
    groups, _, l, _ = z4.shape
    halves = D_HYENA // LANES
    n1 = f1_bf.shape[0] // 2
    conv = conv_w is not None
    in_specs = [pl.BlockSpec((None, 2, l, LANES), lambda i, j: (i, 0, 0, col0 + j))]
    args = [z4]
    scratch = []
    if conv:
        in_specs += [pl.BlockSpec((3, LANES), lambda i, j: (0, col0 + j)),
                     pl.BlockSpec((1, LANES), lambda i, j: (0, col0 + j))]
        args += [conv_w, conv_b]
        scratch = [pltpu.VMEM((2, l, LANES), F32)]
    in_specs.append(pl.BlockSpec(f1_bf.shape, lambda i, j: (0, 0)))
    args.append(f1_bf)
    return pl.pallas_call(
        functools.partial(_ffta_kernel, conv=conv),
        grid=(groups, halves),
        in_specs=in_specs,
        out_specs=pl.BlockSpec((None, None, n1 * FFT_N2, LANES), lambda i, j: (i, j, 0, 0)),
        out_shape=jax.ShapeDtypeStruct((groups, halves, n1 * FFT_N2, LANES), jnp.uint32),
        scratch_shapes=scratch,
        compiler_params=_cparams(),
        name="fft_outer_dft",
    )(*args)


def _fftb_kernel(a_ref, ff_ref, gi_ref, h_ref, o_ref):
    groups, halves, kt = a_ref.shape[:3]
    half = FFT_N2

    def body(kk, carry):
        ff = ff_ref[kk]
        gi = gi_ref[kk]
        hr = h_ref[kk, :half, :]
        hi = h_ref[kk, half:, :]
        for g in range(groups):
            a = jnp.concatenate([pltpu.bitcast(a_ref[g, hh, kk], BF16) for hh in range(halves)], axis=1)
            xh = jnp.dot(ff, a, preferred_element_type=F32)
            xr, xi = xh[:half], xh[half:]
            y = jnp.concatenate([xr * hr - xi * hi, xr * hi + xi * hr], axis=0).astype(BF16)
            b = jnp.dot(gi, y, preferred_element_type=F32).astype(BF16)
            for hh in range(halves):
                o_ref[g, hh, kk] = pltpu.bitcast(b[:, hh * LANES:(hh + 1) * LANES], jnp.uint32)
        return carry

    lax.fori_loop(0, kt, body, 0)


def _fftb_call(a5, ff_bf, gi_bf, spec, order, kt):
    g, halves, n1, n2, _ = a5.shape
    r = 2 * n2
    return pl.pallas_call(
        _fftb_kernel,
        grid=(n1 // kt,),
        in_specs=[pl.BlockSpec((g, halves, kt, n2, LANES), lambda i: (0, 0, i, 0, 0)),
                  pl.BlockSpec((kt, r, r), lambda i: (i, 0, 0)),
                  pl.BlockSpec((kt, r, r), lambda i: (i, 0, 0)),
                  pl.BlockSpec((kt, r, halves * LANES), lambda i: (i, 0, order))],
        out_specs=pl.BlockSpec((g, halves, kt, n2, LANES), lambda i: (0, 0, i, 0, 0)),
        out_shape=jax.ShapeDtypeStruct(a5.shape, jnp.uint32),
        compiler_params=_cparams(),
        name="fft_inner_conv",
    )(a5, ff_bf, gi_bf, spec)


def _fftc_kernel(*refs, z_conv):
    if z_conv:
        b_ref, f_ref, z_ref, zw_ref, zb_ref, g_ref, gw_ref, gb_ref, d_ref, o_ref = refs
    else:
        b_ref, f_ref, z_ref, g_ref, gw_ref, gb_ref, d_ref, o_ref = refs
    n1 = f_ref.shape[1] // 2
    h1 = f_ref.shape[0] // 2
    d = d_ref[...]

    def body(n2, carry):
        b = pltpu.bitcast(b_ref[pl.ds(n2, n1, stride=FFT_N2), :], BF16)
        y = jnp.dot(f_ref[...], b, preferred_element_type=F32)
        rows = pl.ds(n2, h1, stride=FFT_N2)
        for m in range(2):
            o_ref[m, rows, :] = y[m * h1:(m + 1) * h1]
        return carry

    lax.fori_loop(0, FFT_N2, body, 0, unroll=FFT_UNROLL)
    for m in range(2):
        z = _short_conv(z_ref[m], zw_ref[...], zb_ref[...]) if z_conv else z_ref[m]
        gate = _short_conv(g_ref[m], gw_ref[...], gb_ref[...])
        o_ref[m] = gate * (o_ref[m] + d * z)


def _fftc_call(b4, f1i_bf, z4, z_col0, z_conv, hp4, g_col0, conv_w, conv_b, d2, order):
    groups, halves, rows, _ = b4.shape
    l = z4.shape[2]
    pair = lambda col0: pl.BlockSpec((None, 2, l, LANES), lambda i, j: (i, 0, 0, col0 + j))
    taps = lambda col0: [pl.BlockSpec((3, LANES), lambda i, j: (0, col0 + j)),
                         pl.BlockSpec((1, LANES), lambda i, j: (0, col0 + j))]
    in_specs = [pl.BlockSpec((None, None, rows, LANES), lambda i, j: (i, j, 0, 0)),
                pl.BlockSpec(f1i_bf.shape, lambda i, j: (0, 0)),
                pair(z_col0)]
    args = [b4, f1i_bf, z4]
    if z_conv:
        in_specs += taps(z_col0)
        args += [conv_w, conv_b]
    in_specs += [pair(g_col0)] + taps(g_col0) + [pl.BlockSpec((None, 1, LANES), lambda i, j: (order, 0, j))]
    args += [hp4, conv_w, conv_b, d2.reshape(d2.shape[0], 1, d2.shape[1])]
    return pl.pallas_call(
        functools.partial(_fftc_kernel, z_conv=z_conv),
        grid=(groups, halves),
        in_specs=in_specs,
        out_specs=pair(0),
        out_shape=jax.ShapeDtypeStruct((groups, 2, l, D_HYENA), F32),
        compiler_params=_cparams(),
        name="fft_outer_inverse_gate",
    )(*args)


def _hyena_spectrum(buf, seq_len):
    cst = _fft_consts(seq_len)
    n, n1 = cst["n"], cst["n1"]
    a = _fftr_call(buf, jnp.asarray(cst["f1r"]))
    a4 = a.reshape(a.shape[0], n1, 2 * FFT_N2, LANES)
    return _spec_call(a4, jnp.asarray(cst["ff"]), 1.0 / n, min(n1, 8))


def _hyena_latent(hp3, conv_w, conv_b, spec, d):
    b, l, _ = hp3.shape
    cst = _fft_consts(l)
    n1, h1 = cst["n1"], cst["h1"]
    groups = b // 2
    halves = D_HYENA // LANES
    kt = min(n1, 8)
    f1 = jnp.asarray(cst["f1"], BF16)
    ff = jnp.asarray(cst["ff_il"], BF16)
    gi = jnp.asarray(cst["gi_il"], BF16)
    f1i = jnp.asarray(cst["f1i"], BF16)
    hp4 = hp3.reshape(groups, 2, l, hp3.shape[2])
    cb = conv_b.reshape(1, -1)
    z = hp4
    for o in range(HYENA_ORDER):
        first = o == 0
        a = _ffta_call(z, 0, f1, conv_w, cb) if first else _ffta_call(z, 0, f1)
        bb = _fftb_call(a.reshape(groups, halves, n1, FFT_N2, LANES), ff, gi, spec, o, kt)
        z = _fftc_call(bb.reshape(groups, halves, n1 * FFT_N2, LANES), f1i, z, 0, first, hp4, (1 + o) * halves,
                       conv_w, cb, d, o)
    return z.reshape(b, l, D_HYENA)


@functools.lru_cache(maxsize=None)
def _dft_consts(seq_len):
    n = 2 * seq_len
    w = np.exp(-2j * np.pi * np.outer(np.arange(n), np.arange(n)) / n)
    fwd_full = np.concatenate([w.real, w.imag], axis=0)
    fwd = fwd_full[:, :seq_len]
    wi = np.conj(w[:, :seq_len]).T
    inv = np.concatenate([wi.real, -wi.imag], axis=1)
    f32 = lambda a: np.asarray(a, np.float32)
    return dict(n=n, fwd_full=f32(fwd_full), fwd=f32(fwd), inv=f32(inv))


def _cspec_kernel(buf_ref, f_ref, o_ref, *, scale):
    o_ref[...] = jnp.dot(f_ref[...], buf_ref[...], preferred_element_type=F32, precision=HIGHEST) * scale


def _ctx_spectrum(buf, seq_len):
    cst = _dft_consts(seq_len)
    n = cst["n"]
    return pl.pallas_call(
        functools.partial(_cspec_kernel, scale=1.0 / n),
        out_shape=jax.ShapeDtypeStruct((2 * n, buf.shape[1]), F32),
        compiler_params=_cparams(),
        name="ctx_filter_spectrum",
    )(buf, jnp.asarray(cst["fwd_full"]))


def _ctx_hyena_kernel(z_ref, w_ref, b_ref, f_ref, fi_ref, h_ref, d_ref, o_ref):
    n = 2 * z_ref.shape[0]
    zc = _short_conv(z_ref[...], w_ref[...], b_ref[...])
    cur = zc[:, :D_HYENA]
    for o in range(HYENA_ORDER):
        gate = zc[:, (1 + o) * D_HYENA:(2 + o) * D_HYENA]
        xh = jnp.dot(f_ref[...], cur.astype(BF16), preferred_element_type=F32)
        xr, xi = xh[:n], xh[n:]
        hr = h_ref[:n, o * D_HYENA:(o + 1) * D_HYENA]
        hi = h_ref[n:, o * D_HYENA:(o + 1) * D_HYENA]
        y = jnp.concatenate([xr * hr - xi * hi, xr * hi + xi * hr], axis=0).astype(BF16)
        conv = jnp.dot(fi_ref[...], y, preferred_element_type=F32)
        cur = gate * (conv + d_ref[o:o + 1, :] * cur)
    o_ref[...] = cur


def _ctx_hyena_call(hp3, conv_w, conv_b, spec, d):
    b, l, _ = hp3.shape
    cst = _dft_consts(l)
    n = cst["n"]
    w3 = 3 * D_HYENA
    full = lambda shape: pl.BlockSpec(shape, lambda i: (0,) * len(shape))
    return pl.pallas_call(
        _ctx_hyena_kernel,
        grid=(b,),
        in_specs=[pl.BlockSpec((None, l, w3), lambda i: (i, 0, 0)),
                  full((3, w3)), full((1, w3)), full((2 * n, l)), full((l, 2 * n)),
                  full((2 * n, HYENA_ORDER * D_HYENA)), full((HYENA_ORDER, D_HYENA))],
        out_specs=pl.BlockSpec((None, l, D_HYENA), lambda i: (i, 0, 0)),
        out_shape=jax.ShapeDtypeStruct((b, l, D_HYENA), F32),
        compiler_params=_cparams(),
        name="ctx_hyena",
    )(hp3, conv_w, conv_b.reshape(1, -1), jnp.asarray(cst["fwd"], BF16), jnp.asarray(cst["inv"], BF16),
      spec, d)


def _pool_kernel(p_ref, w_ref, sc_ref, o_ref):
    x = p_ref[...]
    l, c = x.shape
    half = pl.program_id(1)
    zpad = jnp.zeros((POOL_PAD, c), F32)
    xp = jnp.concatenate([zpad, x, zpad], axis=0)
    tot = l + 2 * POOL_PAD
    sums = []
    f = xp
    for win in POOL_WINDOWS:
        f = f + pltpu.roll(f, tot - win // 2, axis=0)
        sums.append(pltpu.roll(f, win // 2, axis=0)[POOL_PAD:POOL_PAD + l])
    t = lax.broadcasted_iota(jnp.int32, (l, c), 0)
    grp = lax.broadcasted_iota(jnp.int32, (l, c), 1) // POOL_GROUP + half * (LANES // POOL_GROUP)
    total = sums[-1]
    cnt = None
    for gi in range(len(POOL_WINDOWS) - 1, -1, -1):
        win = POOL_WINDOWS[gi]
        cw = (jnp.minimum(t + (win - win // 2), l) - jnp.maximum(t - win // 2, 0)).astype(F32)
        if cnt is None:
            cnt = cw
        else:
            sel = grp == gi
            total = jnp.where(sel, sums[gi], total)
            cnt = jnp.where(sel, cw, cnt)
    dlt = total / cnt - x
    y = jnp.dot(dlt.astype(BF16), w_ref[0], preferred_element_type=F32)
    o_ref[...] = y * sc_ref[...]


def _pool_call(hp3, pool_w, pool_scale):
    b, l, w = hp3.shape
    nh = D_POOL // LANES
    first = (w - D_POOL) // LANES
    per = LANES // POOL_GROUP
    wbd = jnp.zeros((nh, LANES, LANES), F32)
    for g in range(len(POOL_WINDOWS)):
        r0 = (g % per) * POOL_GROUP
        wbd = wbd.at[g // per, r0:r0 + POOL_GROUP, r0:r0 + POOL_GROUP].set(pool_w[g])
    return pl.pallas_call(
        _pool_kernel,
        grid=(b, nh),
        in_specs=[pl.BlockSpec((None, l, LANES), lambda i, j: (i, 0, first + j)),
                  pl.BlockSpec((1, LANES, LANES), lambda i, j: (j, 0, 0)),
                  pl.BlockSpec((1, LANES), lambda i, j: (0, j))],
        out_specs=pl.BlockSpec((None, l, LANES), lambda i, j: (i, 0, j)),
        out_shape=jax.ShapeDtypeStruct((b, l, D_POOL), F32),
        compiler_params=_cparams(),
        name="pool_mixer",
    )(hp3, wbd.astype(BF16), pool_scale.reshape(1, D_POOL))


def _outproj_kernel(x_ref, att_ref, hy_ref, pl_ref, gate_ref, w_ref, g_ref, b_ref, o_ref, *, alpha):
    mix = jnp.dot(att_ref[...], w_ref[0:D_ATTN, :], preferred_element_type=F32)
    mix += jnp.dot(hy_ref[...].astype(BF16), w_ref[D_ATTN:D_ATTN + D_HYENA, :], preferred_element_type=F32)
    mix += jnp.dot(pl_ref[...].astype(BF16), w_ref[D_ATTN + D_HYENA:, :], preferred_element_type=F32)
    y = alpha * x_ref[...] + gate_ref[0] * mix
    o_ref[...] = _layernorm(y, LN_EPS) * g_ref[...] + b_ref[...]


def _outproj_call(x2, att2, hy2, pl2, mod3, mod_row, w_out_bf, g, b, alpha, tm):
    r, d = x2.shape
    row = lambda w: pl.BlockSpec((tm, w), lambda i: (i, 0))
    vec = pl.BlockSpec((1, d), lambda i: (0, 0))
    return pl.pallas_call(
        functools.partial(_outproj_kernel, alpha=alpha),
        grid=(r // tm,),
        in_specs=[row(d), row(D_ATTN), row(D_HYENA), row(D_POOL),
                  pl.BlockSpec((1, 1, d), lambda i: (mod_row(i) * 6 + 2, 0, 0)),
                  pl.BlockSpec(w_out_bf.shape, lambda i: (0, 0)),
                  vec, vec],
        out_specs=row(d),
        out_shape=jax.ShapeDtypeStruct((r, d), F32),
        compiler_params=_cparams(),
        name="out_proj_deepnorm",
    )(x2, att2, hy2, pl2, mod3, w_out_bf, g.reshape(1, d), b.reshape(1, d))


def _ffn_kernel(x_ref, sh_ref, sc_ref, gate_ref, wg_ref, wu_ref, wd_ref, g_ref, b_ref, o_ref, u_ref, acc_ref, *,
                alpha, n_chunk):
    j = pl.program_id(1)

    @pl.when(j == 0)
    def _():
        u_ref[...] = (_layernorm(x_ref[...], ADALN_EPS) * (1.0 + sc_ref[0]) + sh_ref[0]).astype(BF16)
        acc_ref[...] = jnp.zeros_like(acc_ref)

    u = u_ref[...]
    part = None
    for c0 in range(0, wg_ref.shape[2], FFN_SUB):
        c1 = min(c0 + FFN_SUB, wg_ref.shape[2])
        a = jnp.dot(u, wg_ref[0, :, c0:c1], preferred_element_type=F32)
        bb = jnp.dot(u, wu_ref[0, :, c0:c1], preferred_element_type=F32)
        h = (a * jax.nn.sigmoid(a) * bb).astype(BF16)
        y = jnp.dot(h, wd_ref[0, c0:c1, :], preferred_element_type=F32)
        part = y if part is None else part + y
    acc_ref[...] += part

    @pl.when(j == n_chunk - 1)
    def _():
        y = alpha * x_ref[...] + gate_ref[0] * acc_ref[...]
        o_ref[...] = _layernorm(y, LN_EPS) * g_ref[...] + b_ref[...]


def _ffn_call(x2, mod3, mod_row, w_gate, w_up, w_down, w_idx, g, b, alpha, tm, fc):
    r, d = x2.shape
    dff = w_gate.shape[2]
    n_chunk = dff // fc
    wmode = pl.Buffered(1) if n_chunk == 1 else None
    vec = pl.BlockSpec((1, d), lambda i, j: (0, 0))
    mod = lambda which: pl.BlockSpec((1, 1, d), lambda i, j: (mod_row(i) * 6 + which, 0, 0))
    return pl.pallas_call(
        functools.partial(_ffn_kernel, alpha=alpha, n_chunk=n_chunk),
        grid=(r // tm, n_chunk),
        in_specs=[pl.BlockSpec((tm, d), lambda i, j: (i, 0)), mod(3), mod(4), mod(5),
                  pl.BlockSpec((1, d, fc), lambda i, j: (w_idx, 0, j), pipeline_mode=wmode),
                  pl.BlockSpec((1, d, fc), lambda i, j: (w_idx, 0, j), pipeline_mode=wmode),
                  pl.BlockSpec((1, fc, d), lambda i, j: (w_idx, j, 0), pipeline_mode=wmode),
                  vec, vec],
        out_specs=pl.BlockSpec((tm, d), lambda i, j: (i, 0)),
        out_shape=jax.ShapeDtypeStruct((r, d), F32),
        scratch_shapes=[pltpu.VMEM((tm, d), BF16), pltpu.VMEM((tm, d), F32)],
        compiler_params=_cparams(),
        name="channel_mixer",
    )(x2, mod3, mod3, mod3, w_gate, w_up, w_down, g.reshape(1, d), b.reshape(1, d))


MOE_WINDOW = 1024
MOE_TILE = 64
MOE_SUPER = 16
MOE_FC = 512
MOE_GCHUNK = 1280
MOE_SUB = 256


def _router_kernel(x_ref, sh_ref, sc_ref, rw_ref, tri_ref, u_ref, comb_ref, pos_ref):
    u = (_layernorm(x_ref[...], ADALN_EPS) * (1.0 + sc_ref[0]) + sh_ref[0]).astype(BF16)
    u_ref[...] = u
    logits = jnp.dot(u, rw_ref[...].astype(BF16), preferred_element_type=F32)
    n_exp = logits.shape[1]
    lane = lax.broadcasted_iota(jnp.int32, logits.shape, 1)
    m1 = jnp.max(logits, axis=-1, keepdims=True)
    i1 = jnp.min(jnp.where(logits == m1, lane, n_exp), axis=-1, keepdims=True)
    rest = jnp.where(lane == i1, -jnp.inf, logits)
    m2 = jnp.max(rest, axis=-1, keepdims=True)
    i2 = jnp.min(jnp.where(rest == m2, lane, n_exp), axis=-1, keepdims=True)
    e2 = jnp.exp(m2 - m1)
    den = 1.0 + e2
    comb = jnp.where(lane == i1, 1.0 / den, 0.0) + jnp.where(lane == i2, e2 / den, 0.0)
    comb_ref[...] = comb
    routed = jnp.where(comb > 0.0, 1.0, 0.0).astype(BF16)
    pos_ref[...] = jnp.dot(tri_ref[...], routed, preferred_element_type=F32) - 1.0


def _router_call(x2, mod3, mod_row, router_w):
    r, d = x2.shape
    n_exp = router_w.shape[1]
    tm = MOE_WINDOW
    tri = jnp.asarray(np.tril(np.ones((tm, tm), np.float32)), BF16)
    mod = lambda which: pl.BlockSpec((1, 1, d), lambda i: (mod_row(i) * 6 + which, 0, 0))
    small = pl.BlockSpec((tm, n_exp), lambda i: (i, 0))
    return pl.pallas_call(
        _router_kernel,
        grid=(r // tm,),
        in_specs=[pl.BlockSpec((tm, d), lambda i: (i, 0)), mod(3), mod(4),
                  pl.BlockSpec(router_w.shape, lambda i: (0, 0)),
                  pl.BlockSpec((tm, tm), lambda i: (0, 0))],
        out_specs=[pl.BlockSpec((tm, d), lambda i: (i, 0)), small, small],
        out_shape=[jax.ShapeDtypeStruct((r, d), BF16), jax.ShapeDtypeStruct((r, n_exp), F32),
                   jax.ShapeDtypeStruct((r, n_exp), F32)],
        compiler_params=_cparams(),
        name="moe_router",
    )(x2, mod3, mod3, router_w, tri)


def _moe_plan(comb, pos_f, n_tok):
    n_exp = comb.shape[1]
    win, tile = MOE_WINDOW, MOE_TILE
    n_win = n_tok // win
    cap_tiles = 2 * win // tile + n_exp
    i32 = jnp.int32
    mask = (comb > 0.0).reshape(n_win, win, n_exp)
    pos = pos_f.astype(i32).reshape(n_win, win, n_exp)
    cnt = pos[:, -1, :] + 1
    ntile = (cnt + tile - 1) // tile
    woff = jnp.cumsum(ntile, axis=1) - ntile
    rows = jnp.where(mask, tile * woff[:, None, :] + pos, -1)
    row_a = jnp.max(rows, axis=-1)
    row_b = jnp.max(jnp.where(rows == row_a[..., None], -1, rows), axis=-1)
    rowsel = jnp.stack([row_a, row_b], axis=1)
    win_rows = tile * jnp.sum(ntile, axis=1)

    etiles = jnp.sum(ntile, axis=0)
    nsup = (etiles + MOE_SUPER - 1) // MOE_SUPER
    ebase = MOE_SUPER * (jnp.cumsum(nsup) - nsup)
    erank = ebase[None, :] + jnp.cumsum(ntile, axis=0) - ntile
    n_sup = -(-(n_win * cap_tiles) // MOE_SUPER) + n_exp
    n_rank = n_sup * MOE_SUPER
    slots = win // tile
    s = jnp.arange(slots, dtype=i32)[None, None, :]
    ok = s < ntile[:, :, None]
    rank = jnp.where(ok, erank[:, :, None] + s, n_rank)
    wm_tile = (jnp.arange(n_win, dtype=i32) * cap_tiles)[:, None, None] + woff[:, :, None] + s
    src = jnp.zeros((n_rank + 1,), i32).at[rank.reshape(-1)].set(wm_tile.reshape(-1))[:n_rank]
    sup_total = jnp.sum(nsup)
    g = jnp.arange(n_sup, dtype=i32)
    sup_valid = (g < sup_total).astype(i32)
    sup_exp = jnp.clip(jnp.sum((g[:, None] >= jnp.cumsum(nsup)[None, :]).astype(i32), axis=1), 0, n_exp - 1)
    sup_exp = jnp.where(sup_valid > 0, sup_exp, sup_exp[jnp.maximum(sup_total - 1, 0)])

    tile_rank = jnp.zeros((n_win * cap_tiles + 1,), i32).at[
        jnp.where(ok, wm_tile, n_win * cap_tiles).reshape(-1)].set(
        jnp.minimum(rank, n_rank - 1).reshape(-1))[:n_win * cap_tiles]
    comb3 = comb.reshape(n_win, win, n_exp)
    gate_a = jnp.sum(jnp.where(rows == row_a[..., None], comb3, 0.0), axis=-1)
    gate_b = jnp.sum(jnp.where((rows == row_b[..., None]) & mask, comb3, 0.0), axis=-1)
    tok_sel = jnp.stack([row_a.astype(F32), row_b.astype(F32), gate_a, gate_b], axis=-1).reshape(n_tok, 4)
    return dict(n_win=n_win, cap_tiles=cap_tiles, n_sup=n_sup, n_rank=n_rank,
                rowsel=rowsel, win_rows=win_rows.astype(i32),
                src=src, sup_exp=sup_exp.astype(i32), sup_valid=sup_valid,
                tile_rank=tile_rank, tok_sel=tok_sel)


def _dispatch_kernel(nrows_ref, *refs, offsets):
    u_refs = refs[:len(offsets)]
    sel_ref, o_ref = refs[len(offsets):]
    w = pl.program_id(0)
    u_ref = u_refs[0]
    if len(offsets) > 1:
        tokens = u_refs[0][...]
        for s in range(1, len(offsets)):
            tokens = jnp.where(w >= offsets[s], u_refs[s][...], tokens)
    r1 = sel_ref[0:1, :]
    r2 = sel_ref[1:2, :]
    for base in range(0, o_ref.shape[0], MOE_GCHUNK):
        size = min(MOE_GCHUNK, o_ref.shape[0] - base)
        rows = pl.ds(base, size)

        @pl.when(base < nrows_ref[w])
        def _():
            rid = lax.broadcasted_iota(jnp.int32, (size, r1.shape[1]), 0) + base
            onehot = jnp.where(rid == r1, 1.0, jnp.where(rid == r2, 1.0, 0.0)).astype(BF16)
            src = tokens if len(offsets) > 1 else u_ref[...]
            o_ref[rows, :] = jnp.dot(onehot, src, preferred_element_type=F32).astype(o_ref.dtype)

        @pl.when(base >= nrows_ref[w])
        def _():
            o_ref[rows, :] = jnp.zeros((size, o_ref.shape[1]), o_ref.dtype)


def _dispatch_call(us, plan):
    d = us[0].shape[1]
    cap_rows = plan["cap_tiles"] * MOE_TILE
    n_win = plan["n_win"]
    counts = [u.shape[0] // MOE_WINDOW for u in us]
    offsets = tuple(sum(counts[:s]) for s in range(len(us)))

    def u_spec(s):
        return pl.BlockSpec((MOE_WINDOW, d), lambda w, n: (jnp.clip(w - offsets[s], 0, counts[s] - 1), 0))

    return pl.pallas_call(
        functools.partial(_dispatch_kernel, offsets=offsets),
        grid_spec=pltpu.PrefetchScalarGridSpec(
            num_scalar_prefetch=1,
            grid=(n_win,),
            in_specs=[u_spec(s) for s in range(len(us))] + [
                pl.BlockSpec((None, TOP_K, MOE_WINDOW), lambda w, n: (w, 0, 0))],
            out_specs=pl.BlockSpec((cap_rows, d), lambda w, n: (w, 0))),
        out_shape=jax.ShapeDtypeStruct((n_win * cap_rows, d), BF16),
        compiler_params=_cparams(),
        name="moe_dispatch",
    )(plan["win_rows"], *us, plan["rowsel"])


def _expert_kernel(src_ref, exp_ref, valid_ref, *refs, n_chunk):
    tiles = refs[:MOE_SUPER]
    wg_ref, wu_ref, wd_ref, o_ref, u_ref, acc_ref = refs[MOE_SUPER:]
    g = pl.program_id(0)
    j = pl.program_id(1)
    ok = valid_ref[g] > 0

    @pl.when(j == 0)
    def _():
        for k in range(MOE_SUPER):
            u_ref[k * MOE_TILE:(k + 1) * MOE_TILE, :] = tiles[k][...]
        acc_ref[...] = jnp.zeros_like(acc_ref)

    @pl.when(ok)
    def _():
        u = u_ref[...]
        part = None
        for c0 in range(0, wg_ref.shape[2], MOE_SUB):
            a = jnp.dot(u, wg_ref[0, :, c0:c0 + MOE_SUB].astype(BF16), preferred_element_type=F32)
            b = jnp.dot(u, wu_ref[0, :, c0:c0 + MOE_SUB].astype(BF16), preferred_element_type=F32)
            h = (a * jax.nn.sigmoid(a) * b).astype(BF16)
            y = jnp.dot(h, wd_ref[0, c0:c0 + MOE_SUB, :].astype(BF16), preferred_element_type=F32)
            part = y if part is None else part + y
        acc_ref[...] += part

    @pl.when(j == n_chunk - 1)
    def _():
        o_ref[...] = acc_ref[...].astype(o_ref.dtype)


def _expert_call(xw, plan, w_gate, w_up, w_down, layer_base, fc):
    d = xw.shape[1]
    dff = w_gate.shape[2]
    n_chunk = dff // fc
    rows = MOE_SUPER * MOE_TILE

    def tile_spec(k):
        return pl.BlockSpec((MOE_TILE, d), lambda g, j, src, ex, va: (src[g * MOE_SUPER + k], 0))

    def chunk(j, va, g):
        return jnp.where(va[g] > 0, j, n_chunk - 1)

    in_specs = [tile_spec(k) for k in range(MOE_SUPER)] + [
        pl.BlockSpec((1, d, fc), lambda g, j, src, ex, va: (layer_base + ex[g], 0, chunk(j, va, g))),
        pl.BlockSpec((1, d, fc), lambda g, j, src, ex, va: (layer_base + ex[g], 0, chunk(j, va, g))),
        pl.BlockSpec((1, fc, d), lambda g, j, src, ex, va: (layer_base + ex[g], chunk(j, va, g), 0))]
    return pl.pallas_call(
        functools.partial(_expert_kernel, n_chunk=n_chunk),
        grid_spec=pltpu.PrefetchScalarGridSpec(
            num_scalar_prefetch=3,
            grid=(plan["n_sup"], n_chunk),
            in_specs=in_specs,
            out_specs=pl.BlockSpec((rows, d), lambda g, j, src, ex, va: (g, 0)),
            scratch_shapes=[pltpu.VMEM((rows, d), BF16), pltpu.VMEM((rows, d), F32)]),
        out_shape=jax.ShapeDtypeStruct((plan["n_sup"] * rows, d), BF16),
        compiler_params=_cparams(),
        name="moe_experts",
    )(plan["src"], plan["sup_exp"], plan["sup_valid"], *([xw] * MOE_SUPER), w_gate, w_up, w_down)


def _combine_kernel(rank_ref, *refs, alpha, n_tiles):
    tile_refs = refs[:n_tiles]
    sel_ref, x_ref, gate_ref, g_ref, b_ref, o_ref, rows_ref, acc_ref = refs[n_tiles:]
    for k in range(n_tiles):
        rows_ref[k * MOE_TILE:(k + 1) * MOE_TILE, :] = tile_refs[k][...]
    sel = sel_ref[...]
    lane = lax.broadcasted_iota(jnp.int32, sel.shape, 1)
    col = lambda k: jnp.sum(jnp.where(lane == k, sel, 0.0), axis=-1, keepdims=True)
    row_a = col(0).astype(jnp.int32)
    row_b = col(1).astype(jnp.int32)
    w_a = col(2)
    w_b = col(3)
    n_rows = rows_ref.shape[0]
    for c0 in range(0, n_rows, MOE_GCHUNK):
        size = min(MOE_GCHUNK, n_rows - c0)
        rid = lax.broadcasted_iota(jnp.int32, (sel.shape[0], size), 1) + c0
        q = jnp.where(rid == row_a, w_a, jnp.where(rid == row_b, w_b, 0.0)).astype(BF16)
        part = jnp.dot(q, rows_ref[c0:c0 + size, :], preferred_element_type=F32)
        if c0 == 0:
            acc_ref[...] = part
        else:
            acc_ref[...] += part
    y = alpha * x_ref[...] + gate_ref[0] * acc_ref[...]
    o_ref[...] = _layernorm(y, LN_EPS) * g_ref[...] + b_ref[...]


def _combine_call(ys, plan, x2, win_base, mod3, mod_row, g, b, alpha):
    n_tok, d = x2.shape
    n_tiles = plan["cap_tiles"]

    def tile_spec(k):
        return pl.BlockSpec((MOE_TILE, d), lambda w, rk: (rk[(win_base + w) * n_tiles + k], 0))

    win = lambda width: pl.BlockSpec((MOE_WINDOW, width), lambda w, rk: (w, 0))
    vec = pl.BlockSpec((1, d), lambda w, rk: (0, 0))
    return pl.pallas_call(
        functools.partial(_combine_kernel, alpha=alpha, n_tiles=n_tiles),
        grid_spec=pltpu.PrefetchScalarGridSpec(
            num_scalar_prefetch=1,
            grid=(n_tok // MOE_WINDOW,),
            in_specs=[tile_spec(k) for k in range(n_tiles)] + [
                pl.BlockSpec((MOE_WINDOW, 4), lambda w, rk: (win_base + w, 0)), win(d),
                pl.BlockSpec((1, 1, d), lambda w, rk: (mod_row(w) * 6 + 5, 0, 0)),
                vec, vec],
            out_specs=win(d),
            scratch_shapes=[pltpu.VMEM((n_tiles * MOE_TILE, d), BF16), pltpu.VMEM((MOE_WINDOW, d), F32)]),
        out_shape=jax.ShapeDtypeStruct((n_tok, d), F32),
        compiler_params=_cparams(),
        name="moe_combine_deepnorm",
    )(plan["tile_rank"], *([ys] * n_tiles), plan["tok_sel"], x2, mod3, g.reshape(1, d), b.reshape(1, d))


def _moe_call(parts, mod3, router_w, w_gate, w_up, w_down, layer_base, g, b, alpha, fc):
    routed = [_router_call(x2, mod3, lambda i, r=tok_row: r(i * MOE_WINDOW), router_w) for x2, tok_row in parts]
    comb = jnp.concatenate([r[1] for r in routed], axis=0)
    pos = jnp.concatenate([r[2] for r in routed], axis=0)
    plan = _moe_plan(comb, pos, comb.shape[0])
    xw = _dispatch_call([r[0] for r in routed], plan)
    ys = _expert_call(xw, plan, w_gate, w_up, w_down, layer_base, fc)
    outs = []
    win_base = 0
    for x2, tok_row in parts:
        outs.append(_combine_call(ys, plan, x2, win_base, mod3, lambda w, r=tok_row: r(w * MOE_WINDOW),
                                  g, b, alpha))
        win_base += x2.shape[0] // MOE_WINDOW
    return outs


def _pick_tile(n, pref):
    t = min(n, pref)
    while n % t:
        t //= 2
    return t


def kernel(x, c, ctx, c_ctx, w_mod, b_mod, w_in, q_gain, k_gain, hy_conv_w, hy_conv_b, hy_f_w1, hy_f_b1, hy_f_freq, hy_f_w2, hy_f_b2, hy_f_w3, hy_d, pool_w, pool_scale, w_out, ln1_g, ln1_b, ln2_g, ln2_b, ffn_w_gate, ffn_w_up, ffn_w_down, router_w, moe_w_gate, moe_w_up, moe_w_down):
    bsz, seq, d = x.shape
    clen = ctx.shape[1]
    depth = w_mod.shape[0]
    alpha = (2.0 * depth) ** 0.25
    assert seq % MOE_WINDOW == 0 and (bsz * clen) % MOE_WINDOW == 0 and bsz % 2 == 0
    n_exp = router_w.shape[2]
    moe_wg = moe_w_gate.reshape((-1,) + moe_w_gate.shape[2:])
    moe_wu = moe_w_up.reshape((-1,) + moe_w_up.shape[2:])
    moe_wd = moe_w_down.reshape((-1,) + moe_w_down.shape[2:])
    ffn_wg, ffn_wu, ffn_wd = (w.astype(BF16) for w in (ffn_w_gate, ffn_w_up, ffn_w_down))

    n_rows = -(-(bsz + 1) // 8) * 8
    cc = jnp.zeros((n_rows, d), F32).at[:bsz].set(c).at[bsz].set(c_ctx)
    mod = _mod_call(cc, w_mod, b_mod)

    tm_lat = _pick_tile(seq, 512)
    tm_ctx = _pick_tile(clen, 512)
    tm_ffn = _pick_tile(seq, 1024)
    tm_ffn_ctx = _pick_tile(bsz * clen, 1024)
    lat_row = lambda tm: (lambda i: i // (seq // tm))
    ctx_row = lambda i: bsz
    rope_lat = tuple(jnp.asarray(t) for t in _rope_tables(seq, True))
    rope_ctx = tuple(jnp.asarray(t) for t in _rope_tables(clen, False))
    head_avg = np.kron(np.eye(LANES // HEAD_DIM), np.full((HEAD_DIM, HEAD_DIM), 1.0 / HEAD_DIM))
    head_avg = jnp.asarray(head_avg, BF16)

    xl = x.reshape(bsz * seq, d)
    xc = ctx.reshape(bsz * clen, d)
    for l in range(depth):
        last = l == depth - 1
        mod3 = mod[l].reshape(n_rows * 6, 1, d)
        w_in_bf = w_in[l].astype(BF16)
        w_out_bf = w_out[l].astype(BF16)
        gain = jnp.concatenate([jnp.tile(q_gain[l] * (ATTN_SCALE * LOG2E), N_Q_HEADS),
                                jnp.tile(k_gain[l], N_KV_HEADS)]).reshape(1, QK_W)
        shift = (1.02 * HEAD_DIM * ATTN_SCALE * LOG2E) * jnp.max(jnp.abs(q_gain[l])) * jnp.max(jnp.abs(k_gain[l]))
        lane64 = (jnp.arange(LANES) == HEAD_DIM).astype(F32)
        ext = jnp.stack([lane64, -shift * lane64, lane64])
        fast = (shift <= ATTN_SAFE_SHIFT).astype(jnp.int32).reshape(1)
        filt_w = (hy_f_w1[l], hy_f_b1[l], hy_f_freq[l], hy_f_w2[l], hy_f_b2[l], hy_f_w3[l])
        j = l // 2
        routed = l % 2 == 1
        ffn_w = (ffn_wg, ffn_wu, ffn_wd, j)

        q_c, k_c, v_c, hp_c = _proj_call(xc, mod3, ctx_row, clen, w_in_bf, gain, head_avg, rope_ctx, ext, tm_ctx)
        k_c3 = k_c.reshape(bsz, clen, -1)
        v_c3 = v_c.reshape(bsz, clen, -1)
        if not last:
            hp_c3 = hp_c.reshape(bsz, clen, -1)
            att_c = _attn_call(fast, q_c.reshape(bsz, clen, -1), (k_c3, v_c3), _pick_tile(clen, 256))
            spec_c = _ctx_spectrum(_filter_call(clen, *filt_w), clen)
            hy_c = _ctx_hyena_call(hp_c3, hy_conv_w[l], hy_conv_b[l], spec_c, hy_d[l])
            pl_c = _pool_call(hp_c3, pool_w[l], pool_scale[l])
            xc_mid = _outproj_call(xc, att_c.reshape(bsz * clen, D_ATTN), hy_c.reshape(bsz * clen, D_HYENA),
                                   pl_c.reshape(bsz * clen, D_POOL), mod3, ctx_row, w_out_bf,
                                   ln1_g[l], ln1_b[l], alpha, tm_ctx)

        q_l, k_l, v_l, hp_l = _proj_call(xl, mod3, lat_row(tm_ffn), seq, w_in_bf, gain, head_avg, rope_lat, ext,
                                         tm_ffn)
        hp_l3 = hp_l.reshape(bsz, seq, -1)
        att = _attn_call(fast, q_l.reshape(bsz, seq, -1),
                         (k_l.reshape(bsz, seq, -1), v_l.reshape(bsz, seq, -1), k_c3, v_c3), _pick_tile(seq, 256))
        spec = _hyena_spectrum(_filter_call(seq, *filt_w), seq)
        hyo = _hyena_latent(hp_l3, hy_conv_w[l], hy_conv_b[l], spec, hy_d[l])
        plo = _pool_call(hp_l3, pool_w[l], pool_scale[l])
        xl = _outproj_call(xl, att.reshape(bsz * seq, D_ATTN), hyo.reshape(bsz * seq, D_HYENA),
                           plo.reshape(bsz * seq, D_POOL), mod3, lat_row(tm_lat), w_out_bf,
                           ln1_g[l], ln1_b[l], alpha, tm_lat)
        if routed:
            parts = [(xl, lambda t: t // seq)] + ([] if last else [(xc_mid, lambda t: bsz)])
            outs = _moe_call(parts, mod3, router_w[j], moe_wg, moe_wu, moe_wd, j * n_exp,
                             ln2_g[l], ln2_b[l], alpha, MOE_FC)
            xl = outs[0]
            if not last:
                xc = outs[1]
        else:
            xl = _ffn_call(xl, mod3, lat_row(tm_ffn), *ffn_w, ln2_g[l], ln2_b[l], alpha, tm_ffn, FFN_FC)
            if not last:
                xc = _ffn_call(xc_mid, mod3, ctx_row, *ffn_w, ln2_g[l], ln2_b[l], alpha, tm_ffn_ctx, FFN_FC)
    return xl.reshape(bsz, seq, d)
```
